```python
import math
import jax, jax.numpy as jnp
from jax import lax
import numpy as np

D_MODEL = 1024
BATCH = 32
SEQ = 2048
DEPTH = 4

N_A_LAYERS = DEPTH // 2
N_B_LAYERS = DEPTH - N_A_LAYERS
SSM_EXPAND = 2
SSM_D_INNER = SSM_EXPAND * D_MODEL
SSM_HEAD_DIM = 64
SSM_HEADS = SSM_D_INNER // SSM_HEAD_DIM
SSM_GROUPS = 4
SSM_HEADS_PER_GROUP = SSM_HEADS // SSM_GROUPS
SSM_STATE = 128
SSM_CONV = 4
SSM_CHUNK = 128
SSM_CONV_DIM = SSM_D_INNER + 2 * SSM_GROUPS * SSM_STATE
SSM_IN_DIM = 2 * SSM_D_INNER + 2 * SSM_GROUPS * SSM_STATE + SSM_HEADS
SB_HEADS = 16
SB_HEAD_DIM = 64
SB_WIDTH = SB_HEADS * SB_HEAD_DIM
SB_BLOCK = 128
D_FF = 2816
FFN_CONV = 3
PLE_DIM = 256
NORM_EPS = 1e-6
SSM_NORM_EPS = 1e-5

kernel_name = "yoco_mamba2_stickbreak_hybrid"


def rms_norm(x, gain, eps=NORM_EPS):
    xf = x.astype(jnp.float32)
    y = xf * lax.rsqrt(jnp.mean(xf * xf, axis=-1, keepdims=True) + eps)
    return (y * gain.astype(jnp.float32)).astype(x.dtype)


def causal_depthwise_conv(u, w, b):
    k = w.shape[0]
    out = lax.conv_general_dilated(
        u, w[:, None, :].astype(u.dtype), window_strides=(1,), padding=[(k - 1, 0)],
        dimension_numbers=("NWC", "WIO", "NWC"), feature_group_count=u.shape[-1])
    return out + b.astype(u.dtype)


def ssd_chunked(x, dt, a, b_mat, c_mat):
    bsz, seq = x.shape[0], x.shape[1]
    nc, L = seq // SSM_CHUNK, SSM_CHUNK
    G, R, P, N = SSM_GROUPS, SSM_HEADS_PER_GROUP, SSM_HEAD_DIM, SSM_STATE
    xr = (x * dt[..., None]).reshape(bsz, nc, L, G, R, P)
    br = b_mat.reshape(bsz, nc, L, G, N)
    cr = c_mat.reshape(bsz, nc, L, G, N)
    a_dt = (dt * a).reshape(bsz, nc, L, G, R).transpose(0, 1, 3, 4, 2)
    a_cs = jnp.cumsum(a_dt, axis=-1)
    tri = jnp.tril(jnp.ones((L, L), dtype=bool))
    decay_in = jnp.exp(jnp.where(tri, a_cs[..., :, None] - a_cs[..., None, :], -jnp.inf))
    cb = jnp.einsum("bclgn,bcsgn->bcgls", cr, br)
    y_diag = jnp.einsum("bcgls,bcgrls,bcsgrp->bclgrp", cb, decay_in, xr)
    decay_states = jnp.exp(a_cs[..., -1:] - a_cs)
    states = jnp.einsum("bclgn,bcgrl,bclgrp->bcgrpn", br, decay_states, xr)
    states = jnp.concatenate([jnp.zeros_like(states[:, :1]), states], axis=1)
    chunk_sum = jnp.pad(a_cs[..., -1].transpose(0, 2, 3, 1), ((0, 0), (0, 0), (0, 0), (1, 0)))
    chunk_cs = jnp.cumsum(chunk_sum, axis=-1)
    tri_c = jnp.tril(jnp.ones((nc + 1, nc + 1), dtype=bool))
    decay_chunk = jnp.exp(jnp.where(tri_c, chunk_cs[..., :, None] - chunk_cs[..., None, :], -jnp.inf))
    new_states = jnp.einsum("bgrzc,bcgrpn->bzgrpn", decay_chunk, states)
    prev_states = new_states[:, :-1]
    y_off = jnp.einsum("bclgn,bcgrpn,bcgrl->bclgrp", cr, prev_states, jnp.exp(a_cs))
    return (y_diag + y_off).reshape(bsz, seq, SSM_HEADS, P)


def mamba2_mixer(h, in_proj, conv_w, conv_b, dt_bias, a_log, d_skip, norm_w, out_proj):
    bsz, seq, _ = h.shape
    zxbcdt = h @ in_proj
    z = zxbcdt[..., :SSM_D_INNER]
    xbc = zxbcdt[..., SSM_D_INNER:SSM_D_INNER + SSM_CONV_DIM]
    dt_raw = zxbcdt[..., SSM_D_INNER + SSM_CONV_DIM:]
    xbc = jax.nn.silu(causal_depthwise_conv(xbc, conv_w, conv_b)).astype(jnp.float32)
    xs = xbc[..., :SSM_D_INNER].reshape(bsz, seq, SSM_HEADS, SSM_HEAD_DIM)
    bm = xbc[..., SSM_D_INNER:SSM_D_INNER + SSM_GROUPS * SSM_STATE].reshape(bsz, seq, SSM_GROUPS, SSM_STATE)
    cm = xbc[..., SSM_D_INNER + SSM_GROUPS * SSM_STATE:].reshape(bsz, seq, SSM_GROUPS, SSM_STATE)
    dt = jax.nn.softplus(dt_raw.astype(jnp.float32) + dt_bias.astype(jnp.float32))
    a = -jnp.exp(a_log.astype(jnp.float32))
    y = ssd_chunked(xs, dt, a, bm, cm) + xs * d_skip.astype(jnp.float32)[:, None]
    g = (y.reshape(bsz, seq, SSM_D_INNER) * jax.nn.silu(z.astype(jnp.float32))).reshape(bsz, seq, SSM_GROUPS, -1)
    g = g * lax.rsqrt(jnp.mean(g * g, axis=-1, keepdims=True) + SSM_NORM_EPS)
    g = g.reshape(bsz, seq, SSM_D_INNER) * norm_w.astype(jnp.float32)
    return g.astype(h.dtype) @ out_proj


def stick_breaking_attention(q, k, v):
    seq = q.shape[1]
    scale = SB_HEAD_DIM ** -0.5
    outs = []
    for blk in range(seq // SB_BLOCK):
        q0 = blk * SB_BLOCK
        q1 = q0 + SB_BLOCK
        z = jnp.einsum("bthd,bshd->bhts", q[:, q0:q1], k[:, :q1]).astype(jnp.float32) * scale
        causal = (q0 + jnp.arange(SB_BLOCK))[:, None] > jnp.arange(q1)[None, :]
        log_beta = jax.nn.log_sigmoid(z)
        log_keep = jnp.where(causal, jax.nn.log_sigmoid(-z), 0.0)
        log_rest = lax.cumsum(log_keep, axis=3, reverse=True) - log_keep
        w = jnp.where(causal, jnp.exp(log_beta + log_rest), 0.0)
        outs.append(jnp.einsum("bhts,bshd->bthd", w.astype(v.dtype), v[:, :q1]))
    return jnp.concatenate(outs, axis=1)


def conv_glu_ffn(h, w_up, conv_w, conv_b, w_down):
    u = causal_depthwise_conv(h @ w_up, conv_w, conv_b)
    gate, val = jnp.split(u, 2, axis=-1)
    return (jax.nn.silu(gate) * val) @ w_down


def per_layer_embedding(h, p_i, norm_gain, w_gate, w_proj):
    gate = jax.nn.sigmoid(rms_norm(h, norm_gain) @ w_gate)
    return gate * (p_i @ w_proj)


def _fwd_setup_inputs(seed: int = 0) -> dict:
    key = jax.random.key(seed)
    ks = iter(jax.random.split(key, 32))

    def normal(shape, scale):
        return scale * jax.random.normal(next(ks), shape, jnp.float32)

    def gain(shape):
        return 1.0 + normal(shape, 0.02)

    x = normal((BATCH, SEQ, D_MODEL), 1.0)
    p = normal((DEPTH, BATCH, SEQ, PLE_DIM), 1.0)
    attn_norm = gain((DEPTH, D_MODEL))
    ffn_norm = gain((DEPTH, D_MODEL))
    ple_norm = gain((DEPTH, D_MODEL))
    ssm_in_proj = normal((N_A_LAYERS, D_MODEL, SSM_IN_DIM), D_MODEL ** -0.5)
    ssm_conv_w = normal((N_A_LAYERS, SSM_CONV, SSM_CONV_DIM), SSM_CONV ** -0.5)
    ssm_conv_b = normal((N_A_LAYERS, SSM_CONV_DIM), 0.01)
    u = jax.random.uniform(next(ks), (N_A_LAYERS, SSM_HEADS), jnp.float32)
    dt0 = jnp.exp(u * (math.log(0.1) - math.log(0.001)) + math.log(0.001))
    ssm_dt_bias = dt0 + jnp.log(-jnp.expm1(-dt0))
    ssm_a_log = jnp.log(jax.random.uniform(next(ks), (N_A_LAYERS, SSM_HEADS), jnp.float32, 1.0, 16.0))
    ssm_d = 1.0 + normal((N_A_LAYERS, SSM_HEADS), 0.1)
    ssm_norm = gain((N_A_LAYERS, SSM_D_INNER))
    ssm_out_proj = normal((N_A_LAYERS, SSM_D_INNER, D_MODEL), SSM_D_INNER ** -0.5)
    kv_norm = gain((D_MODEL,))
    w_kv = normal((D_MODEL, 2 * SB_WIDTH), D_MODEL ** -0.5)
    w_q = normal((N_B_LAYERS, D_MODEL, SB_WIDTH), D_MODEL ** -0.5)
    w_o = normal((N_B_LAYERS, SB_WIDTH, D_MODEL), SB_WIDTH ** -0.5)
    ffn_up = normal((DEPTH, D_MODEL, 2 * D_FF), D_MODEL ** -0.5)
    ffn_conv_w = normal((DEPTH, FFN_CONV, 2 * D_FF), FFN_CONV ** -0.5)
    ffn_conv_b = normal((DEPTH, 2 * D_FF), 0.01)
    ffn_down = normal((DEPTH, D_FF, D_MODEL), D_FF ** -0.5)
    ple_gate = normal((DEPTH, D_MODEL, D_MODEL), D_MODEL ** -0.5)
    ple_proj = normal((DEPTH, PLE_DIM, D_MODEL), PLE_DIM ** -0.5)
    final_norm = gain((D_MODEL,))
    return {"x": x, "p": p, "attn_norm": attn_norm, "ffn_norm": ffn_norm, "ple_norm": ple_norm,
            "ssm_in_proj": ssm_in_proj, "ssm_conv_w": ssm_conv_w, "ssm_conv_b": ssm_conv_b,
            "ssm_dt_bias": ssm_dt_bias, "ssm_a_log": ssm_a_log, "ssm_d": ssm_d,
            "ssm_norm": ssm_norm, "ssm_out_proj": ssm_out_proj, "kv_norm": kv_norm, "w_kv": w_kv,
            "w_q": w_q, "w_o": w_o, "ffn_up": ffn_up, "ffn_conv_w": ffn_conv_w,
            "ffn_conv_b": ffn_conv_b, "ffn_down": ffn_down, "ple_gate": ple_gate,
            "ple_proj": ple_proj, "final_norm": final_norm}


def _fwd_reference(x, p, attn_norm, ffn_norm, ple_norm, ssm_in_proj, ssm_conv_w, ssm_conv_b,
              ssm_dt_bias, ssm_a_log, ssm_d, ssm_norm, ssm_out_proj, kv_norm, w_kv,
              w_q, w_o, ffn_up, ffn_conv_w, ffn_conv_b, ffn_down, ple_gate, ple_proj,
              final_norm):
    bsz, seq, _ = x.shape
    h = x
    k_shared = None
    v_shared = None
    for i in range(DEPTH):
        hn = rms_norm(h, attn_norm[i])
        if i < N_A_LAYERS:
            mix = mamba2_mixer(hn, ssm_in_proj[i], ssm_conv_w[i], ssm_conv_b[i], ssm_dt_bias[i],
                               ssm_a_log[i], ssm_d[i], ssm_norm[i], ssm_out_proj[i])
        else:
            j = i - N_A_LAYERS
            q = (hn @ w_q[j]).reshape(bsz, seq, SB_HEADS, SB_HEAD_DIM)
            o = stick_breaking_attention(q, k_shared, v_shared)
            mix = o.reshape(bsz, seq, SB_WIDTH) @ w_o[j]
        h = h + mix
        h = h + conv_glu_ffn(rms_norm(h, ffn_norm[i]), ffn_up[i], ffn_conv_w[i], ffn_conv_b[i], ffn_down[i])
        h = h + per_layer_embedding(h, p[i], ple_norm[i], ple_gate[i], ple_proj[i])
        if i == N_A_LAYERS - 1:
            kv = rms_norm(h, kv_norm) @ w_kv
            k_shared = kv[..., :SB_WIDTH].reshape(bsz, seq, SB_HEADS, SB_HEAD_DIM)
            v_shared = kv[..., SB_WIDTH:].reshape(bsz, seq, SB_HEADS, SB_HEAD_DIM)
    return rms_norm(h, final_norm)


import jax as _jax
import jax.numpy as _jnp

TWIN_FORMAT = 'train_step'
FWD_PARAMS = ['x', 'p', 'attn_norm', 'ffn_norm', 'ple_norm', 'ssm_in_proj', 'ssm_conv_w', 'ssm_conv_b', 'ssm_dt_bias', 'ssm_a_log', 'ssm_d', 'ssm_norm', 'ssm_out_proj', 'kv_norm', 'w_kv', 'w_q', 'w_o', 'ffn_up', 'ffn_conv_w', 'ffn_conv_b', 'ffn_down', 'ple_gate', 'ple_proj', 'final_norm']
TWIN_WEIGHTS = ['attn_norm', 'ffn_norm', 'ple_norm', 'ssm_in_proj', 'ssm_conv_w', 'ssm_conv_b', 'ssm_dt_bias', 'ssm_a_log', 'ssm_d', 'ssm_norm', 'ssm_out_proj', 'kv_norm', 'w_kv', 'w_q', 'w_o', 'ffn_up', 'ffn_conv_w', 'ffn_conv_b', 'ffn_down', 'ple_gate', 'ple_proj', 'final_norm']
TWIN_DIFF_INPUT = 'x'
TWIN_INPUTS = ['x', 'p', 'attn_norm', 'ffn_norm', 'ple_norm', 'ssm_in_proj', 'ssm_conv_w', 'ssm_conv_b', 'ssm_dt_bias', 'ssm_a_log', 'ssm_d', 'ssm_norm', 'ssm_out_proj', 'kv_norm', 'w_kv', 'w_q', 'w_o', 'ffn_up', 'ffn_conv_w', 'ffn_conv_b', 'ffn_down', 'ple_gate', 'ple_proj', 'final_norm', 'loss_target', 'm_attn_norm', 'm_ffn_norm', 'm_ple_norm', 'm_ssm_in_proj', 'm_ssm_conv_w', 'm_ssm_conv_b', 'm_ssm_dt_bias', 'm_ssm_a_log', 'm_ssm_d', 'm_ssm_norm', 'm_ssm_out_proj', 'm_kv_norm', 'm_w_kv', 'm_w_q', 'm_w_o', 'm_ffn_up', 'm_ffn_conv_w', 'm_ffn_conv_b', 'm_ffn_down', 'm_ple_gate', 'm_ple_proj', 'm_final_norm', 'v_attn_norm', 'v_ffn_norm', 'v_ple_norm', 'v_ssm_in_proj', 'v_ssm_conv_w', 'v_ssm_conv_b', 'v_ssm_dt_bias', 'v_ssm_a_log', 'v_ssm_d', 'v_ssm_norm', 'v_ssm_out_proj', 'v_kv_norm', 'v_w_kv', 'v_w_q', 'v_w_o', 'v_ffn_up', 'v_ffn_conv_w', 'v_ffn_conv_b', 'v_ffn_down', 'v_ple_gate', 'v_ple_proj', 'v_final_norm']
TWIN_OUTPUTS = ['loss', 'grad_x', 'grad_attn_norm', 'grad_ffn_norm', 'grad_ple_norm', 'grad_ssm_in_proj', 'grad_ssm_conv_w', 'grad_ssm_conv_b', 'grad_ssm_dt_bias', 'grad_ssm_a_log', 'grad_ssm_d', 'grad_ssm_norm', 'grad_ssm_out_proj', 'grad_kv_norm', 'grad_w_kv', 'grad_w_q', 'grad_w_o', 'grad_ffn_up', 'grad_ffn_conv_w', 'grad_ffn_conv_b', 'grad_ffn_down', 'grad_ple_gate', 'grad_ple_proj', 'grad_final_norm', 'delta_attn_norm', 'delta_ffn_norm', 'delta_ple_norm', 'delta_ssm_in_proj', 'delta_ssm_conv_w', 'delta_ssm_conv_b', 'delta_ssm_dt_bias', 'delta_ssm_a_log', 'delta_ssm_d', 'delta_ssm_norm', 'delta_ssm_out_proj', 'delta_kv_norm', 'delta_w_kv', 'delta_w_q', 'delta_w_o', 'delta_ffn_up', 'delta_ffn_conv_w', 'delta_ffn_conv_b', 'delta_ffn_down', 'delta_ple_gate', 'delta_ple_proj', 'delta_final_norm', 'new_m_attn_norm', 'new_m_ffn_norm', 'new_m_ple_norm', 'new_m_ssm_in_proj', 'new_m_ssm_conv_w', 'new_m_ssm_conv_b', 'new_m_ssm_dt_bias', 'new_m_ssm_a_log', 'new_m_ssm_d', 'new_m_ssm_norm', 'new_m_ssm_out_proj', 'new_m_kv_norm', 'new_m_w_kv', 'new_m_w_q', 'new_m_w_o', 'new_m_ffn_up', 'new_m_ffn_conv_w', 'new_m_ffn_conv_b', 'new_m_ffn_down', 'new_m_ple_gate', 'new_m_ple_proj', 'new_m_final_norm', 'new_v_attn_norm', 'new_v_ffn_norm', 'new_v_ple_norm', 'new_v_ssm_in_proj', 'new_v_ssm_conv_w', 'new_v_ssm_conv_b', 'new_v_ssm_dt_bias', 'new_v_ssm_a_log', 'new_v_ssm_d', 'new_v_ssm_norm', 'new_v_ssm_out_proj', 'new_v_kv_norm', 'new_v_w_kv', 'new_v_w_q', 'new_v_w_o', 'new_v_ffn_up', 'new_v_ffn_conv_w', 'new_v_ffn_conv_b', 'new_v_ffn_down', 'new_v_ple_gate', 'new_v_ple_proj', 'new_v_final_norm']
TWIN_LEAF_KINDS = {'loss': 'loss', 'grad_x': 'grad_x', 'grad_attn_norm': 'grad_w', 'grad_ffn_norm': 'grad_w', 'grad_ple_norm': 'grad_w', 'grad_ssm_in_proj': 'grad_w', 'grad_ssm_conv_w': 'grad_w', 'grad_ssm_conv_b': 'grad_w', 'grad_ssm_dt_bias': 'grad_w', 'grad_ssm_a_log': 'grad_w', 'grad_ssm_d': 'grad_w', 'grad_ssm_norm': 'grad_w', 'grad_ssm_out_proj': 'grad_w', 'grad_kv_norm': 'grad_w', 'grad_w_kv': 'grad_w', 'grad_w_q': 'grad_w', 'grad_w_o': 'grad_w', 'grad_ffn_up': 'grad_w', 'grad_ffn_conv_w': 'grad_w', 'grad_ffn_conv_b': 'grad_w', 'grad_ffn_down': 'grad_w', 'grad_ple_gate': 'grad_w', 'grad_ple_proj': 'grad_w', 'grad_final_norm': 'grad_w', 'delta_attn_norm': 'delta_w', 'delta_ffn_norm': 'delta_w', 'delta_ple_norm': 'delta_w', 'delta_ssm_in_proj': 'delta_w', 'delta_ssm_conv_w': 'delta_w', 'delta_ssm_conv_b': 'delta_w', 'delta_ssm_dt_bias': 'delta_w', 'delta_ssm_a_log': 'delta_w', 'delta_ssm_d': 'delta_w', 'delta_ssm_norm': 'delta_w', 'delta_ssm_out_proj': 'delta_w', 'delta_kv_norm': 'delta_w', 'delta_w_kv': 'delta_w', 'delta_w_q': 'delta_w', 'delta_w_o': 'delta_w', 'delta_ffn_up': 'delta_w', 'delta_ffn_conv_w': 'delta_w', 'delta_ffn_conv_b': 'delta_w', 'delta_ffn_down': 'delta_w', 'delta_ple_gate': 'delta_w', 'delta_ple_proj': 'delta_w', 'delta_final_norm': 'delta_w', 'new_m_attn_norm': 'new_m', 'new_m_ffn_norm': 'new_m', 'new_m_ple_norm': 'new_m', 'new_m_ssm_in_proj': 'new_m', 'new_m_ssm_conv_w': 'new_m', 'new_m_ssm_conv_b': 'new_m', 'new_m_ssm_dt_bias': 'new_m', 'new_m_ssm_a_log': 'new_m', 'new_m_ssm_d': 'new_m', 'new_m_ssm_norm': 'new_m', 'new_m_ssm_out_proj': 'new_m', 'new_m_kv_norm': 'new_m', 'new_m_w_kv': 'new_m', 'new_m_w_q': 'new_m', 'new_m_w_o': 'new_m', 'new_m_ffn_up': 'new_m', 'new_m_ffn_conv_w': 'new_m', 'new_m_ffn_conv_b': 'new_m', 'new_m_ffn_down': 'new_m', 'new_m_ple_gate': 'new_m', 'new_m_ple_proj': 'new_m', 'new_m_final_norm': 'new_m', 'new_v_attn_norm': 'new_v', 'new_v_ffn_norm': 'new_v', 'new_v_ple_norm': 'new_v', 'new_v_ssm_in_proj': 'new_v', 'new_v_ssm_conv_w': 'new_v', 'new_v_ssm_conv_b': 'new_v', 'new_v_ssm_dt_bias': 'new_v', 'new_v_ssm_a_log': 'new_v', 'new_v_ssm_d': 'new_v', 'new_v_ssm_norm': 'new_v', 'new_v_ssm_out_proj': 'new_v', 'new_v_kv_norm': 'new_v', 'new_v_w_kv': 'new_v', 'new_v_w_q': 'new_v', 'new_v_w_o': 'new_v', 'new_v_ffn_up': 'new_v', 'new_v_ffn_conv_w': 'new_v', 'new_v_ffn_conv_b': 'new_v', 'new_v_ffn_down': 'new_v', 'new_v_ple_gate': 'new_v', 'new_v_ple_proj': 'new_v', 'new_v_final_norm': 'new_v'}


def _forward(args):
    return _fwd_reference(*[args[k] for k in FWD_PARAMS])


def _output_shape():
    out = _jax.eval_shape(lambda: _forward(_fwd_setup_inputs(0)))
    return out.shape, out.dtype

N_MICROBATCH = 1
ADAM_LR = 0.001
ADAM_B1 = 0.9
ADAM_B2 = 0.999
ADAM_EPS = 1e-08
ADAM_WD = 0.01
ADAM_STEP = 10
PER_EXAMPLE_BATCH_AXIS = {'x': 0, 'p': 1, 'loss_target': 0}
SHARED_INPUTS = []
_WEIGHT_DTYPES = {'attn_norm': _jnp.float32, 'ffn_norm': _jnp.float32, 'ple_norm': _jnp.float32, 'ssm_in_proj': _jnp.float32, 'ssm_conv_w': _jnp.float32, 'ssm_conv_b': _jnp.float32, 'ssm_dt_bias': _jnp.float32, 'ssm_a_log': _jnp.float32, 'ssm_d': _jnp.float32, 'ssm_norm': _jnp.float32, 'ssm_out_proj': _jnp.float32, 'kv_norm': _jnp.float32, 'w_kv': _jnp.float32, 'w_q': _jnp.float32, 'w_o': _jnp.float32, 'ffn_up': _jnp.float32, 'ffn_conv_w': _jnp.float32, 'ffn_conv_b': _jnp.float32, 'ffn_down': _jnp.float32, 'ple_gate': _jnp.float32, 'ple_proj': _jnp.float32, 'final_norm': _jnp.float32}
MOMENT_SCALE = {'attn_norm': 2.084617e-01, 'ffn_norm': 1.225462e-01, 'ple_norm': 2.871779e-02, 'ssm_in_proj': 1.242057e-01, 'ssm_conv_w': 1.153584e-01, 'ssm_conv_b': 1.673495e-01, 'ssm_dt_bias': 3.088535e-01, 'ssm_a_log': 7.353540e-01, 'ssm_d': 7.242084e-01, 'ssm_norm': 1.339159e-01, 'ssm_out_proj': 1.891555e-01, 'kv_norm': 1.219753e-01, 'w_kv': 7.925653e-02, 'w_q': 3.249585e-02, 'w_o': 7.269573e-02, 'ffn_up': 5.079749e-02, 'ffn_conv_w': 5.073789e-02, 'ffn_conv_b': 5.213014e-02, 'ffn_down': 8.259536e-02, 'ple_gate': 2.870146e-02, 'ple_proj': 7.347113e-02, 'final_norm': 6.405063e+01}


def _to_microbatches(a, axis):
    t = _jnp.moveaxis(a, axis, 0)
    t = t.reshape((N_MICROBATCH, t.shape[0] // N_MICROBATCH) + t.shape[1:])
    return _jnp.moveaxis(t, 1, axis + 1)


def setup_inputs(seed: int = 0) -> dict:
    inp = _fwd_setup_inputs(seed)
    key = _jax.random.fold_in(_jax.random.key(seed), 7919)
    shape, _ = _output_shape()
    out = dict(inp)
    out["loss_target"] = _jax.random.normal(_jax.random.fold_in(key, 0), shape, _jnp.float32)
    for i, name in enumerate(TWIN_WEIGHTS):
        w = inp[name].astype(_jnp.float32)
        if MOMENT_SCALE is None:
            s = _jnp.sqrt(_jnp.mean(_jnp.square(w)) + 1e-30)
        else:
            s = MOMENT_SCALE[name]
        km, kv = _jax.random.split(_jax.random.fold_in(key, i + 1))
        out[name] = w
        out["m_" + name] = s * _jax.random.normal(km, w.shape, _jnp.float32)
        out["v_" + name] = (s * s) * _jax.random.uniform(kv, w.shape, _jnp.float32, 0.5, 1.5)
    if N_MICROBATCH > 1:
        for name, axis in PER_EXAMPLE_BATCH_AXIS.items():
            out[name] = _to_microbatches(out[name], axis)
    return {'x': out['x'], 'p': out['p'], 'attn_norm': out['attn_norm'], 'ffn_norm': out['ffn_norm'], 'ple_norm': out['ple_norm'], 'ssm_in_proj': out['ssm_in_proj'], 'ssm_conv_w': out['ssm_conv_w'], 'ssm_conv_b': out['ssm_conv_b'], 'ssm_dt_bias': out['ssm_dt_bias'], 'ssm_a_log': out['ssm_a_log'], 'ssm_d': out['ssm_d'], 'ssm_norm': out['ssm_norm'], 'ssm_out_proj': out['ssm_out_proj'], 'kv_norm': out['kv_norm'], 'w_kv': out['w_kv'], 'w_q': out['w_q'], 'w_o': out['w_o'], 'ffn_up': out['ffn_up'], 'ffn_conv_w': out['ffn_conv_w'], 'ffn_conv_b': out['ffn_conv_b'], 'ffn_down': out['ffn_down'], 'ple_gate': out['ple_gate'], 'ple_proj': out['ple_proj'], 'final_norm': out['final_norm'], 'loss_target': out['loss_target'], 'm_attn_norm': out['m_attn_norm'], 'm_ffn_norm': out['m_ffn_norm'], 'm_ple_norm': out['m_ple_norm'], 'm_ssm_in_proj': out['m_ssm_in_proj'], 'm_ssm_conv_w': out['m_ssm_conv_w'], 'm_ssm_conv_b': out['m_ssm_conv_b'], 'm_ssm_dt_bias': out['m_ssm_dt_bias'], 'm_ssm_a_log': out['m_ssm_a_log'], 'm_ssm_d': out['m_ssm_d'], 'm_ssm_norm': out['m_ssm_norm'], 'm_ssm_out_proj': out['m_ssm_out_proj'], 'm_kv_norm': out['m_kv_norm'], 'm_w_kv': out['m_w_kv'], 'm_w_q': out['m_w_q'], 'm_w_o': out['m_w_o'], 'm_ffn_up': out['m_ffn_up'], 'm_ffn_conv_w': out['m_ffn_conv_w'], 'm_ffn_conv_b': out['m_ffn_conv_b'], 'm_ffn_down': out['m_ffn_down'], 'm_ple_gate': out['m_ple_gate'], 'm_ple_proj': out['m_ple_proj'], 'm_final_norm': out['m_final_norm'], 'v_attn_norm': out['v_attn_norm'], 'v_ffn_norm': out['v_ffn_norm'], 'v_ple_norm': out['v_ple_norm'], 'v_ssm_in_proj': out['v_ssm_in_proj'], 'v_ssm_conv_w': out['v_ssm_conv_w'], 'v_ssm_conv_b': out['v_ssm_conv_b'], 'v_ssm_dt_bias': out['v_ssm_dt_bias'], 'v_ssm_a_log': out['v_ssm_a_log'], 'v_ssm_d': out['v_ssm_d'], 'v_ssm_norm': out['v_ssm_norm'], 'v_ssm_out_proj': out['v_ssm_out_proj'], 'v_kv_norm': out['v_kv_norm'], 'v_w_kv': out['v_w_kv'], 'v_w_q': out['v_w_q'], 'v_w_o': out['v_w_o'], 'v_ffn_up': out['v_ffn_up'], 'v_ffn_conv_w': out['v_ffn_conv_w'], 'v_ffn_conv_b': out['v_ffn_conv_b'], 'v_ffn_down': out['v_ffn_down'], 'v_ple_gate': out['v_ple_gate'], 'v_ple_proj': out['v_ple_proj'], 'v_final_norm': out['v_final_norm']}


def _loss(weights, diff, rest, loss_target):
    with _jax.named_scope("forward"):
        args = {**rest, TWIN_DIFF_INPUT: diff, **{k: w.astype(_WEIGHT_DTYPES[k]) for k, w in weights.items()}}
        y = _forward(args)
    with _jax.named_scope("loss_head"):
        err = _jnp.square(y.astype(_jnp.float32) - loss_target)
        return 0.5 * _jnp.sum(_jnp.mean(err, axis=-1)) if err.ndim else 0.5 * err


def _adamw(w, g, m, v):
    m = ADAM_B1 * m + (1.0 - ADAM_B1) * g
    v = ADAM_B2 * v + (1.0 - ADAM_B2) * _jnp.square(g)
    m_hat = m / (1.0 - ADAM_B1 ** ADAM_STEP)
    v_hat = v / (1.0 - ADAM_B2 ** ADAM_STEP)
    delta = -ADAM_LR * (m_hat / (_jnp.sqrt(v_hat) + ADAM_EPS) + ADAM_WD * w)
    return delta, m, v


def reference(x, p, attn_norm, ffn_norm, ple_norm, ssm_in_proj, ssm_conv_w, ssm_conv_b, ssm_dt_bias, ssm_a_log, ssm_d, ssm_norm, ssm_out_proj, kv_norm, w_kv, w_q, w_o, ffn_up, ffn_conv_w, ffn_conv_b, ffn_down, ple_gate, ple_proj, final_norm, loss_target, m_attn_norm, m_ffn_norm, m_ple_norm, m_ssm_in_proj, m_ssm_conv_w, m_ssm_conv_b, m_ssm_dt_bias, m_ssm_a_log, m_ssm_d, m_ssm_norm, m_ssm_out_proj, m_kv_norm, m_w_kv, m_w_q, m_w_o, m_ffn_up, m_ffn_conv_w, m_ffn_conv_b, m_ffn_down, m_ple_gate, m_ple_proj, m_final_norm, v_attn_norm, v_ffn_norm, v_ple_norm, v_ssm_in_proj, v_ssm_conv_w, v_ssm_conv_b, v_ssm_dt_bias, v_ssm_a_log, v_ssm_d, v_ssm_norm, v_ssm_out_proj, v_kv_norm, v_w_kv, v_w_q, v_w_o, v_ffn_up, v_ffn_conv_w, v_ffn_conv_b, v_ffn_down, v_ple_gate, v_ple_proj, v_final_norm):
    given = dict(x=x, p=p, attn_norm=attn_norm, ffn_norm=ffn_norm, ple_norm=ple_norm, ssm_in_proj=ssm_in_proj, ssm_conv_w=ssm_conv_w, ssm_conv_b=ssm_conv_b, ssm_dt_bias=ssm_dt_bias, ssm_a_log=ssm_a_log, ssm_d=ssm_d, ssm_norm=ssm_norm, ssm_out_proj=ssm_out_proj, kv_norm=kv_norm, w_kv=w_kv, w_q=w_q, w_o=w_o, ffn_up=ffn_up, ffn_conv_w=ffn_conv_w, ffn_conv_b=ffn_conv_b, ffn_down=ffn_down, ple_gate=ple_gate, ple_proj=ple_proj, final_norm=final_norm, loss_target=loss_target, m_attn_norm=m_attn_norm, m_ffn_norm=m_ffn_norm, m_ple_norm=m_ple_norm, m_ssm_in_proj=m_ssm_in_proj, m_ssm_conv_w=m_ssm_conv_w, m_ssm_conv_b=m_ssm_conv_b, m_ssm_dt_bias=m_ssm_dt_bias, m_ssm_a_log=m_ssm_a_log, m_ssm_d=m_ssm_d, m_ssm_norm=m_ssm_norm, m_ssm_out_proj=m_ssm_out_proj, m_kv_norm=m_kv_norm, m_w_kv=m_w_kv, m_w_q=m_w_q, m_w_o=m_w_o, m_ffn_up=m_ffn_up, m_ffn_conv_w=m_ffn_conv_w, m_ffn_conv_b=m_ffn_conv_b, m_ffn_down=m_ffn_down, m_ple_gate=m_ple_gate, m_ple_proj=m_ple_proj, m_final_norm=m_final_norm, v_attn_norm=v_attn_norm, v_ffn_norm=v_ffn_norm, v_ple_norm=v_ple_norm, v_ssm_in_proj=v_ssm_in_proj, v_ssm_conv_w=v_ssm_conv_w, v_ssm_conv_b=v_ssm_conv_b, v_ssm_dt_bias=v_ssm_dt_bias, v_ssm_a_log=v_ssm_a_log, v_ssm_d=v_ssm_d, v_ssm_norm=v_ssm_norm, v_ssm_out_proj=v_ssm_out_proj, v_kv_norm=v_kv_norm, v_w_kv=v_w_kv, v_w_q=v_w_q, v_w_o=v_w_o, v_ffn_up=v_ffn_up, v_ffn_conv_w=v_ffn_conv_w, v_ffn_conv_b=v_ffn_conv_b, v_ffn_down=v_ffn_down, v_ple_gate=v_ple_gate, v_ple_proj=v_ple_proj, v_final_norm=v_final_norm)
    weights = {n: given[n] for n in TWIN_WEIGHTS}
    shared = {n: given[n] for n in SHARED_INPUTS}
    per_example = {n: given[n] for n in ['x', 'p']}
    grad_fn = _jax.value_and_grad(_loss, argnums=(0, 1))

    def one_microbatch(ex, loss_target):
        ex = dict(ex)
        diff = ex.pop(TWIN_DIFF_INPUT)
        return grad_fn(weights, diff, {**shared, **ex}, loss_target)

    if N_MICROBATCH == 1:
        loss, (grad_w, grad_x) = one_microbatch(per_example, given["loss_target"])
    else:
        def body(carry, xs):
            loss_sum, grad_sum = carry
            l_k, (gw_k, gx_k) = one_microbatch(xs[0], xs[1])
            with _jax.named_scope("update"):
                return (loss_sum + l_k, _jax.tree.map(_jnp.add, grad_sum, gw_k)), gx_k

        init = (_jnp.zeros((), _jnp.float32), _jax.tree.map(_jnp.zeros_like, weights))
        (loss, grad_w), grad_x = _jax.lax.scan(body, init, (per_example, given["loss_target"]))
    with _jax.named_scope("update"):
        delta_w, new_m, new_v = {}, {}, {}
        for n in TWIN_WEIGHTS:
            delta_w[n], new_m[n], new_v[n] = _adamw(weights[n], grad_w[n], given["m_" + n], given["v_" + n])
    return (loss, grad_x, *[grad_w[n] for n in TWIN_WEIGHTS], *[delta_w[n] for n in TWIN_WEIGHTS],
            *[new_m[n] for n in TWIN_WEIGHTS], *[new_v[n] for n in TWIN_WEIGHTS])
```

```python
import functools
import math

import jax
import jax.numpy as jnp
from jax import lax
from jax.experimental import pallas as pl
from jax.experimental.pallas import tpu as pltpu

F32 = jnp.float32
BF16 = jnp.bfloat16

N_DEV = 8
D_MODEL = 1024
SEQ = 2048
DEPTH = 4
N_A = 2
N_B = 2
SSM_D_INNER = 2048
SSM_HEAD_DIM = 64
SSM_HEADS = 32
SSM_GROUPS = 4
SSM_STATE = 128
SSM_CONV = 4
SSM_CHUNK = 128
SSM_CONV_DIM = 3072
SSM_ZX = 5120
SSM_IN_DIM = 5152
SB_HEADS = 16
SB_HEAD_DIM = 64
SB_WIDTH = 1024
D_FF = 2816
FFN_CONV = 3
PLE_DIM = 256
NORM_EPS = 1e-6
SSM_NORM_EPS = 1e-5

ADAM_LR = 0.001
ADAM_B1 = 0.9
ADAM_B2 = 0.999
ADAM_EPS = 1e-08
ADAM_WD = 0.01
ADAM_STEP = 10

LANE = 128
SUBLANE = 8
VMEM_LIMIT = 48 * 1024 * 1024

NN = (((1,), (0,)), ((), ()))
NT = (((1,), (1,)), ((), ()))
TN = (((0,), (0,)), ((), ()))


def _params(sem):
    return pltpu.CompilerParams(dimension_semantics=sem, vmem_limit_bytes=VMEM_LIMIT)


def _pick(n, prefs):
    for p in prefs:
        if n % p == 0:
            return p
    return n


def matmul(a, b, mode, out_dtype, name, add=None):
    if mode == "nn":
        (m, k), (k2, n) = a.shape, b.shape
    elif mode == "nt":
        (m, k), (n, k2) = a.shape, b.shape
    else:
        (k, m), (k2, n) = a.shape, b.shape
    assert k == k2, (a.shape, b.shape, mode)
    tm = _pick(m, (512, 256, 128))
    tn = _pick(n, (1024, 512, 256, 128))
    tk = _pick(k, (1024, 512, 256, 128))
    nk = k // tk
    dims = {"nn": NN, "nt": NT, "tn": TN}[mode]

    def body(*refs):
        if add is None:
            a_ref, b_ref, o_ref, acc_ref = refs
            add_ref = None
        else:
            a_ref, b_ref, add_ref, o_ref, acc_ref = refs
        kk = pl.program_id(2)

        @pl.when(kk == 0)
        def _():
            acc_ref[...] = jnp.zeros_like(acc_ref)

        acc_ref[...] += lax.dot_general(
            a_ref[...].astype(BF16), b_ref[...].astype(BF16), dims, preferred_element_type=F32
        )

        @pl.when(kk == nk - 1)
        def _():
            r = acc_ref[...]
            if add_ref is not None:
                r = r + add_ref[...].astype(F32)
            o_ref[...] = r.astype(o_ref.dtype)

    if mode == "nn":
        a_spec = pl.BlockSpec((tm, tk), lambda i, j, kk: (i, kk))
        b_spec = pl.BlockSpec((tk, tn), lambda i, j, kk: (kk, j))
    elif mode == "nt":
        a_spec = pl.BlockSpec((tm, tk), lambda i, j, kk: (i, kk))
        b_spec = pl.BlockSpec((tn, tk), lambda i, j, kk: (j, kk))
    else:
        a_spec = pl.BlockSpec((tk, tm), lambda i, j, kk: (kk, i))
        b_spec = pl.BlockSpec((tk, tn), lambda i, j, kk: (kk, j))
    o_spec = pl.BlockSpec((tm, tn), lambda i, j, kk: (i, j))
    in_specs = [a_spec, b_spec] + ([o_spec] if add is not None else [])
    args = (a, b) + ((add,) if add is not None else ())
    return pl.pallas_call(
        body,
        name=name,
        grid=(m // tm, n // tn, nk),
        in_specs=in_specs,
        out_specs=o_spec,
        out_shape=jax.ShapeDtypeStruct((m, n), out_dtype),
        scratch_shapes=[pltpu.VMEM((tm, tn), F32)],
        compiler_params=_params(("parallel", "parallel", "arbitrary")),
    )(*args)


ROW_TILE = 512


def rmsnorm_fwd(x, gain, name):
    t, d = x.shape

    def body(x_ref, g_ref, o_ref):
        xv = x_ref[...]
        r = lax.rsqrt(jnp.mean(xv * xv, axis=-1, keepdims=True) + NORM_EPS)
        o_ref[...] = (xv * r * g_ref[...]).astype(o_ref.dtype)

    return pl.pallas_call(
        body,
        name=name,
        grid=(t // ROW_TILE,),
        in_specs=[pl.BlockSpec((ROW_TILE, d), lambda i: (i, 0)), pl.BlockSpec((1, d), lambda i: (0, 0))],
        out_specs=pl.BlockSpec((ROW_TILE, d), lambda i: (i, 0)),
        out_shape=jax.ShapeDtypeStruct((t, d), BF16),
        compiler_params=_params(("parallel",)),
    )(x, gain)


def rmsnorm_bwd(dy, x, gain, dres, name):
    t, d = x.shape

    def body(dy_ref, x_ref, g_ref, dres_ref, dx_ref, dg_ref):
        i = pl.program_id(0)
        xv = x_ref[...]
        r = lax.rsqrt(jnp.mean(xv * xv, axis=-1, keepdims=True) + NORM_EPS)
        xh = xv * r
        dyv = dy_ref[...].astype(F32)
        dxh = dyv * g_ref[...]
        dx = r * (dxh - xh * jnp.mean(dxh * xh, axis=-1, keepdims=True))
        dx_ref[...] = dres_ref[...] + dx

        @pl.when(i == 0)
        def _():
            dg_ref[...] = jnp.zeros_like(dg_ref)

        dg_ref[...] += jnp.sum(dyv * xh, axis=0, keepdims=True)

    row = pl.BlockSpec((ROW_TILE, d), lambda i: (i, 0))
    vec = pl.BlockSpec((1, d), lambda i: (0, 0))
    return pl.pallas_call(
        body,
        name=name,
        grid=(t // ROW_TILE,),
        in_specs=[row, row, vec, row],
        out_specs=[row, vec],
        out_shape=[jax.ShapeDtypeStruct((t, d), F32), jax.ShapeDtypeStruct((1, d), F32)],
        compiler_params=_params(("arbitrary",)),
    )(dy, x, gain, dres)


CONV_ROWS = 512
HALO = SUBLANE


def _conv_apply(ext, w_ref, bias, k, lo, n):
    acc = bias + w_ref[k - 1:k, :] * ext[lo:lo + n]
    for j in range(1, k):
        acc = acc + w_ref[k - 1 - j:k - j, :] * pltpu.roll(ext, j, 0)[lo:lo + n]
    return acc


def _conv_apply_t(ext, w_ref, k, n):
    rows = ext.shape[0]
    acc = w_ref[k - 1:k, :] * ext[:n]
    for j in range(1, k):
        acc = acc + w_ref[k - 1 - j:k - j, :] * pltpu.roll(ext, rows - j, 0)[:n]
    return acc


def _conv_dw(dpre_main, ext_u, k, n, dw_ref):
    for kk in range(k):
        j = k - 1 - kk
        sh = ext_u[HALO:HALO + n] if j == 0 else pltpu.roll(ext_u, j, 0)[HALO:HALO + n]
        dw_ref[kk:kk + 1, :] += jnp.sum(dpre_main * sh, axis=0, keepdims=True)


def _silu_grad(pre):
    sg = jax.nn.sigmoid(pre)
    return sg * (1.0 + pre * (1.0 - sg))


def _conv_blockspecs(seq, r, tc):
    nh = r // HALO
    last = seq // HALO - 1
    main = pl.BlockSpec((None, r, tc), lambda c, b, s: (b, s, c))
    before = pl.BlockSpec((None, HALO, tc), lambda c, b, s: (b, jnp.maximum(s * nh - 1, 0), c))
    after = pl.BlockSpec((None, HALO, tc), lambda c, b, s: (b, jnp.minimum((s + 1) * nh, last), c))
    return main, before, after


def ssm_conv_fwd(u, w, bias, name):
    bsz, seq, ch = u.shape
    r, tc, k = min(CONV_ROWS, seq), 512, SSM_CONV
    main, before, _ = _conv_blockspecs(seq, r, tc)
    wspec = pl.BlockSpec((SUBLANE, tc), lambda c, b, s: (0, c))
    bspec = pl.BlockSpec((1, tc), lambda c, b, s: (0, c))

    def body(u_ref, hb_ref, w_ref, b_ref, o_ref):
        s = pl.program_id(2)
        hb = jnp.where(s == 0, 0.0, hb_ref[...])
        ext = jnp.concatenate([hb, u_ref[...]], axis=0)
        pre = _conv_apply(ext, w_ref, b_ref[...], k, HALO, r)
        o_ref[...] = pre * jax.nn.sigmoid(pre)

    return pl.pallas_call(
        body, name=name, grid=(ch // tc, bsz, seq // r),
        in_specs=[main, before, wspec, bspec], out_specs=main,
        out_shape=jax.ShapeDtypeStruct(u.shape, F32),
        compiler_params=_params(("parallel", "parallel", "parallel")),
    )(u, u, w, bias)


def ssm_conv_bwd(u, dxc, w, bias, name):
    bsz, seq, ch = u.shape
    r, tc, k = min(CONV_ROWS, seq), 512, SSM_CONV
    ns = seq // r
    main, before, after = _conv_blockspecs(seq, r, tc)
    wspec = pl.BlockSpec((SUBLANE, tc), lambda c, b, s: (0, c))
    bspec = pl.BlockSpec((1, tc), lambda c, b, s: (0, c))

    def body(u_ref, hb_ref, ha_ref, d_ref, da_ref, w_ref, b_ref, du_ref, dw_ref, db_ref):
        b, s = pl.program_id(1), pl.program_id(2)
        hb = jnp.where(s == 0, 0.0, hb_ref[...])
        ext_u = jnp.concatenate([hb, u_ref[...], ha_ref[...]], axis=0)
        pre = _conv_apply(ext_u, w_ref, b_ref[...], k, HALO, r + HALO)
        dxe = jnp.concatenate([d_ref[...], jnp.where(s == ns - 1, 0.0, da_ref[...])], axis=0)
        dpre = dxe * _silu_grad(pre)
        du_ref[...] = _conv_apply_t(dpre, w_ref, k, r).astype(du_ref.dtype)

        @pl.when((b == 0) & (s == 0))
        def _():
            dw_ref[...] = jnp.zeros_like(dw_ref)
            db_ref[...] = jnp.zeros_like(db_ref)

        dpm = dpre[:r]
        _conv_dw(dpm, ext_u, k, r, dw_ref)
        db_ref[...] += jnp.sum(dpm, axis=0, keepdims=True)

    return pl.pallas_call(
        body, name=name, grid=(ch // tc, bsz, ns),
        in_specs=[main, before, after, main, after, wspec, bspec],
        out_specs=[main, wspec, bspec],
        out_shape=[jax.ShapeDtypeStruct(u.shape, BF16), jax.ShapeDtypeStruct((SUBLANE, ch), F32),
                   jax.ShapeDtypeStruct((1, ch), F32)],
        compiler_params=_params(("arbitrary", "arbitrary", "arbitrary")),
    )(u, u, u, dxc, dxc, w, bias)


def ffn_conv_fwd(ug, uv, wg, wv, bg, bv, name):
    bsz, seq, ch = ug.shape
    r, tc, k = min(CONV_ROWS, seq), 256, FFN_CONV
    main, before, _ = _conv_blockspecs(seq, r, tc)
    wspec = pl.BlockSpec((SUBLANE, tc), lambda c, b, s: (0, c))
    bspec = pl.BlockSpec((1, tc), lambda c, b, s: (0, c))

    def body(ug_ref, hg_ref, uv_ref, hv_ref, wg_ref, wv_ref, bg_ref, bv_ref, o_ref):
        s = pl.program_id(2)
        eg = jnp.concatenate([jnp.where(s == 0, 0.0, hg_ref[...]), ug_ref[...]], axis=0)
        ev = jnp.concatenate([jnp.where(s == 0, 0.0, hv_ref[...]), uv_ref[...]], axis=0)
        pg = _conv_apply(eg, wg_ref, bg_ref[...], k, HALO, r)
        pv = _conv_apply(ev, wv_ref, bv_ref[...], k, HALO, r)
        o_ref[...] = (pg * jax.nn.sigmoid(pg) * pv).astype(o_ref.dtype)

    return pl.pallas_call(
        body, name=name, grid=(ch // tc, bsz, seq // r),
        in_specs=[main, before, main, before, wspec, wspec, bspec, bspec], out_specs=main,
        out_shape=jax.ShapeDtypeStruct(ug.shape, BF16),
        compiler_params=_params(("parallel", "parallel", "parallel")),
    )(ug, ug, uv, uv, wg, wv, bg, bv)


def ffn_conv_bwd(ug, uv, dact, wg, wv, bg, bv, name):
    bsz, seq, ch = ug.shape
    r, tc, k = min(CONV_ROWS, seq), 256, FFN_CONV
    ns = seq // r
    main, before, after = _conv_blockspecs(seq, r, tc)
    wspec = pl.BlockSpec((SUBLANE, tc), lambda c, b, s: (0, c))
    bspec = pl.BlockSpec((1, tc), lambda c, b, s: (0, c))

    def body(ug_ref, gb_ref, ga_ref, uv_ref, vb_ref, va_ref, d_ref, da_ref, wg_ref, wv_ref, bg_ref, bv_ref,
             dug_ref, duv_ref, dwg_ref, dwv_ref, dbg_ref, dbv_ref):
        b, s = pl.program_id(1), pl.program_id(2)
        eg = jnp.concatenate([jnp.where(s == 0, 0.0, gb_ref[...]), ug_ref[...], ga_ref[...]], axis=0)
        ev = jnp.concatenate([jnp.where(s == 0, 0.0, vb_ref[...]), uv_ref[...], va_ref[...]], axis=0)
        pg = _conv_apply(eg, wg_ref, bg_ref[...], k, HALO, r + HALO)
        pv = _conv_apply(ev, wv_ref, bv_ref[...], k, HALO, r + HALO)
        de = jnp.concatenate([d_ref[...], jnp.where(s == ns - 1, 0.0, da_ref[...])], axis=0)
        sg = jax.nn.sigmoid(pg)
        dpg = de * pv * (sg * (1.0 + pg * (1.0 - sg)))
        dpv = de * (pg * sg)
        dug_ref[...] = _conv_apply_t(dpg, wg_ref, k, r).astype(dug_ref.dtype)
        duv_ref[...] = _conv_apply_t(dpv, wv_ref, k, r).astype(duv_ref.dtype)

        @pl.when((b == 0) & (s == 0))
        def _():
            dwg_ref[...] = jnp.zeros_like(dwg_ref)
            dwv_ref[...] = jnp.zeros_like(dwv_ref)
            dbg_ref[...] = jnp.zeros_like(dbg_ref)
            dbv_ref[...] = jnp.zeros_like(dbv_ref)

        _conv_dw(dpg[:r], eg, k, r, dwg_ref)
        _conv_dw(dpv[:r], ev, k, r, dwv_ref)
        dbg_ref[...] += jnp.sum(dpg[:r], axis=0, keepdims=True)
        dbv_ref[...] += jnp.sum(dpv[:r], axis=0, keepdims=True)

    wshape = jax.ShapeDtypeStruct((SUBLANE, ch), F32)
    bshape = jax.ShapeDtypeStruct((1, ch), F32)
    return pl.pallas_call(
        body, name=name, grid=(ch // tc, bsz, ns),
        in_specs=[main, before, after, main, before, after, main, after, wspec, wspec, bspec, bspec],
        out_specs=[main, main, wspec, wspec, bspec, bspec],
        out_shape=[jax.ShapeDtypeStruct(ug.shape, BF16), jax.ShapeDtypeStruct(ug.shape, BF16), wshape, wshape, bshape, bshape],
        compiler_params=_params(("arbitrary", "arbitrary", "arbitrary")),
    )(ug, ug, ug, uv, uv, uv, dact, dact, wg, wv, bg, bv)


GN_ROWS = 256
GN_GROUP = SSM_D_INNER // SSM_GROUPS


def gated_norm_fwd(y, z, w, name):
    t, d = y.shape

    def body(y_ref, z_ref, w_ref, o_ref):
        for g in range(d // GN_GROUP):
            sl = slice(g * GN_GROUP, (g + 1) * GN_GROUP)
            zv = z_ref[:, sl]
            gv = y_ref[:, sl] * (zv * jax.nn.sigmoid(zv))
            r = lax.rsqrt(jnp.mean(gv * gv, axis=-1, keepdims=True) + SSM_NORM_EPS)
            o_ref[:, sl] = (gv * r * w_ref[:, sl]).astype(o_ref.dtype)

    row = pl.BlockSpec((GN_ROWS, d), lambda i: (i, 0))
    vec = pl.BlockSpec((1, d), lambda i: (0, 0))
    return pl.pallas_call(
        body, name=name, grid=(t // GN_ROWS,), in_specs=[row, row, vec], out_specs=row,
        out_shape=jax.ShapeDtypeStruct((t, d), BF16), compiler_params=_params(("parallel",)),
    )(y, z, w)


def gated_norm_bwd(dgn, y, z, w, name):
    t, d = y.shape

    def body(d_ref, y_ref, z_ref, w_ref, dy_ref, dz_ref, dw_ref):
        i = pl.program_id(0)

        @pl.when(i == 0)
        def _():
            dw_ref[...] = jnp.zeros_like(dw_ref)

        for g in range(d // GN_GROUP):
            sl = slice(g * GN_GROUP, (g + 1) * GN_GROUP)
            zv, yv, dv = z_ref[:, sl], y_ref[:, sl], d_ref[:, sl]
            sg = jax.nn.sigmoid(zv)
            sz = zv * sg
            gv = yv * sz
            r = lax.rsqrt(jnp.mean(gv * gv, axis=-1, keepdims=True) + SSM_NORM_EPS)
            gh = gv * r
            dgh = dv * w_ref[:, sl]
            dw_ref[:, sl] += jnp.sum(dv * gh, axis=0, keepdims=True)
            dg = r * (dgh - gh * jnp.mean(dgh * gh, axis=-1, keepdims=True))
            dy_ref[:, sl] = dg * sz
            dz_ref[:, sl] = (dg * yv * (sg * (1.0 + zv * (1.0 - sg)))).astype(dz_ref.dtype)

    row = pl.BlockSpec((GN_ROWS, d), lambda i: (i, 0))
    vec = pl.BlockSpec((1, d), lambda i: (0, 0))
    return pl.pallas_call(
        body, name=name, grid=(t // GN_ROWS,), in_specs=[row, row, row, vec], out_specs=[row, row, vec],
        out_shape=[jax.ShapeDtypeStruct((t, d), F32), jax.ShapeDtypeStruct((t, d), BF16), jax.ShapeDtypeStruct((1, d), F32)],
        compiler_params=_params(("arbitrary",)),
    )(dgn, y, z, w)


def ple_fwd(h, gpre, pp, name):
    t, d = h.shape

    def body(h_ref, g_ref, p_ref, o_ref):
        o_ref[...] = h_ref[...] + jax.nn.sigmoid(g_ref[...]) * p_ref[...]

    row = pl.BlockSpec((ROW_TILE, d), lambda i: (i, 0))
    return pl.pallas_call(
        body, name=name, grid=(t // ROW_TILE,), in_specs=[row, row, row], out_specs=row,
        out_shape=jax.ShapeDtypeStruct((t, d), F32), compiler_params=_params(("parallel",)),
    )(h, gpre, pp)


def ple_bwd(dh, gpre, pp, name):
    t, d = dh.shape

    def body(d_ref, g_ref, p_ref, dg_ref, dp_ref):
        sg = jax.nn.sigmoid(g_ref[...])
        dv = d_ref[...]
        dg_ref[...] = (dv * p_ref[...] * sg * (1.0 - sg)).astype(dg_ref.dtype)
        dp_ref[...] = (dv * sg).astype(dp_ref.dtype)

    row = pl.BlockSpec((ROW_TILE, d), lambda i: (i, 0))
    return pl.pallas_call(
        body, name=name, grid=(t // ROW_TILE,), in_specs=[row, row, row], out_specs=[row, row],
        out_shape=[jax.ShapeDtypeStruct((t, d), BF16), jax.ShapeDtypeStruct((t, d), BF16)],
        compiler_params=_params(("parallel",)),
    )(dh, gpre, pp)


def loss_head(h, gain, target, name):
    t, d = h.shape

    def body(x_ref, g_ref, t_ref, l_ref, dx_ref, dg_ref):
        i = pl.program_id(0)
        xv = x_ref[...]
        r = lax.rsqrt(jnp.mean(xv * xv, axis=-1, keepdims=True) + NORM_EPS)
        xh = xv * r
        err = xh * g_ref[...] - t_ref[...]
        part = 0.5 * jnp.sum(jnp.mean(err * err, axis=-1, keepdims=True), axis=0, keepdims=True)
        dyv = err * (1.0 / d)
        dxh = dyv * g_ref[...]
        dx_ref[...] = r * (dxh - xh * jnp.mean(dxh * xh, axis=-1, keepdims=True))

        @pl.when(i == 0)
        def _():
            dg_ref[...] = jnp.zeros_like(dg_ref)
            l_ref[...] = jnp.zeros_like(l_ref)

        dg_ref[...] += jnp.sum(dyv * xh, axis=0, keepdims=True)
        l_ref[...] += jnp.broadcast_to(part, l_ref.shape)

    row = pl.BlockSpec((ROW_TILE, d), lambda i: (i, 0))
    vec = pl.BlockSpec((1, d), lambda i: (0, 0))
    lspec = pl.BlockSpec((1, LANE), lambda i: (0, 0))
    return pl.pallas_call(
        body, name=name, grid=(t // ROW_TILE,), in_specs=[row, vec, row], out_specs=[lspec, row, vec],
        out_shape=[jax.ShapeDtypeStruct((1, LANE), F32), jax.ShapeDtypeStruct((t, d), F32), jax.ShapeDtypeStruct((1, d), F32)],
        compiler_params=_params(("arbitrary",)),
    )(h, gain, target)


CHUNK = SSM_CHUNK
HPG = SSM_HEADS // SSM_GROUPS
GW = HPG * SSM_HEAD_DIM
NEG = -1e30


def _ssd_dt_terms(dtr_ref, bias_ref, alog_ref, g):
    shift = (LANE - HPG * g) % LANE
    li = lax.broadcasted_iota(jnp.int32, (CHUNK, CHUNK), 0)
    si = lax.broadcasted_iota(jnp.int32, (CHUNK, CHUNK), 1)
    x0 = pltpu.roll(dtr_ref[...] + bias_ref[...], shift, 1)
    dt = jnp.maximum(x0, 0.0) + jnp.log1p(jnp.exp(-jnp.abs(x0)))
    a = pltpu.roll(jnp.broadcast_to(-jnp.exp(alog_ref[...]), (CHUNK, LANE)), shift, 1)
    adt = dt * a
    acs = lax.dot_general((li >= si).astype(F32), adt, NN, precision=lax.Precision.HIGHEST, preferred_element_type=F32)
    last = jnp.broadcast_to(acs[CHUNK - 1:CHUNK, :], (CHUNK, LANE))
    return dict(li=li, si=si, x0=x0, dt=dt, a=a, acs=acs, acs_t=acs.T, e_acs=jnp.exp(acs),
                w_all=jnp.exp(last - acs), e_last=jnp.exp(last))


def _ssd_specs(nc, rev):
    def ci(c):
        return nc - 1 - c if rev else c
    nb = SSM_D_INNER // LANE
    xs = pl.BlockSpec((None, CHUNK, GW), lambda g, b, c: (b, ci(c), g))
    bm = pl.BlockSpec((None, CHUNK, LANE), lambda g, b, c: (b, ci(c), nb + g))
    cm = pl.BlockSpec((None, CHUNK, LANE), lambda g, b, c: (b, ci(c), nb + SSM_GROUPS + g))
    dtr = pl.BlockSpec((None, CHUNK, LANE), lambda g, b, c: (b, ci(c), 0))
    vec = pl.BlockSpec((1, LANE), lambda g, b, c: (0, 0))
    dexp = pl.BlockSpec((1, GW), lambda g, b, c: (0, g))
    st = pl.BlockSpec((None, None, CHUNK, GW), lambda g, b, c: (b, ci(c), 0, g))
    return xs, bm, cm, dtr, vec, dexp, st


def ssd_fwd(xc, dtr, dt_bias, a_log, d_exp, name):
    bsz, seq, _ = xc.shape
    nc = seq // CHUNK
    xs_s, bm_s, cm_s, dtr_s, vec_s, dexp_s, st_s = _ssd_specs(nc, False)

    def body(xs_ref, b_ref, c_ref, dtr_ref, bias_ref, alog_ref, dexp_ref, y_ref, sp_ref, state_ref):
        g, c = pl.program_id(0), pl.program_id(2)

        @pl.when(c == 0)
        def _():
            state_ref[...] = jnp.zeros_like(state_ref)

        t = _ssd_dt_terms(dtr_ref, bias_ref, alog_ref, g)
        tril = t["li"] >= t["si"]
        xs = xs_ref[...]
        b16 = b_ref[...].astype(BF16)
        c16 = c_ref[...].astype(BF16)
        bt16 = b_ref[...].T.astype(BF16)
        cb = lax.dot_general(c16, b16, NT, preferred_element_type=F32)
        sp = state_ref[...]
        sp_ref[...] = sp
        cs = jnp.dot(c16, sp.astype(BF16), preferred_element_type=F32)
        y_parts, s_parts = [], []
        for r in range(HPG):
            sl = slice(r * SSM_HEAD_DIM, (r + 1) * SSM_HEAD_DIM)
            diff = t["acs"][:, r:r + 1] - t["acs_t"][r:r + 1, :]
            m16 = (cb * jnp.exp(jnp.where(tril, diff, NEG))).astype(BF16)
            xh = xs[:, sl]
            xd = xh * t["dt"][:, r:r + 1]
            yh = jnp.dot(m16, xd.astype(BF16), preferred_element_type=F32)
            yh = yh + t["e_acs"][:, r:r + 1] * cs[:, sl] + xh * dexp_ref[:, sl]
            y_parts.append(yh)
            xw16 = (xd * t["w_all"][:, r:r + 1]).astype(BF16)
            s_parts.append(sp[:, sl] * t["e_last"][:, r:r + 1] + jnp.dot(bt16, xw16, preferred_element_type=F32))
        y_ref[...] = jnp.concatenate(y_parts, axis=1)
        state_ref[...] = jnp.concatenate(s_parts, axis=1)

    return pl.pallas_call(
        body, name=name, grid=(SSM_GROUPS, bsz, nc),
        in_specs=[xs_s, bm_s, cm_s, dtr_s, vec_s, vec_s, dexp_s],
        out_specs=[xs_s, st_s],
        out_shape=[jax.ShapeDtypeStruct((bsz, seq, SSM_D_INNER), F32),
                   jax.ShapeDtypeStruct((bsz, nc, SSM_STATE, SSM_D_INNER), F32)],
        scratch_shapes=[pltpu.VMEM((SSM_STATE, GW), F32)],
        compiler_params=_params(("arbitrary", "arbitrary", "arbitrary")),
    )(xc, xc, xc, dtr, dt_bias, a_log, d_exp)


def ssd_bwd(xc, dtr, dt_bias, a_log, d_exp, dy, sprev, name):
    bsz, seq, _ = xc.shape
    nc = seq // CHUNK
    xs_s, bm_s, cm_s, dtr_s, vec_s, dexp_s, st_s = _ssd_specs(nc, True)
    grp = pl.BlockSpec((None, CHUNK, LANE), lambda g, b, c: (b, nc - 1 - c, g))
    acc = pl.BlockSpec((None, 1, LANE), lambda g, b, c: (g, 0, 0))

    def body(xs_ref, b_ref, c_ref, dtr_ref, bias_ref, alog_ref, dexp_ref, dy_ref, sp_ref,
             dxs_ref, db_ref, dc_ref, ddtr_ref, da_ref, dbias_ref, ddexp_ref, dstate_ref):
        g, b, c = pl.program_id(0), pl.program_id(1), pl.program_id(2)

        @pl.when(c == 0)
        def _():
            dstate_ref[...] = jnp.zeros_like(dstate_ref)

        @pl.when((b == 0) & (c == 0))
        def _():
            da_ref[...] = jnp.zeros_like(da_ref)
            dbias_ref[...] = jnp.zeros_like(dbias_ref)
            ddexp_ref[...] = jnp.zeros_like(ddexp_ref)

        t = _ssd_dt_terms(dtr_ref, bias_ref, alog_ref, g)
        li, si = t["li"], t["si"]
        tril, triu = li >= si, li <= si
        row_is_last = li[:, :1] == CHUNK - 1
        xs, dy = xs_ref[...], dy_ref[...]
        b16 = b_ref[...].astype(BF16)
        c16 = c_ref[...].astype(BF16)
        ct16 = c_ref[...].T.astype(BF16)
        cb = lax.dot_general(c16, b16, NT, preferred_element_type=F32)
        cbt = lax.dot_general(b16, c16, NT, preferred_element_type=F32)
        sp, ds = sp_ref[...], dstate_ref[...]
        sp16, ds16 = sp.astype(BF16), ds.astype(BF16)
        cs = jnp.dot(c16, sp16, preferred_element_type=F32)
        bds = jnp.dot(b16, ds16, preferred_element_type=F32)
        spds = jnp.sum(sp * ds, axis=0, keepdims=True)
        zero = jnp.zeros((CHUNK, CHUNK), F32)
        dsum_g, dsum_gt, dacs, ddt_dir = zero, zero, zero, zero
        dxs_parts, edy_parts, xw_parts, dsp_parts = [], [], [], []
        for r in range(HPG):
            sl = slice(r * SSM_HEAD_DIM, (r + 1) * SSM_HEAD_DIM)
            diff = t["acs"][:, r:r + 1] - t["acs_t"][r:r + 1, :]
            lam = jnp.exp(jnp.where(tril, diff, NEG))
            lam_t = jnp.exp(jnp.where(triu, -diff, NEG))
            m, m_t = cb * lam, cbt * lam_t
            xh, dyh = xs[:, sl], dy[:, sl]
            dtc, wc, ec, elc = t["dt"][:, r:r + 1], t["w_all"][:, r:r + 1], t["e_acs"][:, r:r + 1], t["e_last"][:, r:r + 1]
            xd = xh * dtc
            xd16, dyh16 = xd.astype(BF16), dyh.astype(BF16)
            dxd = jnp.dot(m_t.astype(BF16), dyh16, preferred_element_type=F32) + wc * bds[:, sl]
            dm = lax.dot_general(dyh16, xd16, NT, preferred_element_type=F32)
            dm_t = lax.dot_general(xd16, dyh16, NT, preferred_element_type=F32)
            dsum_g = dsum_g + dm * lam
            dsum_gt = dsum_gt + dm_t * lam_t
            dacs_h = jnp.sum(dm * m, axis=1, keepdims=True) - jnp.sum(dm_t * m_t, axis=1, keepdims=True)
            dacs_h = dacs_h + jnp.sum(dyh * (ec * cs[:, sl]), axis=1, keepdims=True)
            wdw = wc * jnp.sum(xd * bds[:, sl], axis=1, keepdims=True)
            last_h = jnp.sum(wdw, axis=0, keepdims=True) + elc[0:1, :] * jnp.sum(spds[:, sl], axis=1, keepdims=True)
            dacs_h = dacs_h - wdw + jnp.where(row_is_last, last_h, 0.0)
            onehot = (si[:1, :] == r).astype(F32)
            dacs = dacs + dacs_h * onehot
            ddt_dir = ddt_dir + jnp.sum(dxd * xh, axis=1, keepdims=True) * onehot
            dxs_parts.append(dxd * dtc + dyh * dexp_ref[:, sl])
            edy_parts.append(ec * dyh)
            xw_parts.append(wc * xd)
            dsp_parts.append(elc * ds[:, sl])
        dxs_ref[...] = jnp.concatenate(dxs_parts, axis=1)
        edy16 = jnp.concatenate(edy_parts, axis=1).astype(BF16)
        xw16 = jnp.concatenate(xw_parts, axis=1).astype(BF16)
        dc_ref[...] = (jnp.dot(dsum_g.astype(BF16), b16, preferred_element_type=F32)
                       + lax.dot_general(edy16, sp16, NT, preferred_element_type=F32))
        db_ref[...] = (jnp.dot(dsum_gt.astype(BF16), c16, preferred_element_type=F32)
                       + lax.dot_general(xw16, ds16, NT, preferred_element_type=F32))
        dstate_ref[...] = jnp.concatenate(dsp_parts, axis=1) + jnp.dot(ct16, edy16, preferred_element_type=F32)
        dadt = lax.dot_general(triu.astype(F32), dacs, NN, precision=lax.Precision.HIGHEST, preferred_element_type=F32)
        head_lane = si < HPG
        ddtr = jnp.where(head_lane, (dadt * t["a"] + ddt_dir) * jax.nn.sigmoid(t["x0"]), 0.0)
        ddtr_ref[...] = ddtr
        da_ref[...] += jnp.sum(jnp.where(head_lane, dadt * t["dt"] * t["a"], 0.0), axis=0, keepdims=True)
        dbias_ref[...] += jnp.sum(ddtr, axis=0, keepdims=True)
        ddexp_ref[...] += jnp.sum(dy * xs, axis=0, keepdims=True)

    return pl.pallas_call(
        body, name=name, grid=(SSM_GROUPS, bsz, nc),
        in_specs=[xs_s, bm_s, cm_s, dtr_s, vec_s, vec_s, dexp_s, xs_s, st_s],
        out_specs=[xs_s, grp, grp, grp, acc, acc, dexp_s],
        out_shape=[jax.ShapeDtypeStruct((bsz, seq, SSM_D_INNER), F32),
                   jax.ShapeDtypeStruct((bsz, seq, SSM_GROUPS * SSM_STATE), F32),
                   jax.ShapeDtypeStruct((bsz, seq, SSM_GROUPS * SSM_STATE), F32),
                   jax.ShapeDtypeStruct((bsz, seq, SSM_GROUPS * LANE), F32),
                   jax.ShapeDtypeStruct((SSM_GROUPS, 1, LANE), F32),
                   jax.ShapeDtypeStruct((SSM_GROUPS, 1, LANE), F32),
                   jax.ShapeDtypeStruct((1, SSM_D_INNER), F32)],
        scratch_shapes=[pltpu.VMEM((SSM_STATE, GW), F32)],
        compiler_params=_params(("arbitrary", "arbitrary", "arbitrary")),
    )(xc, xc, xc, dtr, dt_bias, a_log, d_exp, dy, sprev)


SB_TQ = 256
SB_TK = 128
SB_SCALE = SB_HEAD_DIM ** -0.5


def _split_dot(x, u16):
    hi = x.astype(BF16)
    lo = (x - hi.astype(F32)).astype(BF16)
    return jnp.dot(hi, u16, preferred_element_type=F32) + jnp.dot(lo, u16, preferred_element_type=F32)


def _sb_scores(qh, kj, t0, j):
    tq = qh.shape[0]
    z = lax.dot_general(qh, kj, NT, preferred_element_type=F32) * SB_SCALE
    ti = t0 + lax.broadcasted_iota(jnp.int32, (tq, SB_TK), 0)
    sj = j * SB_TK + lax.broadcasted_iota(jnp.int32, (tq, SB_TK), 1)
    causal = ti > sj
    l1p = jnp.log1p(jnp.exp(-jnp.abs(z)))
    lb = jnp.minimum(z, 0.0) - l1p
    lk = jnp.where(causal, -jnp.maximum(z, 0.0) - l1p, 0.0)
    return causal, lb, lk


def sb_attention_fwd(q, kv, name):
    bsz, seq, width = q.shape
    tq = min(SB_TQ, seq)
    npair = width // LANE
    qspec = pl.BlockSpec((None, tq, LANE), lambda b, p, i: (b, i, p))
    kspec = pl.BlockSpec((None, seq, LANE), lambda b, p, i: (b, 0, p))
    vspec = pl.BlockSpec((None, seq, LANE), lambda b, p, i: (b, 0, npair + p))

    def body(q_ref, k_ref, v_ref, o_ref, l_ref):
        i = pl.program_id(2)
        t0 = i * tq
        nkt = (i + 1) * (tq // SB_TK)
        lane = lax.broadcasted_iota(jnp.int32, (tq, LANE), 1)
        ui = lax.broadcasted_iota(jnp.int32, (SB_TK, SB_TK), 0)
        uj = lax.broadcasted_iota(jnp.int32, (SB_TK, SB_TK), 1)
        u_excl = (ui > uj).astype(BF16)
        qv = q_ref[...]
        out = jnp.zeros((tq, LANE), F32)
        tot = jnp.zeros((tq, LANE), F32)
        for hh in range(2):
            in_head = (lane // SB_HEAD_DIM) == hh
            qh = jnp.where(in_head, qv, jnp.zeros_like(qv))

            def step(jj, carry, qh=qh):
                acc, run = carry
                j = nkt - 1 - jj
                rows = pl.ds(pl.multiple_of(j * SB_TK, SB_TK), SB_TK)
                causal, lb, lk = _sb_scores(qh, k_ref[rows, :], t0, j)
                rest = _split_dot(lk, u_excl)
                w = jnp.where(causal, jnp.exp(lb + rest + run), 0.0)
                acc = acc + jnp.dot(w.astype(BF16), v_ref[rows, :], preferred_element_type=F32)
                return acc, run + jnp.sum(lk, axis=1, keepdims=True)

            acc, run = lax.fori_loop(0, nkt, step, (jnp.zeros((tq, LANE), F32), jnp.zeros((tq, 1), F32)))
            out = out + jnp.where(in_head, acc, 0.0)
            tot = tot + jnp.where(in_head, run, 0.0)
        o_ref[...] = out.astype(o_ref.dtype)
        l_ref[...] = tot

    return pl.pallas_call(
        body, name=name, grid=(bsz, npair, seq // tq),
        in_specs=[qspec, kspec, vspec], out_specs=[qspec, qspec],
        out_shape=[jax.ShapeDtypeStruct(q.shape, BF16), jax.ShapeDtypeStruct(q.shape, F32)],
        compiler_params=_params(("parallel", "parallel", "arbitrary")),
    )(q, kv, kv)


def sb_attention_bwd(q, kv, ltot, do, dk_in, dv_in, name):
    bsz, seq, width = q.shape
    tq = min(SB_TQ, seq)
    npair = width // LANE
    qspec = pl.BlockSpec((None, tq, LANE), lambda b, p, i: (b, i, p))
    kspec = pl.BlockSpec((None, seq, LANE), lambda b, p, i: (b, 0, p))
    vspec = pl.BlockSpec((None, seq, LANE), lambda b, p, i: (b, 0, npair + p))

    def body(q_ref, k_ref, v_ref, l_ref, do_ref, dki_ref, dvi_ref, dq_ref, dk_ref, dv_ref):
        i = pl.program_id(2)

        @pl.when(i == 0)
        def _():
            dk_ref[...] = dki_ref[...]
            dv_ref[...] = dvi_ref[...]

        t0 = i * tq
        nkt = (i + 1) * (tq // SB_TK)
        lane = lax.broadcasted_iota(jnp.int32, (tq, LANE), 1)
        ui = lax.broadcasted_iota(jnp.int32, (SB_TK, SB_TK), 0)
        uj = lax.broadcasted_iota(jnp.int32, (SB_TK, SB_TK), 1)
        u_le = (ui <= uj).astype(BF16)
        u_lt = (ui < uj).astype(BF16)
        qv, dov = q_ref[...], do_ref[...]
        dq = jnp.zeros((tq, LANE), F32)
        for hh in range(2):
            in_head = (lane // SB_HEAD_DIM) == hh
            qh = jnp.where(in_head, qv, jnp.zeros_like(qv))
            doh = jnp.where(in_head, dov, jnp.zeros_like(dov))
            ltot = l_ref[:, hh * SB_HEAD_DIM:hh * SB_HEAD_DIM + 1]

            def step(j, carry, qh=qh, doh=doh, ltot=ltot):
                acc, run, run_a = carry
                rows = pl.ds(pl.multiple_of(j * SB_TK, SB_TK), SB_TK)
                kj, vj = k_ref[rows, :], v_ref[rows, :]
                causal, lb, lk = _sb_scores(qh, kj, t0, j)
                rest = ltot - run - _split_dot(lk, u_le)
                w = jnp.where(causal, jnp.exp(lb + rest), 0.0)
                dw = lax.dot_general(doh, vj, NT, preferred_element_type=F32)
                a = dw * w
                ca = run_a + _split_dot(a, u_lt)
                sg = jnp.exp(lb)
                dz = jnp.where(causal, a * (1.0 - sg) - ca * sg, 0.0) * SB_SCALE
                dz16 = dz.astype(BF16)
                acc = acc + jnp.dot(dz16, kj, preferred_element_type=F32)
                dk_ref[rows, :] += lax.dot_general(dz16, qh, TN, preferred_element_type=F32)
                dv_ref[rows, :] += lax.dot_general(w.astype(BF16), doh, TN, preferred_element_type=F32)
                return acc, run + jnp.sum(lk, axis=1, keepdims=True), run_a + jnp.sum(a, axis=1, keepdims=True)

            zcol = jnp.zeros((tq, 1), F32)
            acc, _, _ = lax.fori_loop(0, nkt, step, (jnp.zeros((tq, LANE), F32), zcol, zcol))
            dq = dq + jnp.where(in_head, acc, 0.0)
        dq_ref[...] = dq.astype(dq_ref.dtype)

    return pl.pallas_call(
        body, name=name, grid=(bsz, npair, seq // tq),
        in_specs=[qspec, kspec, vspec, qspec, qspec, kspec, kspec], out_specs=[qspec, kspec, kspec],
        out_shape=[jax.ShapeDtypeStruct(q.shape, BF16), jax.ShapeDtypeStruct(q.shape, F32), jax.ShapeDtypeStruct(q.shape, F32)],
        compiler_params=_params(("parallel", "parallel", "arbitrary")),
    )(q, kv, kv, ltot, do, dk_in, dv_in)


def adamw(w, parts, m, v, name):
    r, c = w.shape
    tr = r
    for cand in (256, 176, 128, 64, 32, 16, 8):
        if r % cand == 0:
            tr = cand
            break
    bc1 = 1.0 - ADAM_B1 ** ADAM_STEP
    bc2 = 1.0 - ADAM_B2 ** ADAM_STEP

    def body(w_ref, p_ref, m_ref, v_ref, g_ref, d_ref, mo_ref, vo_ref):
        g = p_ref[0]
        for j in range(1, N_DEV):
            g = g + p_ref[j]
        mn = ADAM_B1 * m_ref[...] + (1.0 - ADAM_B1) * g
        vn = ADAM_B2 * v_ref[...] + (1.0 - ADAM_B2) * (g * g)
        g_ref[...] = g
        mo_ref[...] = mn
        vo_ref[...] = vn
        d_ref[...] = -ADAM_LR * ((mn / bc1) / (jnp.sqrt(vn / bc2) + ADAM_EPS) + ADAM_WD * w_ref[...])

    blk = pl.BlockSpec((tr, c), lambda i: (i, 0))
    pblk = pl.BlockSpec((N_DEV, tr, c), lambda i: (0, i, 0))
    shp = jax.ShapeDtypeStruct((r, c), F32)
    return pl.pallas_call(
        body, name=name, grid=(r // tr,), in_specs=[blk, pblk, blk, blk], out_specs=[blk, blk, blk, blk],
        out_shape=[shp, shp, shp, shp], compiler_params=_params(("parallel",)),
    )(w, parts, m, v)


MESH_ID = pl.DeviceIdType.MESH
ANY = pl.BlockSpec(memory_space=pl.ANY)


def all_gather(x, name):
    r, c = x.shape

    def body(x_ref, out_ref, send_sems, recv_sems, local_sem):
        mx, my, mc = lax.axis_index("x"), lax.axis_index("y"), lax.axis_index("c")
        me, sibling = (mx, my, mc), (mx, my, 1 - mc)
        chips = [(1 - mx, my), (mx, 1 - my), (1 - mx, 1 - my)]

        def slot(px, py, pc):
            return out_ref.at[4 * px + 2 * py + pc]

        def copy(k, block, to, src=None):
            return pltpu.make_async_remote_copy(
                src_ref=slot(*block) if src is None else src, dst_ref=slot(*block),
                send_sem=send_sems.at[k], recv_sem=recv_sems.at[k], device_id=to, device_id_type=MESH_ID)

        mine = pltpu.make_async_copy(x_ref, slot(*me), local_sem)
        mine.start()
        first = [copy(0, me, sibling, src=x_ref)]
        first += [copy(1 + j, me, (*chip, mc), src=x_ref) for j, chip in enumerate(chips)]
        for cp in first:
            cp.start()
        passed = [copy(4 + j, (*chip, mc), sibling) for j, chip in enumerate(chips)]
        for j, chip in enumerate(chips):
            copy(1 + j, (*chip, mc), me).wait_recv()
            passed[j].start()
        copy(0, sibling, me).wait_recv()
        for j, chip in enumerate(chips):
            copy(4 + j, (*chip, 1 - mc), me).wait_recv()
        for cp in first + passed:
            cp.wait_send()
        mine.wait()

    return pl.pallas_call(
        body, name=name, out_shape=jax.ShapeDtypeStruct((N_DEV, r, c), x.dtype),
        in_specs=[ANY], out_specs=ANY,
        scratch_shapes=[pltpu.SemaphoreType.DMA((7,)), pltpu.SemaphoreType.DMA((7,)), pltpu.SemaphoreType.DMA],
    )(x)


def all_to_all(p, name):
    _, r, c = p.shape

    def body(p_ref, out_ref, send_sems, recv_sems, local_sem):
        mx, my, mc = lax.axis_index("x"), lax.axis_index("y"), lax.axis_index("c")
        me = 4 * mx + 2 * my + mc
        mine = pltpu.make_async_copy(p_ref.at[me], out_ref.at[me], local_sem)
        mine.start()
        copies = []
        for j in range(1, N_DEV):
            px = 1 - mx if j & 4 else mx
            py = 1 - my if j & 2 else my
            pc = 1 - mc if j & 1 else mc
            peer = 4 * px + 2 * py + pc
            copies.append(pltpu.make_async_remote_copy(
                src_ref=p_ref.at[peer], dst_ref=out_ref.at[me], send_sem=send_sems.at[j - 1], recv_sem=recv_sems.at[j - 1],
                device_id=(px, py, pc), device_id_type=MESH_ID))
        for cp in copies:
            cp.start()
        for j in range(1, N_DEV):
            px = 1 - mx if j & 4 else mx
            py = 1 - my if j & 2 else my
            pc = 1 - mc if j & 1 else mc
            peer = 4 * px + 2 * py + pc
            pltpu.make_async_remote_copy(
                src_ref=p_ref.at[peer], dst_ref=out_ref.at[peer], send_sem=send_sems.at[j - 1], recv_sem=recv_sems.at[j - 1],
                device_id=(px, py, pc), device_id_type=MESH_ID).wait_recv()
        for cp in copies:
            cp.wait_send()
        mine.wait()

    return pl.pallas_call(
        body, name=name, out_shape=jax.ShapeDtypeStruct(p.shape, p.dtype),
        in_specs=[ANY], out_specs=ANY,
        scratch_shapes=[pltpu.SemaphoreType.DMA((7,)), pltpu.SemaphoreType.DMA((7,)), pltpu.SemaphoreType.DMA],
    )(p)


PACK_COLS = 1024
PACK_SEG = 16 * PACK_COLS

BIG = {"ssm_in_proj": 2, "ssm_out_proj": 1, "w_kv": 1, "w_q": 1, "w_o": 1, "ffn_up": 2, "ffn_down": 1,
       "ple_gate": 1, "ple_proj": 2}
SMALL = {"ssm_conv_w": 2, "ssm_conv_b": 1, "ssm_norm": 1, "ffn_conv_w": 2}
REPL = ["attn_norm", "ffn_norm", "ple_norm", "ssm_dt_bias", "ssm_a_log", "ssm_d", "ffn_conv_b", "kv_norm", "final_norm"]


def _seg(n):
    return -(-n // PACK_SEG) * PACK_SEG


def _pack_rows(flats, lead=()):
    padded = [jnp.pad(f, [(0, 0)] * len(lead) + [(0, _seg(f.shape[-1]) - f.shape[-1])]) for f in flats]
    return jnp.concatenate(padded, axis=-1).reshape(*lead, -1, PACK_COLS)


def _unpack_rows(buf, sizes, lead):
    flat = buf.reshape(*lead, -1)
    out, off = [], 0
    for n in sizes:
        out.append(lax.slice_in_dim(flat, off, off + n, axis=len(lead)))
        off += _seg(n)
    return out


def _to_slots(full, axis):
    shp = full.shape
    blk = shp[axis] // N_DEV
    t = full.reshape(shp[:axis] + (N_DEV, blk) + shp[axis + 1:])
    return jnp.moveaxis(t, axis, 0).reshape(N_DEV, -1)


def _from_slots(slots, local_shape, axis):
    t = jnp.moveaxis(slots.reshape((N_DEV,) + tuple(local_shape)), 0, axis)
    shp = list(local_shape)
    shp[axis] *= N_DEV
    return t.reshape(shp)


def _as2d(a):
    return a.reshape(1, -1) if a.ndim == 1 else a.reshape(-1, a.shape[-1])


def _pad_rows(w, rows=SUBLANE):
    return jnp.pad(w, ((0, rows - w.shape[0]), (0, 0)))


def kernel(x, p, attn_norm, ffn_norm, ple_norm, ssm_in_proj, ssm_conv_w, ssm_conv_b, ssm_dt_bias, ssm_a_log, ssm_d, ssm_norm, ssm_out_proj, kv_norm, w_kv, w_q, w_o, ffn_up, ffn_conv_w, ffn_conv_b, ffn_down, ple_gate, ple_proj, final_norm, loss_target, m_attn_norm, m_ffn_norm, m_ple_norm, m_ssm_in_proj, m_ssm_conv_w, m_ssm_conv_b, m_ssm_dt_bias, m_ssm_a_log, m_ssm_d, m_ssm_norm, m_ssm_out_proj, m_kv_norm, m_w_kv, m_w_q, m_w_o, m_ffn_up, m_ffn_conv_w, m_ffn_conv_b, m_ffn_down, m_ple_gate, m_ple_proj, m_final_norm, v_attn_norm, v_ffn_norm, v_ple_norm, v_ssm_in_proj, v_ssm_conv_w, v_ssm_conv_b, v_ssm_dt_bias, v_ssm_a_log, v_ssm_d, v_ssm_norm, v_ssm_out_proj, v_kv_norm, v_w_kv, v_w_q, v_w_o, v_ffn_up, v_ffn_conv_w, v_ffn_conv_b, v_ffn_down, v_ple_gate, v_ple_proj, v_final_norm):
    given = dict(locals())
    wnames = list(BIG) + list(SMALL) + REPL
    bsz, seq, d = x.shape
    t = bsz * seq
    bs = lambda a: a.reshape(bsz, seq, a.shape[-1])
    fl = lambda a: a.reshape(t, a.shape[-1])

    big_local = [given[n] for n in BIG]
    small_local = [given[n] for n in SMALL]
    gbig = all_gather(_pack_rows([w.astype(BF16).reshape(-1) for w in big_local]), "ag_weights")
    gsmall = all_gather(_pack_rows([w.reshape(-1) for w in small_local]), "ag_small_weights")
    full = {}
    for (n, ax), w, s in zip(BIG.items(), big_local, _unpack_rows(gbig, [w.size for w in big_local], (N_DEV,))):
        full[n] = _from_slots(s, w.shape, ax)
    for (n, ax), w, s in zip(SMALL.items(), small_local, _unpack_rows(gsmall, [w.size for w in small_local], (N_DEV,))):
        full[n] = _from_slots(s, w.shape, ax)

    row = lambda v: v.reshape(1, -1)
    pad_lane = lambda v: jnp.pad(v.reshape(1, -1), ((0, 0), (0, LANE - v.size)))
    h = x.reshape(t, d)
    saved = []
    kv = hnkv = h_kv = None
    for i in range(DEPTH):
        sv = {"h0": h}
        hn = rmsnorm_fwd(h, row(attn_norm[i]), f"attn_norm_f{i}")
        sv["hn"] = hn
        if i < N_A:
            w_in = full["ssm_in_proj"][i]
            wz, wxbc = w_in[:, :SSM_D_INNER], w_in[:, SSM_D_INNER:SSM_ZX]
            wdt = jnp.pad(w_in[:, SSM_ZX:], ((0, 0), (0, LANE - SSM_HEADS)))
            z = matmul(hn, wz, "nn", F32, f"ssm_z_f{i}")
            xbc = bs(matmul(hn, wxbc, "nn", F32, f"ssm_xbc_f{i}"))
            dtr = bs(matmul(hn, wdt, "nn", F32, f"ssm_dt_f{i}"))
            cw = _pad_rows(full["ssm_conv_w"][i])
            cb = row(full["ssm_conv_b"][i])
            xc = ssm_conv_fwd(xbc, cw, cb, f"ssm_conv_f{i}")
            dtb, alog = pad_lane(ssm_dt_bias[i]), pad_lane(ssm_a_log[i])
            dexp = jnp.repeat(ssm_d[i], SSM_HEAD_DIM).reshape(1, -1)
            y, sprev = ssd_fwd(xc, dtr, dtb, alog, dexp, f"ssd_f{i}")
            gn = gated_norm_fwd(fl(y), z, row(full["ssm_norm"][i]), f"ssm_gnorm_f{i}")
            h1 = matmul(gn, full["ssm_out_proj"][i], "nn", F32, f"ssm_out_f{i}", add=h)
            sv.update(wz=wz, wxbc=wxbc, wdt=wdt, z=z, xbc=xbc, dtr=dtr, cw=cw, cb=cb, xc=xc, dtb=dtb, alog=alog,
                      dexp=dexp, y=y, sprev=sprev, gn=gn)
        else:
            j = i - N_A
            q = bs(matmul(hn, full["w_q"][j], "nn", BF16, f"sb_q_f{j}"))
            o, ltot = sb_attention_fwd(q, kv, f"sb_attn_f{j}")
            h1 = matmul(fl(o), full["w_o"][j], "nn", F32, f"sb_o_f{j}", add=h)
            sv.update(q=q, o=o, ltot=ltot)
        sv["h1"] = h1
        hn2 = rmsnorm_fwd(h1, row(ffn_norm[i]), f"ffn_norm_f{i}")
        w_up = full["ffn_up"][i]
        wug, wuv = w_up[:, :D_FF], w_up[:, D_FF:]
        ug = bs(matmul(hn2, wug, "nn", F32, f"ffn_upg_f{i}"))
        uv = bs(matmul(hn2, wuv, "nn", F32, f"ffn_upv_f{i}"))
        fcw = full["ffn_conv_w"][i]
        fwg, fwv = _pad_rows(fcw[:, :D_FF]), _pad_rows(fcw[:, D_FF:])
        fbg, fbv = row(ffn_conv_b[i, :D_FF]), row(ffn_conv_b[i, D_FF:])
        act = ffn_conv_fwd(ug, uv, fwg, fwv, fbg, fbv, f"ffn_conv_f{i}")
        h2 = matmul(fl(act), full["ffn_down"][i], "nn", F32, f"ffn_down_f{i}", add=h1)
        hn3 = rmsnorm_fwd(h2, row(ple_norm[i]), f"ple_norm_f{i}")
        gpre = matmul(hn3, full["ple_gate"][i], "nn", F32, f"ple_gate_f{i}")
        p_i = p[i].reshape(t, PLE_DIM)
        pp = matmul(p_i, full["ple_proj"][i], "nn", F32, f"ple_proj_f{i}")
        h3 = ple_fwd(h2, gpre, pp, f"ple_f{i}")
        sv.update(hn2=hn2, wug=wug, wuv=wuv, ug=ug, uv=uv, fwg=fwg, fwv=fwv, fbg=fbg, fbv=fbv, act=act, h2=h2,
                  hn3=hn3, gpre=gpre, pp=pp, p_i=p_i)
        if i == N_A - 1:
            h_kv = h3
            hnkv = rmsnorm_fwd(h3, row(kv_norm), "kv_norm_f")
            kv = bs(matmul(hnkv, full["w_kv"], "nn", BF16, "kv_f"))
        h = h3
        saved.append(sv)

    loss_row, dh, g_final = loss_head(h, row(final_norm), loss_target.reshape(t, d), "loss_head")
    loss = lax.psum(loss_row[0, 0], ("x", "y", "c"))

    G = {n: [None] * given[n].shape[0] for n in wnames if given[n].ndim > 1 and n not in ("w_kv",)}
    G["final_norm"] = g_final[0]
    dk = jnp.zeros((bsz, seq, SB_WIDTH), F32)
    dv = jnp.zeros((bsz, seq, SB_WIDTH), F32)
    for i in reversed(range(DEPTH)):
        sv = saved[i]
        if i == N_A - 1:
            wk, wv = full["w_kv"][:, :SB_WIDTH], full["w_kv"][:, SB_WIDTH:]
            dkf, dvf = fl(dk), fl(dv)
            dhn = matmul(dkf, wk, "nt", F32, "kv_dx_k")
            dhn = matmul(dvf, wv, "nt", F32, "kv_dx_v", add=dhn)
            G["w_kv"] = jnp.concatenate([matmul(hnkv, dkf, "tn", F32, "kv_dw_k"), matmul(hnkv, dvf, "tn", F32, "kv_dw_v")], axis=1)
            dh, gk = rmsnorm_bwd(dhn, h_kv, row(kv_norm), dh, "kv_norm_b")
            G["kv_norm"] = gk[0]
        dgpre, dpp = ple_bwd(dh, sv["gpre"], sv["pp"], f"ple_b{i}")
        G["ple_proj"][i] = matmul(sv["p_i"], dpp, "tn", F32, f"ple_proj_dw{i}")
        G["ple_gate"][i] = matmul(sv["hn3"], dgpre, "tn", F32, f"ple_gate_dw{i}")
        dhn = matmul(dgpre, full["ple_gate"][i], "nt", F32, f"ple_gate_dx{i}")
        dh, gn_ = rmsnorm_bwd(dhn, sv["h2"], row(ple_norm[i]), dh, f"ple_norm_b{i}")
        G["ple_norm"][i] = gn_[0]
        dact = bs(matmul(dh, full["ffn_down"][i], "nt", F32, f"ffn_down_dx{i}"))
        G["ffn_down"][i] = matmul(fl(sv["act"]), dh, "tn", F32, f"ffn_down_dw{i}")
        dug, duv, dwg, dwv, dbg, dbv = ffn_conv_bwd(sv["ug"], sv["uv"], dact, sv["fwg"], sv["fwv"], sv["fbg"], sv["fbv"], f"ffn_conv_b{i}")
        dug, duv = fl(dug), fl(duv)
        G["ffn_conv_w"][i] = jnp.concatenate([dwg[:FFN_CONV], dwv[:FFN_CONV]], axis=1)
        G["ffn_conv_b"][i] = jnp.concatenate([dbg[0], dbv[0]])
        G["ffn_up"][i] = jnp.concatenate([matmul(sv["hn2"], dug, "tn", F32, f"ffn_upg_dw{i}"),
                                          matmul(sv["hn2"], duv, "tn", F32, f"ffn_upv_dw{i}")], axis=1)
        dhn = matmul(dug, sv["wug"], "nt", F32, f"ffn_upg_dx{i}")
        dhn = matmul(duv, sv["wuv"], "nt", F32, f"ffn_upv_dx{i}", add=dhn)
        dh, gn_ = rmsnorm_bwd(dhn, sv["h1"], row(ffn_norm[i]), dh, f"ffn_norm_b{i}")
        G["ffn_norm"][i] = gn_[0]
        if i < N_A:
            dgn = matmul(dh, full["ssm_out_proj"][i], "nt", F32, f"ssm_out_dx{i}")
            G["ssm_out_proj"][i] = matmul(sv["gn"], dh, "tn", F32, f"ssm_out_dw{i}")
            dy, dz, dnw = gated_norm_bwd(dgn, fl(sv["y"]), sv["z"], row(full["ssm_norm"][i]), f"ssm_gnorm_b{i}")
            G["ssm_norm"][i] = dnw[0]
            dxs, dbm, dcm, ddtr4, da4, dbias4, ddexp = ssd_bwd(sv["xc"], sv["dtr"], sv["dtb"], sv["alog"], sv["dexp"], bs(dy), sv["sprev"], f"ssd_b{i}")
            G["ssm_a_log"][i] = da4[:, 0, :HPG].reshape(-1)
            G["ssm_dt_bias"][i] = dbias4[:, 0, :HPG].reshape(-1)
            G["ssm_d"][i] = ddexp.reshape(SSM_HEADS, SSM_HEAD_DIM).sum(axis=-1)
            dxc = jnp.concatenate([dxs, dbm, dcm], axis=-1)
            dxbc, dcw, dcb = ssm_conv_bwd(sv["xbc"], dxc, sv["cw"], sv["cb"], f"ssm_conv_b{i}")
            dxbc = fl(dxbc)
            G["ssm_conv_w"][i] = dcw[:SSM_CONV]
            G["ssm_conv_b"][i] = dcb[0]
            ddtr = ddtr4.reshape(t, SSM_GROUPS, LANE)[:, :, :HPG].reshape(t, SSM_HEADS)
            ddtr = jnp.pad(ddtr, ((0, 0), (0, LANE - SSM_HEADS))).astype(BF16)
            hn = sv["hn"]
            G["ssm_in_proj"][i] = jnp.concatenate([
                matmul(hn, dz, "tn", F32, f"ssm_z_dw{i}"), matmul(hn, dxbc, "tn", F32, f"ssm_xbc_dw{i}"),
                matmul(hn, ddtr, "tn", F32, f"ssm_dt_dw{i}")[:, :SSM_HEADS]], axis=1)
            dhn = matmul(dz, sv["wz"], "nt", F32, f"ssm_z_dx{i}")
            dhn = matmul(dxbc, sv["wxbc"], "nt", F32, f"ssm_xbc_dx{i}", add=dhn)
            dhn = matmul(ddtr, sv["wdt"], "nt", F32, f"ssm_dt_dx{i}", add=dhn)
        else:
            j = i - N_A
            do = bs(matmul(dh, full["w_o"][j], "nt", BF16, f"sb_o_dx{j}"))
            G["w_o"][j] = matmul(fl(sv["o"]), dh, "tn", F32, f"sb_o_dw{j}")
            dq, dk, dv = sb_attention_bwd(sv["q"], kv, sv["ltot"], do, dk, dv, f"sb_attn_b{j}")
            dq = fl(dq)
            G["w_q"][j] = matmul(sv["hn"], dq, "tn", F32, f"sb_q_dw{j}")
            dhn = matmul(dq, full["w_q"][j], "nt", F32, f"sb_q_dx{j}")
        dh, gn_ = rmsnorm_bwd(dhn, sv["h0"], row(attn_norm[i]), dh, f"attn_norm_b{i}")
        G["attn_norm"][i] = gn_[0]
    grad_x = dh.reshape(bsz, seq, d)
    grads = {n: (jnp.stack(g) if isinstance(g, list) else g) for n, g in G.items()}

    sh_names = list(BIG) + list(SMALL)
    axes = {**BIG, **SMALL}
    parts_sh = all_to_all(_pack_rows([_to_slots(grads[n], axes[n]) for n in sh_names], (N_DEV,)), "a2a_grads")
    parts_rp = all_gather(_pack_rows([grads[n].reshape(-1) for n in REPL]), "ag_repl_grads")
    sh_parts = _unpack_rows(parts_sh, [given[n].size for n in sh_names], (N_DEV,))
    rp_parts = _unpack_rows(parts_rp, [given[n].size for n in REPL], (N_DEV,))
    res = {}
    for n, pr in list(zip(sh_names, sh_parts)) + list(zip(REPL, rp_parts)):
        w2 = _as2d(given[n])
        outs = adamw(w2, pr.reshape((N_DEV,) + w2.shape), _as2d(given["m_" + n]), _as2d(given["v_" + n]), f"adamw_{n}")
        res[n] = [o.reshape(given[n].shape) for o in outs]
    order = ["attn_norm", "ffn_norm", "ple_norm", "ssm_in_proj", "ssm_conv_w", "ssm_conv_b", "ssm_dt_bias", "ssm_a_log",
             "ssm_d", "ssm_norm", "ssm_out_proj", "kv_norm", "w_kv", "w_q", "w_o", "ffn_up", "ffn_conv_w", "ffn_conv_b",
             "ffn_down", "ple_gate", "ple_proj", "final_norm"]
    return (loss, grad_x, *[res[n][0] for n in order], *[res[n][1] for n in order],
            *[res[n][2] for n in order], *[res[n][3] for n in order])
```

```python
import functools
import math

import jax
import jax.numpy as jnp
from jax import lax
from jax.experimental import pallas as pl
from jax.experimental.pallas import tpu as pltpu

F32 = jnp.float32
BF16 = jnp.bfloat16

N_DEV = 8
D_MODEL = 1024
SEQ = 2048
DEPTH = 4
N_A = 2
N_B = 2
SSM_D_INNER = 2048
SSM_HEAD_DIM = 64
SSM_HEADS = 32
SSM_GROUPS = 4
SSM_STATE = 128
SSM_CONV = 4
SSM_CHUNK = 128
SSM_CONV_DIM = 3072
SSM_ZX = 5120
SSM_IN_DIM = 5152
SB_HEADS = 16
SB_HEAD_DIM = 64
SB_WIDTH = 1024
D_FF = 2816
FFN_CONV = 3
PLE_DIM = 256
NORM_EPS = 1e-6
SSM_NORM_EPS = 1e-5

ADAM_LR = 0.001
ADAM_B1 = 0.9
ADAM_B2 = 0.999
ADAM_EPS = 1e-08
ADAM_WD = 0.01
ADAM_STEP = 10

LANE = 128
SUBLANE = 8
VMEM_LIMIT = 48 * 1024 * 1024

NN = (((1,), (0,)), ((), ()))
NT = (((1,), (1,)), ((), ()))
TN = (((0,), (0,)), ((), ()))


def _params(sem):
    return pltpu.CompilerParams(dimension_semantics=sem, vmem_limit_bytes=VMEM_LIMIT)


def _pick(n, prefs):
    for p in prefs:
        if n % p == 0:
            return p
    return n


def matmul(a, b, mode, out_dtype, name, add=None):
    if mode == "nn":
        (m, k), (k2, n) = a.shape, b.shape
    elif mode == "nt":
        (m, k), (n, k2) = a.shape, b.shape
    else:
        (k, m), (k2, n) = a.shape, b.shape
    assert k == k2, (a.shape, b.shape, mode)
    tm = _pick(m, (512, 256, 128))
    tn = _pick(n, (1408, 1024, 512, 256, 128))
    tk = _pick(k, (2816, 2048, 1024, 512, 256, 128))
    nk = k // tk
    dims = {"nn": NN, "nt": NT, "tn": TN}[mode]

    def body(*refs):
        a_ref, b_ref = refs[:2]
        add_ref = refs[2] if add is not None else None
        o_ref = refs[-2] if nk > 1 else refs[-1]
        part = lax.dot_general(a_ref[...].astype(BF16), b_ref[...].astype(BF16), dims, preferred_element_type=F32)

        def finish(r):
            if add_ref is not None:
                r = r + add_ref[...].astype(F32)
            o_ref[...] = r.astype(o_ref.dtype)

        if nk == 1:
            finish(part)
            return
        acc_ref = refs[-1]
        kk = pl.program_id(2)

        @pl.when(kk == 0)
        def _():
            acc_ref[...] = part

        @pl.when((kk > 0) & (kk < nk - 1))
        def _():
            acc_ref[...] += part

        @pl.when(kk == nk - 1)
        def _():
            finish(acc_ref[...] + part)

    if mode == "nn":
        a_spec = pl.BlockSpec((tm, tk), lambda i, j, kk: (i, kk))
        b_spec = pl.BlockSpec((tk, tn), lambda i, j, kk: (kk, j))
    elif mode == "nt":
        a_spec = pl.BlockSpec((tm, tk), lambda i, j, kk: (i, kk))
        b_spec = pl.BlockSpec((tn, tk), lambda i, j, kk: (j, kk))
    else:
        a_spec = pl.BlockSpec((tk, tm), lambda i, j, kk: (kk, i))
        b_spec = pl.BlockSpec((tk, tn), lambda i, j, kk: (kk, j))
    o_spec = pl.BlockSpec((tm, tn), lambda i, j, kk: (i, j))
    in_specs = [a_spec, b_spec] + ([o_spec] if add is not None else [])
    args = (a, b) + ((add,) if add is not None else ())
    return pl.pallas_call(
        body,
        name=name,
        grid=(m // tm, n // tn, nk),
        in_specs=in_specs,
        out_specs=o_spec,
        out_shape=jax.ShapeDtypeStruct((m, n), out_dtype),
        scratch_shapes=[pltpu.VMEM((tm, tn), F32)] if nk > 1 else [],
        compiler_params=_params(("parallel", "parallel", "arbitrary")),
    )(*args)


ROW_TILE = 512


def rmsnorm_fwd(x, gain, name):
    t, d = x.shape

    def body(x_ref, g_ref, o_ref):
        xv = x_ref[...]
        r = lax.rsqrt(jnp.mean(xv * xv, axis=-1, keepdims=True) + NORM_EPS)
        o_ref[...] = (xv * r * g_ref[...]).astype(o_ref.dtype)

    return pl.pallas_call(
        body,
        name=name,
        grid=(t // ROW_TILE,),
        in_specs=[pl.BlockSpec((ROW_TILE, d), lambda i: (i, 0)), pl.BlockSpec((1, d), lambda i: (0, 0))],
        out_specs=pl.BlockSpec((ROW_TILE, d), lambda i: (i, 0)),
        out_shape=jax.ShapeDtypeStruct((t, d), BF16),
        compiler_params=_params(("parallel",)),
    )(x, gain)


def rmsnorm_bwd(dy, x, gain, dres, name):
    t, d = x.shape

    def body(dy_ref, x_ref, g_ref, dres_ref, dx_ref, dg_ref):
        i = pl.program_id(0)
        xv = x_ref[...]
        r = lax.rsqrt(jnp.mean(xv * xv, axis=-1, keepdims=True) + NORM_EPS)
        xh = xv * r
        dyv = dy_ref[...].astype(F32)
        dxh = dyv * g_ref[...]
        dx = r * (dxh - xh * jnp.mean(dxh * xh, axis=-1, keepdims=True))
        dx_ref[...] = dres_ref[...] + dx

        @pl.when(i == 0)
        def _():
            dg_ref[...] = jnp.zeros_like(dg_ref)

        dg_ref[...] += jnp.sum(dyv * xh, axis=0, keepdims=True)

    row = pl.BlockSpec((ROW_TILE, d), lambda i: (i, 0))
    vec = pl.BlockSpec((1, d), lambda i: (0, 0))
    return pl.pallas_call(
        body,
        name=name,
        grid=(t // ROW_TILE,),
        in_specs=[row, row, vec, row],
        out_specs=[row, vec],
        out_shape=[jax.ShapeDtypeStruct((t, d), F32), jax.ShapeDtypeStruct((1, d), F32)],
        compiler_params=_params(("arbitrary",)),
    )(dy, x, gain, dres)


CONV_ROWS = 512
HALO = SUBLANE


def _conv_apply(ext, w_ref, bias, k, lo, n):
    acc = bias + w_ref[k - 1:k, :] * ext[lo:lo + n]
    for j in range(1, k):
        acc = acc + w_ref[k - 1 - j:k - j, :] * pltpu.roll(ext, j, 0)[lo:lo + n]
    return acc


def _conv_apply_t(ext, w_ref, k, n):
    rows = ext.shape[0]
    acc = w_ref[k - 1:k, :] * ext[:n]
    for j in range(1, k):
        acc = acc + w_ref[k - 1 - j:k - j, :] * pltpu.roll(ext, rows - j, 0)[:n]
    return acc


def _conv_dw(dpre_main, ext_u, k, n, dw_ref):
    for kk in range(k):
        j = k - 1 - kk
        sh = ext_u[HALO:HALO + n] if j == 0 else pltpu.roll(ext_u, j, 0)[HALO:HALO + n]
        dw_ref[kk:kk + 1, :] += jnp.sum(dpre_main * sh, axis=0, keepdims=True)


def _silu_grad(pre):
    sg = jax.nn.sigmoid(pre)
    return sg * (1.0 + pre * (1.0 - sg))


def _conv_blockspecs(seq, r, tc):
    nh = r // HALO
    last = seq // HALO - 1
    main = pl.BlockSpec((None, r, tc), lambda c, b, s: (b, s, c))
    before = pl.BlockSpec((None, HALO, tc), lambda c, b, s: (b, jnp.maximum(s * nh - 1, 0), c))
    after = pl.BlockSpec((None, HALO, tc), lambda c, b, s: (b, jnp.minimum((s + 1) * nh, last), c))
    return main, before, after


def ssm_conv_fwd(u, w, bias, name):
    bsz, seq, ch = u.shape
    r, tc, k = min(CONV_ROWS, seq), 512, SSM_CONV
    main, before, _ = _conv_blockspecs(seq, r, tc)
    wspec = pl.BlockSpec((SUBLANE, tc), lambda c, b, s: (0, c))
    bspec = pl.BlockSpec((1, tc), lambda c, b, s: (0, c))

    def body(u_ref, hb_ref, w_ref, b_ref, o_ref):
        s = pl.program_id(2)
        hb = jnp.where(s == 0, 0.0, hb_ref[...])
        ext = jnp.concatenate([hb, u_ref[...]], axis=0)
        pre = _conv_apply(ext, w_ref, b_ref[...], k, HALO, r)
        o_ref[...] = pre * jax.nn.sigmoid(pre)

    return pl.pallas_call(
        body, name=name, grid=(ch // tc, bsz, seq // r),
        in_specs=[main, before, wspec, bspec], out_specs=main,
        out_shape=jax.ShapeDtypeStruct(u.shape, F32),
        compiler_params=_params(("parallel", "parallel", "parallel")),
    )(u, u, w, bias)


def ssm_conv_bwd(u, dxc, w, bias, name):
    bsz, seq, ch = u.shape
    r, tc, k = min(CONV_ROWS, seq), 512, SSM_CONV
    ns = seq // r
    main, before, after = _conv_blockspecs(seq, r, tc)
    wspec = pl.BlockSpec((SUBLANE, tc), lambda c, b, s: (0, c))
    bspec = pl.BlockSpec((1, tc), lambda c, b, s: (0, c))

    def body(u_ref, hb_ref, ha_ref, d_ref, da_ref, w_ref, b_ref, du_ref, dw_ref, db_ref):
        b, s = pl.program_id(1), pl.program_id(2)
        hb = jnp.where(s == 0, 0.0, hb_ref[...])
        ext_u = jnp.concatenate([hb, u_ref[...], ha_ref[...]], axis=0)
        pre = _conv_apply(ext_u, w_ref, b_ref[...], k, HALO, r + HALO)
        dxe = jnp.concatenate([d_ref[...], jnp.where(s == ns - 1, 0.0, da_ref[...])], axis=0)
        dpre = dxe * _silu_grad(pre)
        du_ref[...] = _conv_apply_t(dpre, w_ref, k, r).astype(du_ref.dtype)

        @pl.when((b == 0) & (s == 0))
        def _():
            dw_ref[...] = jnp.zeros_like(dw_ref)
            db_ref[...] = jnp.zeros_like(db_ref)

        dpm = dpre[:r]
        _conv_dw(dpm, ext_u, k, r, dw_ref)
        db_ref[...] += jnp.sum(dpm, axis=0, keepdims=True)

    return pl.pallas_call(
        body, name=name, grid=(ch // tc, bsz, ns),
        in_specs=[main, before, after, main, after, wspec, bspec],
        out_specs=[main, wspec, bspec],
        out_shape=[jax.ShapeDtypeStruct(u.shape, BF16), jax.ShapeDtypeStruct((SUBLANE, ch), F32),
                   jax.ShapeDtypeStruct((1, ch), F32)],
        compiler_params=_params(("arbitrary", "arbitrary", "arbitrary")),
    )(u, u, u, dxc, dxc, w, bias)


def ffn_conv_fwd(ug, uv, wg, wv, bg, bv, name):
    bsz, seq, ch = ug.shape
    r, tc, k = min(CONV_ROWS, seq), 256, FFN_CONV
    main, before, _ = _conv_blockspecs(seq, r, tc)
    wspec = pl.BlockSpec((SUBLANE, tc), lambda c, b, s: (0, c))
    bspec = pl.BlockSpec((1, tc), lambda c, b, s: (0, c))

    def body(ug_ref, hg_ref, uv_ref, hv_ref, wg_ref, wv_ref, bg_ref, bv_ref, o_ref):
        s = pl.program_id(2)
        eg = jnp.concatenate([jnp.where(s == 0, 0.0, hg_ref[...]), ug_ref[...]], axis=0)
        ev = jnp.concatenate([jnp.where(s == 0, 0.0, hv_ref[...]), uv_ref[...]], axis=0)
        pg = _conv_apply(eg, wg_ref, bg_ref[...], k, HALO, r)
        pv = _conv_apply(ev, wv_ref, bv_ref[...], k, HALO, r)
        o_ref[...] = (pg * jax.nn.sigmoid(pg) * pv).astype(o_ref.dtype)

    return pl.pallas_call(
        body, name=name, grid=(ch // tc, bsz, seq // r),
        in_specs=[main, before, main, before, wspec, wspec, bspec, bspec], out_specs=main,
        out_shape=jax.ShapeDtypeStruct(ug.shape, BF16),
        compiler_params=_params(("parallel", "parallel", "parallel")),
    )(ug, ug, uv, uv, wg, wv, bg, bv)


def ffn_conv_bwd(ug, uv, dact, wg, wv, bg, bv, name):
    bsz, seq, ch = ug.shape
    r, tc, k = min(CONV_ROWS, seq), 256, FFN_CONV
    ns = seq // r
    main, before, after = _conv_blockspecs(seq, r, tc)
    wspec = pl.BlockSpec((SUBLANE, tc), lambda c, b, s: (0, c))
    bspec = pl.BlockSpec((1, tc), lambda c, b, s: (0, c))

    def body(ug_ref, gb_ref, ga_ref, uv_ref, vb_ref, va_ref, d_ref, da_ref, wg_ref, wv_ref, bg_ref, bv_ref,
             dug_ref, duv_ref, dwg_ref, dwv_ref, dbg_ref, dbv_ref):
        b, s = pl.program_id(1), pl.program_id(2)
        eg = jnp.concatenate([jnp.where(s == 0, 0.0, gb_ref[...]), ug_ref[...], ga_ref[...]], axis=0)
        ev = jnp.concatenate([jnp.where(s == 0, 0.0, vb_ref[...]), uv_ref[...], va_ref[...]], axis=0)
        pg = _conv_apply(eg, wg_ref, bg_ref[...], k, HALO, r + HALO)
        pv = _conv_apply(ev, wv_ref, bv_ref[...], k, HALO, r + HALO)
        de = jnp.concatenate([d_ref[...], jnp.where(s == ns - 1, 0.0, da_ref[...])], axis=0)
        sg = jax.nn.sigmoid(pg)
        dpg = de * pv * (sg * (1.0 + pg * (1.0 - sg)))
        dpv = de * (pg * sg)
        dug_ref[...] = _conv_apply_t(dpg, wg_ref, k, r).astype(dug_ref.dtype)
        duv_ref[...] = _conv_apply_t(dpv, wv_ref, k, r).astype(duv_ref.dtype)

        @pl.when((b == 0) & (s == 0))
        def _():
            dwg_ref[...] = jnp.zeros_like(dwg_ref)
            dwv_ref[...] = jnp.zeros_like(dwv_ref)
            dbg_ref[...] = jnp.zeros_like(dbg_ref)
            dbv_ref[...] = jnp.zeros_like(dbv_ref)

        _conv_dw(dpg[:r], eg, k, r, dwg_ref)
        _conv_dw(dpv[:r], ev, k, r, dwv_ref)
        dbg_ref[...] += jnp.sum(dpg[:r], axis=0, keepdims=True)
        dbv_ref[...] += jnp.sum(dpv[:r], axis=0, keepdims=True)

    wshape = jax.ShapeDtypeStruct((SUBLANE, ch), F32)
    bshape = jax.ShapeDtypeStruct((1, ch), F32)
    return pl.pallas_call(
        body, name=name, grid=(ch // tc, bsz, ns),
        in_specs=[main, before, after, main, before, after, main, after, wspec, wspec, bspec, bspec],
        out_specs=[main, main, wspec, wspec, bspec, bspec],
        out_shape=[jax.ShapeDtypeStruct(ug.shape, BF16), jax.ShapeDtypeStruct(ug.shape, BF16), wshape, wshape, bshape, bshape],
        compiler_params=_params(("arbitrary", "arbitrary", "arbitrary")),
    )(ug, ug, ug, uv, uv, uv, dact, dact, wg, wv, bg, bv)


GN_ROWS = 256
GN_GROUP = SSM_D_INNER // SSM_GROUPS


def gated_norm_fwd(y, z, w, name):
    t, d = y.shape

    def body(y_ref, z_ref, w_ref, o_ref):
        for g in range(d // GN_GROUP):
            sl = slice(g * GN_GROUP, (g + 1) * GN_GROUP)
            zv = z_ref[:, sl]
            gv = y_ref[:, sl] * (zv * jax.nn.sigmoid(zv))
            r = lax.rsqrt(jnp.mean(gv * gv, axis=-1, keepdims=True) + SSM_NORM_EPS)
            o_ref[:, sl] = (gv * r * w_ref[:, sl]).astype(o_ref.dtype)

    row = pl.BlockSpec((GN_ROWS, d), lambda i: (i, 0))
    vec = pl.BlockSpec((1, d), lambda i: (0, 0))
    return pl.pallas_call(
        body, name=name, grid=(t // GN_ROWS,), in_specs=[row, row, vec], out_specs=row,
        out_shape=jax.ShapeDtypeStruct((t, d), BF16), compiler_params=_params(("parallel",)),
    )(y, z, w)


def gated_norm_bwd(dgn, y, z, w, name):
    t, d = y.shape

    def body(d_ref, y_ref, z_ref, w_ref, dy_ref, dz_ref, dw_ref):
        i = pl.program_id(0)

        @pl.when(i == 0)
        def _():
            dw_ref[...] = jnp.zeros_like(dw_ref)

        for g in range(d // GN_GROUP):
            sl = slice(g * GN_GROUP, (g + 1) * GN_GROUP)
            zv, yv, dv = z_ref[:, sl], y_ref[:, sl], d_ref[:, sl]
            sg = jax.nn.sigmoid(zv)
            sz = zv * sg
            gv = yv * sz
            r = lax.rsqrt(jnp.mean(gv * gv, axis=-1, keepdims=True) + SSM_NORM_EPS)
            gh = gv * r
            dgh = dv * w_ref[:, sl]
            dw_ref[:, sl] += jnp.sum(dv * gh, axis=0, keepdims=True)
            dg = r * (dgh - gh * jnp.mean(dgh * gh, axis=-1, keepdims=True))
            dy_ref[:, sl] = dg * sz
            dz_ref[:, sl] = (dg * yv * (sg * (1.0 + zv * (1.0 - sg)))).astype(dz_ref.dtype)

    row = pl.BlockSpec((GN_ROWS, d), lambda i: (i, 0))
    vec = pl.BlockSpec((1, d), lambda i: (0, 0))
    return pl.pallas_call(
        body, name=name, grid=(t // GN_ROWS,), in_specs=[row, row, row, vec], out_specs=[row, row, vec],
        out_shape=[jax.ShapeDtypeStruct((t, d), F32), jax.ShapeDtypeStruct((t, d), BF16), jax.ShapeDtypeStruct((1, d), F32)],
        compiler_params=_params(("arbitrary",)),
    )(dgn, y, z, w)


def ple_fwd(h, gpre, pp, name):
    t, d = h.shape

    def body(h_ref, g_ref, p_ref, o_ref):
        o_ref[...] = h_ref[...] + jax.nn.sigmoid(g_ref[...]) * p_ref[...]

    row = pl.BlockSpec((ROW_TILE, d), lambda i: (i, 0))
    return pl.pallas_call(
        body, name=name, grid=(t // ROW_TILE,), in_specs=[row, row, row], out_specs=row,
        out_shape=jax.ShapeDtypeStruct((t, d), F32), compiler_params=_params(("parallel",)),
    )(h, gpre, pp)


def ple_bwd(dh, gpre, pp, name):
    t, d = dh.shape

    def body(d_ref, g_ref, p_ref, dg_ref, dp_ref):
        sg = jax.nn.sigmoid(g_ref[...])
        dv = d_ref[...]
        dg_ref[...] = (dv * p_ref[...] * sg * (1.0 - sg)).astype(dg_ref.dtype)
        dp_ref[...] = (dv * sg).astype(dp_ref.dtype)

    row = pl.BlockSpec((ROW_TILE, d), lambda i: (i, 0))
    return pl.pallas_call(
        body, name=name, grid=(t // ROW_TILE,), in_specs=[row, row, row], out_specs=[row, row],
        out_shape=[jax.ShapeDtypeStruct((t, d), BF16), jax.ShapeDtypeStruct((t, d), BF16)],
        compiler_params=_params(("parallel",)),
    )(dh, gpre, pp)


def loss_head(h, gain, target, name):
    t, d = h.shape

    def body(x_ref, g_ref, t_ref, l_ref, dx_ref, dg_ref):
        i = pl.program_id(0)
        xv = x_ref[...]
        r = lax.rsqrt(jnp.mean(xv * xv, axis=-1, keepdims=True) + NORM_EPS)
        xh = xv * r
        err = xh * g_ref[...] - t_ref[...]
        part = 0.5 * jnp.sum(jnp.mean(err * err, axis=-1, keepdims=True), axis=0, keepdims=True)
        dyv = err * (1.0 / d)
        dxh = dyv * g_ref[...]
        dx_ref[...] = r * (dxh - xh * jnp.mean(dxh * xh, axis=-1, keepdims=True))

        @pl.when(i == 0)
        def _():
            dg_ref[...] = jnp.zeros_like(dg_ref)
            l_ref[...] = jnp.zeros_like(l_ref)

        dg_ref[...] += jnp.sum(dyv * xh, axis=0, keepdims=True)
        l_ref[...] += jnp.broadcast_to(part, l_ref.shape)

    row = pl.BlockSpec((ROW_TILE, d), lambda i: (i, 0))
    vec = pl.BlockSpec((1, d), lambda i: (0, 0))
    lspec = pl.BlockSpec((1, LANE), lambda i: (0, 0))
    return pl.pallas_call(
        body, name=name, grid=(t // ROW_TILE,), in_specs=[row, vec, row], out_specs=[lspec, row, vec],
        out_shape=[jax.ShapeDtypeStruct((1, LANE), F32), jax.ShapeDtypeStruct((t, d), F32), jax.ShapeDtypeStruct((1, d), F32)],
        compiler_params=_params(("arbitrary",)),
    )(h, gain, target)


CHUNK = SSM_CHUNK
HPG = SSM_HEADS // SSM_GROUPS
GW = HPG * SSM_HEAD_DIM
NEG = -1e30


def _ssd_dt_terms(dtr_ref, bias_ref, alog_ref, g):
    shift = (LANE - HPG * g) % LANE
    li = lax.broadcasted_iota(jnp.int32, (CHUNK, CHUNK), 0)
    si = lax.broadcasted_iota(jnp.int32, (CHUNK, CHUNK), 1)
    x0 = pltpu.roll(dtr_ref[...] + bias_ref[...], shift, 1)
    dt = jnp.maximum(x0, 0.0) + jnp.log1p(jnp.exp(-jnp.abs(x0)))
    a = pltpu.roll(jnp.broadcast_to(-jnp.exp(alog_ref[...]), (CHUNK, LANE)), shift, 1)
    adt = dt * a
    acs = lax.dot_general((li >= si).astype(F32), adt, NN, precision=lax.Precision.HIGHEST, preferred_element_type=F32)
    last = jnp.broadcast_to(acs[CHUNK - 1:CHUNK, :], (CHUNK, LANE))
    return dict(li=li, si=si, x0=x0, dt=dt, a=a, acs=acs, acs_t=acs.T, e_acs=jnp.exp(acs),
                w_all=jnp.exp(last - acs), e_last=jnp.exp(last))


def _ssd_specs(nc, rev):
    def ci(c):
        return nc - 1 - c if rev else c
    nb = SSM_D_INNER // LANE
    xs = pl.BlockSpec((None, CHUNK, GW), lambda g, b, c: (b, ci(c), g))
    bm = pl.BlockSpec((None, CHUNK, LANE), lambda g, b, c: (b, ci(c), nb + g))
    cm = pl.BlockSpec((None, CHUNK, LANE), lambda g, b, c: (b, ci(c), nb + SSM_GROUPS + g))
    dtr = pl.BlockSpec((None, CHUNK, LANE), lambda g, b, c: (b, ci(c), 0))
    vec = pl.BlockSpec((1, LANE), lambda g, b, c: (0, 0))
    dexp = pl.BlockSpec((1, GW), lambda g, b, c: (0, g))
    st = pl.BlockSpec((None, None, CHUNK, GW), lambda g, b, c: (b, ci(c), 0, g))
    return xs, bm, cm, dtr, vec, dexp, st


def ssd_fwd(xc, dtr, dt_bias, a_log, d_exp, name):
    bsz, seq, _ = xc.shape
    nc = seq // CHUNK
    xs_s, bm_s, cm_s, dtr_s, vec_s, dexp_s, st_s = _ssd_specs(nc, False)

    def body(xs_ref, b_ref, c_ref, dtr_ref, bias_ref, alog_ref, dexp_ref, y_ref, sp_ref, state_ref):
        g, c = pl.program_id(0), pl.program_id(2)

        @pl.when(c == 0)
        def _():
            state_ref[...] = jnp.zeros_like(state_ref)

        t = _ssd_dt_terms(dtr_ref, bias_ref, alog_ref, g)
        tril = t["li"] >= t["si"]
        xs = xs_ref[...]
        b16 = b_ref[...].astype(BF16)
        c16 = c_ref[...].astype(BF16)
        bt16 = b_ref[...].T.astype(BF16)
        cb = lax.dot_general(c16, b16, NT, preferred_element_type=F32)
        sp = state_ref[...]
        sp_ref[...] = sp
        cs = jnp.dot(c16, sp.astype(BF16), preferred_element_type=F32)
        y_parts, s_parts = [], []
        for r in range(HPG):
            sl = slice(r * SSM_HEAD_DIM, (r + 1) * SSM_HEAD_DIM)
            diff = t["acs"][:, r:r + 1] - t["acs_t"][r:r + 1, :]
            m16 = (cb * jnp.exp(jnp.where(tril, diff, NEG))).astype(BF16)
            xh = xs[:, sl]
            xd = xh * t["dt"][:, r:r + 1]
            yh = jnp.dot(m16, xd.astype(BF16), preferred_element_type=F32)
            yh = yh + t["e_acs"][:, r:r + 1] * cs[:, sl] + xh * dexp_ref[:, sl]
            y_parts.append(yh)
            xw16 = (xd * t["w_all"][:, r:r + 1]).astype(BF16)
            s_parts.append(sp[:, sl] * t["e_last"][:, r:r + 1] + jnp.dot(bt16, xw16, preferred_element_type=F32))
        y_ref[...] = jnp.concatenate(y_parts, axis=1)
        state_ref[...] = jnp.concatenate(s_parts, axis=1)

    return pl.pallas_call(
        body, name=name, grid=(SSM_GROUPS, bsz, nc),
        in_specs=[xs_s, bm_s, cm_s, dtr_s, vec_s, vec_s, dexp_s],
        out_specs=[xs_s, st_s],
        out_shape=[jax.ShapeDtypeStruct((bsz, seq, SSM_D_INNER), F32),
                   jax.ShapeDtypeStruct((bsz, nc, SSM_STATE, SSM_D_INNER), F32)],
        scratch_shapes=[pltpu.VMEM((SSM_STATE, GW), F32)],
        compiler_params=_params(("arbitrary", "arbitrary", "arbitrary")),
    )(xc, xc, xc, dtr, dt_bias, a_log, d_exp)


def ssd_bwd(xc, dtr, dt_bias, a_log, d_exp, dy, sprev, name):
    bsz, seq, _ = xc.shape
    nc = seq // CHUNK
    xs_s, bm_s, cm_s, dtr_s, vec_s, dexp_s, st_s = _ssd_specs(nc, True)
    grp = pl.BlockSpec((None, CHUNK, LANE), lambda g, b, c: (b, nc - 1 - c, g))
    acc = pl.BlockSpec((None, 1, LANE), lambda g, b, c: (g, 0, 0))

    def body(xs_ref, b_ref, c_ref, dtr_ref, bias_ref, alog_ref, dexp_ref, dy_ref, sp_ref,
             dxs_ref, db_ref, dc_ref, ddtr_ref, da_ref, dbias_ref, ddexp_ref, dstate_ref):
        g, b, c = pl.program_id(0), pl.program_id(1), pl.program_id(2)

        @pl.when(c == 0)
        def _():
            dstate_ref[...] = jnp.zeros_like(dstate_ref)

        @pl.when((b == 0) & (c == 0))
        def _():
            da_ref[...] = jnp.zeros_like(da_ref)
            dbias_ref[...] = jnp.zeros_like(dbias_ref)
            ddexp_ref[...] = jnp.zeros_like(ddexp_ref)

        t = _ssd_dt_terms(dtr_ref, bias_ref, alog_ref, g)
        li, si = t["li"], t["si"]
        tril, triu = li >= si, li <= si
        row_is_last = li[:, :1] == CHUNK - 1
        xs, dy = xs_ref[...], dy_ref[...]
        b16 = b_ref[...].astype(BF16)
        c16 = c_ref[...].astype(BF16)
        ct16 = c_ref[...].T.astype(BF16)
        cb = lax.dot_general(c16, b16, NT, preferred_element_type=F32)
        cbt = lax.dot_general(b16, c16, NT, preferred_element_type=F32)
        sp, ds = sp_ref[...], dstate_ref[...]
        sp16, ds16 = sp.astype(BF16), ds.astype(BF16)
        cs = jnp.dot(c16, sp16, preferred_element_type=F32)
        bds = jnp.dot(b16, ds16, preferred_element_type=F32)
        spds = jnp.sum(sp * ds, axis=0, keepdims=True)
        zero = jnp.zeros((CHUNK, CHUNK), F32)
        dsum_g, dsum_gt, dacs, ddt_dir = zero, zero, zero, zero
        dxs_parts, edy_parts, xw_parts, dsp_parts = [], [], [], []
        for r in range(HPG):
            sl = slice(r * SSM_HEAD_DIM, (r + 1) * SSM_HEAD_DIM)
            diff = t["acs"][:, r:r + 1] - t["acs_t"][r:r + 1, :]
            lam = jnp.exp(jnp.where(tril, diff, NEG))
            lam_t = jnp.exp(jnp.where(triu, -diff, NEG))
            m, m_t = cb * lam, cbt * lam_t
            xh, dyh = xs[:, sl], dy[:, sl]
            dtc, wc, ec, elc = t["dt"][:, r:r + 1], t["w_all"][:, r:r + 1], t["e_acs"][:, r:r + 1], t["e_last"][:, r:r + 1]
            xd = xh * dtc
            xd16, dyh16 = xd.astype(BF16), dyh.astype(BF16)
            dxd = jnp.dot(m_t.astype(BF16), dyh16, preferred_element_type=F32) + wc * bds[:, sl]
            dm = lax.dot_general(dyh16, xd16, NT, preferred_element_type=F32)
            dm_t = lax.dot_general(xd16, dyh16, NT, preferred_element_type=F32)
            dsum_g = dsum_g + dm * lam
            dsum_gt = dsum_gt + dm_t * lam_t
            dacs_h = jnp.sum(dm * m, axis=1, keepdims=True) - jnp.sum(dm_t * m_t, axis=1, keepdims=True)
            dacs_h = dacs_h + jnp.sum(dyh * (ec * cs[:, sl]), axis=1, keepdims=True)
            wdw = wc * jnp.sum(xd * bds[:, sl], axis=1, keepdims=True)
            last_h = jnp.sum(wdw, axis=0, keepdims=True) + elc[0:1, :] * jnp.sum(spds[:, sl], axis=1, keepdims=True)
            dacs_h = dacs_h - wdw + jnp.where(row_is_last, last_h, 0.0)
            onehot = (si[:1, :] == r).astype(F32)
            dacs = dacs + dacs_h * onehot
            ddt_dir = ddt_dir + jnp.sum(dxd * xh, axis=1, keepdims=True) * onehot
            dxs_parts.append(dxd * dtc + dyh * dexp_ref[:, sl])
            edy_parts.append(ec * dyh)
            xw_parts.append(wc * xd)
            dsp_parts.append(elc * ds[:, sl])
        dxs_ref[...] = jnp.concatenate(dxs_parts, axis=1)
        edy16 = jnp.concatenate(edy_parts, axis=1).astype(BF16)
        xw16 = jnp.concatenate(xw_parts, axis=1).astype(BF16)
        dc_ref[...] = (jnp.dot(dsum_g.astype(BF16), b16, preferred_element_type=F32)
                       + lax.dot_general(edy16, sp16, NT, preferred_element_type=F32))
        db_ref[...] = (jnp.dot(dsum_gt.astype(BF16), c16, preferred_element_type=F32)
                       + lax.dot_general(xw16, ds16, NT, preferred_element_type=F32))
        dstate_ref[...] = jnp.concatenate(dsp_parts, axis=1) + jnp.dot(ct16, edy16, preferred_element_type=F32)
        dadt = lax.dot_general(triu.astype(F32), dacs, NN, precision=lax.Precision.HIGHEST, preferred_element_type=F32)
        head_lane = si < HPG
        ddtr = jnp.where(head_lane, (dadt * t["a"] + ddt_dir) * jax.nn.sigmoid(t["x0"]), 0.0)
        ddtr_ref[...] = ddtr
        da_ref[...] += jnp.sum(jnp.where(head_lane, dadt * t["dt"] * t["a"], 0.0), axis=0, keepdims=True)
        dbias_ref[...] += jnp.sum(ddtr, axis=0, keepdims=True)
        ddexp_ref[...] += jnp.sum(dy * xs, axis=0, keepdims=True)

    return pl.pallas_call(
        body, name=name, grid=(SSM_GROUPS, bsz, nc),
        in_specs=[xs_s, bm_s, cm_s, dtr_s, vec_s, vec_s, dexp_s, xs_s, st_s],
        out_specs=[xs_s, grp, grp, grp, acc, acc, dexp_s],
        out_shape=[jax.ShapeDtypeStruct((bsz, seq, SSM_D_INNER), F32),
                   jax.ShapeDtypeStruct((bsz, seq, SSM_GROUPS * SSM_STATE), F32),
                   jax.ShapeDtypeStruct((bsz, seq, SSM_GROUPS * SSM_STATE), F32),
                   jax.ShapeDtypeStruct((bsz, seq, SSM_GROUPS * LANE), F32),
                   jax.ShapeDtypeStruct((SSM_GROUPS, 1, LANE), F32),
                   jax.ShapeDtypeStruct((SSM_GROUPS, 1, LANE), F32),
                   jax.ShapeDtypeStruct((1, SSM_D_INNER), F32)],
        scratch_shapes=[pltpu.VMEM((SSM_STATE, GW), F32)],
        compiler_params=_params(("arbitrary", "arbitrary", "arbitrary")),
    )(xc, xc, xc, dtr, dt_bias, a_log, d_exp, dy, sprev)


SB_TQ = 256
SB_TK = 128
SB_SCALE = SB_HEAD_DIM ** -0.5


def _split_dot(x, u16):
    hi = x.astype(BF16)
    lo = (x - hi.astype(F32)).astype(BF16)
    return jnp.dot(hi, u16, preferred_element_type=F32) + jnp.dot(lo, u16, preferred_element_type=F32)


def _sb_scores(qh, kj, mask):
    z = lax.dot_general(qh, kj, NT, preferred_element_type=F32)
    l1p = jnp.log(1.0 + jnp.exp(-jnp.abs(z)))
    lb = jnp.minimum(z, 0.0) - l1p
    lk = -jnp.maximum(z, 0.0) - l1p
    if mask is not None:
        lk = jnp.where(mask, lk, 0.0)
    return lb, lk


def _sb_diag_masks(tq):
    r = lax.broadcasted_iota(jnp.int32, (tq, SB_TK), 0)
    c = lax.broadcasted_iota(jnp.int32, (tq, SB_TK), 1)
    return [r > c + d * SB_TK for d in range(tq // SB_TK)]


def _sb_heads(x, lane):
    return [jnp.where((lane // SB_HEAD_DIM) == hh, x, jnp.zeros_like(x)) for hh in range(2)]


def sb_attention_fwd(q, kv, name):
    bsz, seq, width = q.shape
    tq = min(SB_TQ, seq)
    npair = width // LANE
    qspec = pl.BlockSpec((None, tq, LANE), lambda b, p, i: (b, i, p))
    kspec = pl.BlockSpec((None, seq, LANE), lambda b, p, i: (b, 0, p))
    vspec = pl.BlockSpec((None, seq, LANE), lambda b, p, i: (b, 0, npair + p))

    def body(q_ref, k_ref, v_ref, o_ref, l_ref):
        i = pl.program_id(2)
        ndiag = tq // SB_TK
        nfull = i * ndiag
        lane = lax.broadcasted_iota(jnp.int32, (tq, LANE), 1)
        ui = lax.broadcasted_iota(jnp.int32, (SB_TK, SB_TK), 0)
        uj = lax.broadcasted_iota(jnp.int32, (SB_TK, SB_TK), 1)
        u_excl = (ui > uj).astype(BF16)
        qs = _sb_heads(q_ref[...] * SB_SCALE, lane)

        def tile(j, carry, mask):
            rows = pl.ds(pl.multiple_of(j * SB_TK, SB_TK), SB_TK)
            kj, vj = k_ref[rows, :], v_ref[rows, :]
            new = []
            for hh in range(2):
                acc, run = carry[hh]
                lb, lk = _sb_scores(qs[hh], kj, mask)
                w = jnp.exp(lb + _split_dot(lk, u_excl) + run)
                if mask is not None:
                    w = jnp.where(mask, w, 0.0)
                acc = acc + jnp.dot(w.astype(BF16), vj, preferred_element_type=F32)
                new.append((acc, run + jnp.sum(lk, axis=1, keepdims=True)))
            return tuple(new)

        carry = tuple((jnp.zeros((tq, LANE), F32), jnp.zeros((tq, 1), F32)) for _ in range(2))
        masks = _sb_diag_masks(tq)
        for dd in reversed(range(ndiag)):
            carry = tile(nfull + dd, carry, masks[dd])
        carry = lax.fori_loop(0, nfull, lambda jj, cr: tile(nfull - 1 - jj, cr, None), carry)
        in0 = (lane // SB_HEAD_DIM) == 0
        o_ref[...] = jnp.where(in0, carry[0][0], carry[1][0]).astype(o_ref.dtype)
        l_ref[...] = jnp.where(in0, carry[0][1], carry[1][1])

    return pl.pallas_call(
        body, name=name, grid=(bsz, npair, seq // tq),
        in_specs=[qspec, kspec, vspec], out_specs=[qspec, qspec],
        out_shape=[jax.ShapeDtypeStruct(q.shape, BF16), jax.ShapeDtypeStruct(q.shape, F32)],
        compiler_params=_params(("parallel", "parallel", "arbitrary")),
    )(q, kv, kv)


def sb_attention_bwd(q, kv, ltot, do, dk_in, dv_in, name):
    bsz, seq, width = q.shape
    tq = min(SB_TQ, seq)
    npair = width // LANE
    qspec = pl.BlockSpec((None, tq, LANE), lambda b, p, i: (b, i, p))
    kspec = pl.BlockSpec((None, seq, LANE), lambda b, p, i: (b, 0, p))
    vspec = pl.BlockSpec((None, seq, LANE), lambda b, p, i: (b, 0, npair + p))

    def body(q_ref, k_ref, v_ref, l_ref, do_ref, dki_ref, dvi_ref, dq_ref, dk_ref, dv_ref):
        i = pl.program_id(2)

        @pl.when(i == 0)
        def _():
            dk_ref[...] = dki_ref[...]
            dv_ref[...] = dvi_ref[...]

        ndiag = tq // SB_TK
        nfull = i * ndiag
        lane = lax.broadcasted_iota(jnp.int32, (tq, LANE), 1)
        ui = lax.broadcasted_iota(jnp.int32, (SB_TK, SB_TK), 0)
        uj = lax.broadcasted_iota(jnp.int32, (SB_TK, SB_TK), 1)
        u_le = (ui <= uj).astype(BF16)
        u_lt = (ui < uj).astype(BF16)
        qs = _sb_heads(q_ref[...] * SB_SCALE, lane)
        dos = _sb_heads(do_ref[...], lane)
        ltots = [l_ref[:, hh * SB_HEAD_DIM:hh * SB_HEAD_DIM + 1] for hh in range(2)]

        def tile(j, carry, mask):
            rows = pl.ds(pl.multiple_of(j * SB_TK, SB_TK), SB_TK)
            kj, vj = k_ref[rows, :], v_ref[rows, :]
            new, dk_t, dv_t = [], None, None
            for hh in range(2):
                acc, run, run_a = carry[hh]
                lb, lk = _sb_scores(qs[hh], kj, mask)
                w = jnp.exp(lb + (ltots[hh] - run - _split_dot(lk, u_le)))
                if mask is not None:
                    w = jnp.where(mask, w, 0.0)
                a = lax.dot_general(dos[hh], vj, NT, preferred_element_type=F32) * w
                ca = run_a + _split_dot(a, u_lt)
                sg = jnp.exp(lb)
                dz = a * (1.0 - sg) - ca * sg
                if mask is not None:
                    dz = jnp.where(mask, dz, 0.0)
                dz16 = dz.astype(BF16)
                acc = acc + jnp.dot(dz16, kj, preferred_element_type=F32)
                dkh = lax.dot_general(dz16, qs[hh], TN, preferred_element_type=F32)
                dvh = lax.dot_general(w.astype(BF16), dos[hh], TN, preferred_element_type=F32)
                dk_t = dkh if dk_t is None else dk_t + dkh
                dv_t = dvh if dv_t is None else dv_t + dvh
                new.append((acc, run + jnp.sum(lk, axis=1, keepdims=True), run_a + jnp.sum(a, axis=1, keepdims=True)))
            dk_ref[rows, :] += dk_t
            dv_ref[rows, :] += dv_t
            return tuple(new)

        zcol = jnp.zeros((tq, 1), F32)
        carry = tuple((jnp.zeros((tq, LANE), F32), zcol, zcol) for _ in range(2))
        carry = lax.fori_loop(0, nfull, lambda j, cr: tile(j, cr, None), carry)
        masks = _sb_diag_masks(tq)
        for dd in range(ndiag):
            carry = tile(nfull + dd, carry, masks[dd])
        in0 = (lane // SB_HEAD_DIM) == 0
        dq_ref[...] = (jnp.where(in0, carry[0][0], carry[1][0]) * SB_SCALE).astype(dq_ref.dtype)

    return pl.pallas_call(
        body, name=name, grid=(bsz, npair, seq // tq),
        in_specs=[qspec, kspec, vspec, qspec, qspec, kspec, kspec], out_specs=[qspec, kspec, kspec],
        out_shape=[jax.ShapeDtypeStruct(q.shape, BF16), jax.ShapeDtypeStruct(q.shape, F32), jax.ShapeDtypeStruct(q.shape, F32)],
        compiler_params=_params(("parallel", "parallel", "arbitrary")),
    )(q, kv, kv, ltot, do, dk_in, dv_in)


def adamw(w, parts, m, v, name):
    r, c = w.shape
    tr = r
    for cand in (256, 176, 128, 64, 32, 16, 8):
        if r % cand == 0:
            tr = cand
            break
    bc1 = 1.0 - ADAM_B1 ** ADAM_STEP
    bc2 = 1.0 - ADAM_B2 ** ADAM_STEP

    def body(w_ref, p_ref, m_ref, v_ref, g_ref, d_ref, mo_ref, vo_ref):
        g = p_ref[0]
        for j in range(1, N_DEV):
            g = g + p_ref[j]
        mn = ADAM_B1 * m_ref[...] + (1.0 - ADAM_B1) * g
        vn = ADAM_B2 * v_ref[...] + (1.0 - ADAM_B2) * (g * g)
        g_ref[...] = g
        mo_ref[...] = mn
        vo_ref[...] = vn
        d_ref[...] = -ADAM_LR * ((mn / bc1) / (jnp.sqrt(vn / bc2) + ADAM_EPS) + ADAM_WD * w_ref[...])

    blk = pl.BlockSpec((tr, c), lambda i: (i, 0))
    pblk = pl.BlockSpec((N_DEV, tr, c), lambda i: (0, i, 0))
    shp = jax.ShapeDtypeStruct((r, c), F32)
    return pl.pallas_call(
        body, name=name, grid=(r // tr,), in_specs=[blk, pblk, blk, blk], out_specs=[blk, blk, blk, blk],
        out_shape=[shp, shp, shp, shp], compiler_params=_params(("parallel",)),
    )(w, parts, m, v)


MESH_ID = pl.DeviceIdType.MESH
ANY = pl.BlockSpec(memory_space=pl.ANY)


def all_gather(xs, name):
    n = len(xs)

    def body(*refs):
        x_refs, out_refs = refs[:n], refs[n:2 * n]
        send_sems, recv_sems, local_sems = refs[2 * n:]
        mx, my, mc = lax.axis_index("x"), lax.axis_index("y"), lax.axis_index("c")
        me, sibling = (mx, my, mc), (mx, my, 1 - mc)
        chips = [(1 - mx, my), (mx, 1 - my), (1 - mx, 1 - my)]

        def slot(a, px, py, pc):
            return out_refs[a].at[4 * px + 2 * py + pc]

        def copy(a, k, block, to, src=None):
            return pltpu.make_async_remote_copy(
                src_ref=slot(a, *block) if src is None else src, dst_ref=slot(a, *block),
                send_sem=send_sems.at[7 * a + k], recv_sem=recv_sems.at[7 * a + k], device_id=to, device_id_type=MESH_ID)

        mine = [pltpu.make_async_copy(x_refs[a], slot(a, *me), local_sems.at[a]) for a in range(n)]
        for cp in mine:
            cp.start()
        first = []
        for a in range(n):
            first.append(copy(a, 0, me, sibling, src=x_refs[a]))
            first += [copy(a, 1 + j, me, (*chip, mc), src=x_refs[a]) for j, chip in enumerate(chips)]
        for cp in first:
            cp.start()
        passed = []
        for j, chip in enumerate(chips):
            for a in range(n):
                copy(a, 1 + j, (*chip, mc), me).wait_recv()
                cp = copy(a, 4 + j, (*chip, mc), sibling)
                cp.start()
                passed.append(cp)
        for a in range(n):
            copy(a, 0, sibling, me).wait_recv()
            for j, chip in enumerate(chips):
                copy(a, 4 + j, (*chip, 1 - mc), me).wait_recv()
        for cp in first + passed:
            cp.wait_send()
        for cp in mine:
            cp.wait()

    return pl.pallas_call(
        body, name=name, out_shape=[jax.ShapeDtypeStruct((N_DEV,) + x.shape, x.dtype) for x in xs],
        in_specs=[ANY] * n, out_specs=[ANY] * n,
        scratch_shapes=[pltpu.SemaphoreType.DMA((7 * n,)), pltpu.SemaphoreType.DMA((7 * n,)), pltpu.SemaphoreType.DMA((n,))],
    )(*xs)


def all_to_all(ps, name):
    n = len(ps)

    def body(*refs):
        p_refs, out_refs = refs[:n], refs[n:2 * n]
        send_sems, recv_sems, local_sems = refs[2 * n:]
        mx, my, mc = lax.axis_index("x"), lax.axis_index("y"), lax.axis_index("c")
        me = 4 * mx + 2 * my + mc

        def peer_of(j):
            px = 1 - mx if j & 4 else mx
            py = 1 - my if j & 2 else my
            pc = 1 - mc if j & 1 else mc
            return (px, py, pc), 4 * px + 2 * py + pc

        def copy(a, j, dst_slot):
            dev, peer = peer_of(j)
            k = 7 * a + j - 1
            return pltpu.make_async_remote_copy(
                src_ref=p_refs[a].at[peer], dst_ref=out_refs[a].at[dst_slot], send_sem=send_sems.at[k],
                recv_sem=recv_sems.at[k], device_id=dev, device_id_type=MESH_ID)

        mine = [pltpu.make_async_copy(p_refs[a].at[me], out_refs[a].at[me], local_sems.at[a]) for a in range(n)]
        for cp in mine:
            cp.start()
        sends = [copy(a, j, me) for j in range(1, N_DEV) for a in range(n)]
        for cp in sends:
            cp.start()
        for j in range(1, N_DEV):
            for a in range(n):
                copy(a, j, peer_of(j)[1]).wait_recv()
        for cp in sends:
            cp.wait_send()
        for cp in mine:
            cp.wait()

    return pl.pallas_call(
        body, name=name, out_shape=[jax.ShapeDtypeStruct(p.shape, p.dtype) for p in ps],
        in_specs=[ANY] * n, out_specs=[ANY] * n,
        scratch_shapes=[pltpu.SemaphoreType.DMA((7 * n,)), pltpu.SemaphoreType.DMA((7 * n,)), pltpu.SemaphoreType.DMA((n,))],
    )(*ps)


PACK_COLS = 1024
PACK_SEG = 16 * PACK_COLS

BIG = {"ssm_in_proj": 2, "ssm_out_proj": 1, "w_kv": 1, "w_q": 1, "w_o": 1, "ffn_up": 2, "ffn_down": 1,
       "ple_gate": 1, "ple_proj": 2}
SMALL = {"ssm_conv_w": 2, "ssm_conv_b": 1, "ssm_norm": 1, "ffn_conv_w": 2}
REPL = ["attn_norm", "ffn_norm", "ple_norm", "ssm_dt_bias", "ssm_a_log", "ssm_d", "ffn_conv_b", "kv_norm", "final_norm"]


def _seg(n):
    return -(-n // PACK_SEG) * PACK_SEG


def _pack_rows(flats, lead=()):
    padded = [jnp.pad(f, [(0, 0)] * len(lead) + [(0, _seg(f.shape[-1]) - f.shape[-1])]) for f in flats]
    return jnp.concatenate(padded, axis=-1).reshape(*lead, -1, PACK_COLS)


def _unpack_rows(buf, sizes, lead):
    flat = buf.reshape(*lead, -1)
    out, off = [], 0
    for n in sizes:
        out.append(lax.slice_in_dim(flat, off, off + n, axis=len(lead)))
        off += _seg(n)
    return out


def _to_slots(full, axis):
    shp = full.shape
    blk = shp[axis] // N_DEV
    t = full.reshape(shp[:axis] + (N_DEV, blk) + shp[axis + 1:])
    return jnp.moveaxis(t, axis, 0).reshape(N_DEV, -1)


def _from_slots(slots, local_shape, axis):
    t = jnp.moveaxis(slots.reshape((N_DEV,) + tuple(local_shape)), 0, axis)
    shp = list(local_shape)
    shp[axis] *= N_DEV
    return t.reshape(shp)


def _as2d(a):
    return a.reshape(1, -1) if a.ndim == 1 else a.reshape(-1, a.shape[-1])


def _pad_rows(w, rows=SUBLANE):
    return jnp.pad(w, ((0, rows - w.shape[0]), (0, 0)))


def kernel(x, p, attn_norm, ffn_norm, ple_norm, ssm_in_proj, ssm_conv_w, ssm_conv_b, ssm_dt_bias, ssm_a_log, ssm_d, ssm_norm, ssm_out_proj, kv_norm, w_kv, w_q, w_o, ffn_up, ffn_conv_w, ffn_conv_b, ffn_down, ple_gate, ple_proj, final_norm, loss_target, m_attn_norm, m_ffn_norm, m_ple_norm, m_ssm_in_proj, m_ssm_conv_w, m_ssm_conv_b, m_ssm_dt_bias, m_ssm_a_log, m_ssm_d, m_ssm_norm, m_ssm_out_proj, m_kv_norm, m_w_kv, m_w_q, m_w_o, m_ffn_up, m_ffn_conv_w, m_ffn_conv_b, m_ffn_down, m_ple_gate, m_ple_proj, m_final_norm, v_attn_norm, v_ffn_norm, v_ple_norm, v_ssm_in_proj, v_ssm_conv_w, v_ssm_conv_b, v_ssm_dt_bias, v_ssm_a_log, v_ssm_d, v_ssm_norm, v_ssm_out_proj, v_kv_norm, v_w_kv, v_w_q, v_w_o, v_ffn_up, v_ffn_conv_w, v_ffn_conv_b, v_ffn_down, v_ple_gate, v_ple_proj, v_final_norm):
    given = dict(locals())
    wnames = list(BIG) + list(SMALL) + REPL
    bsz, seq, d = x.shape
    t = bsz * seq
    bs = lambda a: a.reshape(bsz, seq, a.shape[-1])
    fl = lambda a: a.reshape(t, a.shape[-1])

    big_local = [given[n] for n in BIG]
    small_local = [given[n] for n in SMALL]
    gathered = all_gather([w.astype(BF16) for w in big_local] + [_pack_rows([w.reshape(-1) for w in small_local])], "ag_weights")
    gbig = dict(zip(BIG, gathered))

    def rows_of(name, i):
        g = gbig[name][:, i]
        return g.reshape(-1, g.shape[-1])

    def cols_of(name, i, lo=0, hi=N_DEV):
        g = gbig[name][lo:hi, i]
        return jnp.transpose(g, (1, 0, 2)).reshape(g.shape[1], -1)

    full = {"w_kv": jnp.transpose(gbig["w_kv"], (1, 0, 2)).reshape(d, -1)}
    for n in ("ssm_out_proj", "w_q", "w_o", "ffn_down", "ple_gate"):
        full[n] = [rows_of(n, i) for i in range(given[n].shape[0])]
    full["ple_proj"] = [cols_of("ple_proj", i) for i in range(DEPTH)]
    full["ssm_in_proj"] = [cols_of("ssm_in_proj", i) for i in range(N_A)]
    for (n, ax), w, s in zip(SMALL.items(), small_local, _unpack_rows(gathered[-1], [w.size for w in small_local], (N_DEV,))):
        full[n] = _from_slots(s, w.shape, ax)

    row = lambda v: v.reshape(1, -1)
    pad_lane = lambda v: jnp.pad(v.reshape(1, -1), ((0, 0), (0, LANE - v.size)))
    h = x.reshape(t, d)
    saved = []
    kv = hnkv = h_kv = None
    for i in range(DEPTH):
        sv = {"h0": h}
        hn = rmsnorm_fwd(h, row(attn_norm[i]), f"attn_norm_f{i}")
        sv["hn"] = hn
        if i < N_A:
            w_in = full["ssm_in_proj"][i]
            wz, wxbc = w_in[:, :SSM_D_INNER], w_in[:, SSM_D_INNER:SSM_ZX]
            wdt = jnp.pad(w_in[:, SSM_ZX:], ((0, 0), (0, LANE - SSM_HEADS)))
            z = matmul(hn, wz, "nn", F32, f"ssm_z_f{i}")
            xbc = bs(matmul(hn, wxbc, "nn", F32, f"ssm_xbc_f{i}"))
            dtr = bs(matmul(hn, wdt, "nn", F32, f"ssm_dt_f{i}"))
            cw = _pad_rows(full["ssm_conv_w"][i])
            cb = row(full["ssm_conv_b"][i])
            xc = ssm_conv_fwd(xbc, cw, cb, f"ssm_conv_f{i}")
            dtb, alog = pad_lane(ssm_dt_bias[i]), pad_lane(ssm_a_log[i])
            dexp = jnp.repeat(ssm_d[i], SSM_HEAD_DIM).reshape(1, -1)
            y, sprev = ssd_fwd(xc, dtr, dtb, alog, dexp, f"ssd_f{i}")
            gn = gated_norm_fwd(fl(y), z, row(full["ssm_norm"][i]), f"ssm_gnorm_f{i}")
            h1 = matmul(gn, full["ssm_out_proj"][i], "nn", F32, f"ssm_out_f{i}", add=h)
            sv.update(wz=wz, wxbc=wxbc, wdt=wdt, z=z, xbc=xbc, dtr=dtr, cw=cw, cb=cb, xc=xc, dtb=dtb, alog=alog,
                      dexp=dexp, y=y, sprev=sprev, gn=gn)
        else:
            j = i - N_A
            q = bs(matmul(hn, full["w_q"][j], "nn", BF16, f"sb_q_f{j}"))
            o, ltot = sb_attention_fwd(q, kv, f"sb_attn_f{j}")
            h1 = matmul(fl(o), full["w_o"][j], "nn", F32, f"sb_o_f{j}", add=h)
            sv.update(q=q, o=o, ltot=ltot)
        sv["h1"] = h1
        hn2 = rmsnorm_fwd(h1, row(ffn_norm[i]), f"ffn_norm_f{i}")
        wug, wuv = cols_of("ffn_up", i, 0, N_DEV // 2), cols_of("ffn_up", i, N_DEV // 2, N_DEV)
        ug = bs(matmul(hn2, wug, "nn", F32, f"ffn_upg_f{i}"))
        uv = bs(matmul(hn2, wuv, "nn", F32, f"ffn_upv_f{i}"))
        fcw = full["ffn_conv_w"][i]
        fwg, fwv = _pad_rows(fcw[:, :D_FF]), _pad_rows(fcw[:, D_FF:])
        fbg, fbv = row(ffn_conv_b[i, :D_FF]), row(ffn_conv_b[i, D_FF:])
        act = ffn_conv_fwd(ug, uv, fwg, fwv, fbg, fbv, f"ffn_conv_f{i}")
        h2 = matmul(fl(act), full["ffn_down"][i], "nn", F32, f"ffn_down_f{i}", add=h1)
        hn3 = rmsnorm_fwd(h2, row(ple_norm[i]), f"ple_norm_f{i}")
        gpre = matmul(hn3, full["ple_gate"][i], "nn", F32, f"ple_gate_f{i}")
        p_i = p[i].reshape(t, PLE_DIM)
        pp = matmul(p_i, full["ple_proj"][i], "nn", F32, f"ple_proj_f{i}")
        h3 = ple_fwd(h2, gpre, pp, f"ple_f{i}")
        sv.update(hn2=hn2, wug=wug, wuv=wuv, ug=ug, uv=uv, fwg=fwg, fwv=fwv, fbg=fbg, fbv=fbv, act=act, h2=h2,
                  hn3=hn3, gpre=gpre, pp=pp, p_i=p_i)
        if i == N_A - 1:
            h_kv = h3
            hnkv = rmsnorm_fwd(h3, row(kv_norm), "kv_norm_f")
            kv = bs(matmul(hnkv, full["w_kv"], "nn", BF16, "kv_f"))
        h = h3
        saved.append(sv)

    loss_row, dh, g_final = loss_head(h, row(final_norm), loss_target.reshape(t, d), "loss_head")
    loss = lax.psum(loss_row[0, 0], ("x", "y", "c"))

    G = {n: [None] * given[n].shape[0] for n in wnames if given[n].ndim > 1 and n not in ("w_kv",)}
    G["final_norm"] = g_final[0]
    dk = jnp.zeros((bsz, seq, SB_WIDTH), F32)
    dv = jnp.zeros((bsz, seq, SB_WIDTH), F32)
    for i in reversed(range(DEPTH)):
        sv = saved[i]
        if i == N_A - 1:
            wk, wv = full["w_kv"][:, :SB_WIDTH], full["w_kv"][:, SB_WIDTH:]
            dkf, dvf = fl(dk), fl(dv)
            dhn = matmul(dkf, wk, "nt", F32, "kv_dx_k")
            dhn = matmul(dvf, wv, "nt", F32, "kv_dx_v", add=dhn)
            G["w_kv"] = jnp.concatenate([matmul(hnkv, dkf, "tn", F32, "kv_dw_k"), matmul(hnkv, dvf, "tn", F32, "kv_dw_v")], axis=1)
            dh, gk = rmsnorm_bwd(dhn, h_kv, row(kv_norm), dh, "kv_norm_b")
            G["kv_norm"] = gk[0]
        dgpre, dpp = ple_bwd(dh, sv["gpre"], sv["pp"], f"ple_b{i}")
        G["ple_proj"][i] = matmul(sv["p_i"], dpp, "tn", F32, f"ple_proj_dw{i}")
        G["ple_gate"][i] = matmul(sv["hn3"], dgpre, "tn", F32, f"ple_gate_dw{i}")
        dhn = matmul(dgpre, full["ple_gate"][i], "nt", F32, f"ple_gate_dx{i}")
        dh, gn_ = rmsnorm_bwd(dhn, sv["h2"], row(ple_norm[i]), dh, f"ple_norm_b{i}")
        G["ple_norm"][i] = gn_[0]
        dact = bs(matmul(dh, full["ffn_down"][i], "nt", F32, f"ffn_down_dx{i}"))
        G["ffn_down"][i] = matmul(fl(sv["act"]), dh, "tn", F32, f"ffn_down_dw{i}")
        dug, duv, dwg, dwv, dbg, dbv = ffn_conv_bwd(sv["ug"], sv["uv"], dact, sv["fwg"], sv["fwv"], sv["fbg"], sv["fbv"], f"ffn_conv_b{i}")
        dug, duv = fl(dug), fl(duv)
        G["ffn_conv_w"][i] = jnp.concatenate([dwg[:FFN_CONV], dwv[:FFN_CONV]], axis=1)
        G["ffn_conv_b"][i] = jnp.concatenate([dbg[0], dbv[0]])
        G["ffn_up"][i] = (matmul(sv["hn2"], dug, "tn", F32, f"ffn_upg_dw{i}"), matmul(sv["hn2"], duv, "tn", F32, f"ffn_upv_dw{i}"))
        dhn = matmul(dug, sv["wug"], "nt", F32, f"ffn_upg_dx{i}")
        dhn = matmul(duv, sv["wuv"], "nt", F32, f"ffn_upv_dx{i}", add=dhn)
        dh, gn_ = rmsnorm_bwd(dhn, sv["h1"], row(ffn_norm[i]), dh, f"ffn_norm_b{i}")
        G["ffn_norm"][i] = gn_[0]
        if i < N_A:
            dgn = matmul(dh, full["ssm_out_proj"][i], "nt", F32, f"ssm_out_dx{i}")
            G["ssm_out_proj"][i] = matmul(sv["gn"], dh, "tn", F32, f"ssm_out_dw{i}")
            dy, dz, dnw = gated_norm_bwd(dgn, fl(sv["y"]), sv["z"], row(full["ssm_norm"][i]), f"ssm_gnorm_b{i}")
            G["ssm_norm"][i] = dnw[0]
            dxs, dbm, dcm, ddtr4, da4, dbias4, ddexp = ssd_bwd(sv["xc"], sv["dtr"], sv["dtb"], sv["alog"], sv["dexp"], bs(dy), sv["sprev"], f"ssd_b{i}")
            G["ssm_a_log"][i] = da4[:, 0, :HPG].reshape(-1)
            G["ssm_dt_bias"][i] = dbias4[:, 0, :HPG].reshape(-1)
            G["ssm_d"][i] = ddexp.reshape(SSM_HEADS, SSM_HEAD_DIM).sum(axis=-1)
            dxc = jnp.concatenate([dxs, dbm, dcm], axis=-1)
            dxbc, dcw, dcb = ssm_conv_bwd(sv["xbc"], dxc, sv["cw"], sv["cb"], f"ssm_conv_b{i}")
            dxbc = fl(dxbc)
            G["ssm_conv_w"][i] = dcw[:SSM_CONV]
            G["ssm_conv_b"][i] = dcb[0]
            ddtr = ddtr4.reshape(t, SSM_GROUPS, LANE)[:, :, :HPG].reshape(t, SSM_HEADS)
            ddtr = jnp.pad(ddtr, ((0, 0), (0, LANE - SSM_HEADS))).astype(BF16)
            hn = sv["hn"]
            G["ssm_in_proj"][i] = jnp.concatenate([
                matmul(hn, dz, "tn", F32, f"ssm_z_dw{i}"), matmul(hn, dxbc, "tn", F32, f"ssm_xbc_dw{i}"),
                matmul(hn, ddtr, "tn", F32, f"ssm_dt_dw{i}")[:, :SSM_HEADS]], axis=1)
            dhn = matmul(dz, sv["wz"], "nt", F32, f"ssm_z_dx{i}")
            dhn = matmul(dxbc, sv["wxbc"], "nt", F32, f"ssm_xbc_dx{i}", add=dhn)
            dhn = matmul(ddtr, sv["wdt"], "nt", F32, f"ssm_dt_dx{i}", add=dhn)
        else:
            j = i - N_A
            do = bs(matmul(dh, full["w_o"][j], "nt", BF16, f"sb_o_dx{j}"))
            G["w_o"][j] = matmul(fl(sv["o"]), dh, "tn", F32, f"sb_o_dw{j}")
            dq, dk, dv = sb_attention_bwd(sv["q"], kv, sv["ltot"], do, dk, dv, f"sb_attn_b{j}")
            dq = fl(dq)
            G["w_q"][j] = matmul(sv["hn"], dq, "tn", F32, f"sb_q_dw{j}")
            dhn = matmul(dq, full["w_q"][j], "nt", F32, f"sb_q_dx{j}")
        dh, gn_ = rmsnorm_bwd(dhn, sv["h0"], row(attn_norm[i]), dh, f"attn_norm_b{i}")
        G["attn_norm"][i] = gn_[0]
    grad_x = dh.reshape(bsz, seq, d)
    grads = {n: jnp.stack(G[n]) if isinstance(G[n], list) else G[n] for n in list(SMALL) + REPL}

    def col_slots(g, ndev=N_DEV):
        return jnp.transpose(g.reshape(g.shape[0], ndev, -1), (1, 0, 2))

    slots = {n: jnp.stack([g.reshape(N_DEV, -1, g.shape[-1]) for g in G[n]], axis=1)
             for n in ("ssm_out_proj", "w_q", "w_o", "ffn_down", "ple_gate")}
    slots["ssm_in_proj"] = jnp.stack([col_slots(g) for g in G["ssm_in_proj"]], axis=1)
    slots["ple_proj"] = jnp.stack([col_slots(g) for g in G["ple_proj"]], axis=1)
    slots["ffn_up"] = jnp.stack([jnp.concatenate([col_slots(gg, N_DEV // 2), col_slots(gv, N_DEV // 2)], axis=0)
                                 for gg, gv in G["ffn_up"]], axis=1)
    slots["w_kv"] = col_slots(G["w_kv"])
    small_slots = _pack_rows([_to_slots(grads[n], SMALL[n]) for n in SMALL], (N_DEV,))
    exchanged = all_to_all([slots[n] for n in BIG] + [small_slots], "a2a_grads")
    parts_rp = all_gather([_pack_rows([grads[n].reshape(-1) for n in REPL])], "ag_repl_grads")[0]
    sh_names = list(BIG) + list(SMALL)
    sh_parts = list(exchanged[:-1]) + _unpack_rows(exchanged[-1], [given[n].size for n in SMALL], (N_DEV,))
    rp_parts = _unpack_rows(parts_rp, [given[n].size for n in REPL], (N_DEV,))
    res = {}
    for n, pr in list(zip(sh_names, sh_parts)) + list(zip(REPL, rp_parts)):
        w2 = _as2d(given[n])
        outs = adamw(w2, pr.reshape((N_DEV,) + w2.shape), _as2d(given["m_" + n]), _as2d(given["v_" + n]), f"adamw_{n}")
        res[n] = [o.reshape(given[n].shape) for o in outs]
    order = ["attn_norm", "ffn_norm", "ple_norm", "ssm_in_proj", "ssm_conv_w", "ssm_conv_b", "ssm_dt_bias", "ssm_a_log",
             "ssm_d", "ssm_norm", "ssm_out_proj", "kv_norm", "w_kv", "w_q", "w_o", "ffn_up", "ffn_conv_w", "ffn_conv_b",
             "ffn_down", "ple_gate", "ple_proj", "final_norm"]
    return (loss, grad_x, *[res[n][0] for n in order], *[res[n][1] for n in order],
            *[res[n][2] for n in order], *[res[n][3] for n in order])
```

```python
import functools
import math

import jax
import jax.numpy as jnp
from jax import lax
from jax.experimental import pallas as pl
from jax.experimental.pallas import tpu as pltpu

F32 = jnp.float32
BF16 = jnp.bfloat16

N_DEV = 8
D_MODEL = 1024
SEQ = 2048
DEPTH = 4
N_A = 2
N_B = 2
SSM_D_INNER = 2048
SSM_HEAD_DIM = 64
SSM_HEADS = 32
SSM_GROUPS = 4
SSM_STATE = 128
SSM_CONV = 4
SSM_CHUNK = 128
SSM_CONV_DIM = 3072
SSM_ZX = 5120
SSM_IN_DIM = 5152
SB_HEADS = 16
SB_HEAD_DIM = 64
SB_WIDTH = 1024
D_FF = 2816
FFN_CONV = 3
PLE_DIM = 256
NORM_EPS = 1e-6
SSM_NORM_EPS = 1e-5

ADAM_LR = 0.001
ADAM_B1 = 0.9
ADAM_B2 = 0.999
ADAM_EPS = 1e-08
ADAM_WD = 0.01
ADAM_STEP = 10

LANE = 128
SUBLANE = 8
VMEM_LIMIT = 48 * 1024 * 1024

NN = (((1,), (0,)), ((), ()))
NT = (((1,), (1,)), ((), ()))
TN = (((0,), (0,)), ((), ()))


def _params(sem):
    return pltpu.CompilerParams(dimension_semantics=sem, vmem_limit_bytes=VMEM_LIMIT)


def _pick(n, prefs):
    for p in prefs:
        if n % p == 0:
            return p
    return n


def matmul(a, b, mode, out_dtype, name, add=None):
    if mode == "nn":
        (m, k), (k2, n) = a.shape, b.shape
    elif mode == "nt":
        (m, k), (n, k2) = a.shape, b.shape
    else:
        (k, m), (k2, n) = a.shape, b.shape
    assert k == k2, (a.shape, b.shape, mode)
    tm = _pick(m, (512, 256, 128))
    tn = _pick(n, (1408, 1024, 512, 256, 128))
    tk = _pick(k, (2816, 2048, 1024, 512, 256, 128))
    nk = k // tk
    dims = {"nn": NN, "nt": NT, "tn": TN}[mode]

    def body(*refs):
        a_ref, b_ref = refs[:2]
        add_ref = refs[2] if add is not None else None
        o_ref = refs[-2] if nk > 1 else refs[-1]
        part = lax.dot_general(a_ref[...].astype(BF16), b_ref[...].astype(BF16), dims, preferred_element_type=F32)

        def finish(r):
            if add_ref is not None:
                r = r + add_ref[...].astype(F32)
            o_ref[...] = r.astype(o_ref.dtype)

        if nk == 1:
            finish(part)
            return
        acc_ref = refs[-1]
        kk = pl.program_id(2)

        @pl.when(kk == 0)
        def _():
            acc_ref[...] = part

        @pl.when((kk > 0) & (kk < nk - 1))
        def _():
            acc_ref[...] += part

        @pl.when(kk == nk - 1)
        def _():
            finish(acc_ref[...] + part)

    if mode == "nn":
        a_spec = pl.BlockSpec((tm, tk), lambda i, j, kk: (i, kk))
        b_spec = pl.BlockSpec((tk, tn), lambda i, j, kk: (kk, j))
    elif mode == "nt":
        a_spec = pl.BlockSpec((tm, tk), lambda i, j, kk: (i, kk))
        b_spec = pl.BlockSpec((tn, tk), lambda i, j, kk: (j, kk))
    else:
        a_spec = pl.BlockSpec((tk, tm), lambda i, j, kk: (kk, i))
        b_spec = pl.BlockSpec((tk, tn), lambda i, j, kk: (kk, j))
    o_spec = pl.BlockSpec((tm, tn), lambda i, j, kk: (i, j))
    in_specs = [a_spec, b_spec] + ([o_spec] if add is not None else [])
    args = (a, b) + ((add,) if add is not None else ())
    return pl.pallas_call(
        body,
        name=name,
        grid=(m // tm, n // tn, nk),
        in_specs=in_specs,
        out_specs=o_spec,
        out_shape=jax.ShapeDtypeStruct((m, n), out_dtype),
        scratch_shapes=[pltpu.VMEM((tm, tn), F32)] if nk > 1 else [],
        compiler_params=_params(("parallel", "parallel", "arbitrary")),
    )(*args)


ROW_TILE = 512


def rmsnorm_fwd(x, gain, name):
    t, d = x.shape

    def body(x_ref, g_ref, o_ref):
        xv = x_ref[...]
        r = lax.rsqrt(jnp.mean(xv * xv, axis=-1, keepdims=True) + NORM_EPS)
        o_ref[...] = (xv * r * g_ref[...]).astype(o_ref.dtype)

    return pl.pallas_call(
        body,
        name=name,
        grid=(t // ROW_TILE,),
        in_specs=[pl.BlockSpec((ROW_TILE, d), lambda i: (i, 0)), pl.BlockSpec((1, d), lambda i: (0, 0))],
        out_specs=pl.BlockSpec((ROW_TILE, d), lambda i: (i, 0)),
        out_shape=jax.ShapeDtypeStruct((t, d), BF16),
        compiler_params=_params(("parallel",)),
    )(x, gain)


def rmsnorm_bwd(dy, x, gain, dres, name):
    t, d = x.shape

    def body(dy_ref, x_ref, g_ref, dres_ref, dx_ref, dg_ref):
        i = pl.program_id(0)
        xv = x_ref[...]
        r = lax.rsqrt(jnp.mean(xv * xv, axis=-1, keepdims=True) + NORM_EPS)
        xh = xv * r
        dyv = dy_ref[...].astype(F32)
        dxh = dyv * g_ref[...]
        dx = r * (dxh - xh * jnp.mean(dxh * xh, axis=-1, keepdims=True))
        dx_ref[...] = dres_ref[...] + dx

        @pl.when(i == 0)
        def _():
            dg_ref[...] = jnp.zeros_like(dg_ref)

        dg_ref[...] += jnp.sum(dyv * xh, axis=0, keepdims=True)

    row = pl.BlockSpec((ROW_TILE, d), lambda i: (i, 0))
    vec = pl.BlockSpec((1, d), lambda i: (0, 0))
    return pl.pallas_call(
        body,
        name=name,
        grid=(t // ROW_TILE,),
        in_specs=[row, row, vec, row],
        out_specs=[row, vec],
        out_shape=[jax.ShapeDtypeStruct((t, d), F32), jax.ShapeDtypeStruct((1, d), F32)],
        compiler_params=_params(("arbitrary",)),
    )(dy, x, gain, dres)


CONV_ROWS = 512
HALO = SUBLANE


def _conv_apply(ext, w_ref, bias, k, lo, n):
    acc = bias + w_ref[k - 1:k, :] * ext[lo:lo + n]
    for j in range(1, k):
        acc = acc + w_ref[k - 1 - j:k - j, :] * pltpu.roll(ext, j, 0)[lo:lo + n]
    return acc


def _conv_apply_t(ext, w_ref, k, n):
    rows = ext.shape[0]
    acc = w_ref[k - 1:k, :] * ext[:n]
    for j in range(1, k):
        acc = acc + w_ref[k - 1 - j:k - j, :] * pltpu.roll(ext, rows - j, 0)[:n]
    return acc


def _conv_dw(dpre_main, ext_u, k, n, dw_ref):
    for kk in range(k):
        j = k - 1 - kk
        sh = ext_u[HALO:HALO + n] if j == 0 else pltpu.roll(ext_u, j, 0)[HALO:HALO + n]
        dw_ref[kk:kk + 1, :] += jnp.sum(dpre_main * sh, axis=0, keepdims=True)


def _silu_grad(pre):
    sg = jax.nn.sigmoid(pre)
    return sg * (1.0 + pre * (1.0 - sg))


def _conv_blockspecs(seq, r, tc):
    nh = r // HALO
    last = seq // HALO - 1
    main = pl.BlockSpec((None, r, tc), lambda c, b, s: (b, s, c))
    before = pl.BlockSpec((None, HALO, tc), lambda c, b, s: (b, jnp.maximum(s * nh - 1, 0), c))
    after = pl.BlockSpec((None, HALO, tc), lambda c, b, s: (b, jnp.minimum((s + 1) * nh, last), c))
    return main, before, after


def ssm_conv_fwd(u, w, bias, name):
    bsz, seq, ch = u.shape
    r, tc, k = min(CONV_ROWS, seq), 512, SSM_CONV
    main, before, _ = _conv_blockspecs(seq, r, tc)
    wspec = pl.BlockSpec((SUBLANE, tc), lambda c, b, s: (0, c))
    bspec = pl.BlockSpec((1, tc), lambda c, b, s: (0, c))

    def body(u_ref, hb_ref, w_ref, b_ref, o_ref):
        s = pl.program_id(2)
        hb = jnp.where(s == 0, 0.0, hb_ref[...])
        ext = jnp.concatenate([hb, u_ref[...]], axis=0)
        pre = _conv_apply(ext, w_ref, b_ref[...], k, HALO, r)
        o_ref[...] = pre * jax.nn.sigmoid(pre)

    return pl.pallas_call(
        body, name=name, grid=(ch // tc, bsz, seq // r),
        in_specs=[main, before, wspec, bspec], out_specs=main,
        out_shape=jax.ShapeDtypeStruct(u.shape, F32),
        compiler_params=_params(("parallel", "parallel", "parallel")),
    )(u, u, w, bias)


def ssm_conv_bwd(u, dxc, w, bias, name):
    bsz, seq, ch = u.shape
    r, tc, k = min(CONV_ROWS, seq), 512, SSM_CONV
    ns = seq // r
    main, before, after = _conv_blockspecs(seq, r, tc)
    wspec = pl.BlockSpec((SUBLANE, tc), lambda c, b, s: (0, c))
    bspec = pl.BlockSpec((1, tc), lambda c, b, s: (0, c))

    def body(u_ref, hb_ref, ha_ref, d_ref, da_ref, w_ref, b_ref, du_ref, dw_ref, db_ref):
        b, s = pl.program_id(1), pl.program_id(2)
        hb = jnp.where(s == 0, 0.0, hb_ref[...])
        ext_u = jnp.concatenate([hb, u_ref[...], ha_ref[...]], axis=0)
        pre = _conv_apply(ext_u, w_ref, b_ref[...], k, HALO, r + HALO)
        dxe = jnp.concatenate([d_ref[...], jnp.where(s == ns - 1, 0.0, da_ref[...])], axis=0)
        dpre = dxe * _silu_grad(pre)
        du_ref[...] = _conv_apply_t(dpre, w_ref, k, r).astype(du_ref.dtype)

        @pl.when((b == 0) & (s == 0))
        def _():
            dw_ref[...] = jnp.zeros_like(dw_ref)
            db_ref[...] = jnp.zeros_like(db_ref)

        dpm = dpre[:r]
        _conv_dw(dpm, ext_u, k, r, dw_ref)
        db_ref[...] += jnp.sum(dpm, axis=0, keepdims=True)

    return pl.pallas_call(
        body, name=name, grid=(ch // tc, bsz, ns),
        in_specs=[main, before, after, main, after, wspec, bspec],
        out_specs=[main, wspec, bspec],
        out_shape=[jax.ShapeDtypeStruct(u.shape, BF16), jax.ShapeDtypeStruct((SUBLANE, ch), F32),
                   jax.ShapeDtypeStruct((1, ch), F32)],
        compiler_params=_params(("arbitrary", "arbitrary", "arbitrary")),
    )(u, u, u, dxc, dxc, w, bias)


def ffn_conv_fwd(ug, uv, wg, wv, bg, bv, name):
    bsz, seq, ch = ug.shape
    r, tc, k = min(CONV_ROWS, seq), 256, FFN_CONV
    main, before, _ = _conv_blockspecs(seq, r, tc)
    wspec = pl.BlockSpec((SUBLANE, tc), lambda c, b, s: (0, c))
    bspec = pl.BlockSpec((1, tc), lambda c, b, s: (0, c))

    def body(ug_ref, hg_ref, uv_ref, hv_ref, wg_ref, wv_ref, bg_ref, bv_ref, o_ref):
        s = pl.program_id(2)
        eg = jnp.concatenate([jnp.where(s == 0, 0.0, hg_ref[...]), ug_ref[...]], axis=0)
        ev = jnp.concatenate([jnp.where(s == 0, 0.0, hv_ref[...]), uv_ref[...]], axis=0)
        pg = _conv_apply(eg, wg_ref, bg_ref[...], k, HALO, r)
        pv = _conv_apply(ev, wv_ref, bv_ref[...], k, HALO, r)
        o_ref[...] = (pg * jax.nn.sigmoid(pg) * pv).astype(o_ref.dtype)

    return pl.pallas_call(
        body, name=name, grid=(ch // tc, bsz, seq // r),
        in_specs=[main, before, main, before, wspec, wspec, bspec, bspec], out_specs=main,
        out_shape=jax.ShapeDtypeStruct(ug.shape, BF16),
        compiler_params=_params(("parallel", "parallel", "parallel")),
    )(ug, ug, uv, uv, wg, wv, bg, bv)


def ffn_conv_bwd(ug, uv, dact, wg, wv, bg, bv, name):
    bsz, seq, ch = ug.shape
    r, tc, k = min(CONV_ROWS, seq), 256, FFN_CONV
    ns = seq // r
    main, before, after = _conv_blockspecs(seq, r, tc)
    wspec = pl.BlockSpec((SUBLANE, tc), lambda c, b, s: (0, c))
    bspec = pl.BlockSpec((1, tc), lambda c, b, s: (0, c))

    def body(ug_ref, gb_ref, ga_ref, uv_ref, vb_ref, va_ref, d_ref, da_ref, wg_ref, wv_ref, bg_ref, bv_ref,
             dug_ref, duv_ref, dwg_ref, dwv_ref, dbg_ref, dbv_ref):
        b, s = pl.program_id(1), pl.program_id(2)
        eg = jnp.concatenate([jnp.where(s == 0, 0.0, gb_ref[...]), ug_ref[...], ga_ref[...]], axis=0)
        ev = jnp.concatenate([jnp.where(s == 0, 0.0, vb_ref[...]), uv_ref[...], va_ref[...]], axis=0)
        pg = _conv_apply(eg, wg_ref, bg_ref[...], k, HALO, r + HALO)
        pv = _conv_apply(ev, wv_ref, bv_ref[...], k, HALO, r + HALO)
        de = jnp.concatenate([d_ref[...], jnp.where(s == ns - 1, 0.0, da_ref[...])], axis=0)
        sg = jax.nn.sigmoid(pg)
        dpg = de * pv * (sg * (1.0 + pg * (1.0 - sg)))
        dpv = de * (pg * sg)
        dug_ref[...] = _conv_apply_t(dpg, wg_ref, k, r).astype(dug_ref.dtype)
        duv_ref[...] = _conv_apply_t(dpv, wv_ref, k, r).astype(duv_ref.dtype)

        @pl.when((b == 0) & (s == 0))
        def _():
            dwg_ref[...] = jnp.zeros_like(dwg_ref)
            dwv_ref[...] = jnp.zeros_like(dwv_ref)
            dbg_ref[...] = jnp.zeros_like(dbg_ref)
            dbv_ref[...] = jnp.zeros_like(dbv_ref)

        _conv_dw(dpg[:r], eg, k, r, dwg_ref)
        _conv_dw(dpv[:r], ev, k, r, dwv_ref)
        dbg_ref[...] += jnp.sum(dpg[:r], axis=0, keepdims=True)
        dbv_ref[...] += jnp.sum(dpv[:r], axis=0, keepdims=True)

    wshape = jax.ShapeDtypeStruct((SUBLANE, ch), F32)
    bshape = jax.ShapeDtypeStruct((1, ch), F32)
    return pl.pallas_call(
        body, name=name, grid=(ch // tc, bsz, ns),
        in_specs=[main, before, after, main, before, after, main, after, wspec, wspec, bspec, bspec],
        out_specs=[main, main, wspec, wspec, bspec, bspec],
        out_shape=[jax.ShapeDtypeStruct(ug.shape, BF16), jax.ShapeDtypeStruct(ug.shape, BF16), wshape, wshape, bshape, bshape],
        compiler_params=_params(("arbitrary", "arbitrary", "arbitrary")),
    )(ug, ug, ug, uv, uv, uv, dact, dact, wg, wv, bg, bv)


GN_ROWS = 256
GN_GROUP = SSM_D_INNER // SSM_GROUPS


def gated_norm_fwd(y, z, w, name):
    t, d = y.shape

    def body(y_ref, z_ref, w_ref, o_ref):
        for g in range(d // GN_GROUP):
            sl = slice(g * GN_GROUP, (g + 1) * GN_GROUP)
            zv = z_ref[:, sl]
            gv = y_ref[:, sl] * (zv * jax.nn.sigmoid(zv))
            r = lax.rsqrt(jnp.mean(gv * gv, axis=-1, keepdims=True) + SSM_NORM_EPS)
            o_ref[:, sl] = (gv * r * w_ref[:, sl]).astype(o_ref.dtype)

    row = pl.BlockSpec((GN_ROWS, d), lambda i: (i, 0))
    vec = pl.BlockSpec((1, d), lambda i: (0, 0))
    return pl.pallas_call(
        body, name=name, grid=(t // GN_ROWS,), in_specs=[row, row, vec], out_specs=row,
        out_shape=jax.ShapeDtypeStruct((t, d), BF16), compiler_params=_params(("parallel",)),
    )(y, z, w)


def gated_norm_bwd(dgn, y, z, w, name):
    t, d = y.shape

    def body(d_ref, y_ref, z_ref, w_ref, dy_ref, dz_ref, dw_ref):
        i = pl.program_id(0)

        @pl.when(i == 0)
        def _():
            dw_ref[...] = jnp.zeros_like(dw_ref)

        for g in range(d // GN_GROUP):
            sl = slice(g * GN_GROUP, (g + 1) * GN_GROUP)
            zv, yv, dv = z_ref[:, sl], y_ref[:, sl], d_ref[:, sl]
            sg = jax.nn.sigmoid(zv)
            sz = zv * sg
            gv = yv * sz
            r = lax.rsqrt(jnp.mean(gv * gv, axis=-1, keepdims=True) + SSM_NORM_EPS)
            gh = gv * r
            dgh = dv * w_ref[:, sl]
            dw_ref[:, sl] += jnp.sum(dv * gh, axis=0, keepdims=True)
            dg = r * (dgh - gh * jnp.mean(dgh * gh, axis=-1, keepdims=True))
            dy_ref[:, sl] = dg * sz
            dz_ref[:, sl] = (dg * yv * (sg * (1.0 + zv * (1.0 - sg)))).astype(dz_ref.dtype)

    row = pl.BlockSpec((GN_ROWS, d), lambda i: (i, 0))
    vec = pl.BlockSpec((1, d), lambda i: (0, 0))
    return pl.pallas_call(
        body, name=name, grid=(t // GN_ROWS,), in_specs=[row, row, row, vec], out_specs=[row, row, vec],
        out_shape=[jax.ShapeDtypeStruct((t, d), F32), jax.ShapeDtypeStruct((t, d), BF16), jax.ShapeDtypeStruct((1, d), F32)],
        compiler_params=_params(("arbitrary",)),
    )(dgn, y, z, w)


def ple_fwd(h, gpre, pp, name):
    t, d = h.shape

    def body(h_ref, g_ref, p_ref, o_ref):
        o_ref[...] = h_ref[...] + jax.nn.sigmoid(g_ref[...]) * p_ref[...]

    row = pl.BlockSpec((ROW_TILE, d), lambda i: (i, 0))
    return pl.pallas_call(
        body, name=name, grid=(t // ROW_TILE,), in_specs=[row, row, row], out_specs=row,
        out_shape=jax.ShapeDtypeStruct((t, d), F32), compiler_params=_params(("parallel",)),
    )(h, gpre, pp)


def ple_bwd(dh, gpre, pp, name):
    t, d = dh.shape

    def body(d_ref, g_ref, p_ref, dg_ref, dp_ref):
        sg = jax.nn.sigmoid(g_ref[...])
        dv = d_ref[...]
        dg_ref[...] = (dv * p_ref[...] * sg * (1.0 - sg)).astype(dg_ref.dtype)
        dp_ref[...] = (dv * sg).astype(dp_ref.dtype)

    row = pl.BlockSpec((ROW_TILE, d), lambda i: (i, 0))
    return pl.pallas_call(
        body, name=name, grid=(t // ROW_TILE,), in_specs=[row, row, row], out_specs=[row, row],
        out_shape=[jax.ShapeDtypeStruct((t, d), BF16), jax.ShapeDtypeStruct((t, d), BF16)],
        compiler_params=_params(("parallel",)),
    )(dh, gpre, pp)


def loss_head(h, gain, target, name):
    t, d = h.shape

    def body(x_ref, g_ref, t_ref, l_ref, dx_ref, dg_ref):
        i = pl.program_id(0)
        xv = x_ref[...]
        r = lax.rsqrt(jnp.mean(xv * xv, axis=-1, keepdims=True) + NORM_EPS)
        xh = xv * r
        err = xh * g_ref[...] - t_ref[...]
        part = 0.5 * jnp.sum(jnp.mean(err * err, axis=-1, keepdims=True), axis=0, keepdims=True)
        dyv = err * (1.0 / d)
        dxh = dyv * g_ref[...]
        dx_ref[...] = r * (dxh - xh * jnp.mean(dxh * xh, axis=-1, keepdims=True))

        @pl.when(i == 0)
        def _():
            dg_ref[...] = jnp.zeros_like(dg_ref)
            l_ref[...] = jnp.zeros_like(l_ref)

        dg_ref[...] += jnp.sum(dyv * xh, axis=0, keepdims=True)
        l_ref[...] += jnp.broadcast_to(part, l_ref.shape)

    row = pl.BlockSpec((ROW_TILE, d), lambda i: (i, 0))
    vec = pl.BlockSpec((1, d), lambda i: (0, 0))
    lspec = pl.BlockSpec((1, LANE), lambda i: (0, 0))
    return pl.pallas_call(
        body, name=name, grid=(t // ROW_TILE,), in_specs=[row, vec, row], out_specs=[lspec, row, vec],
        out_shape=[jax.ShapeDtypeStruct((1, LANE), F32), jax.ShapeDtypeStruct((t, d), F32), jax.ShapeDtypeStruct((1, d), F32)],
        compiler_params=_params(("arbitrary",)),
    )(h, gain, target)


CHUNK = SSM_CHUNK
HPG = SSM_HEADS // SSM_GROUPS
GW = HPG * SSM_HEAD_DIM
NEG = -1e30


def _ssd_dt_terms(dtr_ref, bias_ref, alog_ref, g):
    shift = (LANE - HPG * g) % LANE
    li = lax.broadcasted_iota(jnp.int32, (CHUNK, CHUNK), 0)
    si = lax.broadcasted_iota(jnp.int32, (CHUNK, CHUNK), 1)
    x0 = pltpu.roll(dtr_ref[...] + bias_ref[...], shift, 1)
    dt = jnp.maximum(x0, 0.0) + jnp.log1p(jnp.exp(-jnp.abs(x0)))
    a = pltpu.roll(jnp.broadcast_to(-jnp.exp(alog_ref[...]), (CHUNK, LANE)), shift, 1)
    adt = dt * a
    acs = lax.dot_general((li >= si).astype(F32), adt, NN, precision=lax.Precision.HIGHEST, preferred_element_type=F32)
    last = jnp.broadcast_to(acs[CHUNK - 1:CHUNK, :], (CHUNK, LANE))
    return dict(li=li, si=si, x0=x0, dt=dt, a=a, acs=acs, acs_t=acs.T, e_acs=jnp.exp(acs),
                w_all=jnp.exp(last - acs), e_last=jnp.exp(last))


def _ssd_specs(nc, rev):
    def ci(c):
        return nc - 1 - c if rev else c
    nb = SSM_D_INNER // LANE
    xs = pl.BlockSpec((None, CHUNK, GW), lambda g, b, c: (b, ci(c), g))
    bm = pl.BlockSpec((None, CHUNK, LANE), lambda g, b, c: (b, ci(c), nb + g))
    cm = pl.BlockSpec((None, CHUNK, LANE), lambda g, b, c: (b, ci(c), nb + SSM_GROUPS + g))
    dtr = pl.BlockSpec((None, CHUNK, LANE), lambda g, b, c: (b, ci(c), 0))
    vec = pl.BlockSpec((1, LANE), lambda g, b, c: (0, 0))
    dexp = pl.BlockSpec((1, GW), lambda g, b, c: (0, g))
    st = pl.BlockSpec((None, None, CHUNK, GW), lambda g, b, c: (b, ci(c), 0, g))
    return xs, bm, cm, dtr, vec, dexp, st


def ssd_fwd(xc, dtr, dt_bias, a_log, d_exp, name):
    bsz, seq, _ = xc.shape
    nc = seq // CHUNK
    xs_s, bm_s, cm_s, dtr_s, vec_s, dexp_s, st_s = _ssd_specs(nc, False)

    def body(xs_ref, b_ref, c_ref, dtr_ref, bias_ref, alog_ref, dexp_ref, y_ref, sp_ref, state_ref):
        g, c = pl.program_id(0), pl.program_id(2)

        @pl.when(c == 0)
        def _():
            state_ref[...] = jnp.zeros_like(state_ref)

        t = _ssd_dt_terms(dtr_ref, bias_ref, alog_ref, g)
        tril = t["li"] >= t["si"]
        xs = xs_ref[...]
        b16 = b_ref[...].astype(BF16)
        c16 = c_ref[...].astype(BF16)
        bt16 = b_ref[...].T.astype(BF16)
        cb = lax.dot_general(c16, b16, NT, preferred_element_type=F32)
        sp = state_ref[...]
        sp_ref[...] = sp
        cs = jnp.dot(c16, sp.astype(BF16), preferred_element_type=F32)
        y_parts, s_parts = [], []
        for r in range(HPG):
            sl = slice(r * SSM_HEAD_DIM, (r + 1) * SSM_HEAD_DIM)
            diff = t["acs"][:, r:r + 1] - t["acs_t"][r:r + 1, :]
            m16 = (cb * jnp.exp(jnp.where(tril, diff, NEG))).astype(BF16)
            xh = xs[:, sl]
            xd = xh * t["dt"][:, r:r + 1]
            yh = jnp.dot(m16, xd.astype(BF16), preferred_element_type=F32)
            yh = yh + t["e_acs"][:, r:r + 1] * cs[:, sl] + xh * dexp_ref[:, sl]
            y_parts.append(yh)
            xw16 = (xd * t["w_all"][:, r:r + 1]).astype(BF16)
            s_parts.append(sp[:, sl] * t["e_last"][:, r:r + 1] + jnp.dot(bt16, xw16, preferred_element_type=F32))
        y_ref[...] = jnp.concatenate(y_parts, axis=1)
        state_ref[...] = jnp.concatenate(s_parts, axis=1)

    return pl.pallas_call(
        body, name=name, grid=(SSM_GROUPS, bsz, nc),
        in_specs=[xs_s, bm_s, cm_s, dtr_s, vec_s, vec_s, dexp_s],
        out_specs=[xs_s, st_s],
        out_shape=[jax.ShapeDtypeStruct((bsz, seq, SSM_D_INNER), F32),
                   jax.ShapeDtypeStruct((bsz, nc, SSM_STATE, SSM_D_INNER), F32)],
        scratch_shapes=[pltpu.VMEM((SSM_STATE, GW), F32)],
        compiler_params=_params(("arbitrary", "arbitrary", "arbitrary")),
    )(xc, xc, xc, dtr, dt_bias, a_log, d_exp)


def ssd_bwd(xc, dtr, dt_bias, a_log, d_exp, dy, sprev, name):
    bsz, seq, _ = xc.shape
    nc = seq // CHUNK
    xs_s, bm_s, cm_s, dtr_s, vec_s, dexp_s, st_s = _ssd_specs(nc, True)
    grp = pl.BlockSpec((None, CHUNK, LANE), lambda g, b, c: (b, nc - 1 - c, g))
    acc = pl.BlockSpec((None, 1, LANE), lambda g, b, c: (g, 0, 0))

    def body(xs_ref, b_ref, c_ref, dtr_ref, bias_ref, alog_ref, dexp_ref, dy_ref, sp_ref,
             dxs_ref, db_ref, dc_ref, ddtr_ref, da_ref, dbias_ref, ddexp_ref, dstate_ref):
        g, b, c = pl.program_id(0), pl.program_id(1), pl.program_id(2)

        @pl.when(c == 0)
        def _():
            dstate_ref[...] = jnp.zeros_like(dstate_ref)

        @pl.when((b == 0) & (c == 0))
        def _():
            da_ref[...] = jnp.zeros_like(da_ref)
            dbias_ref[...] = jnp.zeros_like(dbias_ref)
            ddexp_ref[...] = jnp.zeros_like(ddexp_ref)

        t = _ssd_dt_terms(dtr_ref, bias_ref, alog_ref, g)
        li, si = t["li"], t["si"]
        tril, triu = li >= si, li <= si
        row_is_last = li[:, :1] == CHUNK - 1
        xs, dy = xs_ref[...], dy_ref[...]
        b16 = b_ref[...].astype(BF16)
        c16 = c_ref[...].astype(BF16)
        ct16 = c_ref[...].T.astype(BF16)
        cb = lax.dot_general(c16, b16, NT, preferred_element_type=F32)
        cbt = lax.dot_general(b16, c16, NT, preferred_element_type=F32)
        sp, ds = sp_ref[...], dstate_ref[...]
        sp16, ds16 = sp.astype(BF16), ds.astype(BF16)
        cs = jnp.dot(c16, sp16, preferred_element_type=F32)
        bds = jnp.dot(b16, ds16, preferred_element_type=F32)
        spds = jnp.sum(sp * ds, axis=0, keepdims=True)
        zero = jnp.zeros((CHUNK, CHUNK), F32)
        dsum_g, dsum_gt, dacs, ddt_dir = zero, zero, zero, zero
        dxs_parts, edy_parts, xw_parts, dsp_parts = [], [], [], []
        for r in range(HPG):
            sl = slice(r * SSM_HEAD_DIM, (r + 1) * SSM_HEAD_DIM)
            diff = t["acs"][:, r:r + 1] - t["acs_t"][r:r + 1, :]
            lam = jnp.exp(jnp.where(tril, diff, NEG))
            lam_t = jnp.exp(jnp.where(triu, -diff, NEG))
            m, m_t = cb * lam, cbt * lam_t
            xh, dyh = xs[:, sl], dy[:, sl]
            dtc, wc, ec, elc = t["dt"][:, r:r + 1], t["w_all"][:, r:r + 1], t["e_acs"][:, r:r + 1], t["e_last"][:, r:r + 1]
            xd = xh * dtc
            xd16, dyh16 = xd.astype(BF16), dyh.astype(BF16)
            dxd = jnp.dot(m_t.astype(BF16), dyh16, preferred_element_type=F32) + wc * bds[:, sl]
            dm = lax.dot_general(dyh16, xd16, NT, preferred_element_type=F32)
            dm_t = lax.dot_general(xd16, dyh16, NT, preferred_element_type=F32)
            dsum_g = dsum_g + dm * lam
            dsum_gt = dsum_gt + dm_t * lam_t
            dacs_h = jnp.sum(dm * m, axis=1, keepdims=True) - jnp.sum(dm_t * m_t, axis=1, keepdims=True)
            dacs_h = dacs_h + jnp.sum(dyh * (ec * cs[:, sl]), axis=1, keepdims=True)
            wdw = wc * jnp.sum(xd * bds[:, sl], axis=1, keepdims=True)
            last_h = jnp.sum(wdw, axis=0, keepdims=True) + elc[0:1, :] * jnp.sum(spds[:, sl], axis=1, keepdims=True)
            dacs_h = dacs_h - wdw + jnp.where(row_is_last, last_h, 0.0)
            onehot = (si[:1, :] == r).astype(F32)
            dacs = dacs + dacs_h * onehot
            ddt_dir = ddt_dir + jnp.sum(dxd * xh, axis=1, keepdims=True) * onehot
            dxs_parts.append(dxd * dtc + dyh * dexp_ref[:, sl])
            edy_parts.append(ec * dyh)
            xw_parts.append(wc * xd)
            dsp_parts.append(elc * ds[:, sl])
        dxs_ref[...] = jnp.concatenate(dxs_parts, axis=1)
        edy16 = jnp.concatenate(edy_parts, axis=1).astype(BF16)
        xw16 = jnp.concatenate(xw_parts, axis=1).astype(BF16)
        dc_ref[...] = (jnp.dot(dsum_g.astype(BF16), b16, preferred_element_type=F32)
                       + lax.dot_general(edy16, sp16, NT, preferred_element_type=F32))
        db_ref[...] = (jnp.dot(dsum_gt.astype(BF16), c16, preferred_element_type=F32)
                       + lax.dot_general(xw16, ds16, NT, preferred_element_type=F32))
        dstate_ref[...] = jnp.concatenate(dsp_parts, axis=1) + jnp.dot(ct16, edy16, preferred_element_type=F32)
        dadt = lax.dot_general(triu.astype(F32), dacs, NN, precision=lax.Precision.HIGHEST, preferred_element_type=F32)
        head_lane = si < HPG
        ddtr = jnp.where(head_lane, (dadt * t["a"] + ddt_dir) * jax.nn.sigmoid(t["x0"]), 0.0)
        ddtr_ref[...] = ddtr
        da_ref[...] += jnp.sum(jnp.where(head_lane, dadt * t["dt"] * t["a"], 0.0), axis=0, keepdims=True)
        dbias_ref[...] += jnp.sum(ddtr, axis=0, keepdims=True)
        ddexp_ref[...] += jnp.sum(dy * xs, axis=0, keepdims=True)

    return pl.pallas_call(
        body, name=name, grid=(SSM_GROUPS, bsz, nc),
        in_specs=[xs_s, bm_s, cm_s, dtr_s, vec_s, vec_s, dexp_s, xs_s, st_s],
        out_specs=[xs_s, grp, grp, grp, acc, acc, dexp_s],
        out_shape=[jax.ShapeDtypeStruct((bsz, seq, SSM_D_INNER), F32),
                   jax.ShapeDtypeStruct((bsz, seq, SSM_GROUPS * SSM_STATE), F32),
                   jax.ShapeDtypeStruct((bsz, seq, SSM_GROUPS * SSM_STATE), F32),
                   jax.ShapeDtypeStruct((bsz, seq, SSM_GROUPS * LANE), F32),
                   jax.ShapeDtypeStruct((SSM_GROUPS, 1, LANE), F32),
                   jax.ShapeDtypeStruct((SSM_GROUPS, 1, LANE), F32),
                   jax.ShapeDtypeStruct((1, SSM_D_INNER), F32)],
        scratch_shapes=[pltpu.VMEM((SSM_STATE, GW), F32)],
        compiler_params=_params(("arbitrary", "arbitrary", "arbitrary")),
    )(xc, xc, xc, dtr, dt_bias, a_log, d_exp, dy, sprev)


SB_TQ = 512
SB_TK = 128
SB_SCALE = SB_HEAD_DIM ** -0.5


def _split_dot(x, u16):
    hi = x.astype(BF16)
    lo = (x - hi.astype(F32)).astype(BF16)
    return jnp.dot(hi, u16, preferred_element_type=F32) + jnp.dot(lo, u16, preferred_element_type=F32)


def _sb_scores(qh, kj, mask):
    z = lax.dot_general(qh, kj, NT, preferred_element_type=F32)
    l1p = jnp.log(1.0 + jnp.exp(-jnp.abs(z)))
    lb = jnp.minimum(z, 0.0) - l1p
    lk = -jnp.maximum(z, 0.0) - l1p
    if mask is not None:
        lk = jnp.where(mask, lk, 0.0)
    return lb, lk


def _sb_diag_masks(tq):
    r = lax.broadcasted_iota(jnp.int32, (tq, SB_TK), 0)
    c = lax.broadcasted_iota(jnp.int32, (tq, SB_TK), 1)
    return [r > c + d * SB_TK for d in range(tq // SB_TK)]


def _sb_heads(x, lane):
    return [jnp.where((lane // SB_HEAD_DIM) == hh, x, jnp.zeros_like(x)) for hh in range(2)]


def sb_attention_fwd(q, kv, name):
    bsz, seq, width = q.shape
    tq = min(SB_TQ, seq)
    npair = width // LANE
    qspec = pl.BlockSpec((None, tq, LANE), lambda b, p, i: (b, i, p))
    kspec = pl.BlockSpec((None, seq, LANE), lambda b, p, i: (b, 0, p))
    vspec = pl.BlockSpec((None, seq, LANE), lambda b, p, i: (b, 0, npair + p))

    def body(q_ref, k_ref, v_ref, o_ref, l_ref):
        i = pl.program_id(2)
        ndiag = tq // SB_TK
        nfull = i * ndiag
        lane = lax.broadcasted_iota(jnp.int32, (tq, LANE), 1)
        ui = lax.broadcasted_iota(jnp.int32, (SB_TK, SB_TK), 0)
        uj = lax.broadcasted_iota(jnp.int32, (SB_TK, SB_TK), 1)
        u_excl = (ui > uj).astype(BF16)
        qs = _sb_heads(q_ref[...] * SB_SCALE, lane)

        def tile(j, carry, mask):
            rows = pl.ds(pl.multiple_of(j * SB_TK, SB_TK), SB_TK)
            kj, vj = k_ref[rows, :], v_ref[rows, :]
            new = []
            for hh in range(2):
                acc, run = carry[hh]
                lb, lk = _sb_scores(qs[hh], kj, mask)
                w = jnp.exp(lb + _split_dot(lk, u_excl) + run)
                if mask is not None:
                    w = jnp.where(mask, w, 0.0)
                acc = acc + jnp.dot(w.astype(BF16), vj, preferred_element_type=F32)
                new.append((acc, run + jnp.sum(lk, axis=1, keepdims=True)))
            return tuple(new)

        carry = tuple((jnp.zeros((tq, LANE), F32), jnp.zeros((tq, 1), F32)) for _ in range(2))
        masks = _sb_diag_masks(tq)
        for dd in reversed(range(ndiag)):
            carry = tile(nfull + dd, carry, masks[dd])
        carry = lax.fori_loop(0, nfull // 2, lambda jj, cr: tile(nfull - 2 - 2 * jj, tile(nfull - 1 - 2 * jj, cr, None), None), carry)
        in0 = (lane // SB_HEAD_DIM) == 0
        o_ref[...] = jnp.where(in0, carry[0][0], carry[1][0]).astype(o_ref.dtype)
        l_ref[...] = jnp.where(in0, carry[0][1], carry[1][1])

    return pl.pallas_call(
        body, name=name, grid=(bsz, npair, seq // tq),
        in_specs=[qspec, kspec, vspec], out_specs=[qspec, qspec],
        out_shape=[jax.ShapeDtypeStruct(q.shape, BF16), jax.ShapeDtypeStruct(q.shape, F32)],
        compiler_params=_params(("parallel", "parallel", "arbitrary")),
    )(q, kv, kv)


def sb_attention_bwd(q, kv, ltot, do, dk_in, dv_in, name):
    bsz, seq, width = q.shape
    tq = min(SB_TQ, seq)
    npair = width // LANE
    qspec = pl.BlockSpec((None, tq, LANE), lambda b, p, i: (b, i, p))
    kspec = pl.BlockSpec((None, seq, LANE), lambda b, p, i: (b, 0, p))
    vspec = pl.BlockSpec((None, seq, LANE), lambda b, p, i: (b, 0, npair + p))

    def body(q_ref, k_ref, v_ref, l_ref, do_ref, dki_ref, dvi_ref, dq_ref, dk_ref, dv_ref):
        i = pl.program_id(2)

        @pl.when(i == 0)
        def _():
            dk_ref[...] = dki_ref[...]
            dv_ref[...] = dvi_ref[...]

        ndiag = tq // SB_TK
        nfull = i * ndiag
        lane = lax.broadcasted_iota(jnp.int32, (tq, LANE), 1)
        ui = lax.broadcasted_iota(jnp.int32, (SB_TK, SB_TK), 0)
        uj = lax.broadcasted_iota(jnp.int32, (SB_TK, SB_TK), 1)
        u_le = (ui <= uj).astype(BF16)
        u_lt = (ui < uj).astype(BF16)
        qs = _sb_heads(q_ref[...] * SB_SCALE, lane)
        dos = _sb_heads(do_ref[...], lane)
        ltots = [l_ref[:, hh * SB_HEAD_DIM:hh * SB_HEAD_DIM + 1] for hh in range(2)]

        def tile(j, carry, mask):
            rows = pl.ds(pl.multiple_of(j * SB_TK, SB_TK), SB_TK)
            kj, vj = k_ref[rows, :], v_ref[rows, :]
            new, dk_t, dv_t = [], None, None
            for hh in range(2):
                acc, run, run_a = carry[hh]
                lb, lk = _sb_scores(qs[hh], kj, mask)
                w = jnp.exp(lb + (ltots[hh] - run - _split_dot(lk, u_le)))
                if mask is not None:
                    w = jnp.where(mask, w, 0.0)
                a = lax.dot_general(dos[hh], vj, NT, preferred_element_type=F32) * w
                ca = run_a + _split_dot(a, u_lt)
                sg = jnp.exp(lb)
                dz = a * (1.0 - sg) - ca * sg
                if mask is not None:
                    dz = jnp.where(mask, dz, 0.0)
                dz16 = dz.astype(BF16)
                acc = acc + jnp.dot(dz16, kj, preferred_element_type=F32)
                dkh = lax.dot_general(dz16, qs[hh], TN, preferred_element_type=F32)
                dvh = lax.dot_general(w.astype(BF16), dos[hh], TN, preferred_element_type=F32)
                dk_t = dkh if dk_t is None else dk_t + dkh
                dv_t = dvh if dv_t is None else dv_t + dvh
                new.append((acc, run + jnp.sum(lk, axis=1, keepdims=True), run_a + jnp.sum(a, axis=1, keepdims=True)))
            dk_ref[rows, :] += dk_t
            dv_ref[rows, :] += dv_t
            return tuple(new)

        zcol = jnp.zeros((tq, 1), F32)
        carry = tuple((jnp.zeros((tq, LANE), F32), zcol, zcol) for _ in range(2))
        carry = lax.fori_loop(0, nfull // 2, lambda j, cr: tile(2 * j + 1, tile(2 * j, cr, None), None), carry)
        masks = _sb_diag_masks(tq)
        for dd in range(ndiag):
            carry = tile(nfull + dd, carry, masks[dd])
        in0 = (lane // SB_HEAD_DIM) == 0
        dq_ref[...] = (jnp.where(in0, carry[0][0], carry[1][0]) * SB_SCALE).astype(dq_ref.dtype)

    return pl.pallas_call(
        body, name=name, grid=(bsz, npair, seq // tq),
        in_specs=[qspec, kspec, vspec, qspec, qspec, kspec, kspec], out_specs=[qspec, kspec, kspec],
        out_shape=[jax.ShapeDtypeStruct(q.shape, BF16), jax.ShapeDtypeStruct(q.shape, F32), jax.ShapeDtypeStruct(q.shape, F32)],
        compiler_params=_params(("parallel", "parallel", "arbitrary")),
    )(q, kv, kv, ltot, do, dk_in, dv_in)


def adamw(w, parts, m, v, name):
    r, c = w.shape
    tr = r
    for cand in (256, 176, 128, 64, 32, 16, 8):
        if r % cand == 0:
            tr = cand
            break
    bc1 = 1.0 - ADAM_B1 ** ADAM_STEP
    bc2 = 1.0 - ADAM_B2 ** ADAM_STEP

    def body(w_ref, p_ref, m_ref, v_ref, g_ref, d_ref, mo_ref, vo_ref):
        g = p_ref[0].astype(F32)
        for j in range(1, N_DEV):
            g = g + p_ref[j].astype(F32)
        mn = ADAM_B1 * m_ref[...] + (1.0 - ADAM_B1) * g
        vn = ADAM_B2 * v_ref[...] + (1.0 - ADAM_B2) * (g * g)
        g_ref[...] = g
        mo_ref[...] = mn
        vo_ref[...] = vn
        d_ref[...] = -ADAM_LR * ((mn / bc1) / (jnp.sqrt(vn / bc2) + ADAM_EPS) + ADAM_WD * w_ref[...])

    blk = pl.BlockSpec((tr, c), lambda i: (i, 0))
    pblk = pl.BlockSpec((N_DEV, tr, c), lambda i: (0, i, 0))
    shp = jax.ShapeDtypeStruct((r, c), F32)
    return pl.pallas_call(
        body, name=name, grid=(r // tr,), in_specs=[blk, pblk, blk, blk], out_specs=[blk, blk, blk, blk],
        out_shape=[shp, shp, shp, shp], compiler_params=_params(("parallel",)),
    )(w, parts, m, v)


MESH_ID = pl.DeviceIdType.MESH
ANY = pl.BlockSpec(memory_space=pl.ANY)


def all_gather(xs, name):
    n = len(xs)

    def body(*refs):
        x_refs, out_refs = refs[:n], refs[n:2 * n]
        send_sems, recv_sems, local_sems = refs[2 * n:]
        mx, my, mc = lax.axis_index("x"), lax.axis_index("y"), lax.axis_index("c")
        me, sibling = (mx, my, mc), (mx, my, 1 - mc)
        chips = [(1 - mx, my), (mx, 1 - my), (1 - mx, 1 - my)]

        def slot(a, px, py, pc):
            return out_refs[a].at[4 * px + 2 * py + pc]

        def copy(a, k, block, to, src=None):
            return pltpu.make_async_remote_copy(
                src_ref=slot(a, *block) if src is None else src, dst_ref=slot(a, *block),
                send_sem=send_sems.at[7 * a + k], recv_sem=recv_sems.at[7 * a + k], device_id=to, device_id_type=MESH_ID)

        mine = [pltpu.make_async_copy(x_refs[a], slot(a, *me), local_sems.at[a]) for a in range(n)]
        for cp in mine:
            cp.start()
        first = []
        for a in range(n):
            first.append(copy(a, 0, me, sibling, src=x_refs[a]))
            first += [copy(a, 1 + j, me, (*chip, mc), src=x_refs[a]) for j, chip in enumerate(chips)]
        for cp in first:
            cp.start()
        passed = []
        for j, chip in enumerate(chips):
            for a in range(n):
                copy(a, 1 + j, (*chip, mc), me).wait_recv()
                cp = copy(a, 4 + j, (*chip, mc), sibling)
                cp.start()
                passed.append(cp)
        for a in range(n):
            copy(a, 0, sibling, me).wait_recv()
            for j, chip in enumerate(chips):
                copy(a, 4 + j, (*chip, 1 - mc), me).wait_recv()
        for cp in first + passed:
            cp.wait_send()
        for cp in mine:
            cp.wait()

    return pl.pallas_call(
        body, name=name, out_shape=[jax.ShapeDtypeStruct((N_DEV,) + x.shape, x.dtype) for x in xs],
        in_specs=[ANY] * n, out_specs=[ANY] * n,
        scratch_shapes=[pltpu.SemaphoreType.DMA((7 * n,)), pltpu.SemaphoreType.DMA((7 * n,)), pltpu.SemaphoreType.DMA((n,))],
    )(*xs)


def all_to_all(ps, name):
    n = len(ps)

    def body(*refs):
        p_refs, out_refs = refs[:n], refs[n:2 * n]
        send_sems, recv_sems, local_sems = refs[2 * n:]
        mx, my, mc = lax.axis_index("x"), lax.axis_index("y"), lax.axis_index("c")
        me = 4 * mx + 2 * my + mc

        def peer_of(j):
            px = 1 - mx if j & 4 else mx
            py = 1 - my if j & 2 else my
            pc = 1 - mc if j & 1 else mc
            return (px, py, pc), 4 * px + 2 * py + pc

        def copy(a, j, dst_slot):
            dev, peer = peer_of(j)
            k = 7 * a + j - 1
            return pltpu.make_async_remote_copy(
                src_ref=p_refs[a].at[peer], dst_ref=out_refs[a].at[dst_slot], send_sem=send_sems.at[k],
                recv_sem=recv_sems.at[k], device_id=dev, device_id_type=MESH_ID)

        mine = [pltpu.make_async_copy(p_refs[a].at[me], out_refs[a].at[me], local_sems.at[a]) for a in range(n)]
        for cp in mine:
            cp.start()
        sends = [copy(a, j, me) for j in range(1, N_DEV) for a in range(n)]
        for cp in sends:
            cp.start()
        for j in range(1, N_DEV):
            for a in range(n):
                copy(a, j, peer_of(j)[1]).wait_recv()
        for cp in sends:
            cp.wait_send()
        for cp in mine:
            cp.wait()

    return pl.pallas_call(
        body, name=name, out_shape=[jax.ShapeDtypeStruct(p.shape, p.dtype) for p in ps],
        in_specs=[ANY] * n, out_specs=[ANY] * n,
        scratch_shapes=[pltpu.SemaphoreType.DMA((7 * n,)), pltpu.SemaphoreType.DMA((7 * n,)), pltpu.SemaphoreType.DMA((n,))],
    )(*ps)


PACK_COLS = 1024
PACK_SEG = 16 * PACK_COLS

BIG = {"ssm_in_proj": 2, "ssm_out_proj": 1, "w_kv": 1, "w_q": 1, "w_o": 1, "ffn_up": 2, "ffn_down": 1,
       "ple_gate": 1, "ple_proj": 2}
SMALL = {"ssm_conv_w": 2, "ssm_conv_b": 1, "ssm_norm": 1, "ffn_conv_w": 2}
REPL = ["attn_norm", "ffn_norm", "ple_norm", "ssm_dt_bias", "ssm_a_log", "ssm_d", "ffn_conv_b", "kv_norm", "final_norm"]


def _seg(n):
    return -(-n // PACK_SEG) * PACK_SEG


def _pack_rows(flats, lead=()):
    padded = [jnp.pad(f, [(0, 0)] * len(lead) + [(0, _seg(f.shape[-1]) - f.shape[-1])]) for f in flats]
    return jnp.concatenate(padded, axis=-1).reshape(*lead, -1, PACK_COLS)


def _unpack_rows(buf, sizes, lead):
    flat = buf.reshape(*lead, -1)
    out, off = [], 0
    for n in sizes:
        out.append(lax.slice_in_dim(flat, off, off + n, axis=len(lead)))
        off += _seg(n)
    return out


def _to_slots(full, axis):
    shp = full.shape
    blk = shp[axis] // N_DEV
    t = full.reshape(shp[:axis] + (N_DEV, blk) + shp[axis + 1:])
    return jnp.moveaxis(t, axis, 0).reshape(N_DEV, -1)


def _from_slots(slots, local_shape, axis):
    t = jnp.moveaxis(slots.reshape((N_DEV,) + tuple(local_shape)), 0, axis)
    shp = list(local_shape)
    shp[axis] *= N_DEV
    return t.reshape(shp)


def _as2d(a):
    return a.reshape(1, -1) if a.ndim == 1 else a.reshape(-1, a.shape[-1])


def _pad_rows(w, rows=SUBLANE):
    return jnp.pad(w, ((0, rows - w.shape[0]), (0, 0)))


def kernel(x, p, attn_norm, ffn_norm, ple_norm, ssm_in_proj, ssm_conv_w, ssm_conv_b, ssm_dt_bias, ssm_a_log, ssm_d, ssm_norm, ssm_out_proj, kv_norm, w_kv, w_q, w_o, ffn_up, ffn_conv_w, ffn_conv_b, ffn_down, ple_gate, ple_proj, final_norm, loss_target, m_attn_norm, m_ffn_norm, m_ple_norm, m_ssm_in_proj, m_ssm_conv_w, m_ssm_conv_b, m_ssm_dt_bias, m_ssm_a_log, m_ssm_d, m_ssm_norm, m_ssm_out_proj, m_kv_norm, m_w_kv, m_w_q, m_w_o, m_ffn_up, m_ffn_conv_w, m_ffn_conv_b, m_ffn_down, m_ple_gate, m_ple_proj, m_final_norm, v_attn_norm, v_ffn_norm, v_ple_norm, v_ssm_in_proj, v_ssm_conv_w, v_ssm_conv_b, v_ssm_dt_bias, v_ssm_a_log, v_ssm_d, v_ssm_norm, v_ssm_out_proj, v_kv_norm, v_w_kv, v_w_q, v_w_o, v_ffn_up, v_ffn_conv_w, v_ffn_conv_b, v_ffn_down, v_ple_gate, v_ple_proj, v_final_norm):
    given = dict(locals())
    wnames = list(BIG) + list(SMALL) + REPL
    bsz, seq, d = x.shape
    t = bsz * seq
    bs = lambda a: a.reshape(bsz, seq, a.shape[-1])
    fl = lambda a: a.reshape(t, a.shape[-1])

    big_local = [given[n] for n in BIG]
    small_local = [given[n] for n in SMALL]
    gathered = all_gather([w.astype(BF16) for w in big_local] + [_pack_rows([w.reshape(-1) for w in small_local])], "ag_weights")
    gbig = dict(zip(BIG, gathered))

    def rows_of(name, i):
        g = gbig[name][:, i]
        return g.reshape(-1, g.shape[-1])

    def cols_of(name, i, lo=0, hi=N_DEV):
        g = gbig[name][lo:hi, i]
        return jnp.transpose(g, (1, 0, 2)).reshape(g.shape[1], -1)

    full = {"w_kv": jnp.transpose(gbig["w_kv"], (1, 0, 2)).reshape(d, -1)}
    for n in ("ssm_out_proj", "w_q", "w_o", "ffn_down", "ple_gate"):
        full[n] = [rows_of(n, i) for i in range(given[n].shape[0])]
    full["ple_proj"] = [cols_of("ple_proj", i) for i in range(DEPTH)]
    full["ssm_in_proj"] = [cols_of("ssm_in_proj", i) for i in range(N_A)]
    for (n, ax), w, s in zip(SMALL.items(), small_local, _unpack_rows(gathered[-1], [w.size for w in small_local], (N_DEV,))):
        full[n] = _from_slots(s, w.shape, ax)

    row = lambda v: v.reshape(1, -1)
    pad_lane = lambda v: jnp.pad(v.reshape(1, -1), ((0, 0), (0, LANE - v.size)))
    h = x.reshape(t, d)
    saved = []
    kv = hnkv = h_kv = None
    for i in range(DEPTH):
        sv = {"h0": h}
        hn = rmsnorm_fwd(h, row(attn_norm[i]), f"attn_norm_f{i}")
        sv["hn"] = hn
        if i < N_A:
            w_in = full["ssm_in_proj"][i]
            wz, wxbc = w_in[:, :SSM_D_INNER], w_in[:, SSM_D_INNER:SSM_ZX]
            wdt = jnp.pad(w_in[:, SSM_ZX:], ((0, 0), (0, LANE - SSM_HEADS)))
            z = matmul(hn, wz, "nn", F32, f"ssm_z_f{i}")
            xbc = bs(matmul(hn, wxbc, "nn", F32, f"ssm_xbc_f{i}"))
            dtr = bs(matmul(hn, wdt, "nn", F32, f"ssm_dt_f{i}"))
            cw = _pad_rows(full["ssm_conv_w"][i])
            cb = row(full["ssm_conv_b"][i])
            xc = ssm_conv_fwd(xbc, cw, cb, f"ssm_conv_f{i}")
            dtb, alog = pad_lane(ssm_dt_bias[i]), pad_lane(ssm_a_log[i])
            dexp = jnp.repeat(ssm_d[i], SSM_HEAD_DIM).reshape(1, -1)
            y, sprev = ssd_fwd(xc, dtr, dtb, alog, dexp, f"ssd_f{i}")
            gn = gated_norm_fwd(fl(y), z, row(full["ssm_norm"][i]), f"ssm_gnorm_f{i}")
            h1 = matmul(gn, full["ssm_out_proj"][i], "nn", F32, f"ssm_out_f{i}", add=h)
            sv.update(wz=wz, wxbc=wxbc, wdt=wdt, z=z, xbc=xbc, dtr=dtr, cw=cw, cb=cb, xc=xc, dtb=dtb, alog=alog,
                      dexp=dexp, y=y, sprev=sprev, gn=gn)
        else:
            j = i - N_A
            q = bs(matmul(hn, full["w_q"][j], "nn", BF16, f"sb_q_f{j}"))
            o, ltot = sb_attention_fwd(q, kv, f"sb_attn_f{j}")
            h1 = matmul(fl(o), full["w_o"][j], "nn", F32, f"sb_o_f{j}", add=h)
            sv.update(q=q, o=o, ltot=ltot)
        sv["h1"] = h1
        hn2 = rmsnorm_fwd(h1, row(ffn_norm[i]), f"ffn_norm_f{i}")
        wug, wuv = cols_of("ffn_up", i, 0, N_DEV // 2), cols_of("ffn_up", i, N_DEV // 2, N_DEV)
        ug = bs(matmul(hn2, wug, "nn", F32, f"ffn_upg_f{i}"))
        uv = bs(matmul(hn2, wuv, "nn", F32, f"ffn_upv_f{i}"))
        fcw = full["ffn_conv_w"][i]
        fwg, fwv = _pad_rows(fcw[:, :D_FF]), _pad_rows(fcw[:, D_FF:])
        fbg, fbv = row(ffn_conv_b[i, :D_FF]), row(ffn_conv_b[i, D_FF:])
        act = ffn_conv_fwd(ug, uv, fwg, fwv, fbg, fbv, f"ffn_conv_f{i}")
        h2 = matmul(fl(act), full["ffn_down"][i], "nn", F32, f"ffn_down_f{i}", add=h1)
        hn3 = rmsnorm_fwd(h2, row(ple_norm[i]), f"ple_norm_f{i}")
        gpre = matmul(hn3, full["ple_gate"][i], "nn", F32, f"ple_gate_f{i}")
        p_i = p[i].reshape(t, PLE_DIM)
        pp = matmul(p_i, full["ple_proj"][i], "nn", F32, f"ple_proj_f{i}")
        h3 = ple_fwd(h2, gpre, pp, f"ple_f{i}")
        sv.update(hn2=hn2, wug=wug, wuv=wuv, ug=ug, uv=uv, fwg=fwg, fwv=fwv, fbg=fbg, fbv=fbv, act=act, h2=h2,
                  hn3=hn3, gpre=gpre, pp=pp, p_i=p_i)
        if i == N_A - 1:
            h_kv = h3
            hnkv = rmsnorm_fwd(h3, row(kv_norm), "kv_norm_f")
            kv = bs(matmul(hnkv, full["w_kv"], "nn", BF16, "kv_f"))
        h = h3
        saved.append(sv)

    loss_row, dh, g_final = loss_head(h, row(final_norm), loss_target.reshape(t, d), "loss_head")
    loss = lax.psum(loss_row[0, 0], ("x", "y", "c"))

    G = {n: [None] * given[n].shape[0] for n in wnames if given[n].ndim > 1 and n not in ("w_kv",)}
    G["final_norm"] = g_final[0]
    dk = jnp.zeros((bsz, seq, SB_WIDTH), F32)
    dv = jnp.zeros((bsz, seq, SB_WIDTH), F32)
    for i in reversed(range(DEPTH)):
        sv = saved[i]
        if i == N_A - 1:
            wk, wv = full["w_kv"][:, :SB_WIDTH], full["w_kv"][:, SB_WIDTH:]
            dkf, dvf = fl(dk), fl(dv)
            dhn = matmul(dkf, wk, "nt", F32, "kv_dx_k")
            dhn = matmul(dvf, wv, "nt", F32, "kv_dx_v", add=dhn)
            G["w_kv"] = jnp.concatenate([matmul(hnkv, dkf, "tn", F32, "kv_dw_k"), matmul(hnkv, dvf, "tn", F32, "kv_dw_v")], axis=1)
            dh, gk = rmsnorm_bwd(dhn, h_kv, row(kv_norm), dh, "kv_norm_b")
            G["kv_norm"] = gk[0]
        dgpre, dpp = ple_bwd(dh, sv["gpre"], sv["pp"], f"ple_b{i}")
        G["ple_proj"][i] = matmul(sv["p_i"], dpp, "tn", F32, f"ple_proj_dw{i}")
        G["ple_gate"][i] = matmul(sv["hn3"], dgpre, "tn", F32, f"ple_gate_dw{i}")
        dhn = matmul(dgpre, full["ple_gate"][i], "nt", F32, f"ple_gate_dx{i}")
        dh, gn_ = rmsnorm_bwd(dhn, sv["h2"], row(ple_norm[i]), dh, f"ple_norm_b{i}")
        G["ple_norm"][i] = gn_[0]
        dact = bs(matmul(dh, full["ffn_down"][i], "nt", F32, f"ffn_down_dx{i}"))
        G["ffn_down"][i] = matmul(fl(sv["act"]), dh, "tn", F32, f"ffn_down_dw{i}")
        dug, duv, dwg, dwv, dbg, dbv = ffn_conv_bwd(sv["ug"], sv["uv"], dact, sv["fwg"], sv["fwv"], sv["fbg"], sv["fbv"], f"ffn_conv_b{i}")
        dug, duv = fl(dug), fl(duv)
        G["ffn_conv_w"][i] = jnp.concatenate([dwg[:FFN_CONV], dwv[:FFN_CONV]], axis=1)
        G["ffn_conv_b"][i] = jnp.concatenate([dbg[0], dbv[0]])
        G["ffn_up"][i] = (matmul(sv["hn2"], dug, "tn", F32, f"ffn_upg_dw{i}"), matmul(sv["hn2"], duv, "tn", F32, f"ffn_upv_dw{i}"))
        dhn = matmul(dug, sv["wug"], "nt", F32, f"ffn_upg_dx{i}")
        dhn = matmul(duv, sv["wuv"], "nt", F32, f"ffn_upv_dx{i}", add=dhn)
        dh, gn_ = rmsnorm_bwd(dhn, sv["h1"], row(ffn_norm[i]), dh, f"ffn_norm_b{i}")
        G["ffn_norm"][i] = gn_[0]
        if i < N_A:
            dgn = matmul(dh, full["ssm_out_proj"][i], "nt", F32, f"ssm_out_dx{i}")
            G["ssm_out_proj"][i] = matmul(sv["gn"], dh, "tn", F32, f"ssm_out_dw{i}")
            dy, dz, dnw = gated_norm_bwd(dgn, fl(sv["y"]), sv["z"], row(full["ssm_norm"][i]), f"ssm_gnorm_b{i}")
            G["ssm_norm"][i] = dnw[0]
            dxs, dbm, dcm, ddtr4, da4, dbias4, ddexp = ssd_bwd(sv["xc"], sv["dtr"], sv["dtb"], sv["alog"], sv["dexp"], bs(dy), sv["sprev"], f"ssd_b{i}")
            G["ssm_a_log"][i] = da4[:, 0, :HPG].reshape(-1)
            G["ssm_dt_bias"][i] = dbias4[:, 0, :HPG].reshape(-1)
            G["ssm_d"][i] = ddexp.reshape(SSM_HEADS, SSM_HEAD_DIM).sum(axis=-1)
            dxc = jnp.concatenate([dxs, dbm, dcm], axis=-1)
            dxbc, dcw, dcb = ssm_conv_bwd(sv["xbc"], dxc, sv["cw"], sv["cb"], f"ssm_conv_b{i}")
            dxbc = fl(dxbc)
            G["ssm_conv_w"][i] = dcw[:SSM_CONV]
            G["ssm_conv_b"][i] = dcb[0]
            ddtr = ddtr4.reshape(t, SSM_GROUPS, LANE)[:, :, :HPG].reshape(t, SSM_HEADS)
            ddtr = jnp.pad(ddtr, ((0, 0), (0, LANE - SSM_HEADS))).astype(BF16)
            hn = sv["hn"]
            G["ssm_in_proj"][i] = jnp.concatenate([
                matmul(hn, dz, "tn", F32, f"ssm_z_dw{i}"), matmul(hn, dxbc, "tn", F32, f"ssm_xbc_dw{i}"),
                matmul(hn, ddtr, "tn", F32, f"ssm_dt_dw{i}")[:, :SSM_HEADS]], axis=1)
            dhn = matmul(dz, sv["wz"], "nt", F32, f"ssm_z_dx{i}")
            dhn = matmul(dxbc, sv["wxbc"], "nt", F32, f"ssm_xbc_dx{i}", add=dhn)
            dhn = matmul(ddtr, sv["wdt"], "nt", F32, f"ssm_dt_dx{i}", add=dhn)
        else:
            j = i - N_A
            do = bs(matmul(dh, full["w_o"][j], "nt", BF16, f"sb_o_dx{j}"))
            G["w_o"][j] = matmul(fl(sv["o"]), dh, "tn", F32, f"sb_o_dw{j}")
            dq, dk, dv = sb_attention_bwd(sv["q"], kv, sv["ltot"], do, dk, dv, f"sb_attn_b{j}")
            dq = fl(dq)
            G["w_q"][j] = matmul(sv["hn"], dq, "tn", F32, f"sb_q_dw{j}")
            dhn = matmul(dq, full["w_q"][j], "nt", F32, f"sb_q_dx{j}")
        dh, gn_ = rmsnorm_bwd(dhn, sv["h0"], row(attn_norm[i]), dh, f"attn_norm_b{i}")
        G["attn_norm"][i] = gn_[0]
    grad_x = dh.reshape(bsz, seq, d)
    grads = {n: jnp.stack(G[n]) if isinstance(G[n], list) else G[n] for n in list(SMALL) + REPL}

    def col_slots(g, ndev=N_DEV):
        return jnp.transpose(g.reshape(g.shape[0], ndev, -1), (1, 0, 2))

    slots = {n: jnp.stack([g.reshape(N_DEV, -1, g.shape[-1]) for g in G[n]], axis=1)
             for n in ("ssm_out_proj", "w_q", "w_o", "ffn_down", "ple_gate")}
    slots["ssm_in_proj"] = jnp.stack([col_slots(g) for g in G["ssm_in_proj"]], axis=1)
    slots["ple_proj"] = jnp.stack([col_slots(g) for g in G["ple_proj"]], axis=1)
    slots["ffn_up"] = jnp.stack([jnp.concatenate([col_slots(gg, N_DEV // 2), col_slots(gv, N_DEV // 2)], axis=0)
                                 for gg, gv in G["ffn_up"]], axis=1)
    slots["w_kv"] = col_slots(G["w_kv"])
    small_slots = _pack_rows([_to_slots(grads[n], SMALL[n]) for n in SMALL], (N_DEV,))
    exchanged = all_to_all([slots[n].astype(BF16) for n in BIG] + [small_slots], "a2a_grads")
    parts_rp = all_gather([_pack_rows([grads[n].reshape(-1) for n in REPL])], "ag_repl_grads")[0]
    sh_names = list(BIG) + list(SMALL)
    sh_parts = list(exchanged[:-1]) + _unpack_rows(exchanged[-1], [given[n].size for n in SMALL], (N_DEV,))
    rp_parts = _unpack_rows(parts_rp, [given[n].size for n in REPL], (N_DEV,))
    res = {}
    for n, pr in list(zip(sh_names, sh_parts)) + list(zip(REPL, rp_parts)):
        w2 = _as2d(given[n])
        outs = adamw(w2, pr.reshape((N_DEV,) + w2.shape), _as2d(given["m_" + n]), _as2d(given["v_" + n]), f"adamw_{n}")
        res[n] = [o.reshape(given[n].shape) for o in outs]
    order = ["attn_norm", "ffn_norm", "ple_norm", "ssm_in_proj", "ssm_conv_w", "ssm_conv_b", "ssm_dt_bias", "ssm_a_log",
             "ssm_d", "ssm_norm", "ssm_out_proj", "kv_norm", "w_kv", "w_q", "w_o", "ffn_up", "ffn_conv_w", "ffn_conv_b",
             "ffn_down", "ple_gate", "ple_proj", "final_norm"]
    return (loss, grad_x, *[res[n][0] for n in order], *[res[n][1] for n in order],
            *[res[n][2] for n in order], *[res[n][3] for n in order])
```

```python
import functools
import math

import jax
import jax.numpy as jnp
from jax import lax
from jax.experimental import pallas as pl
from jax.experimental.pallas import tpu as pltpu

F32 = jnp.float32
BF16 = jnp.bfloat16

N_DEV = 8
D_MODEL = 1024
SEQ = 2048
DEPTH = 4
N_A = 2
N_B = 2
SSM_D_INNER = 2048
SSM_HEAD_DIM = 64
SSM_HEADS = 32
SSM_GROUPS = 4
SSM_STATE = 128
SSM_CONV = 4
SSM_CHUNK = 128
SSM_CONV_DIM = 3072
SSM_ZX = 5120
SSM_IN_DIM = 5152
SB_HEADS = 16
SB_HEAD_DIM = 64
SB_WIDTH = 1024
D_FF = 2816
FFN_CONV = 3
PLE_DIM = 256
NORM_EPS = 1e-6
SSM_NORM_EPS = 1e-5

ADAM_LR = 0.001
ADAM_B1 = 0.9
ADAM_B2 = 0.999
ADAM_EPS = 1e-08
ADAM_WD = 0.01
ADAM_STEP = 10

LANE = 128
SUBLANE = 8
VMEM_LIMIT = 48 * 1024 * 1024

NN = (((1,), (0,)), ((), ()))
NT = (((1,), (1,)), ((), ()))
TN = (((0,), (0,)), ((), ()))


def _params(sem):
    return pltpu.CompilerParams(dimension_semantics=sem, vmem_limit_bytes=VMEM_LIMIT)


def _pick(n, prefs):
    for p in prefs:
        if n % p == 0:
            return p
    return n


def matmul(a, b, mode, out_dtype, name, add=None):
    if mode == "nn":
        (m, k), (k2, n) = a.shape, b.shape
    elif mode == "nt":
        (m, k), (n, k2) = a.shape, b.shape
    else:
        (k, m), (k2, n) = a.shape, b.shape
    assert k == k2, (a.shape, b.shape, mode)
    tm = _pick(m, (512, 256, 128))
    tn = _pick(n, (1408, 1024, 512, 256, 128))
    tk = _pick(k, (2816, 2048, 1024, 512, 256, 128))
    nk = k // tk
    dims = {"nn": NN, "nt": NT, "tn": TN}[mode]

    def body(*refs):
        a_ref, b_ref = refs[:2]
        add_ref = refs[2] if add is not None else None
        o_ref = refs[-2] if nk > 1 else refs[-1]
        part = lax.dot_general(a_ref[...].astype(BF16), b_ref[...].astype(BF16), dims, preferred_element_type=F32)

        def finish(r):
            if add_ref is not None:
                r = r + add_ref[...].astype(F32)
            o_ref[...] = r.astype(o_ref.dtype)

        if nk == 1:
            finish(part)
            return
        acc_ref = refs[-1]
        kk = pl.program_id(2)

        @pl.when(kk == 0)
        def _():
            acc_ref[...] = part

        @pl.when((kk > 0) & (kk < nk - 1))
        def _():
            acc_ref[...] += part

        @pl.when(kk == nk - 1)
        def _():
            finish(acc_ref[...] + part)

    if mode == "nn":
        a_spec = pl.BlockSpec((tm, tk), lambda i, j, kk: (i, kk))
        b_spec = pl.BlockSpec((tk, tn), lambda i, j, kk: (kk, j))
    elif mode == "nt":
        a_spec = pl.BlockSpec((tm, tk), lambda i, j, kk: (i, kk))
        b_spec = pl.BlockSpec((tn, tk), lambda i, j, kk: (j, kk))
    else:
        a_spec = pl.BlockSpec((tk, tm), lambda i, j, kk: (kk, i))
        b_spec = pl.BlockSpec((tk, tn), lambda i, j, kk: (kk, j))
    o_spec = pl.BlockSpec((tm, tn), lambda i, j, kk: (i, j))
    in_specs = [a_spec, b_spec] + ([o_spec] if add is not None else [])
    args = (a, b) + ((add,) if add is not None else ())
    return pl.pallas_call(
        body,
        name=name,
        grid=(m // tm, n // tn, nk),
        in_specs=in_specs,
        out_specs=o_spec,
        out_shape=jax.ShapeDtypeStruct((m, n), out_dtype),
        scratch_shapes=[pltpu.VMEM((tm, tn), F32)] if nk > 1 else [],
        compiler_params=_params(("parallel", "parallel", "arbitrary")),
    )(*args)


ROW_TILE = 512


def rmsnorm_fwd(x, gain, name):
    t, d = x.shape

    def body(x_ref, g_ref, o_ref):
        xv = x_ref[...]
        r = lax.rsqrt(jnp.mean(xv * xv, axis=-1, keepdims=True) + NORM_EPS)
        o_ref[...] = (xv * r * g_ref[...]).astype(o_ref.dtype)

    return pl.pallas_call(
        body,
        name=name,
        grid=(t // ROW_TILE,),
        in_specs=[pl.BlockSpec((ROW_TILE, d), lambda i: (i, 0)), pl.BlockSpec((1, d), lambda i: (0, 0))],
        out_specs=pl.BlockSpec((ROW_TILE, d), lambda i: (i, 0)),
        out_shape=jax.ShapeDtypeStruct((t, d), BF16),
        compiler_params=_params(("parallel",)),
    )(x, gain)


def rmsnorm_bwd(dy, x, gain, dres, name):
    t, d = x.shape

    def body(dy_ref, x_ref, g_ref, dres_ref, dx_ref, dg_ref):
        i = pl.program_id(0)
        xv = x_ref[...]
        r = lax.rsqrt(jnp.mean(xv * xv, axis=-1, keepdims=True) + NORM_EPS)
        xh = xv * r
        dyv = dy_ref[...].astype(F32)
        dxh = dyv * g_ref[...]
        dx = r * (dxh - xh * jnp.mean(dxh * xh, axis=-1, keepdims=True))
        dx_ref[...] = dres_ref[...] + dx

        @pl.when(i == 0)
        def _():
            dg_ref[...] = jnp.zeros_like(dg_ref)

        dg_ref[...] += jnp.sum(dyv * xh, axis=0, keepdims=True)

    row = pl.BlockSpec((ROW_TILE, d), lambda i: (i, 0))
    vec = pl.BlockSpec((1, d), lambda i: (0, 0))
    return pl.pallas_call(
        body,
        name=name,
        grid=(t // ROW_TILE,),
        in_specs=[row, row, vec, row],
        out_specs=[row, vec],
        out_shape=[jax.ShapeDtypeStruct((t, d), F32), jax.ShapeDtypeStruct((1, d), F32)],
        compiler_params=_params(("arbitrary",)),
    )(dy, x, gain, dres)


CONV_ROWS = 512
HALO = SUBLANE


def _conv_apply(ext, w_ref, bias, k, lo, n):
    acc = bias + w_ref[k - 1:k, :] * ext[lo:lo + n]
    for j in range(1, k):
        acc = acc + w_ref[k - 1 - j:k - j, :] * pltpu.roll(ext, j, 0)[lo:lo + n]
    return acc


def _conv_apply_t(ext, w_ref, k, n):
    rows = ext.shape[0]
    acc = w_ref[k - 1:k, :] * ext[:n]
    for j in range(1, k):
        acc = acc + w_ref[k - 1 - j:k - j, :] * pltpu.roll(ext, rows - j, 0)[:n]
    return acc


def _conv_dw(dpre_main, ext_u, k, n, dw_ref):
    for kk in range(k):
        j = k - 1 - kk
        sh = ext_u[HALO:HALO + n] if j == 0 else pltpu.roll(ext_u, j, 0)[HALO:HALO + n]
        dw_ref[kk:kk + 1, :] += jnp.sum(dpre_main * sh, axis=0, keepdims=True)


def _silu_grad(pre):
    sg = jax.nn.sigmoid(pre)
    return sg * (1.0 + pre * (1.0 - sg))


def _conv_blockspecs(seq, r, tc):
    nh = r // HALO
    last = seq // HALO - 1
    main = pl.BlockSpec((None, r, tc), lambda c, b, s: (b, s, c))
    before = pl.BlockSpec((None, HALO, tc), lambda c, b, s: (b, jnp.maximum(s * nh - 1, 0), c))
    after = pl.BlockSpec((None, HALO, tc), lambda c, b, s: (b, jnp.minimum((s + 1) * nh, last), c))
    return main, before, after


def ssm_conv_fwd(u, w, bias, name):
    bsz, seq, ch = u.shape
    r, tc, k = min(CONV_ROWS, seq), 512, SSM_CONV
    main, before, _ = _conv_blockspecs(seq, r, tc)
    wspec = pl.BlockSpec((SUBLANE, tc), lambda c, b, s: (0, c))
    bspec = pl.BlockSpec((1, tc), lambda c, b, s: (0, c))

    def body(u_ref, hb_ref, w_ref, b_ref, o_ref):
        s = pl.program_id(2)
        hb = jnp.where(s == 0, 0.0, hb_ref[...])
        ext = jnp.concatenate([hb, u_ref[...]], axis=0)
        pre = _conv_apply(ext, w_ref, b_ref[...], k, HALO, r)
        o_ref[...] = pre * jax.nn.sigmoid(pre)

    return pl.pallas_call(
        body, name=name, grid=(ch // tc, bsz, seq // r),
        in_specs=[main, before, wspec, bspec], out_specs=main,
        out_shape=jax.ShapeDtypeStruct(u.shape, F32),
        compiler_params=_params(("parallel", "parallel", "parallel")),
    )(u, u, w, bias)


def ssm_conv_bwd(u, dxc, w, bias, name):
    bsz, seq, ch = u.shape
    r, tc, k = min(CONV_ROWS, seq), 512, SSM_CONV
    ns = seq // r
    main, before, after = _conv_blockspecs(seq, r, tc)
    wspec = pl.BlockSpec((SUBLANE, tc), lambda c, b, s: (0, c))
    bspec = pl.BlockSpec((1, tc), lambda c, b, s: (0, c))

    def body(u_ref, hb_ref, ha_ref, d_ref, da_ref, w_ref, b_ref, du_ref, dw_ref, db_ref):
        b, s = pl.program_id(1), pl.program_id(2)
        hb = jnp.where(s == 0, 0.0, hb_ref[...])
        ext_u = jnp.concatenate([hb, u_ref[...], ha_ref[...]], axis=0)
        pre = _conv_apply(ext_u, w_ref, b_ref[...], k, HALO, r + HALO)
        dxe = jnp.concatenate([d_ref[...], jnp.where(s == ns - 1, 0.0, da_ref[...])], axis=0)
        dpre = dxe * _silu_grad(pre)
        du_ref[...] = _conv_apply_t(dpre, w_ref, k, r).astype(du_ref.dtype)

        @pl.when((b == 0) & (s == 0))
        def _():
            dw_ref[...] = jnp.zeros_like(dw_ref)
            db_ref[...] = jnp.zeros_like(db_ref)

        dpm = dpre[:r]
        _conv_dw(dpm, ext_u, k, r, dw_ref)
        db_ref[...] += jnp.sum(dpm, axis=0, keepdims=True)

    return pl.pallas_call(
        body, name=name, grid=(ch // tc, bsz, ns),
        in_specs=[main, before, after, main, after, wspec, bspec],
        out_specs=[main, wspec, bspec],
        out_shape=[jax.ShapeDtypeStruct(u.shape, BF16), jax.ShapeDtypeStruct((SUBLANE, ch), F32),
                   jax.ShapeDtypeStruct((1, ch), F32)],
        compiler_params=_params(("arbitrary", "arbitrary", "arbitrary")),
    )(u, u, u, dxc, dxc, w, bias)


def ffn_conv_fwd(ug, uv, wg, wv, bg, bv, name):
    bsz, seq, ch = ug.shape
    r, tc, k = min(CONV_ROWS, seq), 256, FFN_CONV
    main, before, _ = _conv_blockspecs(seq, r, tc)
    wspec = pl.BlockSpec((SUBLANE, tc), lambda c, b, s: (0, c))
    bspec = pl.BlockSpec((1, tc), lambda c, b, s: (0, c))

    def body(ug_ref, hg_ref, uv_ref, hv_ref, wg_ref, wv_ref, bg_ref, bv_ref, o_ref):
        s = pl.program_id(2)
        eg = jnp.concatenate([jnp.where(s == 0, 0.0, hg_ref[...]), ug_ref[...]], axis=0)
        ev = jnp.concatenate([jnp.where(s == 0, 0.0, hv_ref[...]), uv_ref[...]], axis=0)
        pg = _conv_apply(eg, wg_ref, bg_ref[...], k, HALO, r)
        pv = _conv_apply(ev, wv_ref, bv_ref[...], k, HALO, r)
        o_ref[...] = (pg * jax.nn.sigmoid(pg) * pv).astype(o_ref.dtype)

    return pl.pallas_call(
        body, name=name, grid=(ch // tc, bsz, seq // r),
        in_specs=[main, before, main, before, wspec, wspec, bspec, bspec], out_specs=main,
        out_shape=jax.ShapeDtypeStruct(ug.shape, BF16),
        compiler_params=_params(("parallel", "parallel", "parallel")),
    )(ug, ug, uv, uv, wg, wv, bg, bv)


def ffn_conv_bwd(ug, uv, dact, wg, wv, bg, bv, name):
    bsz, seq, ch = ug.shape
    r, tc, k = min(CONV_ROWS, seq), 256, FFN_CONV
    ns = seq // r
    main, before, after = _conv_blockspecs(seq, r, tc)
    wspec = pl.BlockSpec((SUBLANE, tc), lambda c, b, s: (0, c))
    bspec = pl.BlockSpec((1, tc), lambda c, b, s: (0, c))

    def body(ug_ref, gb_ref, ga_ref, uv_ref, vb_ref, va_ref, d_ref, da_ref, wg_ref, wv_ref, bg_ref, bv_ref,
             dug_ref, duv_ref, dwg_ref, dwv_ref, dbg_ref, dbv_ref):
        b, s = pl.program_id(1), pl.program_id(2)
        eg = jnp.concatenate([jnp.where(s == 0, 0.0, gb_ref[...]), ug_ref[...], ga_ref[...]], axis=0)
        ev = jnp.concatenate([jnp.where(s == 0, 0.0, vb_ref[...]), uv_ref[...], va_ref[...]], axis=0)
        pg = _conv_apply(eg, wg_ref, bg_ref[...], k, HALO, r + HALO)
        pv = _conv_apply(ev, wv_ref, bv_ref[...], k, HALO, r + HALO)
        de = jnp.concatenate([d_ref[...], jnp.where(s == ns - 1, 0.0, da_ref[...])], axis=0)
        sg = jax.nn.sigmoid(pg)
        dpg = de * pv * (sg * (1.0 + pg * (1.0 - sg)))
        dpv = de * (pg * sg)
        dug_ref[...] = _conv_apply_t(dpg, wg_ref, k, r).astype(dug_ref.dtype)
        duv_ref[...] = _conv_apply_t(dpv, wv_ref, k, r).astype(duv_ref.dtype)

        @pl.when((b == 0) & (s == 0))
        def _():
            dwg_ref[...] = jnp.zeros_like(dwg_ref)
            dwv_ref[...] = jnp.zeros_like(dwv_ref)
            dbg_ref[...] = jnp.zeros_like(dbg_ref)
            dbv_ref[...] = jnp.zeros_like(dbv_ref)

        _conv_dw(dpg[:r], eg, k, r, dwg_ref)
        _conv_dw(dpv[:r], ev, k, r, dwv_ref)
        dbg_ref[...] += jnp.sum(dpg[:r], axis=0, keepdims=True)
        dbv_ref[...] += jnp.sum(dpv[:r], axis=0, keepdims=True)

    wshape = jax.ShapeDtypeStruct((SUBLANE, ch), F32)
    bshape = jax.ShapeDtypeStruct((1, ch), F32)
    return pl.pallas_call(
        body, name=name, grid=(ch // tc, bsz, ns),
        in_specs=[main, before, after, main, before, after, main, after, wspec, wspec, bspec, bspec],
        out_specs=[main, main, wspec, wspec, bspec, bspec],
        out_shape=[jax.ShapeDtypeStruct(ug.shape, BF16), jax.ShapeDtypeStruct(ug.shape, BF16), wshape, wshape, bshape, bshape],
        compiler_params=_params(("arbitrary", "arbitrary", "arbitrary")),
    )(ug, ug, ug, uv, uv, uv, dact, dact, wg, wv, bg, bv)


GN_ROWS = 256
GN_GROUP = SSM_D_INNER // SSM_GROUPS


def gated_norm_fwd(y, z, w, name):
    t, d = y.shape

    def body(y_ref, z_ref, w_ref, o_ref):
        for g in range(d // GN_GROUP):
            sl = slice(g * GN_GROUP, (g + 1) * GN_GROUP)
            zv = z_ref[:, sl]
            gv = y_ref[:, sl] * (zv * jax.nn.sigmoid(zv))
            r = lax.rsqrt(jnp.mean(gv * gv, axis=-1, keepdims=True) + SSM_NORM_EPS)
            o_ref[:, sl] = (gv * r * w_ref[:, sl]).astype(o_ref.dtype)

    row = pl.BlockSpec((GN_ROWS, d), lambda i: (i, 0))
    vec = pl.BlockSpec((1, d), lambda i: (0, 0))
    return pl.pallas_call(
        body, name=name, grid=(t // GN_ROWS,), in_specs=[row, row, vec], out_specs=row,
        out_shape=jax.ShapeDtypeStruct((t, d), BF16), compiler_params=_params(("parallel",)),
    )(y, z, w)


def gated_norm_bwd(dgn, y, z, w, name):
    t, d = y.shape

    def body(d_ref, y_ref, z_ref, w_ref, dy_ref, dz_ref, dw_ref):
        i = pl.program_id(0)

        @pl.when(i == 0)
        def _():
            dw_ref[...] = jnp.zeros_like(dw_ref)

        for g in range(d // GN_GROUP):
            sl = slice(g * GN_GROUP, (g + 1) * GN_GROUP)
            zv, yv, dv = z_ref[:, sl], y_ref[:, sl], d_ref[:, sl]
            sg = jax.nn.sigmoid(zv)
            sz = zv * sg
            gv = yv * sz
            r = lax.rsqrt(jnp.mean(gv * gv, axis=-1, keepdims=True) + SSM_NORM_EPS)
            gh = gv * r
            dgh = dv * w_ref[:, sl]
            dw_ref[:, sl] += jnp.sum(dv * gh, axis=0, keepdims=True)
            dg = r * (dgh - gh * jnp.mean(dgh * gh, axis=-1, keepdims=True))
            dy_ref[:, sl] = dg * sz
            dz_ref[:, sl] = (dg * yv * (sg * (1.0 + zv * (1.0 - sg)))).astype(dz_ref.dtype)

    row = pl.BlockSpec((GN_ROWS, d), lambda i: (i, 0))
    vec = pl.BlockSpec((1, d), lambda i: (0, 0))
    return pl.pallas_call(
        body, name=name, grid=(t // GN_ROWS,), in_specs=[row, row, row, vec], out_specs=[row, row, vec],
        out_shape=[jax.ShapeDtypeStruct((t, d), F32), jax.ShapeDtypeStruct((t, d), BF16), jax.ShapeDtypeStruct((1, d), F32)],
        compiler_params=_params(("arbitrary",)),
    )(dgn, y, z, w)


def ple_fwd(h, gpre, pp, name):
    t, d = h.shape

    def body(h_ref, g_ref, p_ref, o_ref):
        o_ref[...] = h_ref[...] + jax.nn.sigmoid(g_ref[...]) * p_ref[...]

    row = pl.BlockSpec((ROW_TILE, d), lambda i: (i, 0))
    return pl.pallas_call(
        body, name=name, grid=(t // ROW_TILE,), in_specs=[row, row, row], out_specs=row,
        out_shape=jax.ShapeDtypeStruct((t, d), F32), compiler_params=_params(("parallel",)),
    )(h, gpre, pp)


def ple_bwd(dh, gpre, pp, name):
    t, d = dh.shape

    def body(d_ref, g_ref, p_ref, dg_ref, dp_ref):
        sg = jax.nn.sigmoid(g_ref[...])
        dv = d_ref[...]
        dg_ref[...] = (dv * p_ref[...] * sg * (1.0 - sg)).astype(dg_ref.dtype)
        dp_ref[...] = (dv * sg).astype(dp_ref.dtype)

    row = pl.BlockSpec((ROW_TILE, d), lambda i: (i, 0))
    return pl.pallas_call(
        body, name=name, grid=(t // ROW_TILE,), in_specs=[row, row, row], out_specs=[row, row],
        out_shape=[jax.ShapeDtypeStruct((t, d), BF16), jax.ShapeDtypeStruct((t, d), BF16)],
        compiler_params=_params(("parallel",)),
    )(dh, gpre, pp)


def loss_head(h, gain, target, name):
    t, d = h.shape

    def body(x_ref, g_ref, t_ref, l_ref, dx_ref, dg_ref):
        i = pl.program_id(0)
        xv = x_ref[...]
        r = lax.rsqrt(jnp.mean(xv * xv, axis=-1, keepdims=True) + NORM_EPS)
        xh = xv * r
        err = xh * g_ref[...] - t_ref[...]
        part = 0.5 * jnp.sum(jnp.mean(err * err, axis=-1, keepdims=True), axis=0, keepdims=True)
        dyv = err * (1.0 / d)
        dxh = dyv * g_ref[...]
        dx_ref[...] = r * (dxh - xh * jnp.mean(dxh * xh, axis=-1, keepdims=True))

        @pl.when(i == 0)
        def _():
            dg_ref[...] = jnp.zeros_like(dg_ref)
            l_ref[...] = jnp.zeros_like(l_ref)

        dg_ref[...] += jnp.sum(dyv * xh, axis=0, keepdims=True)
        l_ref[...] += jnp.broadcast_to(part, l_ref.shape)

    row = pl.BlockSpec((ROW_TILE, d), lambda i: (i, 0))
    vec = pl.BlockSpec((1, d), lambda i: (0, 0))
    lspec = pl.BlockSpec((1, LANE), lambda i: (0, 0))
    return pl.pallas_call(
        body, name=name, grid=(t // ROW_TILE,), in_specs=[row, vec, row], out_specs=[lspec, row, vec],
        out_shape=[jax.ShapeDtypeStruct((1, LANE), F32), jax.ShapeDtypeStruct((t, d), F32), jax.ShapeDtypeStruct((1, d), F32)],
        compiler_params=_params(("arbitrary",)),
    )(h, gain, target)


CHUNK = SSM_CHUNK
HPG = SSM_HEADS // SSM_GROUPS
GW = HPG * SSM_HEAD_DIM
NEG = -1e30


def _ssd_dt_terms(dtr_ref, bias_ref, alog_ref, g):
    shift = (LANE - HPG * g) % LANE
    li = lax.broadcasted_iota(jnp.int32, (CHUNK, CHUNK), 0)
    si = lax.broadcasted_iota(jnp.int32, (CHUNK, CHUNK), 1)
    x0 = pltpu.roll(dtr_ref[...] + bias_ref[...], shift, 1)
    dt = jnp.maximum(x0, 0.0) + jnp.log1p(jnp.exp(-jnp.abs(x0)))
    a = pltpu.roll(jnp.broadcast_to(-jnp.exp(alog_ref[...]), (CHUNK, LANE)), shift, 1)
    adt = dt * a
    acs = lax.dot_general((li >= si).astype(F32), adt, NN, precision=lax.Precision.HIGHEST, preferred_element_type=F32)
    last = jnp.broadcast_to(acs[CHUNK - 1:CHUNK, :], (CHUNK, LANE))
    return dict(li=li, si=si, x0=x0, dt=dt, a=a, acs=acs, acs_t=acs.T, e_acs=jnp.exp(acs),
                w_all=jnp.exp(last - acs), e_last=jnp.exp(last))


def _ssd_specs(nc, rev):
    def ci(c):
        return nc - 1 - c if rev else c
    nb = SSM_D_INNER // LANE
    xs = pl.BlockSpec((None, CHUNK, GW), lambda g, b, c: (b, ci(c), g))
    bm = pl.BlockSpec((None, CHUNK, LANE), lambda g, b, c: (b, ci(c), nb + g))
    cm = pl.BlockSpec((None, CHUNK, LANE), lambda g, b, c: (b, ci(c), nb + SSM_GROUPS + g))
    dtr = pl.BlockSpec((None, CHUNK, LANE), lambda g, b, c: (b, ci(c), 0))
    vec = pl.BlockSpec((1, LANE), lambda g, b, c: (0, 0))
    dexp = pl.BlockSpec((1, GW), lambda g, b, c: (0, g))
    st = pl.BlockSpec((None, None, CHUNK, GW), lambda g, b, c: (b, ci(c), 0, g))
    return xs, bm, cm, dtr, vec, dexp, st


def ssd_fwd(xc, dtr, dt_bias, a_log, d_exp, name):
    bsz, seq, _ = xc.shape
    nc = seq // CHUNK
    xs_s, bm_s, cm_s, dtr_s, vec_s, dexp_s, st_s = _ssd_specs(nc, False)

    def body(xs_ref, b_ref, c_ref, dtr_ref, bias_ref, alog_ref, dexp_ref, y_ref, sp_ref, state_ref):
        g, c = pl.program_id(0), pl.program_id(2)

        @pl.when(c == 0)
        def _():
            state_ref[...] = jnp.zeros_like(state_ref)

        t = _ssd_dt_terms(dtr_ref, bias_ref, alog_ref, g)
        tril = t["li"] >= t["si"]
        xs = xs_ref[...]
        b16 = b_ref[...].astype(BF16)
        c16 = c_ref[...].astype(BF16)
        bt16 = b_ref[...].T.astype(BF16)
        cb = lax.dot_general(c16, b16, NT, preferred_element_type=F32)
        sp = state_ref[...]
        sp_ref[...] = sp
        cs = jnp.dot(c16, sp.astype(BF16), preferred_element_type=F32)
        y_parts, s_parts = [], []
        for r in range(HPG):
            sl = slice(r * SSM_HEAD_DIM, (r + 1) * SSM_HEAD_DIM)
            diff = t["acs"][:, r:r + 1] - t["acs_t"][r:r + 1, :]
            m16 = (cb * jnp.exp(jnp.where(tril, diff, NEG))).astype(BF16)
            xh = xs[:, sl]
            xd = xh * t["dt"][:, r:r + 1]
            yh = jnp.dot(m16, xd.astype(BF16), preferred_element_type=F32)
            yh = yh + t["e_acs"][:, r:r + 1] * cs[:, sl] + xh * dexp_ref[:, sl]
            y_parts.append(yh)
            xw16 = (xd * t["w_all"][:, r:r + 1]).astype(BF16)
            s_parts.append(sp[:, sl] * t["e_last"][:, r:r + 1] + jnp.dot(bt16, xw16, preferred_element_type=F32))
        y_ref[...] = jnp.concatenate(y_parts, axis=1)
        state_ref[...] = jnp.concatenate(s_parts, axis=1)

    return pl.pallas_call(
        body, name=name, grid=(SSM_GROUPS, bsz, nc),
        in_specs=[xs_s, bm_s, cm_s, dtr_s, vec_s, vec_s, dexp_s],
        out_specs=[xs_s, st_s],
        out_shape=[jax.ShapeDtypeStruct((bsz, seq, SSM_D_INNER), F32),
                   jax.ShapeDtypeStruct((bsz, nc, SSM_STATE, SSM_D_INNER), F32)],
        scratch_shapes=[pltpu.VMEM((SSM_STATE, GW), F32)],
        compiler_params=_params(("arbitrary", "arbitrary", "arbitrary")),
    )(xc, xc, xc, dtr, dt_bias, a_log, d_exp)


def ssd_bwd(xc, dtr, dt_bias, a_log, d_exp, dy, sprev, name):
    bsz, seq, _ = xc.shape
    nc = seq // CHUNK
    xs_s, bm_s, cm_s, dtr_s, vec_s, dexp_s, st_s = _ssd_specs(nc, True)
    grp = pl.BlockSpec((None, CHUNK, LANE), lambda g, b, c: (b, nc - 1 - c, g))
    acc = pl.BlockSpec((None, 1, LANE), lambda g, b, c: (g, 0, 0))

    def body(xs_ref, b_ref, c_ref, dtr_ref, bias_ref, alog_ref, dexp_ref, dy_ref, sp_ref,
             dxs_ref, db_ref, dc_ref, ddtr_ref, da_ref, dbias_ref, ddexp_ref, dstate_ref):
        g, b, c = pl.program_id(0), pl.program_id(1), pl.program_id(2)

        @pl.when(c == 0)
        def _():
            dstate_ref[...] = jnp.zeros_like(dstate_ref)

        @pl.when((b == 0) & (c == 0))
        def _():
            da_ref[...] = jnp.zeros_like(da_ref)
            dbias_ref[...] = jnp.zeros_like(dbias_ref)
            ddexp_ref[...] = jnp.zeros_like(ddexp_ref)

        t = _ssd_dt_terms(dtr_ref, bias_ref, alog_ref, g)
        li, si = t["li"], t["si"]
        tril, triu = li >= si, li <= si
        row_is_last = li[:, :1] == CHUNK - 1
        xs, dy = xs_ref[...], dy_ref[...]
        b16 = b_ref[...].astype(BF16)
        c16 = c_ref[...].astype(BF16)
        ct16 = c_ref[...].T.astype(BF16)
        cb = lax.dot_general(c16, b16, NT, preferred_element_type=F32)
        cbt = lax.dot_general(b16, c16, NT, preferred_element_type=F32)
        sp, ds = sp_ref[...], dstate_ref[...]
        sp16, ds16 = sp.astype(BF16), ds.astype(BF16)
        cs = jnp.dot(c16, sp16, preferred_element_type=F32)
        bds = jnp.dot(b16, ds16, preferred_element_type=F32)
        spds = jnp.sum(sp * ds, axis=0, keepdims=True)
        zero = jnp.zeros((CHUNK, CHUNK), F32)
        dsum_g, dsum_gt, dacs, ddt_dir = zero, zero, zero, zero
        dxs_parts, edy_parts, xw_parts, dsp_parts = [], [], [], []
        for r in range(HPG):
            sl = slice(r * SSM_HEAD_DIM, (r + 1) * SSM_HEAD_DIM)
            diff = t["acs"][:, r:r + 1] - t["acs_t"][r:r + 1, :]
            lam = jnp.exp(jnp.where(tril, diff, NEG))
            lam_t = jnp.exp(jnp.where(triu, -diff, NEG))
            m, m_t = cb * lam, cbt * lam_t
            xh, dyh = xs[:, sl], dy[:, sl]
            dtc, wc, ec, elc = t["dt"][:, r:r + 1], t["w_all"][:, r:r + 1], t["e_acs"][:, r:r + 1], t["e_last"][:, r:r + 1]
            xd = xh * dtc
            xd16, dyh16 = xd.astype(BF16), dyh.astype(BF16)
            dxd = jnp.dot(m_t.astype(BF16), dyh16, preferred_element_type=F32) + wc * bds[:, sl]
            dm = lax.dot_general(dyh16, xd16, NT, preferred_element_type=F32)
            dm_t = lax.dot_general(xd16, dyh16, NT, preferred_element_type=F32)
            dsum_g = dsum_g + dm * lam
            dsum_gt = dsum_gt + dm_t * lam_t
            dacs_h = jnp.sum(dm * m, axis=1, keepdims=True) - jnp.sum(dm_t * m_t, axis=1, keepdims=True)
            dacs_h = dacs_h + jnp.sum(dyh * (ec * cs[:, sl]), axis=1, keepdims=True)
            wdw = wc * jnp.sum(xd * bds[:, sl], axis=1, keepdims=True)
            last_h = jnp.sum(wdw, axis=0, keepdims=True) + elc[0:1, :] * jnp.sum(spds[:, sl], axis=1, keepdims=True)
            dacs_h = dacs_h - wdw + jnp.where(row_is_last, last_h, 0.0)
            onehot = (si[:1, :] == r).astype(F32)
            dacs = dacs + dacs_h * onehot
            ddt_dir = ddt_dir + jnp.sum(dxd * xh, axis=1, keepdims=True) * onehot
            dxs_parts.append(dxd * dtc + dyh * dexp_ref[:, sl])
            edy_parts.append(ec * dyh)
            xw_parts.append(wc * xd)
            dsp_parts.append(elc * ds[:, sl])
        dxs_ref[...] = jnp.concatenate(dxs_parts, axis=1)
        edy16 = jnp.concatenate(edy_parts, axis=1).astype(BF16)
        xw16 = jnp.concatenate(xw_parts, axis=1).astype(BF16)
        dc_ref[...] = (jnp.dot(dsum_g.astype(BF16), b16, preferred_element_type=F32)
                       + lax.dot_general(edy16, sp16, NT, preferred_element_type=F32))
        db_ref[...] = (jnp.dot(dsum_gt.astype(BF16), c16, preferred_element_type=F32)
                       + lax.dot_general(xw16, ds16, NT, preferred_element_type=F32))
        dstate_ref[...] = jnp.concatenate(dsp_parts, axis=1) + jnp.dot(ct16, edy16, preferred_element_type=F32)
        dadt = lax.dot_general(triu.astype(F32), dacs, NN, precision=lax.Precision.HIGHEST, preferred_element_type=F32)
        head_lane = si < HPG
        ddtr = jnp.where(head_lane, (dadt * t["a"] + ddt_dir) * jax.nn.sigmoid(t["x0"]), 0.0)
        ddtr_ref[...] = ddtr
        da_ref[...] += jnp.sum(jnp.where(head_lane, dadt * t["dt"] * t["a"], 0.0), axis=0, keepdims=True)
        dbias_ref[...] += jnp.sum(ddtr, axis=0, keepdims=True)
        ddexp_ref[...] += jnp.sum(dy * xs, axis=0, keepdims=True)

    return pl.pallas_call(
        body, name=name, grid=(SSM_GROUPS, bsz, nc),
        in_specs=[xs_s, bm_s, cm_s, dtr_s, vec_s, vec_s, dexp_s, xs_s, st_s],
        out_specs=[xs_s, grp, grp, grp, acc, acc, dexp_s],
        out_shape=[jax.ShapeDtypeStruct((bsz, seq, SSM_D_INNER), F32),
                   jax.ShapeDtypeStruct((bsz, seq, SSM_GROUPS * SSM_STATE), F32),
                   jax.ShapeDtypeStruct((bsz, seq, SSM_GROUPS * SSM_STATE), F32),
                   jax.ShapeDtypeStruct((bsz, seq, SSM_GROUPS * LANE), F32),
                   jax.ShapeDtypeStruct((SSM_GROUPS, 1, LANE), F32),
                   jax.ShapeDtypeStruct((SSM_GROUPS, 1, LANE), F32),
                   jax.ShapeDtypeStruct((1, SSM_D_INNER), F32)],
        scratch_shapes=[pltpu.VMEM((SSM_STATE, GW), F32)],
        compiler_params=_params(("arbitrary", "arbitrary", "arbitrary")),
    )(xc, xc, xc, dtr, dt_bias, a_log, d_exp, dy, sprev)


SB_TQ = 512
SB_TK = 128
SB_SCALE = SB_HEAD_DIM ** -0.5


def _split_dot(x, u16):
    hi = x.astype(BF16)
    lo = (x - hi.astype(F32)).astype(BF16)
    return jnp.dot(hi, u16, preferred_element_type=F32) + jnp.dot(lo, u16, preferred_element_type=F32)


def _sb_scores(qh, kj, mask):
    z = lax.dot_general(qh, kj, NT, preferred_element_type=F32)
    l1p = jnp.log(1.0 + jnp.exp(-jnp.abs(z)))
    lb = jnp.minimum(z, 0.0) - l1p
    lk = -jnp.maximum(z, 0.0) - l1p
    if mask is not None:
        lk = jnp.where(mask, lk, 0.0)
    return lb, lk


def _sb_diag_masks(tq):
    r = lax.broadcasted_iota(jnp.int32, (tq, SB_TK), 0)
    c = lax.broadcasted_iota(jnp.int32, (tq, SB_TK), 1)
    return [r > c + d * SB_TK for d in range(tq // SB_TK)]


def _sb_heads(x, lane):
    return [jnp.where((lane // SB_HEAD_DIM) == hh, x, jnp.zeros_like(x)) for hh in range(2)]


def sb_attention_fwd(q, kv, name):
    bsz, seq, width = q.shape
    tq = min(SB_TQ, seq)
    npair = width // LANE
    qspec = pl.BlockSpec((None, tq, LANE), lambda b, p, i: (b, i, p))
    kspec = pl.BlockSpec((None, seq, LANE), lambda b, p, i: (b, 0, p))
    vspec = pl.BlockSpec((None, seq, LANE), lambda b, p, i: (b, 0, npair + p))

    def body(q_ref, k_ref, v_ref, o_ref, l_ref):
        i = pl.program_id(2)
        ndiag = tq // SB_TK
        nfull = i * ndiag
        lane = lax.broadcasted_iota(jnp.int32, (tq, LANE), 1)
        ui = lax.broadcasted_iota(jnp.int32, (SB_TK, SB_TK), 0)
        uj = lax.broadcasted_iota(jnp.int32, (SB_TK, SB_TK), 1)
        u_excl = (ui > uj).astype(BF16)
        qs = _sb_heads(q_ref[...] * SB_SCALE, lane)

        def tile(j, carry, mask):
            rows = pl.ds(pl.multiple_of(j * SB_TK, SB_TK), SB_TK)
            kj, vj = k_ref[rows, :], v_ref[rows, :]
            new = []
            for hh in range(2):
                acc, run = carry[hh]
                lb, lk = _sb_scores(qs[hh], kj, mask)
                w = jnp.exp(lb + _split_dot(lk, u_excl) + run)
                if mask is not None:
                    w = jnp.where(mask, w, 0.0)
                acc = acc + jnp.dot(w.astype(BF16), vj, preferred_element_type=F32)
                new.append((acc, run + jnp.sum(lk, axis=1, keepdims=True)))
            return tuple(new)

        carry = tuple((jnp.zeros((tq, LANE), F32), jnp.zeros((tq, 1), F32)) for _ in range(2))
        masks = _sb_diag_masks(tq)
        for dd in reversed(range(ndiag)):
            carry = tile(nfull + dd, carry, masks[dd])
        carry = lax.fori_loop(0, nfull // 2, lambda jj, cr: tile(nfull - 2 - 2 * jj, tile(nfull - 1 - 2 * jj, cr, None), None), carry)
        in0 = (lane // SB_HEAD_DIM) == 0
        o_ref[...] = jnp.where(in0, carry[0][0], carry[1][0]).astype(o_ref.dtype)
        l_ref[...] = jnp.where(in0, carry[0][1], carry[1][1])

    return pl.pallas_call(
        body, name=name, grid=(bsz, npair, seq // tq),
        in_specs=[qspec, kspec, vspec], out_specs=[qspec, qspec],
        out_shape=[jax.ShapeDtypeStruct(q.shape, BF16), jax.ShapeDtypeStruct(q.shape, F32)],
        compiler_params=_params(("parallel", "parallel", "arbitrary")),
    )(q, kv, kv)


def sb_attention_bwd(q, kv, ltot, do, dk_in, dv_in, name):
    bsz, seq, width = q.shape
    tq = min(SB_TQ, seq)
    npair = width // LANE
    qspec = pl.BlockSpec((None, tq, LANE), lambda b, p, i: (b, i, p))
    kspec = pl.BlockSpec((None, seq, LANE), lambda b, p, i: (b, 0, p))
    vspec = pl.BlockSpec((None, seq, LANE), lambda b, p, i: (b, 0, npair + p))

    def body(q_ref, k_ref, v_ref, l_ref, do_ref, dki_ref, dvi_ref, dq_ref, dk_ref, dv_ref):
        i = pl.program_id(2)

        @pl.when(i == 0)
        def _():
            dk_ref[...] = dki_ref[...]
            dv_ref[...] = dvi_ref[...]

        ndiag = tq // SB_TK
        nfull = i * ndiag
        lane = lax.broadcasted_iota(jnp.int32, (tq, LANE), 1)
        ui = lax.broadcasted_iota(jnp.int32, (SB_TK, SB_TK), 0)
        uj = lax.broadcasted_iota(jnp.int32, (SB_TK, SB_TK), 1)
        u_le = (ui <= uj).astype(BF16)
        u_lt = (ui < uj).astype(BF16)
        qs = _sb_heads(q_ref[...] * SB_SCALE, lane)
        dos = _sb_heads(do_ref[...], lane)
        ltots = [l_ref[:, hh * SB_HEAD_DIM:hh * SB_HEAD_DIM + 1] for hh in range(2)]

        def tile(j, carry, mask):
            rows = pl.ds(pl.multiple_of(j * SB_TK, SB_TK), SB_TK)
            kj, vj = k_ref[rows, :], v_ref[rows, :]
            new, dk_t, dv_t = [], None, None
            for hh in range(2):
                acc, run, run_a = carry[hh]
                lb, lk = _sb_scores(qs[hh], kj, mask)
                w = jnp.exp(lb + (ltots[hh] - run - _split_dot(lk, u_le)))
                if mask is not None:
                    w = jnp.where(mask, w, 0.0)
                a = lax.dot_general(dos[hh], vj, NT, preferred_element_type=F32) * w
                ca = run_a + _split_dot(a, u_lt)
                sg = jnp.exp(lb)
                dz = a * (1.0 - sg) - ca * sg
                if mask is not None:
                    dz = jnp.where(mask, dz, 0.0)
                dz16 = dz.astype(BF16)
                acc = acc + jnp.dot(dz16, kj, preferred_element_type=F32)
                dkh = lax.dot_general(dz16, qs[hh], TN, preferred_element_type=F32)
                dvh = lax.dot_general(w.astype(BF16), dos[hh], TN, preferred_element_type=F32)
                dk_t = dkh if dk_t is None else dk_t + dkh
                dv_t = dvh if dv_t is None else dv_t + dvh
                new.append((acc, run + jnp.sum(lk, axis=1, keepdims=True), run_a + jnp.sum(a, axis=1, keepdims=True)))
            dk_ref[rows, :] += dk_t
            dv_ref[rows, :] += dv_t
            return tuple(new)

        zcol = jnp.zeros((tq, 1), F32)
        carry = tuple((jnp.zeros((tq, LANE), F32), zcol, zcol) for _ in range(2))
        carry = lax.fori_loop(0, nfull // 2, lambda j, cr: tile(2 * j + 1, tile(2 * j, cr, None), None), carry)
        masks = _sb_diag_masks(tq)
        for dd in range(ndiag):
            carry = tile(nfull + dd, carry, masks[dd])
        in0 = (lane // SB_HEAD_DIM) == 0
        dq_ref[...] = (jnp.where(in0, carry[0][0], carry[1][0]) * SB_SCALE).astype(dq_ref.dtype)

    return pl.pallas_call(
        body, name=name, grid=(bsz, npair, seq // tq),
        in_specs=[qspec, kspec, vspec, qspec, qspec, kspec, kspec], out_specs=[qspec, kspec, kspec],
        out_shape=[jax.ShapeDtypeStruct(q.shape, BF16), jax.ShapeDtypeStruct(q.shape, F32), jax.ShapeDtypeStruct(q.shape, F32)],
        compiler_params=_params(("parallel", "parallel", "arbitrary")),
    )(q, kv, kv, ltot, do, dk_in, dv_in)


def adamw(w, parts, m, v, name):
    r, c = w.shape
    tr = r
    for cand in (256, 176, 128, 64, 32, 16, 8):
        if r % cand == 0:
            tr = cand
            break
    bc1 = 1.0 - ADAM_B1 ** ADAM_STEP
    bc2 = 1.0 - ADAM_B2 ** ADAM_STEP

    def body(w_ref, p_ref, m_ref, v_ref, g_ref, d_ref, mo_ref, vo_ref):
        g = p_ref[0].astype(F32)
        for j in range(1, N_DEV):
            g = g + p_ref[j].astype(F32)
        mn = ADAM_B1 * m_ref[...] + (1.0 - ADAM_B1) * g
        vn = ADAM_B2 * v_ref[...] + (1.0 - ADAM_B2) * (g * g)
        g_ref[...] = g
        mo_ref[...] = mn
        vo_ref[...] = vn
        d_ref[...] = -ADAM_LR * ((mn / bc1) / (jnp.sqrt(vn / bc2) + ADAM_EPS) + ADAM_WD * w_ref[...])

    blk = pl.BlockSpec((tr, c), lambda i: (i, 0))
    pblk = pl.BlockSpec((N_DEV, tr, c), lambda i: (0, i, 0))
    shp = jax.ShapeDtypeStruct((r, c), F32)
    return pl.pallas_call(
        body, name=name, grid=(r // tr,), in_specs=[blk, pblk, blk, blk], out_specs=[blk, blk, blk, blk],
        out_shape=[shp, shp, shp, shp], compiler_params=_params(("parallel",)),
    )(w, parts, m, v)


MESH_ID = pl.DeviceIdType.MESH
ANY = pl.BlockSpec(memory_space=pl.ANY)


def all_gather(xs, name):
    n = len(xs)

    def body(*refs):
        x_refs, out_refs = refs[:n], refs[n:2 * n]
        send_sems, recv_sems, local_sems = refs[2 * n:]
        mx, my, mc = lax.axis_index("x"), lax.axis_index("y"), lax.axis_index("c")
        me, sibling = (mx, my, mc), (mx, my, 1 - mc)
        chips = [(1 - mx, my), (mx, 1 - my), (1 - mx, 1 - my)]

        def slot(a, px, py, pc):
            return out_refs[a].at[4 * px + 2 * py + pc]

        def copy(a, k, block, to, src=None):
            return pltpu.make_async_remote_copy(
                src_ref=slot(a, *block) if src is None else src, dst_ref=slot(a, *block),
                send_sem=send_sems.at[7 * a + k], recv_sem=recv_sems.at[7 * a + k], device_id=to, device_id_type=MESH_ID)

        mine = [pltpu.make_async_copy(x_refs[a], slot(a, *me), local_sems.at[a]) for a in range(n)]
        for cp in mine:
            cp.start()
        first = []
        for a in range(n):
            first.append(copy(a, 0, me, sibling, src=x_refs[a]))
            first += [copy(a, 1 + j, me, (*chip, mc), src=x_refs[a]) for j, chip in enumerate(chips)]
        for cp in first:
            cp.start()
        passed = []
        for j, chip in enumerate(chips):
            for a in range(n):
                copy(a, 1 + j, (*chip, mc), me).wait_recv()
                cp = copy(a, 4 + j, (*chip, mc), sibling)
                cp.start()
                passed.append(cp)
        for a in range(n):
            copy(a, 0, sibling, me).wait_recv()
            for j, chip in enumerate(chips):
                copy(a, 4 + j, (*chip, 1 - mc), me).wait_recv()
        for cp in first + passed:
            cp.wait_send()
        for cp in mine:
            cp.wait()

    return pl.pallas_call(
        body, name=name, out_shape=[jax.ShapeDtypeStruct((N_DEV,) + x.shape, x.dtype) for x in xs],
        in_specs=[ANY] * n, out_specs=[ANY] * n,
        scratch_shapes=[pltpu.SemaphoreType.DMA((7 * n,)), pltpu.SemaphoreType.DMA((7 * n,)), pltpu.SemaphoreType.DMA((n,))],
    )(*xs)


HBM_SPEC = pl.BlockSpec(memory_space=pltpu.HBM)
SEM_SPEC = pl.BlockSpec(memory_space=pltpu.SEMAPHORE)
DATAFLOW = pltpu.SideEffectType.DATAFLOW_SIDE_EFFECTING


def _peer(j, mx, my, mc):
    px = 1 - mx if j & 4 else mx
    py = 1 - my if j & 2 else my
    pc = 1 - mc if j & 1 else mc
    return (px, py, pc), 4 * px + 2 * py + pc


def _split_copy(kind, src_ref, land_ref, send_sems, recv_sems, a, j, mx, my, mc):
    dev, peer = _peer(j, mx, my, mc)
    me = 4 * mx + 2 * my + mc
    src = src_ref if kind == "gather" else src_ref.at[peer]
    k = 7 * a + j - 1
    return pltpu.make_async_remote_copy(src_ref=src, dst_ref=land_ref.at[me], send_sem=send_sems.at[k],
                                        recv_sem=recv_sems.at[k], device_id=dev, device_id_type=MESH_ID), peer


def exchange_start(kind, srcs, name):
    n = len(srcs)
    blocks = [s.shape if kind == "gather" else s.shape[1:] for s in srcs]
    lands = [lax.empty((N_DEV,) + tuple(b), s.dtype) for b, s in zip(blocks, srcs)]

    def body(*refs):
        src_refs, land_refs = refs[:n], refs[n:2 * n]
        send_sems, recv_sems = refs[2 * n], refs[2 * n + 1]
        token = refs[-1]
        mx, my, mc = lax.axis_index("x"), lax.axis_index("y"), lax.axis_index("c")
        for j in range(1, N_DEV):
            for a in range(n):
                _split_copy(kind, src_refs[a], land_refs[a], send_sems, recv_sems, a, j, mx, my, mc)[0].start()
        token[...] = jnp.zeros_like(token)

    hbm = lambda v: pltpu.HBM(v.shape, v.dtype)
    outs = pl.pallas_call(
        body, name=name,
        out_shape=(pltpu.SemaphoreType.DMA((7 * n,)), pltpu.SemaphoreType.DMA((7 * n,)), *[hbm(s) for s in srcs],
                   *[hbm(l) for l in lands], jax.ShapeDtypeStruct((SUBLANE, LANE), F32)),
        in_specs=[HBM_SPEC] * (2 * n),
        out_specs=(SEM_SPEC, SEM_SPEC, *[HBM_SPEC] * (2 * n), pl.BlockSpec(memory_space=pltpu.VMEM)),
        input_output_aliases={a: 2 + a for a in range(2 * n)},
        compiler_params=pltpu.CompilerParams(has_side_effects=DATAFLOW),
    )(*[pltpu.with_memory_space_constraint(v, pltpu.HBM) for v in list(srcs) + lands])
    return dict(kind=kind, n=n, send=outs[0], recv=outs[1], srcs=outs[2:2 + n], lands=outs[2 + n:2 + 2 * n], token=outs[-1])


def exchange_wait(h, after, name):
    n, kind = h["n"], h["kind"]

    def body(*refs):
        src_refs, land_refs = refs[:n], refs[n:2 * n]
        send_sems, recv_sems = refs[2 * n], refs[2 * n + 1]
        mx, my, mc = lax.axis_index("x"), lax.axis_index("y"), lax.axis_index("c")
        for j in range(1, N_DEV):
            for a in range(n):
                cp, peer = _split_copy(kind, src_refs[a], land_refs[a], send_sems, recv_sems, a, j, mx, my, mc)
                cp.wait_send()
                pltpu.make_async_remote_copy(
                    src_ref=land_refs[a].at[peer], dst_ref=land_refs[a].at[peer], send_sem=send_sems.at[7 * a + j - 1],
                    recv_sem=recv_sems.at[7 * a + j - 1], device_id=_peer(j, mx, my, mc)[0], device_id_type=MESH_ID).wait_recv()

    hbm = lambda v: pltpu.HBM(v.shape, v.dtype)
    outs = pl.pallas_call(
        body, name=name,
        out_shape=tuple(hbm(v) for v in list(h["srcs"]) + list(h["lands"])),
        in_specs=[HBM_SPEC] * (2 * n) + [SEM_SPEC, SEM_SPEC, ANY],
        out_specs=tuple([HBM_SPEC] * (2 * n)),
        input_output_aliases={a: a for a in range(2 * n)},
        compiler_params=pltpu.CompilerParams(has_side_effects=DATAFLOW),
    )(*h["srcs"], *h["lands"], h["send"], h["recv"], after)
    return list(outs[n:])


PACK_COLS = 1024
PACK_SEG = 16 * PACK_COLS

BIG = {"ssm_in_proj": 2, "ssm_out_proj": 1, "w_kv": 1, "w_q": 1, "w_o": 1, "ffn_up": 2, "ffn_down": 1,
       "ple_gate": 1, "ple_proj": 2}
SMALL = {"ssm_conv_w": 2, "ssm_conv_b": 1, "ssm_norm": 1, "ffn_conv_w": 2}
REPL = ["attn_norm", "ffn_norm", "ple_norm", "ssm_dt_bias", "ssm_a_log", "ssm_d", "ffn_conv_b", "kv_norm", "final_norm"]


def _seg(n):
    return -(-n // PACK_SEG) * PACK_SEG


def _pack_rows(flats, lead=()):
    padded = [jnp.pad(f, [(0, 0)] * len(lead) + [(0, _seg(f.shape[-1]) - f.shape[-1])]) for f in flats]
    return jnp.concatenate(padded, axis=-1).reshape(*lead, -1, PACK_COLS)


def _unpack_rows(buf, sizes, lead):
    flat = buf.reshape(*lead, -1)
    out, off = [], 0
    for n in sizes:
        out.append(lax.slice_in_dim(flat, off, off + n, axis=len(lead)))
        off += _seg(n)
    return out


def _to_slots(full, axis):
    shp = full.shape
    blk = shp[axis] // N_DEV
    t = full.reshape(shp[:axis] + (N_DEV, blk) + shp[axis + 1:])
    return jnp.moveaxis(t, axis, 0).reshape(N_DEV, -1)


def _from_slots(slots, local_shape, axis):
    t = jnp.moveaxis(slots.reshape((N_DEV,) + tuple(local_shape)), 0, axis)
    shp = list(local_shape)
    shp[axis] *= N_DEV
    return t.reshape(shp)


def _as2d(a):
    return a.reshape(1, -1) if a.ndim == 1 else a.reshape(-1, a.shape[-1])


def _pad_rows(w, rows=SUBLANE):
    return jnp.pad(w, ((0, rows - w.shape[0]), (0, 0)))


def kernel(x, p, attn_norm, ffn_norm, ple_norm, ssm_in_proj, ssm_conv_w, ssm_conv_b, ssm_dt_bias, ssm_a_log, ssm_d, ssm_norm, ssm_out_proj, kv_norm, w_kv, w_q, w_o, ffn_up, ffn_conv_w, ffn_conv_b, ffn_down, ple_gate, ple_proj, final_norm, loss_target, m_attn_norm, m_ffn_norm, m_ple_norm, m_ssm_in_proj, m_ssm_conv_w, m_ssm_conv_b, m_ssm_dt_bias, m_ssm_a_log, m_ssm_d, m_ssm_norm, m_ssm_out_proj, m_kv_norm, m_w_kv, m_w_q, m_w_o, m_ffn_up, m_ffn_conv_w, m_ffn_conv_b, m_ffn_down, m_ple_gate, m_ple_proj, m_final_norm, v_attn_norm, v_ffn_norm, v_ple_norm, v_ssm_in_proj, v_ssm_conv_w, v_ssm_conv_b, v_ssm_dt_bias, v_ssm_a_log, v_ssm_d, v_ssm_norm, v_ssm_out_proj, v_kv_norm, v_w_kv, v_w_q, v_w_o, v_ffn_up, v_ffn_conv_w, v_ffn_conv_b, v_ffn_down, v_ple_gate, v_ple_proj, v_final_norm):
    given = dict(locals())
    wnames = list(BIG) + list(SMALL) + REPL
    bsz, seq, d = x.shape
    t = bsz * seq
    bs = lambda a: a.reshape(bsz, seq, a.shape[-1])
    fl = lambda a: a.reshape(t, a.shape[-1])

    big_local = [given[n] for n in BIG]
    small_local = [given[n] for n in SMALL]
    me = 4 * lax.axis_index("x") + 2 * lax.axis_index("y") + lax.axis_index("c")

    def layer_items(i):
        j = i - N_A
        mixer = [("ssm_in_proj", i), ("ssm_out_proj", i)] if i < N_A else [("w_q", j), ("w_o", j)]
        shared = [("w_kv", None)] if i == N_A - 1 else []
        return mixer + [("ffn_up", i), ("ffn_down", i), ("ple_gate", i), ("ple_proj", i)] + shared

    def local_block(n, idx):
        return given[n] if idx is None else given[n][idx]

    def put_own(land, own):
        return lax.dynamic_update_slice(land, own[None], (me,) + (0,) * own.ndim)

    small_pack = _pack_rows([w.reshape(-1) for w in small_local])
    gather_src, gather_h = [], []
    for i in range(DEPTH):
        src = [local_block(n, idx).astype(BF16) for n, idx in layer_items(i)] + ([small_pack] if i == 0 else [])
        gather_src.append(src)
        gather_h.append(exchange_start("gather", src, f"ag_start{i}"))
    full = {n: {} for n in BIG}
    rows_of = lambda g: g.reshape(-1, g.shape[-1])
    cols_of = lambda g: jnp.transpose(g, (1, 0, 2)).reshape(g.shape[1], -1)

    def fetch_layer(i, after):
        lands = exchange_wait(gather_h[i], after, f"ag_wait{i}")
        got = [put_own(l, s) for l, s in zip(lands, gather_src[i])]
        for (n, idx), g in zip(layer_items(i), got):
            if n == "ffn_up":
                full[n][idx] = (cols_of(g[:N_DEV // 2]), cols_of(g[N_DEV // 2:]))
            else:
                full[n][idx] = cols_of(g) if BIG[n] == given[n].ndim - 1 else rows_of(g)
        if i == 0:
            for (n, ax), w, s in zip(SMALL.items(), small_local, _unpack_rows(got[-1], [w.size for w in small_local], (N_DEV,))):
                full[n] = _from_slots(s, w.shape, ax)

    row = lambda v: v.reshape(1, -1)
    pad_lane = lambda v: jnp.pad(v.reshape(1, -1), ((0, 0), (0, LANE - v.size)))
    h = x.reshape(t, d)
    saved = []
    kv = hnkv = h_kv = None
    for i in range(DEPTH):
        fetch_layer(i, h)
        sv = {"h0": h}
        hn = rmsnorm_fwd(h, row(attn_norm[i]), f"attn_norm_f{i}")
        sv["hn"] = hn
        if i < N_A:
            w_in = full["ssm_in_proj"][i]
            wz, wxbc = w_in[:, :SSM_D_INNER], w_in[:, SSM_D_INNER:SSM_ZX]
            wdt = jnp.pad(w_in[:, SSM_ZX:], ((0, 0), (0, LANE - SSM_HEADS)))
            z = matmul(hn, wz, "nn", F32, f"ssm_z_f{i}")
            xbc = bs(matmul(hn, wxbc, "nn", F32, f"ssm_xbc_f{i}"))
            dtr = bs(matmul(hn, wdt, "nn", F32, f"ssm_dt_f{i}"))
            cw = _pad_rows(full["ssm_conv_w"][i])
            cb = row(full["ssm_conv_b"][i])
            xc = ssm_conv_fwd(xbc, cw, cb, f"ssm_conv_f{i}")
            dtb, alog = pad_lane(ssm_dt_bias[i]), pad_lane(ssm_a_log[i])
            dexp = jnp.repeat(ssm_d[i], SSM_HEAD_DIM).reshape(1, -1)
            y, sprev = ssd_fwd(xc, dtr, dtb, alog, dexp, f"ssd_f{i}")
            gn = gated_norm_fwd(fl(y), z, row(full["ssm_norm"][i]), f"ssm_gnorm_f{i}")
            h1 = matmul(gn, full["ssm_out_proj"][i], "nn", F32, f"ssm_out_f{i}", add=h)
            sv.update(wz=wz, wxbc=wxbc, wdt=wdt, z=z, xbc=xbc, dtr=dtr, cw=cw, cb=cb, xc=xc, dtb=dtb, alog=alog,
                      dexp=dexp, y=y, sprev=sprev, gn=gn)
        else:
            j = i - N_A
            q = bs(matmul(hn, full["w_q"][j], "nn", BF16, f"sb_q_f{j}"))
            o, ltot = sb_attention_fwd(q, kv, f"sb_attn_f{j}")
            h1 = matmul(fl(o), full["w_o"][j], "nn", F32, f"sb_o_f{j}", add=h)
            sv.update(q=q, o=o, ltot=ltot)
        sv["h1"] = h1
        hn2 = rmsnorm_fwd(h1, row(ffn_norm[i]), f"ffn_norm_f{i}")
        wug, wuv = full["ffn_up"][i]
        ug = bs(matmul(hn2, wug, "nn", F32, f"ffn_upg_f{i}"))
        uv = bs(matmul(hn2, wuv, "nn", F32, f"ffn_upv_f{i}"))
        fcw = full["ffn_conv_w"][i]
        fwg, fwv = _pad_rows(fcw[:, :D_FF]), _pad_rows(fcw[:, D_FF:])
        fbg, fbv = row(ffn_conv_b[i, :D_FF]), row(ffn_conv_b[i, D_FF:])
        act = ffn_conv_fwd(ug, uv, fwg, fwv, fbg, fbv, f"ffn_conv_f{i}")
        h2 = matmul(fl(act), full["ffn_down"][i], "nn", F32, f"ffn_down_f{i}", add=h1)
        hn3 = rmsnorm_fwd(h2, row(ple_norm[i]), f"ple_norm_f{i}")
        gpre = matmul(hn3, full["ple_gate"][i], "nn", F32, f"ple_gate_f{i}")
        p_i = p[i].reshape(t, PLE_DIM)
        pp = matmul(p_i, full["ple_proj"][i], "nn", F32, f"ple_proj_f{i}")
        h3 = ple_fwd(h2, gpre, pp, f"ple_f{i}")
        sv.update(hn2=hn2, wug=wug, wuv=wuv, ug=ug, uv=uv, fwg=fwg, fwv=fwv, fbg=fbg, fbv=fbv, act=act, h2=h2,
                  hn3=hn3, gpre=gpre, pp=pp, p_i=p_i)
        if i == N_A - 1:
            h_kv = h3
            hnkv = rmsnorm_fwd(h3, row(kv_norm), "kv_norm_f")
            kv = bs(matmul(hnkv, full["w_kv"][None], "nn", BF16, "kv_f"))
        h = h3
        saved.append(sv)

    loss_row, dh, g_final = loss_head(h, row(final_norm), loss_target.reshape(t, d), "loss_head")
    loss = lax.psum(loss_row[0, 0], ("x", "y", "c"))

    G = {n: [None] * given[n].shape[0] for n in wnames if given[n].ndim > 1 and n not in ("w_kv",)}
    G["final_norm"] = g_final[0]

    def col_slots(g, ndev=N_DEV):
        return jnp.transpose(g.reshape(g.shape[0], ndev, -1), (1, 0, 2))

    def grad_slots(n, idx):
        g = G[n] if idx is None else G[n][idx]
        if n == "ffn_up":
            s = jnp.concatenate([col_slots(g[0], N_DEV // 2), col_slots(g[1], N_DEV // 2)], axis=0)
        elif BIG[n] == given[n].ndim - 1:
            s = col_slots(g)
        else:
            s = g.reshape(N_DEV, -1, g.shape[-1])
        return s.astype(BF16)

    scatter_src, scatter_h = {}, {}
    dk = jnp.zeros((bsz, seq, SB_WIDTH), F32)
    dv = jnp.zeros((bsz, seq, SB_WIDTH), F32)
    for i in reversed(range(DEPTH)):
        sv = saved[i]
        if i == N_A - 1:
            wk, wv = full["w_kv"][None][:, :SB_WIDTH], full["w_kv"][None][:, SB_WIDTH:]
            dkf, dvf = fl(dk), fl(dv)
            dhn = matmul(dkf, wk, "nt", F32, "kv_dx_k")
            dhn = matmul(dvf, wv, "nt", F32, "kv_dx_v", add=dhn)
            G["w_kv"] = jnp.concatenate([matmul(hnkv, dkf, "tn", F32, "kv_dw_k"), matmul(hnkv, dvf, "tn", F32, "kv_dw_v")], axis=1)
            dh, gk = rmsnorm_bwd(dhn, h_kv, row(kv_norm), dh, "kv_norm_b")
            G["kv_norm"] = gk[0]
        dgpre, dpp = ple_bwd(dh, sv["gpre"], sv["pp"], f"ple_b{i}")
        G["ple_proj"][i] = matmul(sv["p_i"], dpp, "tn", F32, f"ple_proj_dw{i}")
        G["ple_gate"][i] = matmul(sv["hn3"], dgpre, "tn", F32, f"ple_gate_dw{i}")
        dhn = matmul(dgpre, full["ple_gate"][i], "nt", F32, f"ple_gate_dx{i}")
        dh, gn_ = rmsnorm_bwd(dhn, sv["h2"], row(ple_norm[i]), dh, f"ple_norm_b{i}")
        G["ple_norm"][i] = gn_[0]
        dact = bs(matmul(dh, full["ffn_down"][i], "nt", F32, f"ffn_down_dx{i}"))
        G["ffn_down"][i] = matmul(fl(sv["act"]), dh, "tn", F32, f"ffn_down_dw{i}")
        dug, duv, dwg, dwv, dbg, dbv = ffn_conv_bwd(sv["ug"], sv["uv"], dact, sv["fwg"], sv["fwv"], sv["fbg"], sv["fbv"], f"ffn_conv_b{i}")
        dug, duv = fl(dug), fl(duv)
        G["ffn_conv_w"][i] = jnp.concatenate([dwg[:FFN_CONV], dwv[:FFN_CONV]], axis=1)
        G["ffn_conv_b"][i] = jnp.concatenate([dbg[0], dbv[0]])
        G["ffn_up"][i] = (matmul(sv["hn2"], dug, "tn", F32, f"ffn_upg_dw{i}"), matmul(sv["hn2"], duv, "tn", F32, f"ffn_upv_dw{i}"))
        dhn = matmul(dug, sv["wug"], "nt", F32, f"ffn_upg_dx{i}")
        dhn = matmul(duv, sv["wuv"], "nt", F32, f"ffn_upv_dx{i}", add=dhn)
        dh, gn_ = rmsnorm_bwd(dhn, sv["h1"], row(ffn_norm[i]), dh, f"ffn_norm_b{i}")
        G["ffn_norm"][i] = gn_[0]
        if i < N_A:
            dgn = matmul(dh, full["ssm_out_proj"][i], "nt", F32, f"ssm_out_dx{i}")
            G["ssm_out_proj"][i] = matmul(sv["gn"], dh, "tn", F32, f"ssm_out_dw{i}")
            dy, dz, dnw = gated_norm_bwd(dgn, fl(sv["y"]), sv["z"], row(full["ssm_norm"][i]), f"ssm_gnorm_b{i}")
            G["ssm_norm"][i] = dnw[0]
            dxs, dbm, dcm, ddtr4, da4, dbias4, ddexp = ssd_bwd(sv["xc"], sv["dtr"], sv["dtb"], sv["alog"], sv["dexp"], bs(dy), sv["sprev"], f"ssd_b{i}")
            G["ssm_a_log"][i] = da4[:, 0, :HPG].reshape(-1)
            G["ssm_dt_bias"][i] = dbias4[:, 0, :HPG].reshape(-1)
            G["ssm_d"][i] = ddexp.reshape(SSM_HEADS, SSM_HEAD_DIM).sum(axis=-1)
            dxc = jnp.concatenate([dxs, dbm, dcm], axis=-1)
            dxbc, dcw, dcb = ssm_conv_bwd(sv["xbc"], dxc, sv["cw"], sv["cb"], f"ssm_conv_b{i}")
            dxbc = fl(dxbc)
            G["ssm_conv_w"][i] = dcw[:SSM_CONV]
            G["ssm_conv_b"][i] = dcb[0]
            ddtr = ddtr4.reshape(t, SSM_GROUPS, LANE)[:, :, :HPG].reshape(t, SSM_HEADS)
            ddtr = jnp.pad(ddtr, ((0, 0), (0, LANE - SSM_HEADS))).astype(BF16)
            hn = sv["hn"]
            G["ssm_in_proj"][i] = jnp.concatenate([
                matmul(hn, dz, "tn", F32, f"ssm_z_dw{i}"), matmul(hn, dxbc, "tn", F32, f"ssm_xbc_dw{i}"),
                matmul(hn, ddtr, "tn", F32, f"ssm_dt_dw{i}")[:, :SSM_HEADS]], axis=1)
            dhn = matmul(dz, sv["wz"], "nt", F32, f"ssm_z_dx{i}")
            dhn = matmul(dxbc, sv["wxbc"], "nt", F32, f"ssm_xbc_dx{i}", add=dhn)
            dhn = matmul(ddtr, sv["wdt"], "nt", F32, f"ssm_dt_dx{i}", add=dhn)
        else:
            j = i - N_A
            do = bs(matmul(dh, full["w_o"][j], "nt", BF16, f"sb_o_dx{j}"))
            G["w_o"][j] = matmul(fl(sv["o"]), dh, "tn", F32, f"sb_o_dw{j}")
            dq, dk, dv = sb_attention_bwd(sv["q"], kv, sv["ltot"], do, dk, dv, f"sb_attn_b{j}")
            dq = fl(dq)
            G["w_q"][j] = matmul(sv["hn"], dq, "tn", F32, f"sb_q_dw{j}")
            dhn = matmul(dq, full["w_q"][j], "nt", F32, f"sb_q_dx{j}")
        dh, gn_ = rmsnorm_bwd(dhn, sv["h0"], row(attn_norm[i]), dh, f"attn_norm_b{i}")
        G["attn_norm"][i] = gn_[0]
        src = [grad_slots(n, idx) for n, idx in layer_items(i)]
        if i == 0:
            small_g = {n: jnp.stack(G[n]) for n in SMALL}
            src.append(_pack_rows([_to_slots(small_g[n], SMALL[n]) for n in SMALL], (N_DEV,)))
        scatter_src[i] = src
        scatter_h[i] = exchange_start("scatter", src, f"a2a_start{i}")
    grad_x = dh.reshape(bsz, seq, d)
    grads = {n: jnp.stack(G[n]) if isinstance(G[n], list) else G[n] for n in REPL}

    layer_parts = {n: {} for n in BIG}
    small_parts = None
    for i in reversed(range(DEPTH)):
        lands = exchange_wait(scatter_h[i], dh, f"a2a_wait{i}")
        got = [put_own(l, lax.dynamic_index_in_dim(s, me, 0, keepdims=False)) for l, s in zip(lands, scatter_src[i])]
        for (n, idx), g in zip(layer_items(i), got):
            layer_parts[n][idx] = g
        if i == 0:
            small_parts = _unpack_rows(got[-1], [given[n].size for n in SMALL], (N_DEV,))
    big_parts = [jnp.concatenate([layer_parts[n][k] for k in sorted(layer_parts[n], key=lambda v: -1 if v is None else v)], axis=1)
                 for n in BIG]
    parts_rp = all_gather([_pack_rows([grads[n].reshape(-1) for n in REPL])], "ag_repl_grads")[0]
    sh_names = list(BIG) + list(SMALL)
    sh_parts = big_parts + small_parts
    rp_parts = _unpack_rows(parts_rp, [given[n].size for n in REPL], (N_DEV,))
    res = {}
    for n, pr in list(zip(sh_names, sh_parts)) + list(zip(REPL, rp_parts)):
        w2 = _as2d(given[n])
        outs = adamw(w2, pr.reshape((N_DEV,) + w2.shape), _as2d(given["m_" + n]), _as2d(given["v_" + n]), f"adamw_{n}")
        res[n] = [o.reshape(given[n].shape) for o in outs]
    order = ["attn_norm", "ffn_norm", "ple_norm", "ssm_in_proj", "ssm_conv_w", "ssm_conv_b", "ssm_dt_bias", "ssm_a_log",
             "ssm_d", "ssm_norm", "ssm_out_proj", "kv_norm", "w_kv", "w_q", "w_o", "ffn_up", "ffn_conv_w", "ffn_conv_b",
             "ffn_down", "ple_gate", "ple_proj", "final_norm"]
    return (loss, grad_x, *[res[n][0] for n in order], *[res[n][1] for n in order],
            *[res[n][2] for n in order], *[res[n][3] for n in order])
```

```python
import functools
import math

import jax
import jax.numpy as jnp
from jax import lax
from jax.experimental import pallas as pl
from jax.experimental.pallas import tpu as pltpu

F32 = jnp.float32
BF16 = jnp.bfloat16

N_DEV = 8
D_MODEL = 1024
SEQ = 2048
DEPTH = 4
N_A = 2
N_B = 2
SSM_D_INNER = 2048
SSM_HEAD_DIM = 64
SSM_HEADS = 32
SSM_GROUPS = 4
SSM_STATE = 128
SSM_CONV = 4
SSM_CHUNK = 128
SSM_CONV_DIM = 3072
SSM_ZX = 5120
SSM_IN_DIM = 5152
SB_HEADS = 16
SB_HEAD_DIM = 64
SB_WIDTH = 1024
D_FF = 2816
FFN_CONV = 3
PLE_DIM = 256
NORM_EPS = 1e-6
SSM_NORM_EPS = 1e-5

ADAM_LR = 0.001
ADAM_B1 = 0.9
ADAM_B2 = 0.999
ADAM_EPS = 1e-08
ADAM_WD = 0.01
ADAM_STEP = 10

LANE = 128
SUBLANE = 8
VMEM_LIMIT = 48 * 1024 * 1024

NN = (((1,), (0,)), ((), ()))
NT = (((1,), (1,)), ((), ()))
TN = (((0,), (0,)), ((), ()))


def _params(sem):
    return pltpu.CompilerParams(dimension_semantics=sem, vmem_limit_bytes=VMEM_LIMIT)


def _pick(n, prefs):
    for p in prefs:
        if n % p == 0:
            return p
    return n


def matmul(a, b, mode, out_dtype, name, add=None):
    if mode == "nn":
        (m, k), (k2, n) = a.shape, b.shape
    elif mode == "nt":
        (m, k), (n, k2) = a.shape, b.shape
    else:
        (k, m), (k2, n) = a.shape, b.shape
    assert k == k2, (a.shape, b.shape, mode)
    tm = _pick(m, (512, 256, 128))
    tn = _pick(n, (1408, 1024, 512, 256, 128))
    tk = _pick(k, (2816, 2048, 1024, 512, 256, 128))
    nk = k // tk
    dims = {"nn": NN, "nt": NT, "tn": TN}[mode]

    def body(*refs):
        a_ref, b_ref = refs[:2]
        add_ref = refs[2] if add is not None else None
        o_ref = refs[-2] if nk > 1 else refs[-1]
        part = lax.dot_general(a_ref[...].astype(BF16), b_ref[...].astype(BF16), dims, preferred_element_type=F32)

        def finish(r):
            if add_ref is not None:
                r = r + add_ref[...].astype(F32)
            o_ref[...] = r.astype(o_ref.dtype)

        if nk == 1:
            finish(part)
            return
        acc_ref = refs[-1]
        kk = pl.program_id(2)

        @pl.when(kk == 0)
        def _():
            acc_ref[...] = part

        @pl.when((kk > 0) & (kk < nk - 1))
        def _():
            acc_ref[...] += part

        @pl.when(kk == nk - 1)
        def _():
            finish(acc_ref[...] + part)

    if mode == "nn":
        a_spec = pl.BlockSpec((tm, tk), lambda i, j, kk: (i, kk))
        b_spec = pl.BlockSpec((tk, tn), lambda i, j, kk: (kk, j))
    elif mode == "nt":
        a_spec = pl.BlockSpec((tm, tk), lambda i, j, kk: (i, kk))
        b_spec = pl.BlockSpec((tn, tk), lambda i, j, kk: (j, kk))
    else:
        a_spec = pl.BlockSpec((tk, tm), lambda i, j, kk: (kk, i))
        b_spec = pl.BlockSpec((tk, tn), lambda i, j, kk: (kk, j))
    o_spec = pl.BlockSpec((tm, tn), lambda i, j, kk: (i, j))
    in_specs = [a_spec, b_spec] + ([o_spec] if add is not None else [])
    args = (a, b) + ((add,) if add is not None else ())
    return pl.pallas_call(
        body,
        name=name,
        grid=(m // tm, n // tn, nk),
        in_specs=in_specs,
        out_specs=o_spec,
        out_shape=jax.ShapeDtypeStruct((m, n), out_dtype),
        scratch_shapes=[pltpu.VMEM((tm, tn), F32)] if nk > 1 else [],
        compiler_params=_params(("parallel", "parallel", "arbitrary")),
    )(*args)


ROW_TILE = 512


def rmsnorm_fwd(x, gain, name):
    t, d = x.shape

    def body(x_ref, g_ref, o_ref):
        xv = x_ref[...]
        r = lax.rsqrt(jnp.mean(xv * xv, axis=-1, keepdims=True) + NORM_EPS)
        o_ref[...] = (xv * r * g_ref[...]).astype(o_ref.dtype)

    return pl.pallas_call(
        body,
        name=name,
        grid=(t // ROW_TILE,),
        in_specs=[pl.BlockSpec((ROW_TILE, d), lambda i: (i, 0)), pl.BlockSpec((1, d), lambda i: (0, 0))],
        out_specs=pl.BlockSpec((ROW_TILE, d), lambda i: (i, 0)),
        out_shape=jax.ShapeDtypeStruct((t, d), BF16),
        compiler_params=_params(("parallel",)),
    )(x, gain)


def rmsnorm_bwd(dy, x, gain, dres, name):
    t, d = x.shape

    def body(dy_ref, x_ref, g_ref, dres_ref, dx_ref, dg_ref):
        i = pl.program_id(0)
        xv = x_ref[...]
        r = lax.rsqrt(jnp.mean(xv * xv, axis=-1, keepdims=True) + NORM_EPS)
        xh = xv * r
        dyv = dy_ref[...].astype(F32)
        dxh = dyv * g_ref[...]
        dx = r * (dxh - xh * jnp.mean(dxh * xh, axis=-1, keepdims=True))
        dx_ref[...] = dres_ref[...] + dx

        @pl.when(i == 0)
        def _():
            dg_ref[...] = jnp.zeros_like(dg_ref)

        dg_ref[...] += jnp.sum(dyv * xh, axis=0, keepdims=True)

    row = pl.BlockSpec((ROW_TILE, d), lambda i: (i, 0))
    vec = pl.BlockSpec((1, d), lambda i: (0, 0))
    return pl.pallas_call(
        body,
        name=name,
        grid=(t // ROW_TILE,),
        in_specs=[row, row, vec, row],
        out_specs=[row, vec],
        out_shape=[jax.ShapeDtypeStruct((t, d), F32), jax.ShapeDtypeStruct((1, d), F32)],
        compiler_params=_params(("arbitrary",)),
    )(dy, x, gain, dres)


CONV_ROWS = 512
HALO = SUBLANE


def _conv_apply(ext, w_ref, bias, k, lo, n):
    acc = bias + w_ref[k - 1:k, :] * ext[lo:lo + n]
    for j in range(1, k):
        acc = acc + w_ref[k - 1 - j:k - j, :] * pltpu.roll(ext, j, 0)[lo:lo + n]
    return acc


def _conv_apply_t(ext, w_ref, k, n):
    rows = ext.shape[0]
    acc = w_ref[k - 1:k, :] * ext[:n]
    for j in range(1, k):
        acc = acc + w_ref[k - 1 - j:k - j, :] * pltpu.roll(ext, rows - j, 0)[:n]
    return acc


def _conv_dw(dpre_main, ext_u, k, n, dw_ref):
    for kk in range(k):
        j = k - 1 - kk
        sh = ext_u[HALO:HALO + n] if j == 0 else pltpu.roll(ext_u, j, 0)[HALO:HALO + n]
        dw_ref[kk:kk + 1, :] += jnp.sum(dpre_main * sh, axis=0, keepdims=True)


def _silu_grad(pre):
    sg = jax.nn.sigmoid(pre)
    return sg * (1.0 + pre * (1.0 - sg))


def _conv_blockspecs(seq, r, tc):
    nh = r // HALO
    last = seq // HALO - 1
    main = pl.BlockSpec((None, r, tc), lambda c, b, s: (b, s, c))
    before = pl.BlockSpec((None, HALO, tc), lambda c, b, s: (b, jnp.maximum(s * nh - 1, 0), c))
    after = pl.BlockSpec((None, HALO, tc), lambda c, b, s: (b, jnp.minimum((s + 1) * nh, last), c))
    return main, before, after


def ssm_conv_fwd(u, w, bias, name):
    bsz, seq, ch = u.shape
    r, tc, k = min(CONV_ROWS, seq), 512, SSM_CONV
    main, before, _ = _conv_blockspecs(seq, r, tc)
    wspec = pl.BlockSpec((SUBLANE, tc), lambda c, b, s: (0, c))
    bspec = pl.BlockSpec((1, tc), lambda c, b, s: (0, c))

    def body(u_ref, hb_ref, w_ref, b_ref, o_ref):
        s = pl.program_id(2)
        hb = jnp.where(s == 0, 0.0, hb_ref[...])
        ext = jnp.concatenate([hb, u_ref[...]], axis=0)
        pre = _conv_apply(ext, w_ref, b_ref[...], k, HALO, r)
        o_ref[...] = pre * jax.nn.sigmoid(pre)

    return pl.pallas_call(
        body, name=name, grid=(ch // tc, bsz, seq // r),
        in_specs=[main, before, wspec, bspec], out_specs=main,
        out_shape=jax.ShapeDtypeStruct(u.shape, F32),
        compiler_params=_params(("parallel", "parallel", "parallel")),
    )(u, u, w, bias)


def ssm_conv_bwd(u, dxc, w, bias, name):
    bsz, seq, ch = u.shape
    r, tc, k = min(CONV_ROWS, seq), 512, SSM_CONV
    ns = seq // r
    main, before, after = _conv_blockspecs(seq, r, tc)
    wspec = pl.BlockSpec((SUBLANE, tc), lambda c, b, s: (0, c))
    bspec = pl.BlockSpec((1, tc), lambda c, b, s: (0, c))

    def body(u_ref, hb_ref, ha_ref, d_ref, da_ref, w_ref, b_ref, du_ref, dw_ref, db_ref):
        b, s = pl.program_id(1), pl.program_id(2)
        hb = jnp.where(s == 0, 0.0, hb_ref[...])
        ext_u = jnp.concatenate([hb, u_ref[...], ha_ref[...]], axis=0)
        pre = _conv_apply(ext_u, w_ref, b_ref[...], k, HALO, r + HALO)
        dxe = jnp.concatenate([d_ref[...], jnp.where(s == ns - 1, 0.0, da_ref[...])], axis=0)
        dpre = dxe * _silu_grad(pre)
        du_ref[...] = _conv_apply_t(dpre, w_ref, k, r).astype(du_ref.dtype)

        @pl.when((b == 0) & (s == 0))
        def _():
            dw_ref[...] = jnp.zeros_like(dw_ref)
            db_ref[...] = jnp.zeros_like(db_ref)

        dpm = dpre[:r]
        _conv_dw(dpm, ext_u, k, r, dw_ref)
        db_ref[...] += jnp.sum(dpm, axis=0, keepdims=True)

    return pl.pallas_call(
        body, name=name, grid=(ch // tc, bsz, ns),
        in_specs=[main, before, after, main, after, wspec, bspec],
        out_specs=[main, wspec, bspec],
        out_shape=[jax.ShapeDtypeStruct(u.shape, BF16), jax.ShapeDtypeStruct((SUBLANE, ch), F32),
                   jax.ShapeDtypeStruct((1, ch), F32)],
        compiler_params=_params(("arbitrary", "arbitrary", "arbitrary")),
    )(u, u, u, dxc, dxc, w, bias)


def ffn_conv_fwd(ug, uv, wg, wv, bg, bv, name):
    bsz, seq, ch = ug.shape
    r, tc, k = min(CONV_ROWS, seq), 256, FFN_CONV
    main, before, _ = _conv_blockspecs(seq, r, tc)
    wspec = pl.BlockSpec((SUBLANE, tc), lambda c, b, s: (0, c))
    bspec = pl.BlockSpec((1, tc), lambda c, b, s: (0, c))

    def body(ug_ref, hg_ref, uv_ref, hv_ref, wg_ref, wv_ref, bg_ref, bv_ref, o_ref):
        s = pl.program_id(2)
        eg = jnp.concatenate([jnp.where(s == 0, 0.0, hg_ref[...]), ug_ref[...]], axis=0)
        ev = jnp.concatenate([jnp.where(s == 0, 0.0, hv_ref[...]), uv_ref[...]], axis=0)
        pg = _conv_apply(eg, wg_ref, bg_ref[...], k, HALO, r)
        pv = _conv_apply(ev, wv_ref, bv_ref[...], k, HALO, r)
        o_ref[...] = (pg * jax.nn.sigmoid(pg) * pv).astype(o_ref.dtype)

    return pl.pallas_call(
        body, name=name, grid=(ch // tc, bsz, seq // r),
        in_specs=[main, before, main, before, wspec, wspec, bspec, bspec], out_specs=main,
        out_shape=jax.ShapeDtypeStruct(ug.shape, BF16),
        compiler_params=_params(("parallel", "parallel", "parallel")),
    )(ug, ug, uv, uv, wg, wv, bg, bv)


def ffn_conv_bwd(ug, uv, dact, wg, wv, bg, bv, name):
    bsz, seq, ch = ug.shape
    r, tc, k = min(CONV_ROWS, seq), 256, FFN_CONV
    ns = seq // r
    main, before, after = _conv_blockspecs(seq, r, tc)
    wspec = pl.BlockSpec((SUBLANE, tc), lambda c, b, s: (0, c))
    bspec = pl.BlockSpec((1, tc), lambda c, b, s: (0, c))

    def body(ug_ref, gb_ref, ga_ref, uv_ref, vb_ref, va_ref, d_ref, da_ref, wg_ref, wv_ref, bg_ref, bv_ref,
             dug_ref, duv_ref, dwg_ref, dwv_ref, dbg_ref, dbv_ref):
        b, s = pl.program_id(1), pl.program_id(2)
        eg = jnp.concatenate([jnp.where(s == 0, 0.0, gb_ref[...]), ug_ref[...], ga_ref[...]], axis=0)
        ev = jnp.concatenate([jnp.where(s == 0, 0.0, vb_ref[...]), uv_ref[...], va_ref[...]], axis=0)
        pg = _conv_apply(eg, wg_ref, bg_ref[...], k, HALO, r + HALO)
        pv = _conv_apply(ev, wv_ref, bv_ref[...], k, HALO, r + HALO)
        de = jnp.concatenate([d_ref[...], jnp.where(s == ns - 1, 0.0, da_ref[...])], axis=0)
        sg = jax.nn.sigmoid(pg)
        dpg = de * pv * (sg * (1.0 + pg * (1.0 - sg)))
        dpv = de * (pg * sg)
        dug_ref[...] = _conv_apply_t(dpg, wg_ref, k, r).astype(dug_ref.dtype)
        duv_ref[...] = _conv_apply_t(dpv, wv_ref, k, r).astype(duv_ref.dtype)

        @pl.when((b == 0) & (s == 0))
        def _():
            dwg_ref[...] = jnp.zeros_like(dwg_ref)
            dwv_ref[...] = jnp.zeros_like(dwv_ref)
            dbg_ref[...] = jnp.zeros_like(dbg_ref)
            dbv_ref[...] = jnp.zeros_like(dbv_ref)

        _conv_dw(dpg[:r], eg, k, r, dwg_ref)
        _conv_dw(dpv[:r], ev, k, r, dwv_ref)
        dbg_ref[...] += jnp.sum(dpg[:r], axis=0, keepdims=True)
        dbv_ref[...] += jnp.sum(dpv[:r], axis=0, keepdims=True)

    wshape = jax.ShapeDtypeStruct((SUBLANE, ch), F32)
    bshape = jax.ShapeDtypeStruct((1, ch), F32)
    return pl.pallas_call(
        body, name=name, grid=(ch // tc, bsz, ns),
        in_specs=[main, before, after, main, before, after, main, after, wspec, wspec, bspec, bspec],
        out_specs=[main, main, wspec, wspec, bspec, bspec],
        out_shape=[jax.ShapeDtypeStruct(ug.shape, BF16), jax.ShapeDtypeStruct(ug.shape, BF16), wshape, wshape, bshape, bshape],
        compiler_params=_params(("arbitrary", "arbitrary", "arbitrary")),
    )(ug, ug, ug, uv, uv, uv, dact, dact, wg, wv, bg, bv)


GN_ROWS = 256
GN_GROUP = SSM_D_INNER // SSM_GROUPS


def gated_norm_fwd(y, z, w, name):
    t, d = y.shape

    def body(y_ref, z_ref, w_ref, o_ref):
        for g in range(d // GN_GROUP):
            sl = slice(g * GN_GROUP, (g + 1) * GN_GROUP)
            zv = z_ref[:, sl]
            gv = y_ref[:, sl] * (zv * jax.nn.sigmoid(zv))
            r = lax.rsqrt(jnp.mean(gv * gv, axis=-1, keepdims=True) + SSM_NORM_EPS)
            o_ref[:, sl] = (gv * r * w_ref[:, sl]).astype(o_ref.dtype)

    row = pl.BlockSpec((GN_ROWS, d), lambda i: (i, 0))
    vec = pl.BlockSpec((1, d), lambda i: (0, 0))
    return pl.pallas_call(
        body, name=name, grid=(t // GN_ROWS,), in_specs=[row, row, vec], out_specs=row,
        out_shape=jax.ShapeDtypeStruct((t, d), BF16), compiler_params=_params(("parallel",)),
    )(y, z, w)


def gated_norm_bwd(dgn, y, z, w, name):
    t, d = y.shape

    def body(d_ref, y_ref, z_ref, w_ref, dy_ref, dz_ref, dw_ref):
        i = pl.program_id(0)

        @pl.when(i == 0)
        def _():
            dw_ref[...] = jnp.zeros_like(dw_ref)

        for g in range(d // GN_GROUP):
            sl = slice(g * GN_GROUP, (g + 1) * GN_GROUP)
            zv, yv, dv = z_ref[:, sl], y_ref[:, sl], d_ref[:, sl]
            sg = jax.nn.sigmoid(zv)
            sz = zv * sg
            gv = yv * sz
            r = lax.rsqrt(jnp.mean(gv * gv, axis=-1, keepdims=True) + SSM_NORM_EPS)
            gh = gv * r
            dgh = dv * w_ref[:, sl]
            dw_ref[:, sl] += jnp.sum(dv * gh, axis=0, keepdims=True)
            dg = r * (dgh - gh * jnp.mean(dgh * gh, axis=-1, keepdims=True))
            dy_ref[:, sl] = dg * sz
            dz_ref[:, sl] = (dg * yv * (sg * (1.0 + zv * (1.0 - sg)))).astype(dz_ref.dtype)

    row = pl.BlockSpec((GN_ROWS, d), lambda i: (i, 0))
    vec = pl.BlockSpec((1, d), lambda i: (0, 0))
    return pl.pallas_call(
        body, name=name, grid=(t // GN_ROWS,), in_specs=[row, row, row, vec], out_specs=[row, row, vec],
        out_shape=[jax.ShapeDtypeStruct((t, d), F32), jax.ShapeDtypeStruct((t, d), BF16), jax.ShapeDtypeStruct((1, d), F32)],
        compiler_params=_params(("arbitrary",)),
    )(dgn, y, z, w)


def ple_fwd(h, gpre, pp, name):
    t, d = h.shape

    def body(h_ref, g_ref, p_ref, o_ref):
        o_ref[...] = h_ref[...] + jax.nn.sigmoid(g_ref[...]) * p_ref[...]

    row = pl.BlockSpec((ROW_TILE, d), lambda i: (i, 0))
    return pl.pallas_call(
        body, name=name, grid=(t // ROW_TILE,), in_specs=[row, row, row], out_specs=row,
        out_shape=jax.ShapeDtypeStruct((t, d), F32), compiler_params=_params(("parallel",)),
    )(h, gpre, pp)


def ple_bwd(dh, gpre, pp, name):
    t, d = dh.shape

    def body(d_ref, g_ref, p_ref, dg_ref, dp_ref):
        sg = jax.nn.sigmoid(g_ref[...])
        dv = d_ref[...]
        dg_ref[...] = (dv * p_ref[...] * sg * (1.0 - sg)).astype(dg_ref.dtype)
        dp_ref[...] = (dv * sg).astype(dp_ref.dtype)

    row = pl.BlockSpec((ROW_TILE, d), lambda i: (i, 0))
    return pl.pallas_call(
        body, name=name, grid=(t // ROW_TILE,), in_specs=[row, row, row], out_specs=[row, row],
        out_shape=[jax.ShapeDtypeStruct((t, d), BF16), jax.ShapeDtypeStruct((t, d), BF16)],
        compiler_params=_params(("parallel",)),
    )(dh, gpre, pp)


def loss_head(h, gain, target, name):
    t, d = h.shape

    def body(x_ref, g_ref, t_ref, l_ref, dx_ref, dg_ref):
        i = pl.program_id(0)
        xv = x_ref[...]
        r = lax.rsqrt(jnp.mean(xv * xv, axis=-1, keepdims=True) + NORM_EPS)
        xh = xv * r
        err = xh * g_ref[...] - t_ref[...]
        part = 0.5 * jnp.sum(jnp.mean(err * err, axis=-1, keepdims=True), axis=0, keepdims=True)
        dyv = err * (1.0 / d)
        dxh = dyv * g_ref[...]
        dx_ref[...] = r * (dxh - xh * jnp.mean(dxh * xh, axis=-1, keepdims=True))

        @pl.when(i == 0)
        def _():
            dg_ref[...] = jnp.zeros_like(dg_ref)
            l_ref[...] = jnp.zeros_like(l_ref)

        dg_ref[...] += jnp.sum(dyv * xh, axis=0, keepdims=True)
        l_ref[...] += jnp.broadcast_to(part, l_ref.shape)

    row = pl.BlockSpec((ROW_TILE, d), lambda i: (i, 0))
    vec = pl.BlockSpec((1, d), lambda i: (0, 0))
    lspec = pl.BlockSpec((1, LANE), lambda i: (0, 0))
    return pl.pallas_call(
        body, name=name, grid=(t // ROW_TILE,), in_specs=[row, vec, row], out_specs=[lspec, row, vec],
        out_shape=[jax.ShapeDtypeStruct((1, LANE), F32), jax.ShapeDtypeStruct((t, d), F32), jax.ShapeDtypeStruct((1, d), F32)],
        compiler_params=_params(("arbitrary",)),
    )(h, gain, target)


CHUNK = SSM_CHUNK
HPG = SSM_HEADS // SSM_GROUPS
GW = HPG * SSM_HEAD_DIM
NEG = -1e30


def _ssd_dt_terms(dtr_ref, bias_ref, alog_ref, g):
    shift = (LANE - HPG * g) % LANE
    li = lax.broadcasted_iota(jnp.int32, (CHUNK, CHUNK), 0)
    si = lax.broadcasted_iota(jnp.int32, (CHUNK, CHUNK), 1)
    x0 = pltpu.roll(dtr_ref[...] + bias_ref[...], shift, 1)
    dt = jnp.maximum(x0, 0.0) + jnp.log1p(jnp.exp(-jnp.abs(x0)))
    a = pltpu.roll(jnp.broadcast_to(-jnp.exp(alog_ref[...]), (CHUNK, LANE)), shift, 1)
    adt = dt * a
    acs = lax.dot_general((li >= si).astype(F32), adt, NN, precision=lax.Precision.HIGHEST, preferred_element_type=F32)
    last = jnp.broadcast_to(acs[CHUNK - 1:CHUNK, :], (CHUNK, LANE))
    return dict(li=li, si=si, x0=x0, dt=dt, a=a, acs=acs, acs_t=acs.T, e_acs=jnp.exp(acs),
                w_all=jnp.exp(last - acs), e_last=jnp.exp(last))


def _ssd_specs(nc, rev):
    def ci(c):
        return nc - 1 - c if rev else c
    nb = SSM_D_INNER // LANE
    xs = pl.BlockSpec((None, CHUNK, GW), lambda g, b, c: (b, ci(c), g))
    bm = pl.BlockSpec((None, CHUNK, LANE), lambda g, b, c: (b, ci(c), nb + g))
    cm = pl.BlockSpec((None, CHUNK, LANE), lambda g, b, c: (b, ci(c), nb + SSM_GROUPS + g))
    dtr = pl.BlockSpec((None, CHUNK, LANE), lambda g, b, c: (b, ci(c), 0))
    vec = pl.BlockSpec((1, LANE), lambda g, b, c: (0, 0))
    dexp = pl.BlockSpec((1, GW), lambda g, b, c: (0, g))
    st = pl.BlockSpec((None, None, CHUNK, GW), lambda g, b, c: (b, ci(c), 0, g))
    return xs, bm, cm, dtr, vec, dexp, st


def ssd_fwd(xc, dtr, dt_bias, a_log, d_exp, name):
    bsz, seq, _ = xc.shape
    nc = seq // CHUNK
    xs_s, bm_s, cm_s, dtr_s, vec_s, dexp_s, st_s = _ssd_specs(nc, False)

    def body(xs_ref, b_ref, c_ref, dtr_ref, bias_ref, alog_ref, dexp_ref, y_ref, sp_ref, state_ref):
        g, c = pl.program_id(0), pl.program_id(2)

        @pl.when(c == 0)
        def _():
            state_ref[...] = jnp.zeros_like(state_ref)

        t = _ssd_dt_terms(dtr_ref, bias_ref, alog_ref, g)
        tril = t["li"] >= t["si"]
        xs = xs_ref[...]
        b16 = b_ref[...].astype(BF16)
        c16 = c_ref[...].astype(BF16)
        bt16 = b_ref[...].T.astype(BF16)
        cb = lax.dot_general(c16, b16, NT, preferred_element_type=F32)
        sp = state_ref[...]
        sp_ref[...] = sp
        cs = jnp.dot(c16, sp.astype(BF16), preferred_element_type=F32)
        y_parts, s_parts = [], []
        for r in range(HPG):
            sl = slice(r * SSM_HEAD_DIM, (r + 1) * SSM_HEAD_DIM)
            diff = t["acs"][:, r:r + 1] - t["acs_t"][r:r + 1, :]
            m16 = (cb * jnp.exp(jnp.where(tril, diff, NEG))).astype(BF16)
            xh = xs[:, sl]
            xd = xh * t["dt"][:, r:r + 1]
            yh = jnp.dot(m16, xd.astype(BF16), preferred_element_type=F32)
            yh = yh + t["e_acs"][:, r:r + 1] * cs[:, sl] + xh * dexp_ref[:, sl]
            y_parts.append(yh)
            xw16 = (xd * t["w_all"][:, r:r + 1]).astype(BF16)
            s_parts.append(sp[:, sl] * t["e_last"][:, r:r + 1] + jnp.dot(bt16, xw16, preferred_element_type=F32))
        y_ref[...] = jnp.concatenate(y_parts, axis=1)
        state_ref[...] = jnp.concatenate(s_parts, axis=1)

    return pl.pallas_call(
        body, name=name, grid=(SSM_GROUPS, bsz, nc),
        in_specs=[xs_s, bm_s, cm_s, dtr_s, vec_s, vec_s, dexp_s],
        out_specs=[xs_s, st_s],
        out_shape=[jax.ShapeDtypeStruct((bsz, seq, SSM_D_INNER), F32),
                   jax.ShapeDtypeStruct((bsz, nc, SSM_STATE, SSM_D_INNER), F32)],
        scratch_shapes=[pltpu.VMEM((SSM_STATE, GW), F32)],
        compiler_params=_params(("arbitrary", "arbitrary", "arbitrary")),
    )(xc, xc, xc, dtr, dt_bias, a_log, d_exp)


def ssd_bwd(xc, dtr, dt_bias, a_log, d_exp, dy, sprev, name):
    bsz, seq, _ = xc.shape
    nc = seq // CHUNK
    xs_s, bm_s, cm_s, dtr_s, vec_s, dexp_s, st_s = _ssd_specs(nc, True)
    grp = pl.BlockSpec((None, CHUNK, LANE), lambda g, b, c: (b, nc - 1 - c, g))
    acc = pl.BlockSpec((None, 1, LANE), lambda g, b, c: (g, 0, 0))

    def body(xs_ref, b_ref, c_ref, dtr_ref, bias_ref, alog_ref, dexp_ref, dy_ref, sp_ref,
             dxs_ref, db_ref, dc_ref, ddtr_ref, da_ref, dbias_ref, ddexp_ref, dstate_ref):
        g, b, c = pl.program_id(0), pl.program_id(1), pl.program_id(2)

        @pl.when(c == 0)
        def _():
            dstate_ref[...] = jnp.zeros_like(dstate_ref)

        @pl.when((b == 0) & (c == 0))
        def _():
            da_ref[...] = jnp.zeros_like(da_ref)
            dbias_ref[...] = jnp.zeros_like(dbias_ref)
            ddexp_ref[...] = jnp.zeros_like(ddexp_ref)

        t = _ssd_dt_terms(dtr_ref, bias_ref, alog_ref, g)
        li, si = t["li"], t["si"]
        tril, triu = li >= si, li <= si
        row_is_last = li[:, :1] == CHUNK - 1
        xs, dy = xs_ref[...], dy_ref[...]
        b16 = b_ref[...].astype(BF16)
        c16 = c_ref[...].astype(BF16)
        ct16 = c_ref[...].T.astype(BF16)
        cb = lax.dot_general(c16, b16, NT, preferred_element_type=F32)
        cbt = lax.dot_general(b16, c16, NT, preferred_element_type=F32)
        sp, ds = sp_ref[...], dstate_ref[...]
        sp16, ds16 = sp.astype(BF16), ds.astype(BF16)
        cs = jnp.dot(c16, sp16, preferred_element_type=F32)
        bds = jnp.dot(b16, ds16, preferred_element_type=F32)
        spds = jnp.sum(sp * ds, axis=0, keepdims=True)
        zero = jnp.zeros((CHUNK, CHUNK), F32)
        dsum_g, dsum_gt, dacs, ddt_dir = zero, zero, zero, zero
        dxs_parts, edy_parts, xw_parts, dsp_parts = [], [], [], []
        for r in range(HPG):
            sl = slice(r * SSM_HEAD_DIM, (r + 1) * SSM_HEAD_DIM)
            diff = t["acs"][:, r:r + 1] - t["acs_t"][r:r + 1, :]
            lam = jnp.exp(jnp.where(tril, diff, NEG))
            lam_t = jnp.exp(jnp.where(triu, -diff, NEG))
            m, m_t = cb * lam, cbt * lam_t
            xh, dyh = xs[:, sl], dy[:, sl]
            dtc, wc, ec, elc = t["dt"][:, r:r + 1], t["w_all"][:, r:r + 1], t["e_acs"][:, r:r + 1], t["e_last"][:, r:r + 1]
            xd = xh * dtc
            xd16, dyh16 = xd.astype(BF16), dyh.astype(BF16)
            dxd = jnp.dot(m_t.astype(BF16), dyh16, preferred_element_type=F32) + wc * bds[:, sl]
            dm = lax.dot_general(dyh16, xd16, NT, preferred_element_type=F32)
            dm_t = lax.dot_general(xd16, dyh16, NT, preferred_element_type=F32)
            dsum_g = dsum_g + dm * lam
            dsum_gt = dsum_gt + dm_t * lam_t
            dacs_h = jnp.sum(dm * m, axis=1, keepdims=True) - jnp.sum(dm_t * m_t, axis=1, keepdims=True)
            dacs_h = dacs_h + jnp.sum(dyh * (ec * cs[:, sl]), axis=1, keepdims=True)
            wdw = wc * jnp.sum(xd * bds[:, sl], axis=1, keepdims=True)
            last_h = jnp.sum(wdw, axis=0, keepdims=True) + elc[0:1, :] * jnp.sum(spds[:, sl], axis=1, keepdims=True)
            dacs_h = dacs_h - wdw + jnp.where(row_is_last, last_h, 0.0)
            onehot = (si[:1, :] == r).astype(F32)
            dacs = dacs + dacs_h * onehot
            ddt_dir = ddt_dir + jnp.sum(dxd * xh, axis=1, keepdims=True) * onehot
            dxs_parts.append(dxd * dtc + dyh * dexp_ref[:, sl])
            edy_parts.append(ec * dyh)
            xw_parts.append(wc * xd)
            dsp_parts.append(elc * ds[:, sl])
        dxs_ref[...] = jnp.concatenate(dxs_parts, axis=1)
        edy16 = jnp.concatenate(edy_parts, axis=1).astype(BF16)
        xw16 = jnp.concatenate(xw_parts, axis=1).astype(BF16)
        dc_ref[...] = (jnp.dot(dsum_g.astype(BF16), b16, preferred_element_type=F32)
                       + lax.dot_general(edy16, sp16, NT, preferred_element_type=F32))
        db_ref[...] = (jnp.dot(dsum_gt.astype(BF16), c16, preferred_element_type=F32)
                       + lax.dot_general(xw16, ds16, NT, preferred_element_type=F32))
        dstate_ref[...] = jnp.concatenate(dsp_parts, axis=1) + jnp.dot(ct16, edy16, preferred_element_type=F32)
        dadt = lax.dot_general(triu.astype(F32), dacs, NN, precision=lax.Precision.HIGHEST, preferred_element_type=F32)
        head_lane = si < HPG
        ddtr = jnp.where(head_lane, (dadt * t["a"] + ddt_dir) * jax.nn.sigmoid(t["x0"]), 0.0)
        ddtr_ref[...] = ddtr
        da_ref[...] += jnp.sum(jnp.where(head_lane, dadt * t["dt"] * t["a"], 0.0), axis=0, keepdims=True)
        dbias_ref[...] += jnp.sum(ddtr, axis=0, keepdims=True)
        ddexp_ref[...] += jnp.sum(dy * xs, axis=0, keepdims=True)

    return pl.pallas_call(
        body, name=name, grid=(SSM_GROUPS, bsz, nc),
        in_specs=[xs_s, bm_s, cm_s, dtr_s, vec_s, vec_s, dexp_s, xs_s, st_s],
        out_specs=[xs_s, grp, grp, grp, acc, acc, dexp_s],
        out_shape=[jax.ShapeDtypeStruct((bsz, seq, SSM_D_INNER), F32),
                   jax.ShapeDtypeStruct((bsz, seq, SSM_GROUPS * SSM_STATE), F32),
                   jax.ShapeDtypeStruct((bsz, seq, SSM_GROUPS * SSM_STATE), F32),
                   jax.ShapeDtypeStruct((bsz, seq, SSM_GROUPS * LANE), F32),
                   jax.ShapeDtypeStruct((SSM_GROUPS, 1, LANE), F32),
                   jax.ShapeDtypeStruct((SSM_GROUPS, 1, LANE), F32),
                   jax.ShapeDtypeStruct((1, SSM_D_INNER), F32)],
        scratch_shapes=[pltpu.VMEM((SSM_STATE, GW), F32)],
        compiler_params=_params(("arbitrary", "arbitrary", "arbitrary")),
    )(xc, xc, xc, dtr, dt_bias, a_log, d_exp, dy, sprev)


SB_TQ = 512
SB_TK = 128
SB_SCALE = SB_HEAD_DIM ** -0.5


def _split_dot(x, u16):
    hi = x.astype(BF16)
    lo = (x - hi.astype(F32)).astype(BF16)
    return jnp.dot(hi, u16, preferred_element_type=F32) + jnp.dot(lo, u16, preferred_element_type=F32)


def _sb_scores(qh, kj, mask):
    z = lax.dot_general(qh, kj, NT, preferred_element_type=F32)
    l1p = jnp.log(1.0 + jnp.exp(-jnp.abs(z)))
    lb = jnp.minimum(z, 0.0) - l1p
    lk = -jnp.maximum(z, 0.0) - l1p
    if mask is not None:
        lk = jnp.where(mask, lk, 0.0)
    return lb, lk


def _sb_diag_masks(tq):
    r = lax.broadcasted_iota(jnp.int32, (tq, SB_TK), 0)
    c = lax.broadcasted_iota(jnp.int32, (tq, SB_TK), 1)
    return [r > c + d * SB_TK for d in range(tq // SB_TK)]


def _sb_heads(x, lane):
    return [jnp.where((lane // SB_HEAD_DIM) == hh, x, jnp.zeros_like(x)) for hh in range(2)]


def sb_attention_fwd(q, kv, name):
    bsz, seq, width = q.shape
    tq = min(SB_TQ, seq)
    npair = width // LANE
    qspec = pl.BlockSpec((None, tq, LANE), lambda b, p, i: (b, i, p))
    kspec = pl.BlockSpec((None, seq, LANE), lambda b, p, i: (b, 0, p))
    vspec = pl.BlockSpec((None, seq, LANE), lambda b, p, i: (b, 0, npair + p))

    def body(q_ref, k_ref, v_ref, o_ref, l_ref):
        i = pl.program_id(2)
        ndiag = tq // SB_TK
        nfull = i * ndiag
        lane = lax.broadcasted_iota(jnp.int32, (tq, LANE), 1)
        ui = lax.broadcasted_iota(jnp.int32, (SB_TK, SB_TK), 0)
        uj = lax.broadcasted_iota(jnp.int32, (SB_TK, SB_TK), 1)
        u_excl = (ui > uj).astype(BF16)
        qs = _sb_heads(q_ref[...] * SB_SCALE, lane)

        def tile(j, carry, mask):
            rows = pl.ds(pl.multiple_of(j * SB_TK, SB_TK), SB_TK)
            kj, vj = k_ref[rows, :], v_ref[rows, :]
            new = []
            for hh in range(2):
                acc, run = carry[hh]
                lb, lk = _sb_scores(qs[hh], kj, mask)
                w = jnp.exp(lb + _split_dot(lk, u_excl) + run)
                if mask is not None:
                    w = jnp.where(mask, w, 0.0)
                acc = acc + jnp.dot(w.astype(BF16), vj, preferred_element_type=F32)
                new.append((acc, run + jnp.sum(lk, axis=1, keepdims=True)))
            return tuple(new)

        carry = tuple((jnp.zeros((tq, LANE), F32), jnp.zeros((tq, 1), F32)) for _ in range(2))
        masks = _sb_diag_masks(tq)
        for dd in reversed(range(ndiag)):
            carry = tile(nfull + dd, carry, masks[dd])
        carry = lax.fori_loop(0, nfull // 2, lambda jj, cr: tile(nfull - 2 - 2 * jj, tile(nfull - 1 - 2 * jj, cr, None), None), carry)
        in0 = (lane // SB_HEAD_DIM) == 0
        o_ref[...] = jnp.where(in0, carry[0][0], carry[1][0]).astype(o_ref.dtype)
        l_ref[...] = jnp.where(in0, carry[0][1], carry[1][1])

    return pl.pallas_call(
        body, name=name, grid=(bsz, npair, seq // tq),
        in_specs=[qspec, kspec, vspec], out_specs=[qspec, qspec],
        out_shape=[jax.ShapeDtypeStruct(q.shape, BF16), jax.ShapeDtypeStruct(q.shape, F32)],
        compiler_params=_params(("parallel", "parallel", "arbitrary")),
    )(q, kv, kv)


def sb_attention_bwd(q, kv, ltot, do, dk_in, dv_in, name):
    bsz, seq, width = q.shape
    tq = min(SB_TQ, seq)
    npair = width // LANE
    qspec = pl.BlockSpec((None, tq, LANE), lambda b, p, i: (b, i, p))
    kspec = pl.BlockSpec((None, seq, LANE), lambda b, p, i: (b, 0, p))
    vspec = pl.BlockSpec((None, seq, LANE), lambda b, p, i: (b, 0, npair + p))

    def body(q_ref, k_ref, v_ref, l_ref, do_ref, dki_ref, dvi_ref, dq_ref, dk_ref, dv_ref):
        i = pl.program_id(2)

        @pl.when(i == 0)
        def _():
            dk_ref[...] = dki_ref[...]
            dv_ref[...] = dvi_ref[...]

        ndiag = tq // SB_TK
        nfull = i * ndiag
        lane = lax.broadcasted_iota(jnp.int32, (tq, LANE), 1)
        ui = lax.broadcasted_iota(jnp.int32, (SB_TK, SB_TK), 0)
        uj = lax.broadcasted_iota(jnp.int32, (SB_TK, SB_TK), 1)
        u_le = (ui <= uj).astype(BF16)
        u_lt = (ui < uj).astype(BF16)
        qs = _sb_heads(q_ref[...] * SB_SCALE, lane)
        dos = _sb_heads(do_ref[...], lane)
        ltots = [l_ref[:, hh * SB_HEAD_DIM:hh * SB_HEAD_DIM + 1] for hh in range(2)]

        def tile(j, carry, mask, r0=0):
            rows = pl.ds(pl.multiple_of(j * SB_TK, SB_TK), SB_TK)
            kj, vj = k_ref[rows, :], v_ref[rows, :]
            new, dk_t, dv_t = [], None, None
            for hh in range(2):
                acc, run, run_a = carry[hh]
                qh, doh = qs[hh][r0:], dos[hh][r0:]
                lb, lk = _sb_scores(qh, kj, mask)
                w = jnp.exp(lb + (ltots[hh][r0:] - run[r0:] - _split_dot(lk, u_le)))
                if mask is not None:
                    w = jnp.where(mask, w, 0.0)
                a = lax.dot_general(doh, vj, NT, preferred_element_type=F32) * w
                ca = run_a[r0:] + _split_dot(a, u_lt)
                sg = jnp.exp(lb)
                dz = a * (1.0 - sg) - ca * sg
                if mask is not None:
                    dz = jnp.where(mask, dz, 0.0)
                dz16 = dz.astype(BF16)
                part = [acc[r0:] + jnp.dot(dz16, kj, preferred_element_type=F32),
                        run[r0:] + jnp.sum(lk, axis=1, keepdims=True), run_a[r0:] + jnp.sum(a, axis=1, keepdims=True)]
                if r0:
                    part = [jnp.concatenate([old[:r0], p], axis=0) for old, p in zip((acc, run, run_a), part)]
                dkh = lax.dot_general(dz16, qh, TN, preferred_element_type=F32)
                dvh = lax.dot_general(w.astype(BF16), doh, TN, preferred_element_type=F32)
                dk_t = dkh if dk_t is None else dk_t + dkh
                dv_t = dvh if dv_t is None else dv_t + dvh
                new.append(tuple(part))
            dk_ref[rows, :] += dk_t
            dv_ref[rows, :] += dv_t
            return tuple(new)

        zcol = jnp.zeros((tq, 1), F32)
        carry = tuple((jnp.zeros((tq, LANE), F32), zcol, zcol) for _ in range(2))
        carry = lax.fori_loop(0, nfull // 2, lambda j, cr: tile(2 * j + 1, tile(2 * j, cr, None), None), carry)
        masks = _sb_diag_masks(tq)
        for dd in range(ndiag):
            carry = tile(nfull + dd, carry, masks[dd][dd * SB_TK:], dd * SB_TK)
        in0 = (lane // SB_HEAD_DIM) == 0
        dq_ref[...] = (jnp.where(in0, carry[0][0], carry[1][0]) * SB_SCALE).astype(dq_ref.dtype)

    return pl.pallas_call(
        body, name=name, grid=(bsz, npair, seq // tq),
        in_specs=[qspec, kspec, vspec, qspec, qspec, kspec, kspec], out_specs=[qspec, kspec, kspec],
        out_shape=[jax.ShapeDtypeStruct(q.shape, BF16), jax.ShapeDtypeStruct(q.shape, F32), jax.ShapeDtypeStruct(q.shape, F32)],
        compiler_params=_params(("parallel", "parallel", "arbitrary")),
    )(q, kv, kv, ltot, do, dk_in, dv_in)


def adamw(w, parts, m, v, name):
    r, c = w.shape
    tr = r
    for cand in (256, 176, 128, 64, 32, 16, 8):
        if r % cand == 0:
            tr = cand
            break
    bc1 = 1.0 - ADAM_B1 ** ADAM_STEP
    bc2 = 1.0 - ADAM_B2 ** ADAM_STEP

    def body(w_ref, p_ref, m_ref, v_ref, g_ref, d_ref, mo_ref, vo_ref):
        g = p_ref[0].astype(F32)
        for j in range(1, N_DEV):
            g = g + p_ref[j].astype(F32)
        mn = ADAM_B1 * m_ref[...] + (1.0 - ADAM_B1) * g
        vn = ADAM_B2 * v_ref[...] + (1.0 - ADAM_B2) * (g * g)
        g_ref[...] = g
        mo_ref[...] = mn
        vo_ref[...] = vn
        d_ref[...] = -ADAM_LR * ((mn / bc1) / (jnp.sqrt(vn / bc2) + ADAM_EPS) + ADAM_WD * w_ref[...])

    blk = pl.BlockSpec((tr, c), lambda i: (i, 0))
    pblk = pl.BlockSpec((N_DEV, tr, c), lambda i: (0, i, 0))
    shp = jax.ShapeDtypeStruct((r, c), F32)
    return pl.pallas_call(
        body, name=name, grid=(r // tr,), in_specs=[blk, pblk, blk, blk], out_specs=[blk, blk, blk, blk],
        out_shape=[shp, shp, shp, shp], compiler_params=_params(("parallel",)),
    )(w, parts, m, v)


MESH_ID = pl.DeviceIdType.MESH
ANY = pl.BlockSpec(memory_space=pl.ANY)


def all_gather(xs, name):
    n = len(xs)

    def body(*refs):
        x_refs, out_refs = refs[:n], refs[n:2 * n]
        send_sems, recv_sems, local_sems = refs[2 * n:]
        mx, my, mc = lax.axis_index("x"), lax.axis_index("y"), lax.axis_index("c")
        me, sibling = (mx, my, mc), (mx, my, 1 - mc)
        chips = [(1 - mx, my), (mx, 1 - my), (1 - mx, 1 - my)]

        def slot(a, px, py, pc):
            return out_refs[a].at[4 * px + 2 * py + pc]

        def copy(a, k, block, to, src=None):
            return pltpu.make_async_remote_copy(
                src_ref=slot(a, *block) if src is None else src, dst_ref=slot(a, *block),
                send_sem=send_sems.at[7 * a + k], recv_sem=recv_sems.at[7 * a + k], device_id=to, device_id_type=MESH_ID)

        mine = [pltpu.make_async_copy(x_refs[a], slot(a, *me), local_sems.at[a]) for a in range(n)]
        for cp in mine:
            cp.start()
        first = []
        for a in range(n):
            first.append(copy(a, 0, me, sibling, src=x_refs[a]))
            first += [copy(a, 1 + j, me, (*chip, mc), src=x_refs[a]) for j, chip in enumerate(chips)]
        for cp in first:
            cp.start()
        passed = []
        for j, chip in enumerate(chips):
            for a in range(n):
                copy(a, 1 + j, (*chip, mc), me).wait_recv()
                cp = copy(a, 4 + j, (*chip, mc), sibling)
                cp.start()
                passed.append(cp)
        for a in range(n):
            copy(a, 0, sibling, me).wait_recv()
            for j, chip in enumerate(chips):
                copy(a, 4 + j, (*chip, 1 - mc), me).wait_recv()
        for cp in first + passed:
            cp.wait_send()
        for cp in mine:
            cp.wait()

    return pl.pallas_call(
        body, name=name, out_shape=[jax.ShapeDtypeStruct((N_DEV,) + x.shape, x.dtype) for x in xs],
        in_specs=[ANY] * n, out_specs=[ANY] * n,
        scratch_shapes=[pltpu.SemaphoreType.DMA((7 * n,)), pltpu.SemaphoreType.DMA((7 * n,)), pltpu.SemaphoreType.DMA((n,))],
    )(*xs)


HBM_SPEC = pl.BlockSpec(memory_space=pltpu.HBM)
SEM_SPEC = pl.BlockSpec(memory_space=pltpu.SEMAPHORE)
DATAFLOW = pltpu.SideEffectType.DATAFLOW_SIDE_EFFECTING


def _peer(j, mx, my, mc):
    px = 1 - mx if j & 4 else mx
    py = 1 - my if j & 2 else my
    pc = 1 - mc if j & 1 else mc
    return (px, py, pc), 4 * px + 2 * py + pc


def _split_copy(kind, src_ref, land_ref, send_sems, recv_sems, a, j, mx, my, mc):
    dev, peer = _peer(j, mx, my, mc)
    me = 4 * mx + 2 * my + mc
    src = src_ref if kind == "gather" else src_ref.at[peer]
    k = 7 * a + j - 1
    return pltpu.make_async_remote_copy(src_ref=src, dst_ref=land_ref.at[me], send_sem=send_sems.at[k],
                                        recv_sem=recv_sems.at[k], device_id=dev, device_id_type=MESH_ID), peer


def exchange_start(kind, srcs, name):
    n = len(srcs)
    blocks = [s.shape if kind == "gather" else s.shape[1:] for s in srcs]
    lands = [lax.empty((N_DEV,) + tuple(b), s.dtype) for b, s in zip(blocks, srcs)]

    def body(*refs):
        src_refs, land_refs = refs[:n], refs[n:2 * n]
        send_sems, recv_sems = refs[2 * n], refs[2 * n + 1]
        token = refs[-1]
        mx, my, mc = lax.axis_index("x"), lax.axis_index("y"), lax.axis_index("c")
        for j in range(1, N_DEV):
            for a in range(n):
                _split_copy(kind, src_refs[a], land_refs[a], send_sems, recv_sems, a, j, mx, my, mc)[0].start()
        token[...] = jnp.zeros_like(token)

    hbm = lambda v: pltpu.HBM(v.shape, v.dtype)
    outs = pl.pallas_call(
        body, name=name,
        out_shape=(pltpu.SemaphoreType.DMA((7 * n,)), pltpu.SemaphoreType.DMA((7 * n,)), *[hbm(s) for s in srcs],
                   *[hbm(l) for l in lands], jax.ShapeDtypeStruct((SUBLANE, LANE), F32)),
        in_specs=[HBM_SPEC] * (2 * n),
        out_specs=(SEM_SPEC, SEM_SPEC, *[HBM_SPEC] * (2 * n), pl.BlockSpec(memory_space=pltpu.VMEM)),
        input_output_aliases={a: 2 + a for a in range(2 * n)},
        compiler_params=pltpu.CompilerParams(has_side_effects=DATAFLOW),
    )(*[pltpu.with_memory_space_constraint(v, pltpu.HBM) for v in list(srcs) + lands])
    return dict(kind=kind, n=n, send=outs[0], recv=outs[1], srcs=outs[2:2 + n], lands=outs[2 + n:2 + 2 * n], token=outs[-1])


def exchange_wait(h, after, name):
    n, kind = h["n"], h["kind"]

    def body(*refs):
        src_refs, land_refs = refs[:n], refs[n:2 * n]
        send_sems, recv_sems = refs[2 * n], refs[2 * n + 1]
        mx, my, mc = lax.axis_index("x"), lax.axis_index("y"), lax.axis_index("c")
        for j in range(1, N_DEV):
            for a in range(n):
                cp, peer = _split_copy(kind, src_refs[a], land_refs[a], send_sems, recv_sems, a, j, mx, my, mc)
                cp.wait_send()
                pltpu.make_async_remote_copy(
                    src_ref=land_refs[a].at[peer], dst_ref=land_refs[a].at[peer], send_sem=send_sems.at[7 * a + j - 1],
                    recv_sem=recv_sems.at[7 * a + j - 1], device_id=_peer(j, mx, my, mc)[0], device_id_type=MESH_ID).wait_recv()

    hbm = lambda v: pltpu.HBM(v.shape, v.dtype)
    outs = pl.pallas_call(
        body, name=name,
        out_shape=tuple(hbm(v) for v in list(h["srcs"]) + list(h["lands"])),
        in_specs=[HBM_SPEC] * (2 * n) + [SEM_SPEC, SEM_SPEC, ANY],
        out_specs=tuple([HBM_SPEC] * (2 * n)),
        input_output_aliases={a: a for a in range(2 * n)},
        compiler_params=pltpu.CompilerParams(has_side_effects=DATAFLOW),
    )(*h["srcs"], *h["lands"], h["send"], h["recv"], after)
    return list(outs[n:])


PACK_COLS = 1024
PACK_SEG = 16 * PACK_COLS

BIG = {"ssm_in_proj": 2, "ssm_out_proj": 1, "w_kv": 1, "w_q": 1, "w_o": 1, "ffn_up": 2, "ffn_down": 1,
       "ple_gate": 1, "ple_proj": 2}
SMALL = {"ssm_conv_w": 2, "ssm_conv_b": 1, "ssm_norm": 1, "ffn_conv_w": 2}
REPL = ["attn_norm", "ffn_norm", "ple_norm", "ssm_dt_bias", "ssm_a_log", "ssm_d", "ffn_conv_b", "kv_norm", "final_norm"]


def _seg(n):
    return -(-n // PACK_SEG) * PACK_SEG


def _pack_rows(flats, lead=()):
    padded = [jnp.pad(f, [(0, 0)] * len(lead) + [(0, _seg(f.shape[-1]) - f.shape[-1])]) for f in flats]
    return jnp.concatenate(padded, axis=-1).reshape(*lead, -1, PACK_COLS)


def _unpack_rows(buf, sizes, lead):
    flat = buf.reshape(*lead, -1)
    out, off = [], 0
    for n in sizes:
        out.append(lax.slice_in_dim(flat, off, off + n, axis=len(lead)))
        off += _seg(n)
    return out


def _to_slots(full, axis):
    shp = full.shape
    blk = shp[axis] // N_DEV
    t = full.reshape(shp[:axis] + (N_DEV, blk) + shp[axis + 1:])
    return jnp.moveaxis(t, axis, 0).reshape(N_DEV, -1)


def _from_slots(slots, local_shape, axis):
    t = jnp.moveaxis(slots.reshape((N_DEV,) + tuple(local_shape)), 0, axis)
    shp = list(local_shape)
    shp[axis] *= N_DEV
    return t.reshape(shp)


def _as2d(a):
    return a.reshape(1, -1) if a.ndim == 1 else a.reshape(-1, a.shape[-1])


def _pad_rows(w, rows=SUBLANE):
    return jnp.pad(w, ((0, rows - w.shape[0]), (0, 0)))


def kernel(x, p, attn_norm, ffn_norm, ple_norm, ssm_in_proj, ssm_conv_w, ssm_conv_b, ssm_dt_bias, ssm_a_log, ssm_d, ssm_norm, ssm_out_proj, kv_norm, w_kv, w_q, w_o, ffn_up, ffn_conv_w, ffn_conv_b, ffn_down, ple_gate, ple_proj, final_norm, loss_target, m_attn_norm, m_ffn_norm, m_ple_norm, m_ssm_in_proj, m_ssm_conv_w, m_ssm_conv_b, m_ssm_dt_bias, m_ssm_a_log, m_ssm_d, m_ssm_norm, m_ssm_out_proj, m_kv_norm, m_w_kv, m_w_q, m_w_o, m_ffn_up, m_ffn_conv_w, m_ffn_conv_b, m_ffn_down, m_ple_gate, m_ple_proj, m_final_norm, v_attn_norm, v_ffn_norm, v_ple_norm, v_ssm_in_proj, v_ssm_conv_w, v_ssm_conv_b, v_ssm_dt_bias, v_ssm_a_log, v_ssm_d, v_ssm_norm, v_ssm_out_proj, v_kv_norm, v_w_kv, v_w_q, v_w_o, v_ffn_up, v_ffn_conv_w, v_ffn_conv_b, v_ffn_down, v_ple_gate, v_ple_proj, v_final_norm):
    given = dict(locals())
    wnames = list(BIG) + list(SMALL) + REPL
    bsz, seq, d = x.shape
    t = bsz * seq
    bs = lambda a: a.reshape(bsz, seq, a.shape[-1])
    fl = lambda a: a.reshape(t, a.shape[-1])

    big_local = [given[n] for n in BIG]
    small_local = [given[n] for n in SMALL]
    me = 4 * lax.axis_index("x") + 2 * lax.axis_index("y") + lax.axis_index("c")

    def layer_items(i):
        j = i - N_A
        mixer = [("ssm_in_proj", i), ("ssm_out_proj", i)] if i < N_A else [("w_q", j), ("w_o", j)]
        shared = [("w_kv", None)] if i == N_A - 1 else []
        return mixer + [("ffn_up", i), ("ffn_down", i), ("ple_gate", i), ("ple_proj", i)] + shared

    def local_block(n, idx):
        return given[n] if idx is None else given[n][idx]

    def put_own(land, own):
        return lax.dynamic_update_slice(land, own[None], (me,) + (0,) * own.ndim)

    small_pack = _pack_rows([w.reshape(-1) for w in small_local])
    gather_src, gather_h = [], []
    for i in range(DEPTH):
        src = [local_block(n, idx).astype(BF16) for n, idx in layer_items(i)] + ([small_pack] if i == 0 else [])
        gather_src.append(src)
        gather_h.append(exchange_start("gather", src, f"ag_start{i}"))
    full = {n: {} for n in BIG}
    rows_of = lambda g: g.reshape(-1, g.shape[-1])
    cols_of = lambda g: jnp.transpose(g, (1, 0, 2)).reshape(g.shape[1], -1)

    def fetch_layer(i, after):
        lands = exchange_wait(gather_h[i], after, f"ag_wait{i}")
        got = [put_own(l, s) for l, s in zip(lands, gather_src[i])]
        for (n, idx), g in zip(layer_items(i), got):
            if n == "ffn_up":
                full[n][idx] = (cols_of(g[:N_DEV // 2]), cols_of(g[N_DEV // 2:]))
            else:
                full[n][idx] = cols_of(g) if BIG[n] == given[n].ndim - 1 else rows_of(g)
        if i == 0:
            for (n, ax), w, s in zip(SMALL.items(), small_local, _unpack_rows(got[-1], [w.size for w in small_local], (N_DEV,))):
                full[n] = _from_slots(s, w.shape, ax)

    row = lambda v: v.reshape(1, -1)
    pad_lane = lambda v: jnp.pad(v.reshape(1, -1), ((0, 0), (0, LANE - v.size)))
    h = x.reshape(t, d)
    saved = []
    kv = hnkv = h_kv = None
    for i in range(DEPTH):
        fetch_layer(i, h)
        sv = {"h0": h}
        gain = row(attn_norm[i])
        if i == 0:
            gain = gain + sum(hd["token"][0, 0] for hd in gather_h)
        hn = rmsnorm_fwd(h, gain, f"attn_norm_f{i}")
        sv["hn"] = hn
        if i < N_A:
            w_in = full["ssm_in_proj"][i]
            wz, wxbc = w_in[:, :SSM_D_INNER], w_in[:, SSM_D_INNER:SSM_ZX]
            wdt = jnp.pad(w_in[:, SSM_ZX:], ((0, 0), (0, LANE - SSM_HEADS)))
            z = matmul(hn, wz, "nn", F32, f"ssm_z_f{i}")
            xbc = bs(matmul(hn, wxbc, "nn", F32, f"ssm_xbc_f{i}"))
            dtr = bs(matmul(hn, wdt, "nn", F32, f"ssm_dt_f{i}"))
            cw = _pad_rows(full["ssm_conv_w"][i])
            cb = row(full["ssm_conv_b"][i])
            xc = ssm_conv_fwd(xbc, cw, cb, f"ssm_conv_f{i}")
            dtb, alog = pad_lane(ssm_dt_bias[i]), pad_lane(ssm_a_log[i])
            dexp = jnp.repeat(ssm_d[i], SSM_HEAD_DIM).reshape(1, -1)
            y, sprev = ssd_fwd(xc, dtr, dtb, alog, dexp, f"ssd_f{i}")
            gn = gated_norm_fwd(fl(y), z, row(full["ssm_norm"][i]), f"ssm_gnorm_f{i}")
            h1 = matmul(gn, full["ssm_out_proj"][i], "nn", F32, f"ssm_out_f{i}", add=h)
            sv.update(wz=wz, wxbc=wxbc, wdt=wdt, z=z, xbc=xbc, dtr=dtr, cw=cw, cb=cb, xc=xc, dtb=dtb, alog=alog,
                      dexp=dexp, y=y, sprev=sprev, gn=gn)
        else:
            j = i - N_A
            q = bs(matmul(hn, full["w_q"][j], "nn", BF16, f"sb_q_f{j}"))
            o, ltot = sb_attention_fwd(q, kv, f"sb_attn_f{j}")
            h1 = matmul(fl(o), full["w_o"][j], "nn", F32, f"sb_o_f{j}", add=h)
            sv.update(q=q, o=o, ltot=ltot)
        sv["h1"] = h1
        hn2 = rmsnorm_fwd(h1, row(ffn_norm[i]), f"ffn_norm_f{i}")
        wug, wuv = full["ffn_up"][i]
        ug = bs(matmul(hn2, wug, "nn", F32, f"ffn_upg_f{i}"))
        uv = bs(matmul(hn2, wuv, "nn", F32, f"ffn_upv_f{i}"))
        fcw = full["ffn_conv_w"][i]
        fwg, fwv = _pad_rows(fcw[:, :D_FF]), _pad_rows(fcw[:, D_FF:])
        fbg, fbv = row(ffn_conv_b[i, :D_FF]), row(ffn_conv_b[i, D_FF:])
        act = ffn_conv_fwd(ug, uv, fwg, fwv, fbg, fbv, f"ffn_conv_f{i}")
        h2 = matmul(fl(act), full["ffn_down"][i], "nn", F32, f"ffn_down_f{i}", add=h1)
        hn3 = rmsnorm_fwd(h2, row(ple_norm[i]), f"ple_norm_f{i}")
        gpre = matmul(hn3, full["ple_gate"][i], "nn", F32, f"ple_gate_f{i}")
        p_i = p[i].reshape(t, PLE_DIM)
        pp = matmul(p_i, full["ple_proj"][i], "nn", F32, f"ple_proj_f{i}")
        h3 = ple_fwd(h2, gpre, pp, f"ple_f{i}")
        sv.update(hn2=hn2, wug=wug, wuv=wuv, ug=ug, uv=uv, fwg=fwg, fwv=fwv, fbg=fbg, fbv=fbv, act=act, h2=h2,
                  hn3=hn3, gpre=gpre, pp=pp, p_i=p_i)
        if i == N_A - 1:
            h_kv = h3
            hnkv = rmsnorm_fwd(h3, row(kv_norm), "kv_norm_f")
            kv = bs(matmul(hnkv, full["w_kv"][None], "nn", BF16, "kv_f"))
        h = h3
        saved.append(sv)

    loss_row, dh, g_final = loss_head(h, row(final_norm), loss_target.reshape(t, d), "loss_head")
    loss = lax.psum(loss_row[0, 0], ("x", "y", "c"))

    G = {n: [None] * given[n].shape[0] for n in wnames if given[n].ndim > 1 and n not in ("w_kv",)}
    G["final_norm"] = g_final[0]

    def col_slots(g, ndev=N_DEV):
        return jnp.transpose(g.reshape(g.shape[0], ndev, -1), (1, 0, 2))

    def grad_slots(n, idx):
        g = G[n] if idx is None else G[n][idx]
        if n == "ffn_up":
            s = jnp.concatenate([col_slots(g[0], N_DEV // 2), col_slots(g[1], N_DEV // 2)], axis=0)
        elif BIG[n] == given[n].ndim - 1:
            s = col_slots(g)
        else:
            s = g.reshape(N_DEV, -1, g.shape[-1])
        return s.astype(BF16)

    scatter_src, scatter_h = {}, {}
    dk = jnp.zeros((bsz, seq, SB_WIDTH), F32)
    dv = jnp.zeros((bsz, seq, SB_WIDTH), F32)
    for i in reversed(range(DEPTH)):
        sv = saved[i]
        if i == N_A - 1:
            wk, wv = full["w_kv"][None][:, :SB_WIDTH], full["w_kv"][None][:, SB_WIDTH:]
            dkf, dvf = fl(dk), fl(dv)
            dhn = matmul(dkf, wk, "nt", F32, "kv_dx_k")
            dhn = matmul(dvf, wv, "nt", F32, "kv_dx_v", add=dhn)
            G["w_kv"] = jnp.concatenate([matmul(hnkv, dkf, "tn", F32, "kv_dw_k"), matmul(hnkv, dvf, "tn", F32, "kv_dw_v")], axis=1)
            dh, gk = rmsnorm_bwd(dhn, h_kv, row(kv_norm), dh, "kv_norm_b")
            G["kv_norm"] = gk[0]
        dgpre, dpp = ple_bwd(dh, sv["gpre"], sv["pp"], f"ple_b{i}")
        G["ple_proj"][i] = matmul(sv["p_i"], dpp, "tn", F32, f"ple_proj_dw{i}")
        G["ple_gate"][i] = matmul(sv["hn3"], dgpre, "tn", F32, f"ple_gate_dw{i}")
        dhn = matmul(dgpre, full["ple_gate"][i], "nt", F32, f"ple_gate_dx{i}")
        gain = row(ple_norm[i])
        if i + 1 in scatter_h:
            gain = gain + scatter_h[i + 1]["token"][0, 0]
        dh, gn_ = rmsnorm_bwd(dhn, sv["h2"], gain, dh, f"ple_norm_b{i}")
        G["ple_norm"][i] = gn_[0]
        dact = bs(matmul(dh, full["ffn_down"][i], "nt", F32, f"ffn_down_dx{i}"))
        G["ffn_down"][i] = matmul(fl(sv["act"]), dh, "tn", F32, f"ffn_down_dw{i}")
        dug, duv, dwg, dwv, dbg, dbv = ffn_conv_bwd(sv["ug"], sv["uv"], dact, sv["fwg"], sv["fwv"], sv["fbg"], sv["fbv"], f"ffn_conv_b{i}")
        dug, duv = fl(dug), fl(duv)
        G["ffn_conv_w"][i] = jnp.concatenate([dwg[:FFN_CONV], dwv[:FFN_CONV]], axis=1)
        G["ffn_conv_b"][i] = jnp.concatenate([dbg[0], dbv[0]])
        G["ffn_up"][i] = (matmul(sv["hn2"], dug, "tn", F32, f"ffn_upg_dw{i}"), matmul(sv["hn2"], duv, "tn", F32, f"ffn_upv_dw{i}"))
        dhn = matmul(dug, sv["wug"], "nt", F32, f"ffn_upg_dx{i}")
        dhn = matmul(duv, sv["wuv"], "nt", F32, f"ffn_upv_dx{i}", add=dhn)
        dh, gn_ = rmsnorm_bwd(dhn, sv["h1"], row(ffn_norm[i]), dh, f"ffn_norm_b{i}")
        G["ffn_norm"][i] = gn_[0]
        if i < N_A:
            dgn = matmul(dh, full["ssm_out_proj"][i], "nt", F32, f"ssm_out_dx{i}")
            G["ssm_out_proj"][i] = matmul(sv["gn"], dh, "tn", F32, f"ssm_out_dw{i}")
            dy, dz, dnw = gated_norm_bwd(dgn, fl(sv["y"]), sv["z"], row(full["ssm_norm"][i]), f"ssm_gnorm_b{i}")
            G["ssm_norm"][i] = dnw[0]
            dxs, dbm, dcm, ddtr4, da4, dbias4, ddexp = ssd_bwd(sv["xc"], sv["dtr"], sv["dtb"], sv["alog"], sv["dexp"], bs(dy), sv["sprev"], f"ssd_b{i}")
            G["ssm_a_log"][i] = da4[:, 0, :HPG].reshape(-1)
            G["ssm_dt_bias"][i] = dbias4[:, 0, :HPG].reshape(-1)
            G["ssm_d"][i] = ddexp.reshape(SSM_HEADS, SSM_HEAD_DIM).sum(axis=-1)
            dxc = jnp.concatenate([dxs, dbm, dcm], axis=-1)
            dxbc, dcw, dcb = ssm_conv_bwd(sv["xbc"], dxc, sv["cw"], sv["cb"], f"ssm_conv_b{i}")
            dxbc = fl(dxbc)
            G["ssm_conv_w"][i] = dcw[:SSM_CONV]
            G["ssm_conv_b"][i] = dcb[0]
            ddtr = ddtr4.reshape(t, SSM_GROUPS, LANE)[:, :, :HPG].reshape(t, SSM_HEADS)
            ddtr = jnp.pad(ddtr, ((0, 0), (0, LANE - SSM_HEADS))).astype(BF16)
            hn = sv["hn"]
            G["ssm_in_proj"][i] = jnp.concatenate([
                matmul(hn, dz, "tn", F32, f"ssm_z_dw{i}"), matmul(hn, dxbc, "tn", F32, f"ssm_xbc_dw{i}"),
                matmul(hn, ddtr, "tn", F32, f"ssm_dt_dw{i}")[:, :SSM_HEADS]], axis=1)
            dhn = matmul(dz, sv["wz"], "nt", F32, f"ssm_z_dx{i}")
            dhn = matmul(dxbc, sv["wxbc"], "nt", F32, f"ssm_xbc_dx{i}", add=dhn)
            dhn = matmul(ddtr, sv["wdt"], "nt", F32, f"ssm_dt_dx{i}", add=dhn)
        else:
            j = i - N_A
            do = bs(matmul(dh, full["w_o"][j], "nt", BF16, f"sb_o_dx{j}"))
            G["w_o"][j] = matmul(fl(sv["o"]), dh, "tn", F32, f"sb_o_dw{j}")
            dq, dk, dv = sb_attention_bwd(sv["q"], kv, sv["ltot"], do, dk, dv, f"sb_attn_b{j}")
            dq = fl(dq)
            G["w_q"][j] = matmul(sv["hn"], dq, "tn", F32, f"sb_q_dw{j}")
            dhn = matmul(dq, full["w_q"][j], "nt", F32, f"sb_q_dx{j}")
        dh, gn_ = rmsnorm_bwd(dhn, sv["h0"], row(attn_norm[i]), dh, f"attn_norm_b{i}")
        G["attn_norm"][i] = gn_[0]
        src = [grad_slots(n, idx) for n, idx in layer_items(i)]
        if i == 0:
            small_g = {n: jnp.stack(G[n]) for n in SMALL}
            src.append(_pack_rows([_to_slots(small_g[n], SMALL[n]) for n in SMALL], (N_DEV,)))
        scatter_src[i] = src
        scatter_h[i] = exchange_start("scatter", src, f"a2a_start{i}")
    grad_x = dh.reshape(bsz, seq, d)
    grads = {n: jnp.stack(G[n]) if isinstance(G[n], list) else G[n] for n in REPL}

    layer_parts = {n: {} for n in BIG}
    small_parts = None
    for i in reversed(range(DEPTH)):
        lands = exchange_wait(scatter_h[i], dh, f"a2a_wait{i}")
        got = [put_own(l, lax.dynamic_index_in_dim(s, me, 0, keepdims=False)) for l, s in zip(lands, scatter_src[i])]
        for (n, idx), g in zip(layer_items(i), got):
            layer_parts[n][idx] = g
        if i == 0:
            small_parts = _unpack_rows(got[-1], [given[n].size for n in SMALL], (N_DEV,))
    big_parts = [jnp.concatenate([layer_parts[n][k] for k in sorted(layer_parts[n], key=lambda v: -1 if v is None else v)], axis=1)
                 for n in BIG]
    parts_rp = all_gather([_pack_rows([grads[n].reshape(-1) for n in REPL])], "ag_repl_grads")[0]
    sh_names = list(BIG) + list(SMALL)
    sh_parts = big_parts + small_parts
    rp_parts = _unpack_rows(parts_rp, [given[n].size for n in REPL], (N_DEV,))
    res = {}
    for n, pr in list(zip(sh_names, sh_parts)) + list(zip(REPL, rp_parts)):
        w2 = _as2d(given[n])
        outs = adamw(w2, pr.reshape((N_DEV,) + w2.shape), _as2d(given["m_" + n]), _as2d(given["v_" + n]), f"adamw_{n}")
        res[n] = [o.reshape(given[n].shape) for o in outs]
    order = ["attn_norm", "ffn_norm", "ple_norm", "ssm_in_proj", "ssm_conv_w", "ssm_conv_b", "ssm_dt_bias", "ssm_a_log",
             "ssm_d", "ssm_norm", "ssm_out_proj", "kv_norm", "w_kv", "w_q", "w_o", "ffn_up", "ffn_conv_w", "ffn_conv_b",
             "ffn_down", "ple_gate", "ple_proj", "final_norm"]
    return (loss, grad_x, *[res[n][0] for n in order], *[res[n][1] for n in order],
            *[res[n][2] for n in order], *[res[n][3] for n in order])
```

```python
import functools
import math

import jax
import jax.numpy as jnp
from jax import lax
from jax.experimental import pallas as pl
from jax.experimental.pallas import tpu as pltpu

F32 = jnp.float32
BF16 = jnp.bfloat16

N_DEV = 8
D_MODEL = 1024
SEQ = 2048
DEPTH = 4
N_A = 2
N_B = 2
SSM_D_INNER = 2048
SSM_HEAD_DIM = 64
SSM_HEADS = 32
SSM_GROUPS = 4
SSM_STATE = 128
SSM_CONV = 4
SSM_CHUNK = 128
SSM_CONV_DIM = 3072
SSM_ZX = 5120
SSM_IN_DIM = 5152
SB_HEADS = 16
SB_HEAD_DIM = 64
SB_WIDTH = 1024
D_FF = 2816
FFN_CONV = 3
PLE_DIM = 256
NORM_EPS = 1e-6
SSM_NORM_EPS = 1e-5

ADAM_LR = 0.001
ADAM_B1 = 0.9
ADAM_B2 = 0.999
ADAM_EPS = 1e-08
ADAM_WD = 0.01
ADAM_STEP = 10

LANE = 128
SUBLANE = 8
VMEM_LIMIT = 48 * 1024 * 1024

NN = (((1,), (0,)), ((), ()))
NT = (((1,), (1,)), ((), ()))
TN = (((0,), (0,)), ((), ()))


def _params(sem):
    return pltpu.CompilerParams(dimension_semantics=sem, vmem_limit_bytes=VMEM_LIMIT)


def _pick(n, prefs):
    for p in prefs:
        if n % p == 0:
            return p
    return n


def matmul(a, b, mode, out_dtype, name, add=None):
    if mode == "nn":
        (m, k), (k2, n) = a.shape, b.shape
    elif mode == "nt":
        (m, k), (n, k2) = a.shape, b.shape
    else:
        (k, m), (k2, n) = a.shape, b.shape
    assert k == k2, (a.shape, b.shape, mode)
    tm = _pick(m, (512, 256, 128))
    tn = _pick(n, (1408, 1024, 512, 256, 128))
    tk = _pick(k, (2816, 2048, 1024, 512, 256, 128))
    nk = k // tk
    dims = {"nn": NN, "nt": NT, "tn": TN}[mode]

    def body(*refs):
        a_ref, b_ref = refs[:2]
        add_ref = refs[2] if add is not None else None
        o_ref = refs[-2] if nk > 1 else refs[-1]
        part = lax.dot_general(a_ref[...].astype(BF16), b_ref[...].astype(BF16), dims, preferred_element_type=F32)

        def finish(r):
            if add_ref is not None:
                r = r + add_ref[...].astype(F32)
            o_ref[...] = r.astype(o_ref.dtype)

        if nk == 1:
            finish(part)
            return
        acc_ref = refs[-1]
        kk = pl.program_id(2)

        @pl.when(kk == 0)
        def _():
            acc_ref[...] = part

        @pl.when((kk > 0) & (kk < nk - 1))
        def _():
            acc_ref[...] += part

        @pl.when(kk == nk - 1)
        def _():
            finish(acc_ref[...] + part)

    if mode == "nn":
        a_spec = pl.BlockSpec((tm, tk), lambda i, j, kk: (i, kk))
        b_spec = pl.BlockSpec((tk, tn), lambda i, j, kk: (kk, j))
    elif mode == "nt":
        a_spec = pl.BlockSpec((tm, tk), lambda i, j, kk: (i, kk))
        b_spec = pl.BlockSpec((tn, tk), lambda i, j, kk: (j, kk))
    else:
        a_spec = pl.BlockSpec((tk, tm), lambda i, j, kk: (kk, i))
        b_spec = pl.BlockSpec((tk, tn), lambda i, j, kk: (kk, j))
    o_spec = pl.BlockSpec((tm, tn), lambda i, j, kk: (i, j))
    in_specs = [a_spec, b_spec] + ([o_spec] if add is not None else [])
    args = (a, b) + ((add,) if add is not None else ())
    return pl.pallas_call(
        body,
        name=name,
        grid=(m // tm, n // tn, nk),
        in_specs=in_specs,
        out_specs=o_spec,
        out_shape=jax.ShapeDtypeStruct((m, n), out_dtype),
        scratch_shapes=[pltpu.VMEM((tm, tn), F32)] if nk > 1 else [],
        compiler_params=_params(("parallel", "parallel", "arbitrary")),
    )(*args)


ROW_TILE = 512


def rmsnorm_fwd(x, gain, name):
    t, d = x.shape

    def body(x_ref, g_ref, o_ref):
        xv = x_ref[...]
        r = lax.rsqrt(jnp.mean(xv * xv, axis=-1, keepdims=True) + NORM_EPS)
        o_ref[...] = (xv * r * g_ref[...]).astype(o_ref.dtype)

    return pl.pallas_call(
        body,
        name=name,
        grid=(t // ROW_TILE,),
        in_specs=[pl.BlockSpec((ROW_TILE, d), lambda i: (i, 0)), pl.BlockSpec((1, d), lambda i: (0, 0))],
        out_specs=pl.BlockSpec((ROW_TILE, d), lambda i: (i, 0)),
        out_shape=jax.ShapeDtypeStruct((t, d), BF16),
        compiler_params=_params(("parallel",)),
    )(x, gain)


def rmsnorm_bwd(dy, x, gain, dres, name):
    t, d = x.shape

    def body(dy_ref, x_ref, g_ref, dres_ref, dx_ref, dg_ref):
        i = pl.program_id(0)
        xv = x_ref[...]
        r = lax.rsqrt(jnp.mean(xv * xv, axis=-1, keepdims=True) + NORM_EPS)
        xh = xv * r
        dyv = dy_ref[...].astype(F32)
        dxh = dyv * g_ref[...]
        dx = r * (dxh - xh * jnp.mean(dxh * xh, axis=-1, keepdims=True))
        dx_ref[...] = dres_ref[...] + dx

        @pl.when(i == 0)
        def _():
            dg_ref[...] = jnp.zeros_like(dg_ref)

        dg_ref[...] += jnp.sum(dyv * xh, axis=0, keepdims=True)

    row = pl.BlockSpec((ROW_TILE, d), lambda i: (i, 0))
    vec = pl.BlockSpec((1, d), lambda i: (0, 0))
    return pl.pallas_call(
        body,
        name=name,
        grid=(t // ROW_TILE,),
        in_specs=[row, row, vec, row],
        out_specs=[row, vec],
        out_shape=[jax.ShapeDtypeStruct((t, d), F32), jax.ShapeDtypeStruct((1, d), F32)],
        compiler_params=_params(("arbitrary",)),
    )(dy, x, gain, dres)


CONV_ROWS = 512
HALO = SUBLANE


def _conv_apply(ext, w_ref, bias, k, lo, n):
    acc = bias + w_ref[k - 1:k, :] * ext[lo:lo + n]
    for j in range(1, k):
        acc = acc + w_ref[k - 1 - j:k - j, :] * pltpu.roll(ext, j, 0)[lo:lo + n]
    return acc


def _conv_apply_t(ext, w_ref, k, n):
    rows = ext.shape[0]
    acc = w_ref[k - 1:k, :] * ext[:n]
    for j in range(1, k):
        acc = acc + w_ref[k - 1 - j:k - j, :] * pltpu.roll(ext, rows - j, 0)[:n]
    return acc


def _conv_dw(dpre_main, ext_u, k, n, dw_ref):
    for kk in range(k):
        j = k - 1 - kk
        sh = ext_u[HALO:HALO + n] if j == 0 else pltpu.roll(ext_u, j, 0)[HALO:HALO + n]
        dw_ref[kk:kk + 1, :] += jnp.sum(dpre_main * sh, axis=0, keepdims=True)


def _silu_grad(pre):
    sg = jax.nn.sigmoid(pre)
    return sg * (1.0 + pre * (1.0 - sg))


def _conv_blockspecs(seq, r, tc):
    nh = r // HALO
    last = seq // HALO - 1
    main = pl.BlockSpec((None, r, tc), lambda c, b, s: (b, s, c))
    before = pl.BlockSpec((None, HALO, tc), lambda c, b, s: (b, jnp.maximum(s * nh - 1, 0), c))
    after = pl.BlockSpec((None, HALO, tc), lambda c, b, s: (b, jnp.minimum((s + 1) * nh, last), c))
    return main, before, after


def ssm_conv_fwd(u, w, bias, name):
    bsz, seq, ch = u.shape
    r, tc, k = min(CONV_ROWS, seq), 512, SSM_CONV
    main, before, _ = _conv_blockspecs(seq, r, tc)
    wspec = pl.BlockSpec((SUBLANE, tc), lambda c, b, s: (0, c))
    bspec = pl.BlockSpec((1, tc), lambda c, b, s: (0, c))

    def body(u_ref, hb_ref, w_ref, b_ref, o_ref):
        s = pl.program_id(2)
        hb = jnp.where(s == 0, 0.0, hb_ref[...])
        ext = jnp.concatenate([hb, u_ref[...]], axis=0)
        pre = _conv_apply(ext, w_ref, b_ref[...], k, HALO, r)
        o_ref[...] = pre * jax.nn.sigmoid(pre)

    return pl.pallas_call(
        body, name=name, grid=(ch // tc, bsz, seq // r),
        in_specs=[main, before, wspec, bspec], out_specs=main,
        out_shape=jax.ShapeDtypeStruct(u.shape, F32),
        compiler_params=_params(("parallel", "parallel", "parallel")),
    )(u, u, w, bias)


def ssm_conv_bwd(u, dxc, w, bias, name):
    bsz, seq, ch = u.shape
    r, tc, k = min(CONV_ROWS, seq), 512, SSM_CONV
    ns = seq // r
    main, before, after = _conv_blockspecs(seq, r, tc)
    wspec = pl.BlockSpec((SUBLANE, tc), lambda c, b, s: (0, c))
    bspec = pl.BlockSpec((1, tc), lambda c, b, s: (0, c))

    def body(u_ref, hb_ref, ha_ref, d_ref, da_ref, w_ref, b_ref, du_ref, dw_ref, db_ref):
        b, s = pl.program_id(1), pl.program_id(2)
        hb = jnp.where(s == 0, 0.0, hb_ref[...])
        ext_u = jnp.concatenate([hb, u_ref[...], ha_ref[...]], axis=0)
        pre = _conv_apply(ext_u, w_ref, b_ref[...], k, HALO, r + HALO)
        dxe = jnp.concatenate([d_ref[...], jnp.where(s == ns - 1, 0.0, da_ref[...])], axis=0)
        dpre = dxe * _silu_grad(pre)
        du_ref[...] = _conv_apply_t(dpre, w_ref, k, r).astype(du_ref.dtype)

        @pl.when((b == 0) & (s == 0))
        def _():
            dw_ref[...] = jnp.zeros_like(dw_ref)
            db_ref[...] = jnp.zeros_like(db_ref)

        dpm = dpre[:r]
        _conv_dw(dpm, ext_u, k, r, dw_ref)
        db_ref[...] += jnp.sum(dpm, axis=0, keepdims=True)

    return pl.pallas_call(
        body, name=name, grid=(ch // tc, bsz, ns),
        in_specs=[main, before, after, main, after, wspec, bspec],
        out_specs=[main, wspec, bspec],
        out_shape=[jax.ShapeDtypeStruct(u.shape, BF16), jax.ShapeDtypeStruct((SUBLANE, ch), F32),
                   jax.ShapeDtypeStruct((1, ch), F32)],
        compiler_params=_params(("arbitrary", "arbitrary", "arbitrary")),
    )(u, u, u, dxc, dxc, w, bias)


def ffn_conv_fwd(ug, uv, wg, wv, bg, bv, name):
    bsz, seq, ch = ug.shape
    r, tc, k = min(CONV_ROWS, seq), 256, FFN_CONV
    main, before, _ = _conv_blockspecs(seq, r, tc)
    wspec = pl.BlockSpec((SUBLANE, tc), lambda c, b, s: (0, c))
    bspec = pl.BlockSpec((1, tc), lambda c, b, s: (0, c))

    def body(ug_ref, hg_ref, uv_ref, hv_ref, wg_ref, wv_ref, bg_ref, bv_ref, o_ref):
        s = pl.program_id(2)
        eg = jnp.concatenate([jnp.where(s == 0, 0.0, hg_ref[...]), ug_ref[...]], axis=0)
        ev = jnp.concatenate([jnp.where(s == 0, 0.0, hv_ref[...]), uv_ref[...]], axis=0)
        pg = _conv_apply(eg, wg_ref, bg_ref[...], k, HALO, r)
        pv = _conv_apply(ev, wv_ref, bv_ref[...], k, HALO, r)
        o_ref[...] = (pg * jax.nn.sigmoid(pg) * pv).astype(o_ref.dtype)

    return pl.pallas_call(
        body, name=name, grid=(ch // tc, bsz, seq // r),
        in_specs=[main, before, main, before, wspec, wspec, bspec, bspec], out_specs=main,
        out_shape=jax.ShapeDtypeStruct(ug.shape, BF16),
        compiler_params=_params(("parallel", "parallel", "parallel")),
    )(ug, ug, uv, uv, wg, wv, bg, bv)


def ffn_conv_bwd(ug, uv, dact, wg, wv, bg, bv, name):
    bsz, seq, ch = ug.shape
    r, tc, k = min(CONV_ROWS, seq), 256, FFN_CONV
    ns = seq // r
    main, before, after = _conv_blockspecs(seq, r, tc)
    wspec = pl.BlockSpec((SUBLANE, tc), lambda c, b, s: (0, c))
    bspec = pl.BlockSpec((1, tc), lambda c, b, s: (0, c))

    def body(ug_ref, gb_ref, ga_ref, uv_ref, vb_ref, va_ref, d_ref, da_ref, wg_ref, wv_ref, bg_ref, bv_ref,
             dug_ref, duv_ref, dwg_ref, dwv_ref, dbg_ref, dbv_ref):
        b, s = pl.program_id(1), pl.program_id(2)
        eg = jnp.concatenate([jnp.where(s == 0, 0.0, gb_ref[...]), ug_ref[...], ga_ref[...]], axis=0)
        ev = jnp.concatenate([jnp.where(s == 0, 0.0, vb_ref[...]), uv_ref[...], va_ref[...]], axis=0)
        pg = _conv_apply(eg, wg_ref, bg_ref[...], k, HALO, r + HALO)
        pv = _conv_apply(ev, wv_ref, bv_ref[...], k, HALO, r + HALO)
        de = jnp.concatenate([d_ref[...], jnp.where(s == ns - 1, 0.0, da_ref[...])], axis=0)
        sg = jax.nn.sigmoid(pg)
        dpg = de * pv * (sg * (1.0 + pg * (1.0 - sg)))
        dpv = de * (pg * sg)
        dug_ref[...] = _conv_apply_t(dpg, wg_ref, k, r).astype(dug_ref.dtype)
        duv_ref[...] = _conv_apply_t(dpv, wv_ref, k, r).astype(duv_ref.dtype)

        @pl.when((b == 0) & (s == 0))
        def _():
            dwg_ref[...] = jnp.zeros_like(dwg_ref)
            dwv_ref[...] = jnp.zeros_like(dwv_ref)
            dbg_ref[...] = jnp.zeros_like(dbg_ref)
            dbv_ref[...] = jnp.zeros_like(dbv_ref)

        _conv_dw(dpg[:r], eg, k, r, dwg_ref)
        _conv_dw(dpv[:r], ev, k, r, dwv_ref)
        dbg_ref[...] += jnp.sum(dpg[:r], axis=0, keepdims=True)
        dbv_ref[...] += jnp.sum(dpv[:r], axis=0, keepdims=True)

    wshape = jax.ShapeDtypeStruct((SUBLANE, ch), F32)
    bshape = jax.ShapeDtypeStruct((1, ch), F32)
    return pl.pallas_call(
        body, name=name, grid=(ch // tc, bsz, ns),
        in_specs=[main, before, after, main, before, after, main, after, wspec, wspec, bspec, bspec],
        out_specs=[main, main, wspec, wspec, bspec, bspec],
        out_shape=[jax.ShapeDtypeStruct(ug.shape, BF16), jax.ShapeDtypeStruct(ug.shape, BF16), wshape, wshape, bshape, bshape],
        compiler_params=_params(("arbitrary", "arbitrary", "arbitrary")),
    )(ug, ug, ug, uv, uv, uv, dact, dact, wg, wv, bg, bv)


GN_ROWS = 256
GN_GROUP = SSM_D_INNER // SSM_GROUPS


def gated_norm_fwd(y, z, w, name):
    t, d = y.shape

    def body(y_ref, z_ref, w_ref, o_ref):
        for g in range(d // GN_GROUP):
            sl = slice(g * GN_GROUP, (g + 1) * GN_GROUP)
            zv = z_ref[:, sl]
            gv = y_ref[:, sl] * (zv * jax.nn.sigmoid(zv))
            r = lax.rsqrt(jnp.mean(gv * gv, axis=-1, keepdims=True) + SSM_NORM_EPS)
            o_ref[:, sl] = (gv * r * w_ref[:, sl]).astype(o_ref.dtype)

    row = pl.BlockSpec((GN_ROWS, d), lambda i: (i, 0))
    vec = pl.BlockSpec((1, d), lambda i: (0, 0))
    return pl.pallas_call(
        body, name=name, grid=(t // GN_ROWS,), in_specs=[row, row, vec], out_specs=row,
        out_shape=jax.ShapeDtypeStruct((t, d), BF16), compiler_params=_params(("parallel",)),
    )(y, z, w)


def gated_norm_bwd(dgn, y, z, w, name):
    t, d = y.shape

    def body(d_ref, y_ref, z_ref, w_ref, dy_ref, dz_ref, dw_ref):
        i = pl.program_id(0)

        @pl.when(i == 0)
        def _():
            dw_ref[...] = jnp.zeros_like(dw_ref)

        for g in range(d // GN_GROUP):
            sl = slice(g * GN_GROUP, (g + 1) * GN_GROUP)
            zv, yv, dv = z_ref[:, sl], y_ref[:, sl], d_ref[:, sl]
            sg = jax.nn.sigmoid(zv)
            sz = zv * sg
            gv = yv * sz
            r = lax.rsqrt(jnp.mean(gv * gv, axis=-1, keepdims=True) + SSM_NORM_EPS)
            gh = gv * r
            dgh = dv * w_ref[:, sl]
            dw_ref[:, sl] += jnp.sum(dv * gh, axis=0, keepdims=True)
            dg = r * (dgh - gh * jnp.mean(dgh * gh, axis=-1, keepdims=True))
            dy_ref[:, sl] = dg * sz
            dz_ref[:, sl] = (dg * yv * (sg * (1.0 + zv * (1.0 - sg)))).astype(dz_ref.dtype)

    row = pl.BlockSpec((GN_ROWS, d), lambda i: (i, 0))
    vec = pl.BlockSpec((1, d), lambda i: (0, 0))
    return pl.pallas_call(
        body, name=name, grid=(t // GN_ROWS,), in_specs=[row, row, row, vec], out_specs=[row, row, vec],
        out_shape=[jax.ShapeDtypeStruct((t, d), F32), jax.ShapeDtypeStruct((t, d), BF16), jax.ShapeDtypeStruct((1, d), F32)],
        compiler_params=_params(("arbitrary",)),
    )(dgn, y, z, w)


def ple_fwd(h, gpre, pp, name):
    t, d = h.shape

    def body(h_ref, g_ref, p_ref, o_ref):
        o_ref[...] = h_ref[...] + jax.nn.sigmoid(g_ref[...]) * p_ref[...]

    row = pl.BlockSpec((ROW_TILE, d), lambda i: (i, 0))
    return pl.pallas_call(
        body, name=name, grid=(t // ROW_TILE,), in_specs=[row, row, row], out_specs=row,
        out_shape=jax.ShapeDtypeStruct((t, d), F32), compiler_params=_params(("parallel",)),
    )(h, gpre, pp)


def ple_bwd(dh, gpre, pp, name):
    t, d = dh.shape

    def body(d_ref, g_ref, p_ref, dg_ref, dp_ref):
        sg = jax.nn.sigmoid(g_ref[...])
        dv = d_ref[...]
        dg_ref[...] = (dv * p_ref[...] * sg * (1.0 - sg)).astype(dg_ref.dtype)
        dp_ref[...] = (dv * sg).astype(dp_ref.dtype)

    row = pl.BlockSpec((ROW_TILE, d), lambda i: (i, 0))
    return pl.pallas_call(
        body, name=name, grid=(t // ROW_TILE,), in_specs=[row, row, row], out_specs=[row, row],
        out_shape=[jax.ShapeDtypeStruct((t, d), BF16), jax.ShapeDtypeStruct((t, d), BF16)],
        compiler_params=_params(("parallel",)),
    )(dh, gpre, pp)


def loss_head(h, gain, target, name):
    t, d = h.shape

    def body(x_ref, g_ref, t_ref, l_ref, dx_ref, dg_ref):
        i = pl.program_id(0)
        xv = x_ref[...]
        r = lax.rsqrt(jnp.mean(xv * xv, axis=-1, keepdims=True) + NORM_EPS)
        xh = xv * r
        err = xh * g_ref[...] - t_ref[...]
        part = 0.5 * jnp.sum(jnp.mean(err * err, axis=-1, keepdims=True), axis=0, keepdims=True)
        dyv = err * (1.0 / d)
        dxh = dyv * g_ref[...]
        dx_ref[...] = r * (dxh - xh * jnp.mean(dxh * xh, axis=-1, keepdims=True))

        @pl.when(i == 0)
        def _():
            dg_ref[...] = jnp.zeros_like(dg_ref)
            l_ref[...] = jnp.zeros_like(l_ref)

        dg_ref[...] += jnp.sum(dyv * xh, axis=0, keepdims=True)
        l_ref[...] += jnp.broadcast_to(part, l_ref.shape)

    row = pl.BlockSpec((ROW_TILE, d), lambda i: (i, 0))
    vec = pl.BlockSpec((1, d), lambda i: (0, 0))
    lspec = pl.BlockSpec((1, LANE), lambda i: (0, 0))
    return pl.pallas_call(
        body, name=name, grid=(t // ROW_TILE,), in_specs=[row, vec, row], out_specs=[lspec, row, vec],
        out_shape=[jax.ShapeDtypeStruct((1, LANE), F32), jax.ShapeDtypeStruct((t, d), F32), jax.ShapeDtypeStruct((1, d), F32)],
        compiler_params=_params(("arbitrary",)),
    )(h, gain, target)


CHUNK = SSM_CHUNK
HPG = SSM_HEADS // SSM_GROUPS
GW = HPG * SSM_HEAD_DIM
NEG = -1e30


def _ssd_dt_terms(dtr_ref, bias_ref, alog_ref, g):
    shift = (LANE - HPG * g) % LANE
    li = lax.broadcasted_iota(jnp.int32, (CHUNK, CHUNK), 0)
    si = lax.broadcasted_iota(jnp.int32, (CHUNK, CHUNK), 1)
    x0 = pltpu.roll(dtr_ref[...] + bias_ref[...], shift, 1)
    dt = jnp.maximum(x0, 0.0) + jnp.log1p(jnp.exp(-jnp.abs(x0)))
    a = pltpu.roll(jnp.broadcast_to(-jnp.exp(alog_ref[...]), (CHUNK, LANE)), shift, 1)
    adt = dt * a
    acs = lax.dot_general((li >= si).astype(F32), adt, NN, precision=lax.Precision.HIGHEST, preferred_element_type=F32)
    last = jnp.broadcast_to(acs[CHUNK - 1:CHUNK, :], (CHUNK, LANE))
    return dict(li=li, si=si, x0=x0, dt=dt, a=a, acs=acs, acs_t=acs.T, e_acs=jnp.exp(acs),
                w_all=jnp.exp(last - acs), e_last=jnp.exp(last))


def _ssd_specs(nc, rev):
    def ci(c):
        return nc - 1 - c if rev else c
    nb = SSM_D_INNER // LANE
    xs = pl.BlockSpec((None, CHUNK, GW), lambda g, b, c: (b, ci(c), g))
    bm = pl.BlockSpec((None, CHUNK, LANE), lambda g, b, c: (b, ci(c), nb + g))
    cm = pl.BlockSpec((None, CHUNK, LANE), lambda g, b, c: (b, ci(c), nb + SSM_GROUPS + g))
    dtr = pl.BlockSpec((None, CHUNK, LANE), lambda g, b, c: (b, ci(c), 0))
    vec = pl.BlockSpec((1, LANE), lambda g, b, c: (0, 0))
    dexp = pl.BlockSpec((1, GW), lambda g, b, c: (0, g))
    st = pl.BlockSpec((None, None, CHUNK, GW), lambda g, b, c: (b, ci(c), 0, g))
    return xs, bm, cm, dtr, vec, dexp, st


def ssd_fwd(xc, dtr, dt_bias, a_log, d_exp, name):
    bsz, seq, _ = xc.shape
    nc = seq // CHUNK
    xs_s, bm_s, cm_s, dtr_s, vec_s, dexp_s, st_s = _ssd_specs(nc, False)

    def body(xs_ref, b_ref, c_ref, dtr_ref, bias_ref, alog_ref, dexp_ref, y_ref, sp_ref, state_ref):
        g, c = pl.program_id(0), pl.program_id(2)

        @pl.when(c == 0)
        def _():
            state_ref[...] = jnp.zeros_like(state_ref)

        t = _ssd_dt_terms(dtr_ref, bias_ref, alog_ref, g)
        tril = t["li"] >= t["si"]
        xs = xs_ref[...]
        b16 = b_ref[...].astype(BF16)
        c16 = c_ref[...].astype(BF16)
        bt16 = b_ref[...].T.astype(BF16)
        cb = lax.dot_general(c16, b16, NT, preferred_element_type=F32)
        sp = state_ref[...]
        sp_ref[...] = sp
        cs = jnp.dot(c16, sp.astype(BF16), preferred_element_type=F32)
        y_parts, s_parts = [], []
        for r in range(HPG):
            sl = slice(r * SSM_HEAD_DIM, (r + 1) * SSM_HEAD_DIM)
            diff = t["acs"][:, r:r + 1] - t["acs_t"][r:r + 1, :]
            m16 = (cb * jnp.exp(jnp.where(tril, diff, NEG))).astype(BF16)
            xh = xs[:, sl]
            xd = xh * t["dt"][:, r:r + 1]
            yh = jnp.dot(m16, xd.astype(BF16), preferred_element_type=F32)
            yh = yh + t["e_acs"][:, r:r + 1] * cs[:, sl] + xh * dexp_ref[:, sl]
            y_parts.append(yh)
            xw16 = (xd * t["w_all"][:, r:r + 1]).astype(BF16)
            s_parts.append(sp[:, sl] * t["e_last"][:, r:r + 1] + jnp.dot(bt16, xw16, preferred_element_type=F32))
        y_ref[...] = jnp.concatenate(y_parts, axis=1)
        state_ref[...] = jnp.concatenate(s_parts, axis=1)

    return pl.pallas_call(
        body, name=name, grid=(SSM_GROUPS, bsz, nc),
        in_specs=[xs_s, bm_s, cm_s, dtr_s, vec_s, vec_s, dexp_s],
        out_specs=[xs_s, st_s],
        out_shape=[jax.ShapeDtypeStruct((bsz, seq, SSM_D_INNER), F32),
                   jax.ShapeDtypeStruct((bsz, nc, SSM_STATE, SSM_D_INNER), F32)],
        scratch_shapes=[pltpu.VMEM((SSM_STATE, GW), F32)],
        compiler_params=_params(("arbitrary", "arbitrary", "arbitrary")),
    )(xc, xc, xc, dtr, dt_bias, a_log, d_exp)


def ssd_bwd(xc, dtr, dt_bias, a_log, d_exp, dy, sprev, name):
    bsz, seq, _ = xc.shape
    nc = seq // CHUNK
    xs_s, bm_s, cm_s, dtr_s, vec_s, dexp_s, st_s = _ssd_specs(nc, True)
    grp = pl.BlockSpec((None, CHUNK, LANE), lambda g, b, c: (b, nc - 1 - c, g))
    acc = pl.BlockSpec((None, 1, LANE), lambda g, b, c: (g, 0, 0))

    def body(xs_ref, b_ref, c_ref, dtr_ref, bias_ref, alog_ref, dexp_ref, dy_ref, sp_ref,
             dxs_ref, db_ref, dc_ref, ddtr_ref, da_ref, dbias_ref, ddexp_ref, dstate_ref):
        g, b, c = pl.program_id(0), pl.program_id(1), pl.program_id(2)

        @pl.when(c == 0)
        def _():
            dstate_ref[...] = jnp.zeros_like(dstate_ref)

        @pl.when((b == 0) & (c == 0))
        def _():
            da_ref[...] = jnp.zeros_like(da_ref)
            dbias_ref[...] = jnp.zeros_like(dbias_ref)
            ddexp_ref[...] = jnp.zeros_like(ddexp_ref)

        t = _ssd_dt_terms(dtr_ref, bias_ref, alog_ref, g)
        li, si = t["li"], t["si"]
        tril, triu = li >= si, li <= si
        row_is_last = li[:, :1] == CHUNK - 1
        xs, dy = xs_ref[...], dy_ref[...]
        b16 = b_ref[...].astype(BF16)
        c16 = c_ref[...].astype(BF16)
        ct16 = c_ref[...].T.astype(BF16)
        cb = lax.dot_general(c16, b16, NT, preferred_element_type=F32)
        cbt = lax.dot_general(b16, c16, NT, preferred_element_type=F32)
        sp, ds = sp_ref[...], dstate_ref[...]
        sp16, ds16 = sp.astype(BF16), ds.astype(BF16)
        cs = jnp.dot(c16, sp16, preferred_element_type=F32)
        bds = jnp.dot(b16, ds16, preferred_element_type=F32)
        spds = jnp.sum(sp * ds, axis=0, keepdims=True)
        zero = jnp.zeros((CHUNK, CHUNK), F32)
        dsum_g, dsum_gt, dacs, ddt_dir = zero, zero, zero, zero
        dxs_parts, edy_parts, xw_parts, dsp_parts = [], [], [], []
        for r in range(HPG):
            sl = slice(r * SSM_HEAD_DIM, (r + 1) * SSM_HEAD_DIM)
            diff = t["acs"][:, r:r + 1] - t["acs_t"][r:r + 1, :]
            lam = jnp.exp(jnp.where(tril, diff, NEG))
            lam_t = jnp.exp(jnp.where(triu, -diff, NEG))
            m, m_t = cb * lam, cbt * lam_t
            xh, dyh = xs[:, sl], dy[:, sl]
            dtc, wc, ec, elc = t["dt"][:, r:r + 1], t["w_all"][:, r:r + 1], t["e_acs"][:, r:r + 1], t["e_last"][:, r:r + 1]
            xd = xh * dtc
            xd16, dyh16 = xd.astype(BF16), dyh.astype(BF16)
            dxd = jnp.dot(m_t.astype(BF16), dyh16, preferred_element_type=F32) + wc * bds[:, sl]
            dm = lax.dot_general(dyh16, xd16, NT, preferred_element_type=F32)
            dm_t = lax.dot_general(xd16, dyh16, NT, preferred_element_type=F32)
            dsum_g = dsum_g + dm * lam
            dsum_gt = dsum_gt + dm_t * lam_t
            dacs_h = jnp.sum(dm * m, axis=1, keepdims=True) - jnp.sum(dm_t * m_t, axis=1, keepdims=True)
            dacs_h = dacs_h + jnp.sum(dyh * (ec * cs[:, sl]), axis=1, keepdims=True)
            wdw = wc * jnp.sum(xd * bds[:, sl], axis=1, keepdims=True)
            last_h = jnp.sum(wdw, axis=0, keepdims=True) + elc[0:1, :] * jnp.sum(spds[:, sl], axis=1, keepdims=True)
            dacs_h = dacs_h - wdw + jnp.where(row_is_last, last_h, 0.0)
            onehot = (si[:1, :] == r).astype(F32)
            dacs = dacs + dacs_h * onehot
            ddt_dir = ddt_dir + jnp.sum(dxd * xh, axis=1, keepdims=True) * onehot
            dxs_parts.append(dxd * dtc + dyh * dexp_ref[:, sl])
            edy_parts.append(ec * dyh)
            xw_parts.append(wc * xd)
            dsp_parts.append(elc * ds[:, sl])
        dxs_ref[...] = jnp.concatenate(dxs_parts, axis=1)
        edy16 = jnp.concatenate(edy_parts, axis=1).astype(BF16)
        xw16 = jnp.concatenate(xw_parts, axis=1).astype(BF16)
        dc_ref[...] = (jnp.dot(dsum_g.astype(BF16), b16, preferred_element_type=F32)
                       + lax.dot_general(edy16, sp16, NT, preferred_element_type=F32))
        db_ref[...] = (jnp.dot(dsum_gt.astype(BF16), c16, preferred_element_type=F32)
                       + lax.dot_general(xw16, ds16, NT, preferred_element_type=F32))
        dstate_ref[...] = jnp.concatenate(dsp_parts, axis=1) + jnp.dot(ct16, edy16, preferred_element_type=F32)
        dadt = lax.dot_general(triu.astype(F32), dacs, NN, precision=lax.Precision.HIGHEST, preferred_element_type=F32)
        head_lane = si < HPG
        ddtr = jnp.where(head_lane, (dadt * t["a"] + ddt_dir) * jax.nn.sigmoid(t["x0"]), 0.0)
        ddtr_ref[...] = ddtr
        da_ref[...] += jnp.sum(jnp.where(head_lane, dadt * t["dt"] * t["a"], 0.0), axis=0, keepdims=True)
        dbias_ref[...] += jnp.sum(ddtr, axis=0, keepdims=True)
        ddexp_ref[...] += jnp.sum(dy * xs, axis=0, keepdims=True)

    return pl.pallas_call(
        body, name=name, grid=(SSM_GROUPS, bsz, nc),
        in_specs=[xs_s, bm_s, cm_s, dtr_s, vec_s, vec_s, dexp_s, xs_s, st_s],
        out_specs=[xs_s, grp, grp, grp, acc, acc, dexp_s],
        out_shape=[jax.ShapeDtypeStruct((bsz, seq, SSM_D_INNER), F32),
                   jax.ShapeDtypeStruct((bsz, seq, SSM_GROUPS * SSM_STATE), F32),
                   jax.ShapeDtypeStruct((bsz, seq, SSM_GROUPS * SSM_STATE), F32),
                   jax.ShapeDtypeStruct((bsz, seq, SSM_GROUPS * LANE), F32),
                   jax.ShapeDtypeStruct((SSM_GROUPS, 1, LANE), F32),
                   jax.ShapeDtypeStruct((SSM_GROUPS, 1, LANE), F32),
                   jax.ShapeDtypeStruct((1, SSM_D_INNER), F32)],
        scratch_shapes=[pltpu.VMEM((SSM_STATE, GW), F32)],
        compiler_params=_params(("arbitrary", "arbitrary", "arbitrary")),
    )(xc, xc, xc, dtr, dt_bias, a_log, d_exp, dy, sprev)


SB_TQ = 512
SB_TK = 128
SB_SCALE = SB_HEAD_DIM ** -0.5


def _split_dot(x, u16):
    hi = x.astype(BF16)
    lo = (x - hi.astype(F32)).astype(BF16)
    return jnp.dot(hi, u16, preferred_element_type=F32) + jnp.dot(lo, u16, preferred_element_type=F32)


def _sb_scores(qh, kj, mask):
    z = lax.dot_general(qh, kj, NT, preferred_element_type=F32)
    l1p = jnp.log(1.0 + jnp.exp(-jnp.abs(z)))
    lb = jnp.minimum(z, 0.0) - l1p
    lk = -jnp.maximum(z, 0.0) - l1p
    if mask is not None:
        lk = jnp.where(mask, lk, 0.0)
    return lb, lk


def _sb_diag_masks(tq):
    r = lax.broadcasted_iota(jnp.int32, (tq, SB_TK), 0)
    c = lax.broadcasted_iota(jnp.int32, (tq, SB_TK), 1)
    return [r > c + d * SB_TK for d in range(tq // SB_TK)]


def _sb_heads(x, lane):
    return [jnp.where((lane // SB_HEAD_DIM) == hh, x, jnp.zeros_like(x)) for hh in range(2)]


def sb_attention_fwd(q, kv, name):
    bsz, seq, width = q.shape
    tq = min(SB_TQ, seq)
    npair = width // LANE
    qspec = pl.BlockSpec((None, tq, LANE), lambda b, p, i: (b, i, p))
    kspec = pl.BlockSpec((None, seq, LANE), lambda b, p, i: (b, 0, p))
    vspec = pl.BlockSpec((None, seq, LANE), lambda b, p, i: (b, 0, npair + p))

    def body(q_ref, k_ref, v_ref, o_ref, l_ref):
        i = pl.program_id(2)
        ndiag = tq // SB_TK
        nfull = i * ndiag
        lane = lax.broadcasted_iota(jnp.int32, (tq, LANE), 1)
        ui = lax.broadcasted_iota(jnp.int32, (SB_TK, SB_TK), 0)
        uj = lax.broadcasted_iota(jnp.int32, (SB_TK, SB_TK), 1)
        u_excl = (ui > uj).astype(BF16)
        qs = _sb_heads(q_ref[...] * SB_SCALE, lane)

        def tile(j, carry, mask):
            rows = pl.ds(pl.multiple_of(j * SB_TK, SB_TK), SB_TK)
            kj, vj = k_ref[rows, :], v_ref[rows, :]
            new = []
            for hh in range(2):
                acc, run = carry[hh]
                lb, lk = _sb_scores(qs[hh], kj, mask)
                w = jnp.exp(lb + _split_dot(lk, u_excl) + run)
                if mask is not None:
                    w = jnp.where(mask, w, 0.0)
                acc = acc + jnp.dot(w.astype(BF16), vj, preferred_element_type=F32)
                new.append((acc, run + jnp.sum(lk, axis=1, keepdims=True)))
            return tuple(new)

        carry = tuple((jnp.zeros((tq, LANE), F32), jnp.zeros((tq, 1), F32)) for _ in range(2))
        masks = _sb_diag_masks(tq)
        for dd in reversed(range(ndiag)):
            carry = tile(nfull + dd, carry, masks[dd])
        carry = lax.fori_loop(0, nfull // 2, lambda jj, cr: tile(nfull - 2 - 2 * jj, tile(nfull - 1 - 2 * jj, cr, None), None), carry)
        in0 = (lane // SB_HEAD_DIM) == 0
        o_ref[...] = jnp.where(in0, carry[0][0], carry[1][0]).astype(o_ref.dtype)
        l_ref[...] = jnp.where(in0, carry[0][1], carry[1][1])

    return pl.pallas_call(
        body, name=name, grid=(bsz, npair, seq // tq),
        in_specs=[qspec, kspec, vspec], out_specs=[qspec, qspec],
        out_shape=[jax.ShapeDtypeStruct(q.shape, BF16), jax.ShapeDtypeStruct(q.shape, F32)],
        compiler_params=_params(("parallel", "parallel", "arbitrary")),
    )(q, kv, kv)


def sb_attention_bwd(q, kv, ltot, do, dk_in, dv_in, name):
    bsz, seq, width = q.shape
    tq = min(SB_TQ, seq)
    npair = width // LANE
    qspec = pl.BlockSpec((None, tq, LANE), lambda b, p, i: (b, i, p))
    kspec = pl.BlockSpec((None, seq, LANE), lambda b, p, i: (b, 0, p))
    vspec = pl.BlockSpec((None, seq, LANE), lambda b, p, i: (b, 0, npair + p))

    def body(q_ref, k_ref, v_ref, l_ref, do_ref, dki_ref, dvi_ref, dq_ref, dk_ref, dv_ref):
        i = pl.program_id(2)

        @pl.when(i == 0)
        def _():
            dk_ref[...] = dki_ref[...]
            dv_ref[...] = dvi_ref[...]

        ndiag = tq // SB_TK
        nfull = i * ndiag
        lane = lax.broadcasted_iota(jnp.int32, (tq, LANE), 1)
        ui = lax.broadcasted_iota(jnp.int32, (SB_TK, SB_TK), 0)
        uj = lax.broadcasted_iota(jnp.int32, (SB_TK, SB_TK), 1)
        u_le = (ui <= uj).astype(BF16)
        u_lt = (ui < uj).astype(BF16)
        qs = _sb_heads(q_ref[...] * SB_SCALE, lane)
        dos = _sb_heads(do_ref[...], lane)
        ltots = [l_ref[:, hh * SB_HEAD_DIM:hh * SB_HEAD_DIM + 1] for hh in range(2)]

        def tile(j, carry, mask, r0=0):
            rows = pl.ds(pl.multiple_of(j * SB_TK, SB_TK), SB_TK)
            kj, vj = k_ref[rows, :], v_ref[rows, :]
            new, dk_t, dv_t = [], None, None
            for hh in range(2):
                acc, run, run_a = carry[hh]
                qh, doh = qs[hh][r0:], dos[hh][r0:]
                lb, lk = _sb_scores(qh, kj, mask)
                w = jnp.exp(lb + (ltots[hh][r0:] - run[r0:] - _split_dot(lk, u_le)))
                if mask is not None:
                    w = jnp.where(mask, w, 0.0)
                a = lax.dot_general(doh, vj, NT, preferred_element_type=F32) * w
                ca = run_a[r0:] + _split_dot(a, u_lt)
                sg = jnp.exp(lb)
                dz = a * (1.0 - sg) - ca * sg
                if mask is not None:
                    dz = jnp.where(mask, dz, 0.0)
                dz16 = dz.astype(BF16)
                part = [acc[r0:] + jnp.dot(dz16, kj, preferred_element_type=F32),
                        run[r0:] + jnp.sum(lk, axis=1, keepdims=True), run_a[r0:] + jnp.sum(a, axis=1, keepdims=True)]
                if r0:
                    part = [jnp.concatenate([old[:r0], p], axis=0) for old, p in zip((acc, run, run_a), part)]
                dkh = lax.dot_general(dz16, qh, TN, preferred_element_type=F32)
                dvh = lax.dot_general(w.astype(BF16), doh, TN, preferred_element_type=F32)
                dk_t = dkh if dk_t is None else dk_t + dkh
                dv_t = dvh if dv_t is None else dv_t + dvh
                new.append(tuple(part))
            dk_ref[rows, :] += dk_t
            dv_ref[rows, :] += dv_t
            return tuple(new)

        zcol = jnp.zeros((tq, 1), F32)
        carry = tuple((jnp.zeros((tq, LANE), F32), zcol, zcol) for _ in range(2))
        carry = lax.fori_loop(0, nfull // 2, lambda j, cr: tile(2 * j + 1, tile(2 * j, cr, None), None), carry)
        masks = _sb_diag_masks(tq)
        for dd in range(ndiag):
            carry = tile(nfull + dd, carry, masks[dd][dd * SB_TK:], dd * SB_TK)
        in0 = (lane // SB_HEAD_DIM) == 0
        dq_ref[...] = (jnp.where(in0, carry[0][0], carry[1][0]) * SB_SCALE).astype(dq_ref.dtype)

    return pl.pallas_call(
        body, name=name, grid=(bsz, npair, seq // tq),
        in_specs=[qspec, kspec, vspec, qspec, qspec, kspec, kspec], out_specs=[qspec, kspec, kspec],
        out_shape=[jax.ShapeDtypeStruct(q.shape, BF16), jax.ShapeDtypeStruct(q.shape, F32), jax.ShapeDtypeStruct(q.shape, F32)],
        compiler_params=_params(("parallel", "parallel", "arbitrary")),
    )(q, kv, kv, ltot, do, dk_in, dv_in)


def adamw(w, parts, m, v, name):
    r, c = w.shape
    tr = r
    for cand in (256, 176, 128, 64, 32, 16, 8):
        if r % cand == 0:
            tr = cand
            break
    bc1 = 1.0 - ADAM_B1 ** ADAM_STEP
    bc2 = 1.0 - ADAM_B2 ** ADAM_STEP

    def body(w_ref, p_ref, m_ref, v_ref, g_ref, d_ref, mo_ref, vo_ref):
        g = p_ref[0].astype(F32)
        for j in range(1, N_DEV):
            g = g + p_ref[j].astype(F32)
        mn = ADAM_B1 * m_ref[...] + (1.0 - ADAM_B1) * g
        vn = ADAM_B2 * v_ref[...] + (1.0 - ADAM_B2) * (g * g)
        g_ref[...] = g
        mo_ref[...] = mn
        vo_ref[...] = vn
        d_ref[...] = -ADAM_LR * ((mn / bc1) / (jnp.sqrt(vn / bc2) + ADAM_EPS) + ADAM_WD * w_ref[...])

    blk = pl.BlockSpec((tr, c), lambda i: (i, 0))
    pblk = pl.BlockSpec((N_DEV, tr, c), lambda i: (0, i, 0))
    shp = jax.ShapeDtypeStruct((r, c), F32)
    return pl.pallas_call(
        body, name=name, grid=(r // tr,), in_specs=[blk, pblk, blk, blk], out_specs=[blk, blk, blk, blk],
        out_shape=[shp, shp, shp, shp], compiler_params=_params(("parallel",)),
    )(w, parts, m, v)


MESH_ID = pl.DeviceIdType.MESH
ANY = pl.BlockSpec(memory_space=pl.ANY)


def all_gather(xs, name):
    n = len(xs)

    def body(*refs):
        x_refs, out_refs = refs[:n], refs[n:2 * n]
        send_sems, recv_sems, local_sems = refs[2 * n:]
        mx, my, mc = lax.axis_index("x"), lax.axis_index("y"), lax.axis_index("c")
        me, sibling = (mx, my, mc), (mx, my, 1 - mc)
        chips = [(1 - mx, my), (mx, 1 - my), (1 - mx, 1 - my)]

        def slot(a, px, py, pc):
            return out_refs[a].at[4 * px + 2 * py + pc]

        def copy(a, k, block, to, src=None):
            return pltpu.make_async_remote_copy(
                src_ref=slot(a, *block) if src is None else src, dst_ref=slot(a, *block),
                send_sem=send_sems.at[7 * a + k], recv_sem=recv_sems.at[7 * a + k], device_id=to, device_id_type=MESH_ID)

        mine = [pltpu.make_async_copy(x_refs[a], slot(a, *me), local_sems.at[a]) for a in range(n)]
        for cp in mine:
            cp.start()
        first = []
        for a in range(n):
            first.append(copy(a, 0, me, sibling, src=x_refs[a]))
            first += [copy(a, 1 + j, me, (*chip, mc), src=x_refs[a]) for j, chip in enumerate(chips)]
        for cp in first:
            cp.start()
        passed = []
        for j, chip in enumerate(chips):
            for a in range(n):
                copy(a, 1 + j, (*chip, mc), me).wait_recv()
                cp = copy(a, 4 + j, (*chip, mc), sibling)
                cp.start()
                passed.append(cp)
        for a in range(n):
            copy(a, 0, sibling, me).wait_recv()
            for j, chip in enumerate(chips):
                copy(a, 4 + j, (*chip, 1 - mc), me).wait_recv()
        for cp in first + passed:
            cp.wait_send()
        for cp in mine:
            cp.wait()

    return pl.pallas_call(
        body, name=name, out_shape=[jax.ShapeDtypeStruct((N_DEV,) + x.shape, x.dtype) for x in xs],
        in_specs=[ANY] * n, out_specs=[ANY] * n,
        scratch_shapes=[pltpu.SemaphoreType.DMA((7 * n,)), pltpu.SemaphoreType.DMA((7 * n,)), pltpu.SemaphoreType.DMA((n,))],
    )(*xs)


HBM_SPEC = pl.BlockSpec(memory_space=pltpu.HBM)
SEM_SPEC = pl.BlockSpec(memory_space=pltpu.SEMAPHORE)
DATAFLOW = pltpu.SideEffectType.DATAFLOW_SIDE_EFFECTING


def _peer(j, mx, my, mc):
    px = 1 - mx if j & 4 else mx
    py = 1 - my if j & 2 else my
    pc = 1 - mc if j & 1 else mc
    return (px, py, pc), 4 * px + 2 * py + pc


def _split_copy(kind, src_ref, land_ref, send_sems, recv_sems, a, j, mx, my, mc):
    dev, peer = _peer(j, mx, my, mc)
    me = 4 * mx + 2 * my + mc
    src = src_ref if kind == "gather" else src_ref.at[peer]
    k = 7 * a + j - 1
    return pltpu.make_async_remote_copy(src_ref=src, dst_ref=land_ref.at[me], send_sem=send_sems.at[k],
                                        recv_sem=recv_sems.at[k], device_id=dev, device_id_type=MESH_ID), peer


def exchange_start(kind, srcs, name):
    n = len(srcs)
    blocks = [s.shape if kind == "gather" else s.shape[1:] for s in srcs]
    lands = [lax.empty((N_DEV,) + tuple(b), s.dtype) for b, s in zip(blocks, srcs)]

    def body(*refs):
        src_refs, land_refs = refs[:n], refs[n:2 * n]
        send_sems, recv_sems = refs[2 * n], refs[2 * n + 1]
        token = refs[-1]
        mx, my, mc = lax.axis_index("x"), lax.axis_index("y"), lax.axis_index("c")
        for j in range(1, N_DEV):
            for a in range(n):
                _split_copy(kind, src_refs[a], land_refs[a], send_sems, recv_sems, a, j, mx, my, mc)[0].start()
        token[...] = jnp.zeros_like(token)

    hbm = lambda v: pltpu.HBM(v.shape, v.dtype)
    outs = pl.pallas_call(
        body, name=name,
        out_shape=(pltpu.SemaphoreType.DMA((7 * n,)), pltpu.SemaphoreType.DMA((7 * n,)), *[hbm(s) for s in srcs],
                   *[hbm(l) for l in lands], jax.ShapeDtypeStruct((SUBLANE, LANE), F32)),
        in_specs=[HBM_SPEC] * (2 * n),
        out_specs=(SEM_SPEC, SEM_SPEC, *[HBM_SPEC] * (2 * n), pl.BlockSpec(memory_space=pltpu.VMEM)),
        input_output_aliases={a: 2 + a for a in range(2 * n)},
        compiler_params=pltpu.CompilerParams(has_side_effects=DATAFLOW),
    )(*[pltpu.with_memory_space_constraint(v, pltpu.HBM) for v in list(srcs) + lands])
    return dict(kind=kind, n=n, send=outs[0], recv=outs[1], srcs=outs[2:2 + n], lands=outs[2 + n:2 + 2 * n], token=outs[-1])


def exchange_wait(h, after, name):
    n, kind = h["n"], h["kind"]

    def body(*refs):
        src_refs, land_refs = refs[:n], refs[n:2 * n]
        send_sems, recv_sems = refs[2 * n], refs[2 * n + 1]
        mx, my, mc = lax.axis_index("x"), lax.axis_index("y"), lax.axis_index("c")
        for j in range(1, N_DEV):
            for a in range(n):
                cp, peer = _split_copy(kind, src_refs[a], land_refs[a], send_sems, recv_sems, a, j, mx, my, mc)
                cp.wait_send()
                pltpu.make_async_remote_copy(
                    src_ref=land_refs[a].at[peer], dst_ref=land_refs[a].at[peer], send_sem=send_sems.at[7 * a + j - 1],
                    recv_sem=recv_sems.at[7 * a + j - 1], device_id=_peer(j, mx, my, mc)[0], device_id_type=MESH_ID).wait_recv()

    hbm = lambda v: pltpu.HBM(v.shape, v.dtype)
    outs = pl.pallas_call(
        body, name=name,
        out_shape=tuple(hbm(v) for v in list(h["srcs"]) + list(h["lands"])),
        in_specs=[HBM_SPEC] * (2 * n) + [SEM_SPEC, SEM_SPEC, ANY],
        out_specs=tuple([HBM_SPEC] * (2 * n)),
        input_output_aliases={a: a for a in range(2 * n)},
        compiler_params=pltpu.CompilerParams(has_side_effects=DATAFLOW),
    )(*h["srcs"], *h["lands"], h["send"], h["recv"], after)
    return list(outs[n:])


PACK_COLS = 1024
PACK_SEG = 16 * PACK_COLS

BIG = {"ssm_in_proj": 2, "ssm_out_proj": 1, "w_kv": 1, "w_q": 1, "w_o": 1, "ffn_up": 2, "ffn_down": 1,
       "ple_gate": 1, "ple_proj": 2}
SMALL = {"ssm_conv_w": 2, "ssm_conv_b": 1, "ssm_norm": 1, "ffn_conv_w": 2}
REPL = ["attn_norm", "ffn_norm", "ple_norm", "ssm_dt_bias", "ssm_a_log", "ssm_d", "ffn_conv_b", "kv_norm", "final_norm"]


def _seg(n):
    return -(-n // PACK_SEG) * PACK_SEG


def _pack_rows(flats, lead=()):
    padded = [jnp.pad(f, [(0, 0)] * len(lead) + [(0, _seg(f.shape[-1]) - f.shape[-1])]) for f in flats]
    return jnp.concatenate(padded, axis=-1).reshape(*lead, -1, PACK_COLS)


def _unpack_rows(buf, sizes, lead):
    flat = buf.reshape(*lead, -1)
    out, off = [], 0
    for n in sizes:
        out.append(lax.slice_in_dim(flat, off, off + n, axis=len(lead)))
        off += _seg(n)
    return out


def _to_slots(full, axis):
    shp = full.shape
    blk = shp[axis] // N_DEV
    t = full.reshape(shp[:axis] + (N_DEV, blk) + shp[axis + 1:])
    return jnp.moveaxis(t, axis, 0).reshape(N_DEV, -1)


def _from_slots(slots, local_shape, axis):
    t = jnp.moveaxis(slots.reshape((N_DEV,) + tuple(local_shape)), 0, axis)
    shp = list(local_shape)
    shp[axis] *= N_DEV
    return t.reshape(shp)


def _as2d(a):
    return a.reshape(1, -1) if a.ndim == 1 else a.reshape(-1, a.shape[-1])


def _pad_rows(w, rows=SUBLANE):
    return jnp.pad(w, ((0, rows - w.shape[0]), (0, 0)))


def kernel(x, p, attn_norm, ffn_norm, ple_norm, ssm_in_proj, ssm_conv_w, ssm_conv_b, ssm_dt_bias, ssm_a_log, ssm_d, ssm_norm, ssm_out_proj, kv_norm, w_kv, w_q, w_o, ffn_up, ffn_conv_w, ffn_conv_b, ffn_down, ple_gate, ple_proj, final_norm, loss_target, m_attn_norm, m_ffn_norm, m_ple_norm, m_ssm_in_proj, m_ssm_conv_w, m_ssm_conv_b, m_ssm_dt_bias, m_ssm_a_log, m_ssm_d, m_ssm_norm, m_ssm_out_proj, m_kv_norm, m_w_kv, m_w_q, m_w_o, m_ffn_up, m_ffn_conv_w, m_ffn_conv_b, m_ffn_down, m_ple_gate, m_ple_proj, m_final_norm, v_attn_norm, v_ffn_norm, v_ple_norm, v_ssm_in_proj, v_ssm_conv_w, v_ssm_conv_b, v_ssm_dt_bias, v_ssm_a_log, v_ssm_d, v_ssm_norm, v_ssm_out_proj, v_kv_norm, v_w_kv, v_w_q, v_w_o, v_ffn_up, v_ffn_conv_w, v_ffn_conv_b, v_ffn_down, v_ple_gate, v_ple_proj, v_final_norm):
    given = dict(locals())
    wnames = list(BIG) + list(SMALL) + REPL
    bsz, seq, d = x.shape
    t = bsz * seq
    bs = lambda a: a.reshape(bsz, seq, a.shape[-1])
    fl = lambda a: a.reshape(t, a.shape[-1])

    big_local = [given[n] for n in BIG]
    small_local = [given[n] for n in SMALL]
    me = 4 * lax.axis_index("x") + 2 * lax.axis_index("y") + lax.axis_index("c")

    N_GROUPS = 2 * DEPTH

    def layer_items(k):
        i, j = k // 2, k // 2 - N_A
        if k % 2 == 0:
            return [("ssm_in_proj", i), ("ssm_out_proj", i)] if i < N_A else [("w_q", j), ("w_o", j)]
        shared = [("w_kv", None)] if i == N_A - 1 else []
        return [("ffn_up", i), ("ffn_down", i), ("ple_gate", i), ("ple_proj", i)] + shared

    def local_block(n, idx):
        return given[n] if idx is None else given[n][idx]

    def put_own(land, own):
        return lax.dynamic_update_slice(land, own[None], (me,) + (0,) * own.ndim)

    small_pack = _pack_rows([w.reshape(-1) for w in small_local])
    gather_src, gather_h = [], []
    for i in range(N_GROUPS):
        src = [local_block(n, idx).astype(BF16) for n, idx in layer_items(i)] + ([small_pack] if i == 0 else [])
        gather_src.append(src)
        gather_h.append(exchange_start("gather", src, f"ag_start{i}"))
    full = {n: {} for n in BIG}
    rows_of = lambda g: g.reshape(-1, g.shape[-1])
    cols_of = lambda g: jnp.transpose(g, (1, 0, 2)).reshape(g.shape[1], -1)

    def fetch_layer(i, after):
        lands = exchange_wait(gather_h[i], after, f"ag_wait{i}")
        got = [put_own(l, s) for l, s in zip(lands, gather_src[i])]
        for (n, idx), g in zip(layer_items(i), got):
            if n == "ffn_up":
                full[n][idx] = (cols_of(g[:N_DEV // 2]), cols_of(g[N_DEV // 2:]))
            else:
                full[n][idx] = cols_of(g) if BIG[n] == given[n].ndim - 1 else rows_of(g)
        if i == 0:
            for (n, ax), w, s in zip(SMALL.items(), small_local, _unpack_rows(got[-1], [w.size for w in small_local], (N_DEV,))):
                full[n] = _from_slots(s, w.shape, ax)

    row = lambda v: v.reshape(1, -1)
    pad_lane = lambda v: jnp.pad(v.reshape(1, -1), ((0, 0), (0, LANE - v.size)))
    h = x.reshape(t, d)
    saved = []
    kv = hnkv = h_kv = None
    for i in range(DEPTH):
        fetch_layer(2 * i, h)
        sv = {"h0": h}
        gain = row(attn_norm[i])
        if i == 0:
            gain = gain + sum(hd["token"][0, 0] for hd in gather_h)
        hn = rmsnorm_fwd(h, gain, f"attn_norm_f{i}")
        sv["hn"] = hn
        if i < N_A:
            w_in = full["ssm_in_proj"][i]
            wz, wxbc = w_in[:, :SSM_D_INNER], w_in[:, SSM_D_INNER:SSM_ZX]
            wdt = jnp.pad(w_in[:, SSM_ZX:], ((0, 0), (0, LANE - SSM_HEADS)))
            z = matmul(hn, wz, "nn", F32, f"ssm_z_f{i}")
            xbc = bs(matmul(hn, wxbc, "nn", F32, f"ssm_xbc_f{i}"))
            dtr = bs(matmul(hn, wdt, "nn", F32, f"ssm_dt_f{i}"))
            cw = _pad_rows(full["ssm_conv_w"][i])
            cb = row(full["ssm_conv_b"][i])
            xc = ssm_conv_fwd(xbc, cw, cb, f"ssm_conv_f{i}")
            dtb, alog = pad_lane(ssm_dt_bias[i]), pad_lane(ssm_a_log[i])
            dexp = jnp.repeat(ssm_d[i], SSM_HEAD_DIM).reshape(1, -1)
            y, sprev = ssd_fwd(xc, dtr, dtb, alog, dexp, f"ssd_f{i}")
            gn = gated_norm_fwd(fl(y), z, row(full["ssm_norm"][i]), f"ssm_gnorm_f{i}")
            h1 = matmul(gn, full["ssm_out_proj"][i], "nn", F32, f"ssm_out_f{i}", add=h)
            sv.update(wz=wz, wxbc=wxbc, wdt=wdt, z=z, xbc=xbc, dtr=dtr, cw=cw, cb=cb, xc=xc, dtb=dtb, alog=alog,
                      dexp=dexp, y=y, sprev=sprev, gn=gn)
        else:
            j = i - N_A
            q = bs(matmul(hn, full["w_q"][j], "nn", BF16, f"sb_q_f{j}"))
            o, ltot = sb_attention_fwd(q, kv, f"sb_attn_f{j}")
            h1 = matmul(fl(o), full["w_o"][j], "nn", F32, f"sb_o_f{j}", add=h)
            sv.update(q=q, o=o, ltot=ltot)
        sv["h1"] = h1
        fetch_layer(2 * i + 1, h1)
        hn2 = rmsnorm_fwd(h1, row(ffn_norm[i]), f"ffn_norm_f{i}")
        wug, wuv = full["ffn_up"][i]
        ug = bs(matmul(hn2, wug, "nn", F32, f"ffn_upg_f{i}"))
        uv = bs(matmul(hn2, wuv, "nn", F32, f"ffn_upv_f{i}"))
        fcw = full["ffn_conv_w"][i]
        fwg, fwv = _pad_rows(fcw[:, :D_FF]), _pad_rows(fcw[:, D_FF:])
        fbg, fbv = row(ffn_conv_b[i, :D_FF]), row(ffn_conv_b[i, D_FF:])
        act = ffn_conv_fwd(ug, uv, fwg, fwv, fbg, fbv, f"ffn_conv_f{i}")
        h2 = matmul(fl(act), full["ffn_down"][i], "nn", F32, f"ffn_down_f{i}", add=h1)
        hn3 = rmsnorm_fwd(h2, row(ple_norm[i]), f"ple_norm_f{i}")
        gpre = matmul(hn3, full["ple_gate"][i], "nn", F32, f"ple_gate_f{i}")
        p_i = p[i].reshape(t, PLE_DIM)
        pp = matmul(p_i, full["ple_proj"][i], "nn", F32, f"ple_proj_f{i}")
        h3 = ple_fwd(h2, gpre, pp, f"ple_f{i}")
        sv.update(hn2=hn2, wug=wug, wuv=wuv, ug=ug, uv=uv, fwg=fwg, fwv=fwv, fbg=fbg, fbv=fbv, act=act, h2=h2,
                  hn3=hn3, gpre=gpre, pp=pp, p_i=p_i)
        if i == N_A - 1:
            h_kv = h3
            hnkv = rmsnorm_fwd(h3, row(kv_norm), "kv_norm_f")
            kv = bs(matmul(hnkv, full["w_kv"][None], "nn", BF16, "kv_f"))
        h = h3
        saved.append(sv)

    loss_row, dh, g_final = loss_head(h, row(final_norm), loss_target.reshape(t, d), "loss_head")
    loss = lax.psum(loss_row[0, 0], ("x", "y", "c"))

    G = {n: [None] * given[n].shape[0] for n in wnames if given[n].ndim > 1 and n not in ("w_kv",)}
    G["final_norm"] = g_final[0]

    def col_slots(g, ndev=N_DEV):
        return jnp.transpose(g.reshape(g.shape[0], ndev, -1), (1, 0, 2))

    def grad_slots(n, idx):
        g = G[n] if idx is None else G[n][idx]
        if n == "ffn_up":
            s = jnp.concatenate([col_slots(g[0], N_DEV // 2), col_slots(g[1], N_DEV // 2)], axis=0)
        elif BIG[n] == given[n].ndim - 1:
            s = col_slots(g)
        else:
            s = g.reshape(N_DEV, -1, g.shape[-1])
        return s.astype(BF16)

    scatter_src, scatter_h = {}, {}
    dk = jnp.zeros((bsz, seq, SB_WIDTH), F32)
    dv = jnp.zeros((bsz, seq, SB_WIDTH), F32)
    for i in reversed(range(DEPTH)):
        sv = saved[i]
        if i == N_A - 1:
            wk, wv = full["w_kv"][None][:, :SB_WIDTH], full["w_kv"][None][:, SB_WIDTH:]
            dkf, dvf = fl(dk), fl(dv)
            dhn = matmul(dkf, wk, "nt", F32, "kv_dx_k")
            dhn = matmul(dvf, wv, "nt", F32, "kv_dx_v", add=dhn)
            G["w_kv"] = jnp.concatenate([matmul(hnkv, dkf, "tn", F32, "kv_dw_k"), matmul(hnkv, dvf, "tn", F32, "kv_dw_v")], axis=1)
            dh, gk = rmsnorm_bwd(dhn, h_kv, row(kv_norm), dh, "kv_norm_b")
            G["kv_norm"] = gk[0]
        dgpre, dpp = ple_bwd(dh, sv["gpre"], sv["pp"], f"ple_b{i}")
        G["ple_proj"][i] = matmul(sv["p_i"], dpp, "tn", F32, f"ple_proj_dw{i}")
        G["ple_gate"][i] = matmul(sv["hn3"], dgpre, "tn", F32, f"ple_gate_dw{i}")
        dhn = matmul(dgpre, full["ple_gate"][i], "nt", F32, f"ple_gate_dx{i}")
        gain = row(ple_norm[i])
        if 2 * i + 2 in scatter_h:
            gain = gain + scatter_h[2 * i + 2]["token"][0, 0]
        dh, gn_ = rmsnorm_bwd(dhn, sv["h2"], gain, dh, f"ple_norm_b{i}")
        G["ple_norm"][i] = gn_[0]
        dact = bs(matmul(dh, full["ffn_down"][i], "nt", F32, f"ffn_down_dx{i}"))
        G["ffn_down"][i] = matmul(fl(sv["act"]), dh, "tn", F32, f"ffn_down_dw{i}")
        dug, duv, dwg, dwv, dbg, dbv = ffn_conv_bwd(sv["ug"], sv["uv"], dact, sv["fwg"], sv["fwv"], sv["fbg"], sv["fbv"], f"ffn_conv_b{i}")
        dug, duv = fl(dug), fl(duv)
        G["ffn_conv_w"][i] = jnp.concatenate([dwg[:FFN_CONV], dwv[:FFN_CONV]], axis=1)
        G["ffn_conv_b"][i] = jnp.concatenate([dbg[0], dbv[0]])
        G["ffn_up"][i] = (matmul(sv["hn2"], dug, "tn", F32, f"ffn_upg_dw{i}"), matmul(sv["hn2"], duv, "tn", F32, f"ffn_upv_dw{i}"))
        dhn = matmul(dug, sv["wug"], "nt", F32, f"ffn_upg_dx{i}")
        dhn = matmul(duv, sv["wuv"], "nt", F32, f"ffn_upv_dx{i}", add=dhn)
        dh, gn_ = rmsnorm_bwd(dhn, sv["h1"], row(ffn_norm[i]), dh, f"ffn_norm_b{i}")
        G["ffn_norm"][i] = gn_[0]
        scatter_src[2 * i + 1] = [grad_slots(n, idx) for n, idx in layer_items(2 * i + 1)]
        scatter_h[2 * i + 1] = exchange_start("scatter", scatter_src[2 * i + 1], f"a2a_start{2 * i + 1}")
        tok = scatter_h[2 * i + 1]["token"][0, 0]
        if i < N_A:
            dgn = matmul(dh, full["ssm_out_proj"][i], "nt", F32, f"ssm_out_dx{i}")
            G["ssm_out_proj"][i] = matmul(sv["gn"], dh, "tn", F32, f"ssm_out_dw{i}")
            dy, dz, dnw = gated_norm_bwd(dgn, fl(sv["y"]), sv["z"], row(full["ssm_norm"][i]) + tok, f"ssm_gnorm_b{i}")
            G["ssm_norm"][i] = dnw[0]
            dxs, dbm, dcm, ddtr4, da4, dbias4, ddexp = ssd_bwd(sv["xc"], sv["dtr"], sv["dtb"], sv["alog"], sv["dexp"], bs(dy), sv["sprev"], f"ssd_b{i}")
            G["ssm_a_log"][i] = da4[:, 0, :HPG].reshape(-1)
            G["ssm_dt_bias"][i] = dbias4[:, 0, :HPG].reshape(-1)
            G["ssm_d"][i] = ddexp.reshape(SSM_HEADS, SSM_HEAD_DIM).sum(axis=-1)
            dxc = jnp.concatenate([dxs, dbm, dcm], axis=-1)
            dxbc, dcw, dcb = ssm_conv_bwd(sv["xbc"], dxc, sv["cw"], sv["cb"], f"ssm_conv_b{i}")
            dxbc = fl(dxbc)
            G["ssm_conv_w"][i] = dcw[:SSM_CONV]
            G["ssm_conv_b"][i] = dcb[0]
            ddtr = ddtr4.reshape(t, SSM_GROUPS, LANE)[:, :, :HPG].reshape(t, SSM_HEADS)
            ddtr = jnp.pad(ddtr, ((0, 0), (0, LANE - SSM_HEADS))).astype(BF16)
            hn = sv["hn"]
            G["ssm_in_proj"][i] = jnp.concatenate([
                matmul(hn, dz, "tn", F32, f"ssm_z_dw{i}"), matmul(hn, dxbc, "tn", F32, f"ssm_xbc_dw{i}"),
                matmul(hn, ddtr, "tn", F32, f"ssm_dt_dw{i}")[:, :SSM_HEADS]], axis=1)
            dhn = matmul(dz, sv["wz"], "nt", F32, f"ssm_z_dx{i}")
            dhn = matmul(dxbc, sv["wxbc"], "nt", F32, f"ssm_xbc_dx{i}", add=dhn)
            dhn = matmul(ddtr, sv["wdt"], "nt", F32, f"ssm_dt_dx{i}", add=dhn)
        else:
            j = i - N_A
            do = bs(matmul(dh, full["w_o"][j], "nt", BF16, f"sb_o_dx{j}"))
            G["w_o"][j] = matmul(fl(sv["o"]), dh, "tn", F32, f"sb_o_dw{j}")
            dq, dk, dv = sb_attention_bwd(sv["q"], kv, sv["ltot"], do, dk, dv, f"sb_attn_b{j}")
            dq = fl(dq)
            G["w_q"][j] = matmul(sv["hn"], dq, "tn", F32, f"sb_q_dw{j}")
            dhn = matmul(dq, full["w_q"][j], "nt", F32, f"sb_q_dx{j}")
        dh, gn_ = rmsnorm_bwd(dhn, sv["h0"], row(attn_norm[i]) + (0.0 if i < N_A else tok), dh, f"attn_norm_b{i}")
        G["attn_norm"][i] = gn_[0]
        src = [grad_slots(n, idx) for n, idx in layer_items(2 * i)]
        if i == 0:
            small_g = {n: jnp.stack(G[n]) for n in SMALL}
            src.append(_pack_rows([_to_slots(small_g[n], SMALL[n]) for n in SMALL], (N_DEV,)))
        scatter_src[2 * i] = src
        scatter_h[2 * i] = exchange_start("scatter", src, f"a2a_start{2 * i}")
    grad_x = dh.reshape(bsz, seq, d)
    grads = {n: jnp.stack(G[n]) if isinstance(G[n], list) else G[n] for n in REPL}

    layer_parts = {n: {} for n in BIG}
    small_parts = None
    for i in reversed(range(N_GROUPS)):
        lands = exchange_wait(scatter_h[i], dh, f"a2a_wait{i}")
        got = [put_own(l, lax.dynamic_index_in_dim(s, me, 0, keepdims=False)) for l, s in zip(lands, scatter_src[i])]
        for (n, idx), g in zip(layer_items(i), got):
            layer_parts[n][idx] = g
        if i == 0:
            small_parts = _unpack_rows(got[-1], [given[n].size for n in SMALL], (N_DEV,))
    big_parts = [jnp.concatenate([layer_parts[n][k] for k in sorted(layer_parts[n], key=lambda v: -1 if v is None else v)], axis=1)
                 for n in BIG]
    parts_rp = all_gather([_pack_rows([grads[n].reshape(-1) for n in REPL])], "ag_repl_grads")[0]
    sh_names = list(BIG) + list(SMALL)
    sh_parts = big_parts + small_parts
    rp_parts = _unpack_rows(parts_rp, [given[n].size for n in REPL], (N_DEV,))
    res = {}
    for n, pr in list(zip(sh_names, sh_parts)) + list(zip(REPL, rp_parts)):
        w2 = _as2d(given[n])
        outs = adamw(w2, pr.reshape((N_DEV,) + w2.shape), _as2d(given["m_" + n]), _as2d(given["v_" + n]), f"adamw_{n}")
        res[n] = [o.reshape(given[n].shape) for o in outs]
    order = ["attn_norm", "ffn_norm", "ple_norm", "ssm_in_proj", "ssm_conv_w", "ssm_conv_b", "ssm_dt_bias", "ssm_a_log",
             "ssm_d", "ssm_norm", "ssm_out_proj", "kv_norm", "w_kv", "w_q", "w_o", "ffn_up", "ffn_conv_w", "ffn_conv_b",
             "ffn_down", "ple_gate", "ple_proj", "final_norm"]
    return (loss, grad_x, *[res[n][0] for n in order], *[res[n][1] for n in order],
            *[res[n][2] for n in order], *[res[n][3] for n in order])
```

```python
import functools
import math

import jax
import jax.numpy as jnp
from jax import lax
from jax.experimental import pallas as pl
from jax.experimental.pallas import tpu as pltpu

F32 = jnp.float32
BF16 = jnp.bfloat16

N_DEV = 8
D_MODEL = 1024
SEQ = 2048
DEPTH = 4
N_A = 2
N_B = 2
SSM_D_INNER = 2048
SSM_HEAD_DIM = 64
SSM_HEADS = 32
SSM_GROUPS = 4
SSM_STATE = 128
SSM_CONV = 4
SSM_CHUNK = 128
SSM_CONV_DIM = 3072
SSM_ZX = 5120
SSM_IN_DIM = 5152
SB_HEADS = 16
SB_HEAD_DIM = 64
SB_WIDTH = 1024
D_FF = 2816
FFN_CONV = 3
PLE_DIM = 256
NORM_EPS = 1e-6
SSM_NORM_EPS = 1e-5

ADAM_LR = 0.001
ADAM_B1 = 0.9
ADAM_B2 = 0.999
ADAM_EPS = 1e-08
ADAM_WD = 0.01
ADAM_STEP = 10

LANE = 128
SUBLANE = 8
VMEM_LIMIT = 48 * 1024 * 1024

NN = (((1,), (0,)), ((), ()))
NT = (((1,), (1,)), ((), ()))
TN = (((0,), (0,)), ((), ()))


def _params(sem):
    return pltpu.CompilerParams(dimension_semantics=sem, vmem_limit_bytes=VMEM_LIMIT)


def _pick(n, prefs):
    for p in prefs:
        if n % p == 0:
            return p
    return n


def matmul(a, b, mode, out_dtype, name, add=None):
    if mode == "nn":
        (m, k), (k2, n) = a.shape, b.shape
    elif mode == "nt":
        (m, k), (n, k2) = a.shape, b.shape
    else:
        (k, m), (k2, n) = a.shape, b.shape
    assert k == k2, (a.shape, b.shape, mode)
    tm = _pick(m, (512, 256, 128))
    tn = _pick(n, (1408, 1024, 512, 256, 128))
    tk = _pick(k, (2816, 2048, 1024, 512, 256, 128))
    nk = k // tk
    dims = {"nn": NN, "nt": NT, "tn": TN}[mode]

    def body(*refs):
        a_ref, b_ref = refs[:2]
        add_ref = refs[2] if add is not None else None
        o_ref = refs[-2] if nk > 1 else refs[-1]
        part = lax.dot_general(a_ref[...].astype(BF16), b_ref[...].astype(BF16), dims, preferred_element_type=F32)

        def finish(r):
            if add_ref is not None:
                r = r + add_ref[...].astype(F32)
            o_ref[...] = r.astype(o_ref.dtype)

        if nk == 1:
            finish(part)
            return
        acc_ref = refs[-1]
        kk = pl.program_id(2)

        @pl.when(kk == 0)
        def _():
            acc_ref[...] = part

        @pl.when((kk > 0) & (kk < nk - 1))
        def _():
            acc_ref[...] += part

        @pl.when(kk == nk - 1)
        def _():
            finish(acc_ref[...] + part)

    if mode == "nn":
        a_spec = pl.BlockSpec((tm, tk), lambda i, j, kk: (i, kk))
        b_spec = pl.BlockSpec((tk, tn), lambda i, j, kk: (kk, j))
    elif mode == "nt":
        a_spec = pl.BlockSpec((tm, tk), lambda i, j, kk: (i, kk))
        b_spec = pl.BlockSpec((tn, tk), lambda i, j, kk: (j, kk))
    else:
        a_spec = pl.BlockSpec((tk, tm), lambda i, j, kk: (kk, i))
        b_spec = pl.BlockSpec((tk, tn), lambda i, j, kk: (kk, j))
    o_spec = pl.BlockSpec((tm, tn), lambda i, j, kk: (i, j))
    in_specs = [a_spec, b_spec] + ([o_spec] if add is not None else [])
    args = (a, b) + ((add,) if add is not None else ())
    return pl.pallas_call(
        body,
        name=name,
        grid=(m // tm, n // tn, nk),
        in_specs=in_specs,
        out_specs=o_spec,
        out_shape=jax.ShapeDtypeStruct((m, n), out_dtype),
        scratch_shapes=[pltpu.VMEM((tm, tn), F32)] if nk > 1 else [],
        compiler_params=_params(("parallel", "parallel", "arbitrary")),
    )(*args)


ROW_TILE = 512


def rmsnorm_fwd(x, gain, name):
    t, d = x.shape

    def body(x_ref, g_ref, o_ref):
        xv = x_ref[...]
        r = lax.rsqrt(jnp.mean(xv * xv, axis=-1, keepdims=True) + NORM_EPS)
        o_ref[...] = (xv * r * g_ref[...]).astype(o_ref.dtype)

    return pl.pallas_call(
        body,
        name=name,
        grid=(t // ROW_TILE,),
        in_specs=[pl.BlockSpec((ROW_TILE, d), lambda i: (i, 0)), pl.BlockSpec((1, d), lambda i: (0, 0))],
        out_specs=pl.BlockSpec((ROW_TILE, d), lambda i: (i, 0)),
        out_shape=jax.ShapeDtypeStruct((t, d), BF16),
        compiler_params=_params(("parallel",)),
    )(x, gain)


def rmsnorm_bwd(dy, x, gain, dres, name):
    t, d = x.shape

    def body(dy_ref, x_ref, g_ref, dres_ref, dx_ref, dg_ref):
        i = pl.program_id(0)
        xv = x_ref[...]
        r = lax.rsqrt(jnp.mean(xv * xv, axis=-1, keepdims=True) + NORM_EPS)
        xh = xv * r
        dyv = dy_ref[...].astype(F32)
        dxh = dyv * g_ref[...]
        dx = r * (dxh - xh * jnp.mean(dxh * xh, axis=-1, keepdims=True))
        dx_ref[...] = dres_ref[...] + dx

        @pl.when(i == 0)
        def _():
            dg_ref[...] = jnp.zeros_like(dg_ref)

        dg_ref[...] += jnp.sum(dyv * xh, axis=0, keepdims=True)

    row = pl.BlockSpec((ROW_TILE, d), lambda i: (i, 0))
    vec = pl.BlockSpec((1, d), lambda i: (0, 0))
    return pl.pallas_call(
        body,
        name=name,
        grid=(t // ROW_TILE,),
        in_specs=[row, row, vec, row],
        out_specs=[row, vec],
        out_shape=[jax.ShapeDtypeStruct((t, d), F32), jax.ShapeDtypeStruct((1, d), F32)],
        compiler_params=_params(("arbitrary",)),
    )(dy, x, gain, dres)


CONV_ROWS = 512
HALO = SUBLANE


def _conv_apply(ext, w_ref, bias, k, lo, n):
    acc = bias + w_ref[k - 1:k, :] * ext[lo:lo + n]
    for j in range(1, k):
        acc = acc + w_ref[k - 1 - j:k - j, :] * pltpu.roll(ext, j, 0)[lo:lo + n]
    return acc


def _conv_apply_t(ext, w_ref, k, n):
    rows = ext.shape[0]
    acc = w_ref[k - 1:k, :] * ext[:n]
    for j in range(1, k):
        acc = acc + w_ref[k - 1 - j:k - j, :] * pltpu.roll(ext, rows - j, 0)[:n]
    return acc


def _conv_dw(dpre_main, ext_u, k, n, dw_ref):
    for kk in range(k):
        j = k - 1 - kk
        sh = ext_u[HALO:HALO + n] if j == 0 else pltpu.roll(ext_u, j, 0)[HALO:HALO + n]
        dw_ref[kk:kk + 1, :] += jnp.sum(dpre_main * sh, axis=0, keepdims=True)


def _silu_grad(pre):
    sg = jax.nn.sigmoid(pre)
    return sg * (1.0 + pre * (1.0 - sg))


def _conv_blockspecs(seq, r, tc):
    nh = r // HALO
    last = seq // HALO - 1
    main = pl.BlockSpec((None, r, tc), lambda c, b, s: (b, s, c))
    before = pl.BlockSpec((None, HALO, tc), lambda c, b, s: (b, jnp.maximum(s * nh - 1, 0), c))
    after = pl.BlockSpec((None, HALO, tc), lambda c, b, s: (b, jnp.minimum((s + 1) * nh, last), c))
    return main, before, after


def ssm_conv_fwd(u, w, bias, name):
    bsz, seq, ch = u.shape
    r, tc, k = min(CONV_ROWS, seq), 512, SSM_CONV
    main, before, _ = _conv_blockspecs(seq, r, tc)
    wspec = pl.BlockSpec((SUBLANE, tc), lambda c, b, s: (0, c))
    bspec = pl.BlockSpec((1, tc), lambda c, b, s: (0, c))

    def body(u_ref, hb_ref, w_ref, b_ref, o_ref):
        s = pl.program_id(2)
        hb = jnp.where(s == 0, 0.0, hb_ref[...])
        ext = jnp.concatenate([hb, u_ref[...]], axis=0)
        pre = _conv_apply(ext, w_ref, b_ref[...], k, HALO, r)
        o_ref[...] = pre * jax.nn.sigmoid(pre)

    return pl.pallas_call(
        body, name=name, grid=(ch // tc, bsz, seq // r),
        in_specs=[main, before, wspec, bspec], out_specs=main,
        out_shape=jax.ShapeDtypeStruct(u.shape, F32),
        compiler_params=_params(("parallel", "parallel", "parallel")),
    )(u, u, w, bias)


def ssm_conv_bwd(u, dxc, w, bias, name):
    bsz, seq, ch = u.shape
    r, tc, k = min(CONV_ROWS, seq), 512, SSM_CONV
    ns = seq // r
    main, before, after = _conv_blockspecs(seq, r, tc)
    wspec = pl.BlockSpec((SUBLANE, tc), lambda c, b, s: (0, c))
    bspec = pl.BlockSpec((1, tc), lambda c, b, s: (0, c))

    def body(u_ref, hb_ref, ha_ref, d_ref, da_ref, w_ref, b_ref, du_ref, dw_ref, db_ref):
        b, s = pl.program_id(1), pl.program_id(2)
        hb = jnp.where(s == 0, 0.0, hb_ref[...])
        ext_u = jnp.concatenate([hb, u_ref[...], ha_ref[...]], axis=0)
        pre = _conv_apply(ext_u, w_ref, b_ref[...], k, HALO, r + HALO)
        dxe = jnp.concatenate([d_ref[...], jnp.where(s == ns - 1, 0.0, da_ref[...])], axis=0)
        dpre = dxe * _silu_grad(pre)
        du_ref[...] = _conv_apply_t(dpre, w_ref, k, r).astype(du_ref.dtype)

        @pl.when((b == 0) & (s == 0))
        def _():
            dw_ref[...] = jnp.zeros_like(dw_ref)
            db_ref[...] = jnp.zeros_like(db_ref)

        dpm = dpre[:r]
        _conv_dw(dpm, ext_u, k, r, dw_ref)
        db_ref[...] += jnp.sum(dpm, axis=0, keepdims=True)

    return pl.pallas_call(
        body, name=name, grid=(ch // tc, bsz, ns),
        in_specs=[main, before, after, main, after, wspec, bspec],
        out_specs=[main, wspec, bspec],
        out_shape=[jax.ShapeDtypeStruct(u.shape, BF16), jax.ShapeDtypeStruct((SUBLANE, ch), F32),
                   jax.ShapeDtypeStruct((1, ch), F32)],
        compiler_params=_params(("arbitrary", "arbitrary", "arbitrary")),
    )(u, u, u, dxc, dxc, w, bias)


def ffn_conv_fwd(ug, uv, wg, wv, bg, bv, name):
    bsz, seq, ch = ug.shape
    r, tc, k = min(CONV_ROWS, seq), 256, FFN_CONV
    main, before, _ = _conv_blockspecs(seq, r, tc)
    wspec = pl.BlockSpec((SUBLANE, tc), lambda c, b, s: (0, c))
    bspec = pl.BlockSpec((1, tc), lambda c, b, s: (0, c))

    def body(ug_ref, hg_ref, uv_ref, hv_ref, wg_ref, wv_ref, bg_ref, bv_ref, o_ref):
        s = pl.program_id(2)
        eg = jnp.concatenate([jnp.where(s == 0, 0.0, hg_ref[...]), ug_ref[...]], axis=0)
        ev = jnp.concatenate([jnp.where(s == 0, 0.0, hv_ref[...]), uv_ref[...]], axis=0)
        pg = _conv_apply(eg, wg_ref, bg_ref[...], k, HALO, r)
        pv = _conv_apply(ev, wv_ref, bv_ref[...], k, HALO, r)
        o_ref[...] = (pg * jax.nn.sigmoid(pg) * pv).astype(o_ref.dtype)

    return pl.pallas_call(
        body, name=name, grid=(ch // tc, bsz, seq // r),
        in_specs=[main, before, main, before, wspec, wspec, bspec, bspec], out_specs=main,
        out_shape=jax.ShapeDtypeStruct(ug.shape, BF16),
        compiler_params=_params(("parallel", "parallel", "parallel")),
    )(ug, ug, uv, uv, wg, wv, bg, bv)


def ffn_conv_bwd(ug, uv, dact, wg, wv, bg, bv, name):
    bsz, seq, ch = ug.shape
    r, tc, k = min(CONV_ROWS, seq), 256, FFN_CONV
    ns = seq // r
    main, before, after = _conv_blockspecs(seq, r, tc)
    wspec = pl.BlockSpec((SUBLANE, tc), lambda c, b, s: (0, c))
    bspec = pl.BlockSpec((1, tc), lambda c, b, s: (0, c))

    def body(ug_ref, gb_ref, ga_ref, uv_ref, vb_ref, va_ref, d_ref, da_ref, wg_ref, wv_ref, bg_ref, bv_ref,
             dug_ref, duv_ref, dwg_ref, dwv_ref, dbg_ref, dbv_ref):
        b, s = pl.program_id(1), pl.program_id(2)
        eg = jnp.concatenate([jnp.where(s == 0, 0.0, gb_ref[...]), ug_ref[...], ga_ref[...]], axis=0)
        ev = jnp.concatenate([jnp.where(s == 0, 0.0, vb_ref[...]), uv_ref[...], va_ref[...]], axis=0)
        pg = _conv_apply(eg, wg_ref, bg_ref[...], k, HALO, r + HALO)
        pv = _conv_apply(ev, wv_ref, bv_ref[...], k, HALO, r + HALO)
        de = jnp.concatenate([d_ref[...], jnp.where(s == ns - 1, 0.0, da_ref[...])], axis=0)
        sg = jax.nn.sigmoid(pg)
        dpg = de * pv * (sg * (1.0 + pg * (1.0 - sg)))
        dpv = de * (pg * sg)
        dug_ref[...] = _conv_apply_t(dpg, wg_ref, k, r).astype(dug_ref.dtype)
        duv_ref[...] = _conv_apply_t(dpv, wv_ref, k, r).astype(duv_ref.dtype)

        @pl.when((b == 0) & (s == 0))
        def _():
            dwg_ref[...] = jnp.zeros_like(dwg_ref)
            dwv_ref[...] = jnp.zeros_like(dwv_ref)
            dbg_ref[...] = jnp.zeros_like(dbg_ref)
            dbv_ref[...] = jnp.zeros_like(dbv_ref)

        _conv_dw(dpg[:r], eg, k, r, dwg_ref)
        _conv_dw(dpv[:r], ev, k, r, dwv_ref)
        dbg_ref[...] += jnp.sum(dpg[:r], axis=0, keepdims=True)
        dbv_ref[...] += jnp.sum(dpv[:r], axis=0, keepdims=True)

    wshape = jax.ShapeDtypeStruct((SUBLANE, ch), F32)
    bshape = jax.ShapeDtypeStruct((1, ch), F32)
    return pl.pallas_call(
        body, name=name, grid=(ch // tc, bsz, ns),
        in_specs=[main, before, after, main, before, after, main, after, wspec, wspec, bspec, bspec],
        out_specs=[main, main, wspec, wspec, bspec, bspec],
        out_shape=[jax.ShapeDtypeStruct(ug.shape, BF16), jax.ShapeDtypeStruct(ug.shape, BF16), wshape, wshape, bshape, bshape],
        compiler_params=_params(("arbitrary", "arbitrary", "arbitrary")),
    )(ug, ug, ug, uv, uv, uv, dact, dact, wg, wv, bg, bv)


GN_ROWS = 256
GN_GROUP = SSM_D_INNER // SSM_GROUPS


def gated_norm_fwd(y, z, w, name):
    t, d = y.shape

    def body(y_ref, z_ref, w_ref, o_ref):
        for g in range(d // GN_GROUP):
            sl = slice(g * GN_GROUP, (g + 1) * GN_GROUP)
            zv = z_ref[:, sl]
            gv = y_ref[:, sl] * (zv * jax.nn.sigmoid(zv))
            r = lax.rsqrt(jnp.mean(gv * gv, axis=-1, keepdims=True) + SSM_NORM_EPS)
            o_ref[:, sl] = (gv * r * w_ref[:, sl]).astype(o_ref.dtype)

    row = pl.BlockSpec((GN_ROWS, d), lambda i: (i, 0))
    vec = pl.BlockSpec((1, d), lambda i: (0, 0))
    return pl.pallas_call(
        body, name=name, grid=(t // GN_ROWS,), in_specs=[row, row, vec], out_specs=row,
        out_shape=jax.ShapeDtypeStruct((t, d), BF16), compiler_params=_params(("parallel",)),
    )(y, z, w)


def gated_norm_bwd(dgn, y, z, w, name):
    t, d = y.shape

    def body(d_ref, y_ref, z_ref, w_ref, dy_ref, dz_ref, dw_ref):
        i = pl.program_id(0)

        @pl.when(i == 0)
        def _():
            dw_ref[...] = jnp.zeros_like(dw_ref)

        for g in range(d // GN_GROUP):
            sl = slice(g * GN_GROUP, (g + 1) * GN_GROUP)
            zv, yv, dv = z_ref[:, sl], y_ref[:, sl], d_ref[:, sl]
            sg = jax.nn.sigmoid(zv)
            sz = zv * sg
            gv = yv * sz
            r = lax.rsqrt(jnp.mean(gv * gv, axis=-1, keepdims=True) + SSM_NORM_EPS)
            gh = gv * r
            dgh = dv * w_ref[:, sl]
            dw_ref[:, sl] += jnp.sum(dv * gh, axis=0, keepdims=True)
            dg = r * (dgh - gh * jnp.mean(dgh * gh, axis=-1, keepdims=True))
            dy_ref[:, sl] = dg * sz
            dz_ref[:, sl] = (dg * yv * (sg * (1.0 + zv * (1.0 - sg)))).astype(dz_ref.dtype)

    row = pl.BlockSpec((GN_ROWS, d), lambda i: (i, 0))
    vec = pl.BlockSpec((1, d), lambda i: (0, 0))
    return pl.pallas_call(
        body, name=name, grid=(t // GN_ROWS,), in_specs=[row, row, row, vec], out_specs=[row, row, vec],
        out_shape=[jax.ShapeDtypeStruct((t, d), F32), jax.ShapeDtypeStruct((t, d), BF16), jax.ShapeDtypeStruct((1, d), F32)],
        compiler_params=_params(("arbitrary",)),
    )(dgn, y, z, w)


def ple_fwd(h, gpre, pp, name):
    t, d = h.shape

    def body(h_ref, g_ref, p_ref, o_ref):
        o_ref[...] = h_ref[...] + jax.nn.sigmoid(g_ref[...]) * p_ref[...]

    row = pl.BlockSpec((ROW_TILE, d), lambda i: (i, 0))
    return pl.pallas_call(
        body, name=name, grid=(t // ROW_TILE,), in_specs=[row, row, row], out_specs=row,
        out_shape=jax.ShapeDtypeStruct((t, d), F32), compiler_params=_params(("parallel",)),
    )(h, gpre, pp)


def ple_bwd(dh, gpre, pp, name):
    t, d = dh.shape

    def body(d_ref, g_ref, p_ref, dg_ref, dp_ref):
        sg = jax.nn.sigmoid(g_ref[...])
        dv = d_ref[...]
        dg_ref[...] = (dv * p_ref[...] * sg * (1.0 - sg)).astype(dg_ref.dtype)
        dp_ref[...] = (dv * sg).astype(dp_ref.dtype)

    row = pl.BlockSpec((ROW_TILE, d), lambda i: (i, 0))
    return pl.pallas_call(
        body, name=name, grid=(t // ROW_TILE,), in_specs=[row, row, row], out_specs=[row, row],
        out_shape=[jax.ShapeDtypeStruct((t, d), BF16), jax.ShapeDtypeStruct((t, d), BF16)],
        compiler_params=_params(("parallel",)),
    )(dh, gpre, pp)


def loss_head(h, gain, target, name):
    t, d = h.shape

    def body(x_ref, g_ref, t_ref, l_ref, dx_ref, dg_ref):
        i = pl.program_id(0)
        xv = x_ref[...]
        r = lax.rsqrt(jnp.mean(xv * xv, axis=-1, keepdims=True) + NORM_EPS)
        xh = xv * r
        err = xh * g_ref[...] - t_ref[...]
        part = 0.5 * jnp.sum(jnp.mean(err * err, axis=-1, keepdims=True), axis=0, keepdims=True)
        dyv = err * (1.0 / d)
        dxh = dyv * g_ref[...]
        dx_ref[...] = r * (dxh - xh * jnp.mean(dxh * xh, axis=-1, keepdims=True))

        @pl.when(i == 0)
        def _():
            dg_ref[...] = jnp.zeros_like(dg_ref)
            l_ref[...] = jnp.zeros_like(l_ref)

        dg_ref[...] += jnp.sum(dyv * xh, axis=0, keepdims=True)
        l_ref[...] += jnp.broadcast_to(part, l_ref.shape)

    row = pl.BlockSpec((ROW_TILE, d), lambda i: (i, 0))
    vec = pl.BlockSpec((1, d), lambda i: (0, 0))
    lspec = pl.BlockSpec((1, LANE), lambda i: (0, 0))
    return pl.pallas_call(
        body, name=name, grid=(t // ROW_TILE,), in_specs=[row, vec, row], out_specs=[lspec, row, vec],
        out_shape=[jax.ShapeDtypeStruct((1, LANE), F32), jax.ShapeDtypeStruct((t, d), F32), jax.ShapeDtypeStruct((1, d), F32)],
        compiler_params=_params(("arbitrary",)),
    )(h, gain, target)


CHUNK = SSM_CHUNK
HPG = SSM_HEADS // SSM_GROUPS
GW = HPG * SSM_HEAD_DIM
NEG = -1e30


def _split3(x):
    hi = x.astype(BF16)
    r1 = x - hi.astype(F32)
    mid = r1.astype(BF16)
    return hi, mid, (r1 - mid.astype(F32)).astype(BF16)


def _ssd_dt_terms(dtr_ref, bias_ref, alog_ref, g):
    shift = (LANE - HPG * g) % LANE
    li = lax.broadcasted_iota(jnp.int32, (CHUNK, CHUNK), 0)
    si = lax.broadcasted_iota(jnp.int32, (CHUNK, CHUNK), 1)
    x0 = pltpu.roll(dtr_ref[...] + bias_ref[...], shift, 1)
    dt = jnp.maximum(x0, 0.0) + jnp.log1p(jnp.exp(-jnp.abs(x0)))
    a = pltpu.roll(jnp.broadcast_to(-jnp.exp(alog_ref[...]), (CHUNK, LANE)), shift, 1)
    adt = dt * a
    acs = lax.dot_general((li >= si).astype(F32), adt, NN, precision=lax.Precision.HIGHEST, preferred_element_type=F32)
    last = jnp.broadcast_to(acs[CHUNK - 1:CHUNK, :], (CHUNK, LANE))
    e_last = jnp.exp(last)
    hr = lax.broadcasted_iota(jnp.int32, (LANE, GW), 0)
    hc = lax.broadcasted_iota(jnp.int32, (LANE, GW), 1)
    spread = (hc // SSM_HEAD_DIM == hr).astype(BF16)
    stack = jnp.concatenate([dt, jnp.exp(last - acs), jnp.exp(acs), e_last], axis=0)
    wide = jnp.dot(jnp.concatenate(_split3(stack), axis=1), jnp.concatenate([spread] * 3, axis=0), preferred_element_type=F32)
    return dict(li=li, si=si, x0=x0, dt=dt, a=a, acs=acs, acs_t=acs.T, e_last=e_last,
                dt_x=wide[:CHUNK], w_x=wide[CHUNK:2 * CHUNK], e_x=wide[2 * CHUNK:3 * CHUNK], el_x=wide[3 * CHUNK:])


def _ssd_pair_halves(parts):
    low = lax.broadcasted_iota(jnp.int32, (CHUNK, LANE), 1) < SSM_HEAD_DIM
    return jnp.concatenate([jnp.where(low, parts[2 * p], parts[2 * p + 1]) for p in range(HPG // 2)], axis=1)


def _ssd_specs(nc, rev):
    def ci(c):
        return nc - 1 - c if rev else c
    nb = SSM_D_INNER // LANE
    xs = pl.BlockSpec((None, CHUNK, GW), lambda g, b, c: (b, ci(c), g))
    bm = pl.BlockSpec((None, CHUNK, LANE), lambda g, b, c: (b, ci(c), nb + g))
    cm = pl.BlockSpec((None, CHUNK, LANE), lambda g, b, c: (b, ci(c), nb + SSM_GROUPS + g))
    dtr = pl.BlockSpec((None, CHUNK, LANE), lambda g, b, c: (b, ci(c), 0))
    vec = pl.BlockSpec((1, LANE), lambda g, b, c: (0, 0))
    dexp = pl.BlockSpec((1, GW), lambda g, b, c: (0, g))
    st = pl.BlockSpec((None, None, CHUNK, GW), lambda g, b, c: (b, ci(c), 0, g))
    return xs, bm, cm, dtr, vec, dexp, st


def ssd_fwd(xc, dtr, dt_bias, a_log, d_exp, name):
    bsz, seq, _ = xc.shape
    nc = seq // CHUNK
    xs_s, bm_s, cm_s, dtr_s, vec_s, dexp_s, st_s = _ssd_specs(nc, False)

    def body(xs_ref, b_ref, c_ref, dtr_ref, bias_ref, alog_ref, dexp_ref, y_ref, sp_ref, state_ref):
        g, c = pl.program_id(0), pl.program_id(2)

        @pl.when(c == 0)
        def _():
            state_ref[...] = jnp.zeros_like(state_ref)

        t = _ssd_dt_terms(dtr_ref, bias_ref, alog_ref, g)
        tril = t["li"] >= t["si"]
        xs = xs_ref[...]
        b16 = b_ref[...].astype(BF16)
        c16 = c_ref[...].astype(BF16)
        bt16 = b_ref[...].T.astype(BF16)
        cb = lax.dot_general(c16, b16, NT, preferred_element_type=F32)
        sp = state_ref[...]
        sp_ref[...] = sp
        cs = jnp.dot(c16, sp.astype(BF16), preferred_element_type=F32)
        xd = xs * t["dt_x"]
        xd16 = xd.astype(BF16)
        y_parts = []
        for r in range(HPG):
            pair = slice((r // 2) * LANE, (r // 2 + 1) * LANE)
            diff = t["acs"][:, r:r + 1] - t["acs_t"][r:r + 1, :]
            m16 = (cb * jnp.exp(jnp.where(tril, diff, NEG))).astype(BF16)
            y_parts.append(jnp.dot(m16, xd16[:, pair], preferred_element_type=F32))
        y_ref[...] = _ssd_pair_halves(y_parts) + t["e_x"] * cs + xs * dexp_ref[...]
        xw16 = (xd * t["w_x"]).astype(BF16)
        state_ref[...] = sp * t["el_x"] + jnp.dot(bt16, xw16, preferred_element_type=F32)

    return pl.pallas_call(
        body, name=name, grid=(SSM_GROUPS, bsz, nc),
        in_specs=[xs_s, bm_s, cm_s, dtr_s, vec_s, vec_s, dexp_s],
        out_specs=[xs_s, st_s],
        out_shape=[jax.ShapeDtypeStruct((bsz, seq, SSM_D_INNER), F32),
                   jax.ShapeDtypeStruct((bsz, nc, SSM_STATE, SSM_D_INNER), F32)],
        scratch_shapes=[pltpu.VMEM((SSM_STATE, GW), F32)],
        compiler_params=_params(("arbitrary", "arbitrary", "arbitrary")),
    )(xc, xc, xc, dtr, dt_bias, a_log, d_exp)


def ssd_bwd(xc, dtr, dt_bias, a_log, d_exp, dy, sprev, name):
    bsz, seq, _ = xc.shape
    nc = seq // CHUNK
    xs_s, bm_s, cm_s, dtr_s, vec_s, dexp_s, st_s = _ssd_specs(nc, True)
    grp = pl.BlockSpec((None, CHUNK, LANE), lambda g, b, c: (b, nc - 1 - c, g))
    acc = pl.BlockSpec((None, 1, LANE), lambda g, b, c: (g, 0, 0))

    def body(xs_ref, b_ref, c_ref, dtr_ref, bias_ref, alog_ref, dexp_ref, dy_ref, sp_ref,
             dxs_ref, db_ref, dc_ref, ddtr_ref, da_ref, dbias_ref, ddexp_ref, dstate_ref):
        g, b, c = pl.program_id(0), pl.program_id(1), pl.program_id(2)

        @pl.when(c == 0)
        def _():
            dstate_ref[...] = jnp.zeros_like(dstate_ref)

        @pl.when((b == 0) & (c == 0))
        def _():
            da_ref[...] = jnp.zeros_like(da_ref)
            dbias_ref[...] = jnp.zeros_like(dbias_ref)
            ddexp_ref[...] = jnp.zeros_like(ddexp_ref)

        t = _ssd_dt_terms(dtr_ref, bias_ref, alog_ref, g)
        li, si = t["li"], t["si"]
        tril, triu = li >= si, li <= si
        row_is_last = li[:, :1] == CHUNK - 1
        xs, dy = xs_ref[...], dy_ref[...]
        b16 = b_ref[...].astype(BF16)
        c16 = c_ref[...].astype(BF16)
        ct16 = c_ref[...].T.astype(BF16)
        cb = lax.dot_general(c16, b16, NT, preferred_element_type=F32)
        cbt = lax.dot_general(b16, c16, NT, preferred_element_type=F32)
        sp, ds = sp_ref[...], dstate_ref[...]
        sp16, ds16 = sp.astype(BF16), ds.astype(BF16)
        cs = jnp.dot(c16, sp16, preferred_element_type=F32)
        bds = jnp.dot(b16, ds16, preferred_element_type=F32)
        spds = jnp.sum(sp * ds, axis=0, keepdims=True)
        xd = xs * t["dt_x"]
        xw = xd * t["w_x"]
        edy = dy * t["e_x"]
        xd16, dy16, xw16, edy16 = xd.astype(BF16), dy.astype(BF16), xw.astype(BF16), edy.astype(BF16)
        low = lax.broadcasted_iota(jnp.int32, (CHUNK, LANE), 1) < SSM_HEAD_DIM
        zero = jnp.zeros((CHUNK, CHUNK), F32)
        dsum_g, dsum_gt, dacs = zero, zero, zero
        dxd_parts = []
        for r in range(HPG):
            pair = slice((r // 2) * LANE, (r // 2 + 1) * LANE)
            diff = t["acs"][:, r:r + 1] - t["acs_t"][r:r + 1, :]
            lam = jnp.exp(jnp.where(tril, diff, NEG))
            lam_t = jnp.exp(jnp.where(triu, -diff, NEG))
            m, m_t = cb * lam, cbt * lam_t
            dyh16 = jnp.where(low if r % 2 == 0 else ~low, dy16[:, pair], jnp.zeros_like(dy16[:, pair]))
            dxd_parts.append(jnp.dot(m_t.astype(BF16), dy16[:, pair], preferred_element_type=F32))
            dm = lax.dot_general(dyh16, xd16[:, pair], NT, preferred_element_type=F32)
            dm_t = lax.dot_general(xd16[:, pair], dyh16, NT, preferred_element_type=F32)
            dsum_g = dsum_g + dm * lam
            dsum_gt = dsum_gt + dm_t * lam_t
            dacs = dacs + jnp.sum(dm * m - dm_t * m_t, axis=1, keepdims=True) * (si[:1, :] == r).astype(F32)
        xwb = xw * bds
        dxd = _ssd_pair_halves(dxd_parts) + t["w_x"] * bds
        dxs_ref[...] = dxd * t["dt_x"] + dy * dexp_ref[...]
        gr = lax.broadcasted_iota(jnp.int32, (GW, LANE), 0)
        gc = lax.broadcasted_iota(jnp.int32, (GW, LANE), 1)
        gather = (gr // SSM_HEAD_DIM == gc).astype(BF16)
        tail = jnp.concatenate([jnp.sum(xwb, axis=0, keepdims=True), spds, jnp.zeros((SUBLANE - 2, GW), F32)], axis=0)
        stack = jnp.concatenate([edy * cs - xwb, dxd * xs, tail], axis=0)
        hs = sum(jnp.dot(part, gather, preferred_element_type=F32) for part in _split3(stack))
        last = hs[2 * CHUNK:2 * CHUNK + 1] + t["e_last"][0:1, :] * hs[2 * CHUNK + 1:2 * CHUNK + 2]
        dacs = dacs + hs[:CHUNK] + jnp.where(row_is_last, last, 0.0)
        ddt_dir = hs[CHUNK:2 * CHUNK]
        dc_ref[...] = (jnp.dot(dsum_g.astype(BF16), b16, preferred_element_type=F32)
                       + lax.dot_general(edy16, sp16, NT, preferred_element_type=F32))
        db_ref[...] = (jnp.dot(dsum_gt.astype(BF16), c16, preferred_element_type=F32)
                       + lax.dot_general(xw16, ds16, NT, preferred_element_type=F32))
        dstate_ref[...] = ds * t["el_x"] + jnp.dot(ct16, edy16, preferred_element_type=F32)
        dadt = lax.dot_general(triu.astype(F32), dacs, NN, precision=lax.Precision.HIGHEST, preferred_element_type=F32)
        head_lane = si < HPG
        ddtr = jnp.where(head_lane, (dadt * t["a"] + ddt_dir) * jax.nn.sigmoid(t["x0"]), 0.0)
        ddtr_ref[...] = ddtr
        da_ref[...] += jnp.sum(jnp.where(head_lane, dadt * t["dt"] * t["a"], 0.0), axis=0, keepdims=True)
        dbias_ref[...] += jnp.sum(ddtr, axis=0, keepdims=True)
        ddexp_ref[...] += jnp.sum(dy * xs, axis=0, keepdims=True)

    return pl.pallas_call(
        body, name=name, grid=(SSM_GROUPS, bsz, nc),
        in_specs=[xs_s, bm_s, cm_s, dtr_s, vec_s, vec_s, dexp_s, xs_s, st_s],
        out_specs=[xs_s, grp, grp, grp, acc, acc, dexp_s],
        out_shape=[jax.ShapeDtypeStruct((bsz, seq, SSM_D_INNER), F32),
                   jax.ShapeDtypeStruct((bsz, seq, SSM_GROUPS * SSM_STATE), F32),
                   jax.ShapeDtypeStruct((bsz, seq, SSM_GROUPS * SSM_STATE), F32),
                   jax.ShapeDtypeStruct((bsz, seq, SSM_GROUPS * LANE), F32),
                   jax.ShapeDtypeStruct((SSM_GROUPS, 1, LANE), F32),
                   jax.ShapeDtypeStruct((SSM_GROUPS, 1, LANE), F32),
                   jax.ShapeDtypeStruct((1, SSM_D_INNER), F32)],
        scratch_shapes=[pltpu.VMEM((SSM_STATE, GW), F32)],
        compiler_params=_params(("arbitrary", "arbitrary", "arbitrary")),
    )(xc, xc, xc, dtr, dt_bias, a_log, d_exp, dy, sprev)


SB_TQ = 512
SB_TK = 128
SB_SCALE = SB_HEAD_DIM ** -0.5


def _split_dot(x, u16):
    hi = x.astype(BF16)
    lo = (x - hi.astype(F32)).astype(BF16)
    return jnp.dot(hi, u16, preferred_element_type=F32) + jnp.dot(lo, u16, preferred_element_type=F32)


def _sb_scores(qh, kj, mask):
    z = lax.dot_general(qh, kj, NT, preferred_element_type=F32)
    l1p = jnp.log(1.0 + jnp.exp(-jnp.abs(z)))
    lb = jnp.minimum(z, 0.0) - l1p
    lk = -jnp.maximum(z, 0.0) - l1p
    if mask is not None:
        lk = jnp.where(mask, lk, 0.0)
    return lb, lk


def _sb_diag_masks(tq):
    r = lax.broadcasted_iota(jnp.int32, (tq, SB_TK), 0)
    c = lax.broadcasted_iota(jnp.int32, (tq, SB_TK), 1)
    return [r > c + d * SB_TK for d in range(tq // SB_TK)]


def _sb_heads(x, lane):
    return [jnp.where((lane // SB_HEAD_DIM) == hh, x, jnp.zeros_like(x)) for hh in range(2)]


def sb_attention_fwd(q, kv, name):
    bsz, seq, width = q.shape
    tq = min(SB_TQ, seq)
    npair = width // LANE
    qspec = pl.BlockSpec((None, tq, LANE), lambda b, p, i: (b, i, p))
    kspec = pl.BlockSpec((None, seq, LANE), lambda b, p, i: (b, 0, p))
    vspec = pl.BlockSpec((None, seq, LANE), lambda b, p, i: (b, 0, npair + p))

    def body(q_ref, k_ref, v_ref, o_ref, l_ref):
        i = pl.program_id(2)
        ndiag = tq // SB_TK
        nfull = i * ndiag
        lane = lax.broadcasted_iota(jnp.int32, (tq, LANE), 1)
        ui = lax.broadcasted_iota(jnp.int32, (SB_TK, SB_TK), 0)
        uj = lax.broadcasted_iota(jnp.int32, (SB_TK, SB_TK), 1)
        u_excl = (ui > uj).astype(BF16)
        qs = _sb_heads(q_ref[...] * SB_SCALE, lane)

        def tile(j, carry, mask):
            rows = pl.ds(pl.multiple_of(j * SB_TK, SB_TK), SB_TK)
            kj, vj = k_ref[rows, :], v_ref[rows, :]
            new = []
            for hh in range(2):
                acc, run = carry[hh]
                lb, lk = _sb_scores(qs[hh], kj, mask)
                w = jnp.exp(lb + _split_dot(lk, u_excl) + run)
                if mask is not None:
                    w = jnp.where(mask, w, 0.0)
                acc = acc + jnp.dot(w.astype(BF16), vj, preferred_element_type=F32)
                new.append((acc, run + jnp.sum(lk, axis=1, keepdims=True)))
            return tuple(new)

        carry = tuple((jnp.zeros((tq, LANE), F32), jnp.zeros((tq, 1), F32)) for _ in range(2))
        masks = _sb_diag_masks(tq)
        for dd in reversed(range(ndiag)):
            carry = tile(nfull + dd, carry, masks[dd])
        carry = lax.fori_loop(0, nfull // 2, lambda jj, cr: tile(nfull - 2 - 2 * jj, tile(nfull - 1 - 2 * jj, cr, None), None), carry)
        in0 = (lane // SB_HEAD_DIM) == 0
        o_ref[...] = jnp.where(in0, carry[0][0], carry[1][0]).astype(o_ref.dtype)
        l_ref[...] = jnp.where(in0, carry[0][1], carry[1][1])

    return pl.pallas_call(
        body, name=name, grid=(bsz, npair, seq // tq),
        in_specs=[qspec, kspec, vspec], out_specs=[qspec, qspec],
        out_shape=[jax.ShapeDtypeStruct(q.shape, BF16), jax.ShapeDtypeStruct(q.shape, F32)],
        compiler_params=_params(("parallel", "parallel", "arbitrary")),
    )(q, kv, kv)


def sb_attention_bwd(q, kv, ltot, do, dk_in, dv_in, name):
    bsz, seq, width = q.shape
    tq = min(SB_TQ, seq)
    npair = width // LANE
    qspec = pl.BlockSpec((None, tq, LANE), lambda b, p, i: (b, i, p))
    kspec = pl.BlockSpec((None, seq, LANE), lambda b, p, i: (b, 0, p))
    vspec = pl.BlockSpec((None, seq, LANE), lambda b, p, i: (b, 0, npair + p))

    def body(q_ref, k_ref, v_ref, l_ref, do_ref, dki_ref, dvi_ref, dq_ref, dk_ref, dv_ref):
        i = pl.program_id(2)

        @pl.when(i == 0)
        def _():
            dk_ref[...] = dki_ref[...]
            dv_ref[...] = dvi_ref[...]

        ndiag = tq // SB_TK
        nfull = i * ndiag
        lane = lax.broadcasted_iota(jnp.int32, (tq, LANE), 1)
        ui = lax.broadcasted_iota(jnp.int32, (SB_TK, SB_TK), 0)
        uj = lax.broadcasted_iota(jnp.int32, (SB_TK, SB_TK), 1)
        u_le = (ui <= uj).astype(BF16)
        u_lt = (ui < uj).astype(BF16)
        qs = _sb_heads(q_ref[...] * SB_SCALE, lane)
        dos = _sb_heads(do_ref[...], lane)
        ltots = [l_ref[:, hh * SB_HEAD_DIM:hh * SB_HEAD_DIM + 1] for hh in range(2)]

        def tile(j, carry, mask, r0=0):
            rows = pl.ds(pl.multiple_of(j * SB_TK, SB_TK), SB_TK)
            kj, vj = k_ref[rows, :], v_ref[rows, :]
            new, dk_t, dv_t = [], None, None
            for hh in range(2):
                acc, run, run_a = carry[hh]
                qh, doh = qs[hh][r0:], dos[hh][r0:]
                lb, lk = _sb_scores(qh, kj, mask)
                w = jnp.exp(lb + (ltots[hh][r0:] - run[r0:] - _split_dot(lk, u_le)))
                if mask is not None:
                    w = jnp.where(mask, w, 0.0)
                a = lax.dot_general(doh, vj, NT, preferred_element_type=F32) * w
                ca = run_a[r0:] + _split_dot(a, u_lt)
                sg = jnp.exp(lb)
                dz = a * (1.0 - sg) - ca * sg
                if mask is not None:
                    dz = jnp.where(mask, dz, 0.0)
                dz16 = dz.astype(BF16)
                part = [acc[r0:] + jnp.dot(dz16, kj, preferred_element_type=F32),
                        run[r0:] + jnp.sum(lk, axis=1, keepdims=True), run_a[r0:] + jnp.sum(a, axis=1, keepdims=True)]
                if r0:
                    part = [jnp.concatenate([old[:r0], p], axis=0) for old, p in zip((acc, run, run_a), part)]
                dkh = lax.dot_general(dz16, qh, TN, preferred_element_type=F32)
                dvh = lax.dot_general(w.astype(BF16), doh, TN, preferred_element_type=F32)
                dk_t = dkh if dk_t is None else dk_t + dkh
                dv_t = dvh if dv_t is None else dv_t + dvh
                new.append(tuple(part))
            dk_ref[rows, :] += dk_t
            dv_ref[rows, :] += dv_t
            return tuple(new)

        zcol = jnp.zeros((tq, 1), F32)
        carry = tuple((jnp.zeros((tq, LANE), F32), zcol, zcol) for _ in range(2))
        carry = lax.fori_loop(0, nfull // 2, lambda j, cr: tile(2 * j + 1, tile(2 * j, cr, None), None), carry)
        masks = _sb_diag_masks(tq)
        for dd in range(ndiag):
            carry = tile(nfull + dd, carry, masks[dd][dd * SB_TK:], dd * SB_TK)
        in0 = (lane // SB_HEAD_DIM) == 0
        dq_ref[...] = (jnp.where(in0, carry[0][0], carry[1][0]) * SB_SCALE).astype(dq_ref.dtype)

    return pl.pallas_call(
        body, name=name, grid=(bsz, npair, seq // tq),
        in_specs=[qspec, kspec, vspec, qspec, qspec, kspec, kspec], out_specs=[qspec, kspec, kspec],
        out_shape=[jax.ShapeDtypeStruct(q.shape, BF16), jax.ShapeDtypeStruct(q.shape, F32), jax.ShapeDtypeStruct(q.shape, F32)],
        compiler_params=_params(("parallel", "parallel", "arbitrary")),
    )(q, kv, kv, ltot, do, dk_in, dv_in)


def adamw(w, parts, m, v, name):
    r, c = w.shape
    tr = r
    for cand in (256, 176, 128, 64, 32, 16, 8):
        if r % cand == 0:
            tr = cand
            break
    bc1 = 1.0 - ADAM_B1 ** ADAM_STEP
    bc2 = 1.0 - ADAM_B2 ** ADAM_STEP

    def body(w_ref, p_ref, m_ref, v_ref, g_ref, d_ref, mo_ref, vo_ref):
        g = p_ref[0].astype(F32)
        for j in range(1, N_DEV):
            g = g + p_ref[j].astype(F32)
        mn = ADAM_B1 * m_ref[...] + (1.0 - ADAM_B1) * g
        vn = ADAM_B2 * v_ref[...] + (1.0 - ADAM_B2) * (g * g)
        g_ref[...] = g
        mo_ref[...] = mn
        vo_ref[...] = vn
        d_ref[...] = -ADAM_LR * ((mn / bc1) / (jnp.sqrt(vn / bc2) + ADAM_EPS) + ADAM_WD * w_ref[...])

    blk = pl.BlockSpec((tr, c), lambda i: (i, 0))
    pblk = pl.BlockSpec((N_DEV, tr, c), lambda i: (0, i, 0))
    shp = jax.ShapeDtypeStruct((r, c), F32)
    return pl.pallas_call(
        body, name=name, grid=(r // tr,), in_specs=[blk, pblk, blk, blk], out_specs=[blk, blk, blk, blk],
        out_shape=[shp, shp, shp, shp], compiler_params=_params(("parallel",)),
    )(w, parts, m, v)


MESH_ID = pl.DeviceIdType.MESH
ANY = pl.BlockSpec(memory_space=pl.ANY)


def all_gather(xs, name):
    n = len(xs)

    def body(*refs):
        x_refs, out_refs = refs[:n], refs[n:2 * n]
        send_sems, recv_sems, local_sems = refs[2 * n:]
        mx, my, mc = lax.axis_index("x"), lax.axis_index("y"), lax.axis_index("c")
        me, sibling = (mx, my, mc), (mx, my, 1 - mc)
        chips = [(1 - mx, my), (mx, 1 - my), (1 - mx, 1 - my)]

        def slot(a, px, py, pc):
            return out_refs[a].at[4 * px + 2 * py + pc]

        def copy(a, k, block, to, src=None):
            return pltpu.make_async_remote_copy(
                src_ref=slot(a, *block) if src is None else src, dst_ref=slot(a, *block),
                send_sem=send_sems.at[7 * a + k], recv_sem=recv_sems.at[7 * a + k], device_id=to, device_id_type=MESH_ID)

        mine = [pltpu.make_async_copy(x_refs[a], slot(a, *me), local_sems.at[a]) for a in range(n)]
        for cp in mine:
            cp.start()
        first = []
        for a in range(n):
            first.append(copy(a, 0, me, sibling, src=x_refs[a]))
            first += [copy(a, 1 + j, me, (*chip, mc), src=x_refs[a]) for j, chip in enumerate(chips)]
        for cp in first:
            cp.start()
        passed = []
        for j, chip in enumerate(chips):
            for a in range(n):
                copy(a, 1 + j, (*chip, mc), me).wait_recv()
                cp = copy(a, 4 + j, (*chip, mc), sibling)
                cp.start()
                passed.append(cp)
        for a in range(n):
            copy(a, 0, sibling, me).wait_recv()
            for j, chip in enumerate(chips):
                copy(a, 4 + j, (*chip, 1 - mc), me).wait_recv()
        for cp in first + passed:
            cp.wait_send()
        for cp in mine:
            cp.wait()

    return pl.pallas_call(
        body, name=name, out_shape=[jax.ShapeDtypeStruct((N_DEV,) + x.shape, x.dtype) for x in xs],
        in_specs=[ANY] * n, out_specs=[ANY] * n,
        scratch_shapes=[pltpu.SemaphoreType.DMA((7 * n,)), pltpu.SemaphoreType.DMA((7 * n,)), pltpu.SemaphoreType.DMA((n,))],
    )(*xs)


HBM_SPEC = pl.BlockSpec(memory_space=pltpu.HBM)
SEM_SPEC = pl.BlockSpec(memory_space=pltpu.SEMAPHORE)
DATAFLOW = pltpu.SideEffectType.DATAFLOW_SIDE_EFFECTING


def _peer(j, mx, my, mc):
    px = 1 - mx if j & 4 else mx
    py = 1 - my if j & 2 else my
    pc = 1 - mc if j & 1 else mc
    return (px, py, pc), 4 * px + 2 * py + pc


def _split_copy(kind, src_ref, land_ref, send_sems, recv_sems, a, j, mx, my, mc):
    dev, peer = _peer(j, mx, my, mc)
    me = 4 * mx + 2 * my + mc
    src = src_ref if kind == "gather" else src_ref.at[peer]
    k = 7 * a + j - 1
    return pltpu.make_async_remote_copy(src_ref=src, dst_ref=land_ref.at[me], send_sem=send_sems.at[k],
                                        recv_sem=recv_sems.at[k], device_id=dev, device_id_type=MESH_ID), peer


def exchange_start(kind, srcs, name):
    n = len(srcs)
    blocks = [s.shape if kind == "gather" else s.shape[1:] for s in srcs]
    lands = [lax.empty((N_DEV,) + tuple(b), s.dtype) for b, s in zip(blocks, srcs)]

    def body(*refs):
        src_refs, land_refs = refs[:n], refs[n:2 * n]
        send_sems, recv_sems = refs[2 * n], refs[2 * n + 1]
        token = refs[-1]
        mx, my, mc = lax.axis_index("x"), lax.axis_index("y"), lax.axis_index("c")
        for j in range(1, N_DEV):
            for a in range(n):
                _split_copy(kind, src_refs[a], land_refs[a], send_sems, recv_sems, a, j, mx, my, mc)[0].start()
        token[...] = jnp.zeros_like(token)

    hbm = lambda v: pltpu.HBM(v.shape, v.dtype)
    outs = pl.pallas_call(
        body, name=name,
        out_shape=(pltpu.SemaphoreType.DMA((7 * n,)), pltpu.SemaphoreType.DMA((7 * n,)), *[hbm(s) for s in srcs],
                   *[hbm(l) for l in lands], jax.ShapeDtypeStruct((SUBLANE, LANE), F32)),
        in_specs=[HBM_SPEC] * (2 * n),
        out_specs=(SEM_SPEC, SEM_SPEC, *[HBM_SPEC] * (2 * n), pl.BlockSpec(memory_space=pltpu.VMEM)),
        input_output_aliases={a: 2 + a for a in range(2 * n)},
        compiler_params=pltpu.CompilerParams(has_side_effects=DATAFLOW),
    )(*[pltpu.with_memory_space_constraint(v, pltpu.HBM) for v in list(srcs) + lands])
    return dict(kind=kind, n=n, send=outs[0], recv=outs[1], srcs=outs[2:2 + n], lands=outs[2 + n:2 + 2 * n], token=outs[-1])


def exchange_wait(h, after, name):
    n, kind = h["n"], h["kind"]

    def body(*refs):
        src_refs, land_refs = refs[:n], refs[n:2 * n]
        send_sems, recv_sems = refs[2 * n], refs[2 * n + 1]
        mx, my, mc = lax.axis_index("x"), lax.axis_index("y"), lax.axis_index("c")
        for j in range(1, N_DEV):
            for a in range(n):
                cp, peer = _split_copy(kind, src_refs[a], land_refs[a], send_sems, recv_sems, a, j, mx, my, mc)
                cp.wait_send()
                pltpu.make_async_remote_copy(
                    src_ref=land_refs[a].at[peer], dst_ref=land_refs[a].at[peer], send_sem=send_sems.at[7 * a + j - 1],
                    recv_sem=recv_sems.at[7 * a + j - 1], device_id=_peer(j, mx, my, mc)[0], device_id_type=MESH_ID).wait_recv()

    hbm = lambda v: pltpu.HBM(v.shape, v.dtype)
    outs = pl.pallas_call(
        body, name=name,
        out_shape=tuple(hbm(v) for v in list(h["srcs"]) + list(h["lands"])),
        in_specs=[HBM_SPEC] * (2 * n) + [SEM_SPEC, SEM_SPEC, ANY],
        out_specs=tuple([HBM_SPEC] * (2 * n)),
        input_output_aliases={a: a for a in range(2 * n)},
        compiler_params=pltpu.CompilerParams(has_side_effects=DATAFLOW),
    )(*h["srcs"], *h["lands"], h["send"], h["recv"], after)
    return list(outs[n:])


PACK_COLS = 1024
PACK_SEG = 16 * PACK_COLS

BIG = {"ssm_in_proj": 2, "ssm_out_proj": 1, "w_kv": 1, "w_q": 1, "w_o": 1, "ffn_up": 2, "ffn_down": 1,
       "ple_gate": 1, "ple_proj": 2}
SMALL = {"ssm_conv_w": 2, "ssm_conv_b": 1, "ssm_norm": 1, "ffn_conv_w": 2}
REPL = ["attn_norm", "ffn_norm", "ple_norm", "ssm_dt_bias", "ssm_a_log", "ssm_d", "ffn_conv_b", "kv_norm", "final_norm"]


def _seg(n):
    return -(-n // PACK_SEG) * PACK_SEG


def _pack_rows(flats, lead=()):
    padded = [jnp.pad(f, [(0, 0)] * len(lead) + [(0, _seg(f.shape[-1]) - f.shape[-1])]) for f in flats]
    return jnp.concatenate(padded, axis=-1).reshape(*lead, -1, PACK_COLS)


def _unpack_rows(buf, sizes, lead):
    flat = buf.reshape(*lead, -1)
    out, off = [], 0
    for n in sizes:
        out.append(lax.slice_in_dim(flat, off, off + n, axis=len(lead)))
        off += _seg(n)
    return out


def _to_slots(full, axis):
    shp = full.shape
    blk = shp[axis] // N_DEV
    t = full.reshape(shp[:axis] + (N_DEV, blk) + shp[axis + 1:])
    return jnp.moveaxis(t, axis, 0).reshape(N_DEV, -1)


def _from_slots(slots, local_shape, axis):
    t = jnp.moveaxis(slots.reshape((N_DEV,) + tuple(local_shape)), 0, axis)
    shp = list(local_shape)
    shp[axis] *= N_DEV
    return t.reshape(shp)


def _as2d(a):
    return a.reshape(1, -1) if a.ndim == 1 else a.reshape(-1, a.shape[-1])


def _pad_rows(w, rows=SUBLANE):
    return jnp.pad(w, ((0, rows - w.shape[0]), (0, 0)))


def kernel(x, p, attn_norm, ffn_norm, ple_norm, ssm_in_proj, ssm_conv_w, ssm_conv_b, ssm_dt_bias, ssm_a_log, ssm_d, ssm_norm, ssm_out_proj, kv_norm, w_kv, w_q, w_o, ffn_up, ffn_conv_w, ffn_conv_b, ffn_down, ple_gate, ple_proj, final_norm, loss_target, m_attn_norm, m_ffn_norm, m_ple_norm, m_ssm_in_proj, m_ssm_conv_w, m_ssm_conv_b, m_ssm_dt_bias, m_ssm_a_log, m_ssm_d, m_ssm_norm, m_ssm_out_proj, m_kv_norm, m_w_kv, m_w_q, m_w_o, m_ffn_up, m_ffn_conv_w, m_ffn_conv_b, m_ffn_down, m_ple_gate, m_ple_proj, m_final_norm, v_attn_norm, v_ffn_norm, v_ple_norm, v_ssm_in_proj, v_ssm_conv_w, v_ssm_conv_b, v_ssm_dt_bias, v_ssm_a_log, v_ssm_d, v_ssm_norm, v_ssm_out_proj, v_kv_norm, v_w_kv, v_w_q, v_w_o, v_ffn_up, v_ffn_conv_w, v_ffn_conv_b, v_ffn_down, v_ple_gate, v_ple_proj, v_final_norm):
    given = dict(locals())
    wnames = list(BIG) + list(SMALL) + REPL
    bsz, seq, d = x.shape
    t = bsz * seq
    bs = lambda a: a.reshape(bsz, seq, a.shape[-1])
    fl = lambda a: a.reshape(t, a.shape[-1])

    big_local = [given[n] for n in BIG]
    small_local = [given[n] for n in SMALL]
    me = 4 * lax.axis_index("x") + 2 * lax.axis_index("y") + lax.axis_index("c")

    N_GROUPS = 2 * DEPTH

    def layer_items(k):
        i, j = k // 2, k // 2 - N_A
        if k % 2 == 0:
            return [("ssm_in_proj", i), ("ssm_out_proj", i)] if i < N_A else [("w_q", j), ("w_o", j)]
        shared = [("w_kv", None)] if i == N_A - 1 else []
        return [("ffn_up", i), ("ffn_down", i), ("ple_gate", i), ("ple_proj", i)] + shared

    def local_block(n, idx):
        return given[n] if idx is None else given[n][idx]

    def put_own(land, own):
        return lax.dynamic_update_slice(land, own[None], (me,) + (0,) * own.ndim)

    small_pack = _pack_rows([w.reshape(-1) for w in small_local])
    gather_src, gather_h = [], []
    for i in range(N_GROUPS):
        src = [local_block(n, idx).astype(BF16) for n, idx in layer_items(i)] + ([small_pack] if i == 0 else [])
        gather_src.append(src)
        gather_h.append(exchange_start("gather", src, f"ag_start{i}"))
    full = {n: {} for n in BIG}
    rows_of = lambda g: g.reshape(-1, g.shape[-1])
    cols_of = lambda g: jnp.transpose(g, (1, 0, 2)).reshape(g.shape[1], -1)

    def fetch_layer(i, after):
        lands = exchange_wait(gather_h[i], after, f"ag_wait{i}")
        got = [put_own(l, s) for l, s in zip(lands, gather_src[i])]
        for (n, idx), g in zip(layer_items(i), got):
            if n == "ffn_up":
                full[n][idx] = (cols_of(g[:N_DEV // 2]), cols_of(g[N_DEV // 2:]))
            else:
                full[n][idx] = cols_of(g) if BIG[n] == given[n].ndim - 1 else rows_of(g)
        if i == 0:
            for (n, ax), w, s in zip(SMALL.items(), small_local, _unpack_rows(got[-1], [w.size for w in small_local], (N_DEV,))):
                full[n] = _from_slots(s, w.shape, ax)

    row = lambda v: v.reshape(1, -1)
    pad_lane = lambda v: jnp.pad(v.reshape(1, -1), ((0, 0), (0, LANE - v.size)))
    h = x.reshape(t, d)
    saved = []
    kv = hnkv = h_kv = None
    for i in range(DEPTH):
        fetch_layer(2 * i, h)
        sv = {"h0": h}
        gain = row(attn_norm[i])
        if i == 0:
            gain = gain + sum(hd["token"][0, 0] for hd in gather_h)
        hn = rmsnorm_fwd(h, gain, f"attn_norm_f{i}")
        sv["hn"] = hn
        if i < N_A:
            w_in = full["ssm_in_proj"][i]
            wz, wxbc = w_in[:, :SSM_D_INNER], w_in[:, SSM_D_INNER:SSM_ZX]
            wdt = jnp.pad(w_in[:, SSM_ZX:], ((0, 0), (0, LANE - SSM_HEADS)))
            z = matmul(hn, wz, "nn", F32, f"ssm_z_f{i}")
            xbc = bs(matmul(hn, wxbc, "nn", F32, f"ssm_xbc_f{i}"))
            dtr = bs(matmul(hn, wdt, "nn", F32, f"ssm_dt_f{i}"))
            cw = _pad_rows(full["ssm_conv_w"][i])
            cb = row(full["ssm_conv_b"][i])
            xc = ssm_conv_fwd(xbc, cw, cb, f"ssm_conv_f{i}")
            dtb, alog = pad_lane(ssm_dt_bias[i]), pad_lane(ssm_a_log[i])
            dexp = jnp.repeat(ssm_d[i], SSM_HEAD_DIM).reshape(1, -1)
            y, sprev = ssd_fwd(xc, dtr, dtb, alog, dexp, f"ssd_f{i}")
            gn = gated_norm_fwd(fl(y), z, row(full["ssm_norm"][i]), f"ssm_gnorm_f{i}")
            h1 = matmul(gn, full["ssm_out_proj"][i], "nn", F32, f"ssm_out_f{i}", add=h)
            sv.update(wz=wz, wxbc=wxbc, wdt=wdt, z=z, xbc=xbc, dtr=dtr, cw=cw, cb=cb, xc=xc, dtb=dtb, alog=alog,
                      dexp=dexp, y=y, sprev=sprev, gn=gn)
        else:
            j = i - N_A
            q = bs(matmul(hn, full["w_q"][j], "nn", BF16, f"sb_q_f{j}"))
            o, ltot = sb_attention_fwd(q, kv, f"sb_attn_f{j}")
            h1 = matmul(fl(o), full["w_o"][j], "nn", F32, f"sb_o_f{j}", add=h)
            sv.update(q=q, o=o, ltot=ltot)
        sv["h1"] = h1
        fetch_layer(2 * i + 1, h1)
        hn2 = rmsnorm_fwd(h1, row(ffn_norm[i]), f"ffn_norm_f{i}")
        wug, wuv = full["ffn_up"][i]
        ug = bs(matmul(hn2, wug, "nn", F32, f"ffn_upg_f{i}"))
        uv = bs(matmul(hn2, wuv, "nn", F32, f"ffn_upv_f{i}"))
        fcw = full["ffn_conv_w"][i]
        fwg, fwv = _pad_rows(fcw[:, :D_FF]), _pad_rows(fcw[:, D_FF:])
        fbg, fbv = row(ffn_conv_b[i, :D_FF]), row(ffn_conv_b[i, D_FF:])
        act = ffn_conv_fwd(ug, uv, fwg, fwv, fbg, fbv, f"ffn_conv_f{i}")
        h2 = matmul(fl(act), full["ffn_down"][i], "nn", F32, f"ffn_down_f{i}", add=h1)
        hn3 = rmsnorm_fwd(h2, row(ple_norm[i]), f"ple_norm_f{i}")
        gpre = matmul(hn3, full["ple_gate"][i], "nn", F32, f"ple_gate_f{i}")
        p_i = p[i].reshape(t, PLE_DIM)
        pp = matmul(p_i, full["ple_proj"][i], "nn", F32, f"ple_proj_f{i}")
        h3 = ple_fwd(h2, gpre, pp, f"ple_f{i}")
        sv.update(hn2=hn2, wug=wug, wuv=wuv, ug=ug, uv=uv, fwg=fwg, fwv=fwv, fbg=fbg, fbv=fbv, act=act, h2=h2,
                  hn3=hn3, gpre=gpre, pp=pp, p_i=p_i)
        if i == N_A - 1:
            h_kv = h3
            hnkv = rmsnorm_fwd(h3, row(kv_norm), "kv_norm_f")
            kv = bs(matmul(hnkv, full["w_kv"][None], "nn", BF16, "kv_f"))
        h = h3
        saved.append(sv)

    loss_row, dh, g_final = loss_head(h, row(final_norm), loss_target.reshape(t, d), "loss_head")
    loss = lax.psum(loss_row[0, 0], ("x", "y", "c"))

    G = {n: [None] * given[n].shape[0] for n in wnames if given[n].ndim > 1 and n not in ("w_kv",)}
    G["final_norm"] = g_final[0]

    def col_slots(g, ndev=N_DEV):
        return jnp.transpose(g.reshape(g.shape[0], ndev, -1), (1, 0, 2))

    def grad_slots(n, idx):
        g = G[n] if idx is None else G[n][idx]
        if n == "ffn_up":
            s = jnp.concatenate([col_slots(g[0], N_DEV // 2), col_slots(g[1], N_DEV // 2)], axis=0)
        elif BIG[n] == given[n].ndim - 1:
            s = col_slots(g)
        else:
            s = g.reshape(N_DEV, -1, g.shape[-1])
        return s.astype(BF16)

    scatter_src, scatter_h = {}, {}
    dk = jnp.zeros((bsz, seq, SB_WIDTH), F32)
    dv = jnp.zeros((bsz, seq, SB_WIDTH), F32)
    for i in reversed(range(DEPTH)):
        sv = saved[i]
        if i == N_A - 1:
            wk, wv = full["w_kv"][None][:, :SB_WIDTH], full["w_kv"][None][:, SB_WIDTH:]
            dkf, dvf = fl(dk), fl(dv)
            dhn = matmul(dkf, wk, "nt", F32, "kv_dx_k")
            dhn = matmul(dvf, wv, "nt", F32, "kv_dx_v", add=dhn)
            G["w_kv"] = jnp.concatenate([matmul(hnkv, dkf, "tn", F32, "kv_dw_k"), matmul(hnkv, dvf, "tn", F32, "kv_dw_v")], axis=1)
            dh, gk = rmsnorm_bwd(dhn, h_kv, row(kv_norm), dh, "kv_norm_b")
            G["kv_norm"] = gk[0]
        dgpre, dpp = ple_bwd(dh, sv["gpre"], sv["pp"], f"ple_b{i}")
        G["ple_proj"][i] = matmul(sv["p_i"], dpp, "tn", F32, f"ple_proj_dw{i}")
        G["ple_gate"][i] = matmul(sv["hn3"], dgpre, "tn", F32, f"ple_gate_dw{i}")
        dhn = matmul(dgpre, full["ple_gate"][i], "nt", F32, f"ple_gate_dx{i}")
        gain = row(ple_norm[i])
        if 2 * i + 2 in scatter_h:
            gain = gain + scatter_h[2 * i + 2]["token"][0, 0]
        dh, gn_ = rmsnorm_bwd(dhn, sv["h2"], gain, dh, f"ple_norm_b{i}")
        G["ple_norm"][i] = gn_[0]
        dact = bs(matmul(dh, full["ffn_down"][i], "nt", F32, f"ffn_down_dx{i}"))
        G["ffn_down"][i] = matmul(fl(sv["act"]), dh, "tn", F32, f"ffn_down_dw{i}")
        dug, duv, dwg, dwv, dbg, dbv = ffn_conv_bwd(sv["ug"], sv["uv"], dact, sv["fwg"], sv["fwv"], sv["fbg"], sv["fbv"], f"ffn_conv_b{i}")
        dug, duv = fl(dug), fl(duv)
        G["ffn_conv_w"][i] = jnp.concatenate([dwg[:FFN_CONV], dwv[:FFN_CONV]], axis=1)
        G["ffn_conv_b"][i] = jnp.concatenate([dbg[0], dbv[0]])
        G["ffn_up"][i] = (matmul(sv["hn2"], dug, "tn", F32, f"ffn_upg_dw{i}"), matmul(sv["hn2"], duv, "tn", F32, f"ffn_upv_dw{i}"))
        dhn = matmul(dug, sv["wug"], "nt", F32, f"ffn_upg_dx{i}")
        dhn = matmul(duv, sv["wuv"], "nt", F32, f"ffn_upv_dx{i}", add=dhn)
        dh, gn_ = rmsnorm_bwd(dhn, sv["h1"], row(ffn_norm[i]), dh, f"ffn_norm_b{i}")
        G["ffn_norm"][i] = gn_[0]
        scatter_src[2 * i + 1] = [grad_slots(n, idx) for n, idx in layer_items(2 * i + 1)]
        scatter_h[2 * i + 1] = exchange_start("scatter", scatter_src[2 * i + 1], f"a2a_start{2 * i + 1}")
        tok = scatter_h[2 * i + 1]["token"][0, 0]
        if i < N_A:
            dgn = matmul(dh, full["ssm_out_proj"][i], "nt", F32, f"ssm_out_dx{i}")
            G["ssm_out_proj"][i] = matmul(sv["gn"], dh, "tn", F32, f"ssm_out_dw{i}")
            dy, dz, dnw = gated_norm_bwd(dgn, fl(sv["y"]), sv["z"], row(full["ssm_norm"][i]) + tok, f"ssm_gnorm_b{i}")
            G["ssm_norm"][i] = dnw[0]
            dxs, dbm, dcm, ddtr4, da4, dbias4, ddexp = ssd_bwd(sv["xc"], sv["dtr"], sv["dtb"], sv["alog"], sv["dexp"], bs(dy), sv["sprev"], f"ssd_b{i}")
            G["ssm_a_log"][i] = da4[:, 0, :HPG].reshape(-1)
            G["ssm_dt_bias"][i] = dbias4[:, 0, :HPG].reshape(-1)
            G["ssm_d"][i] = ddexp.reshape(SSM_HEADS, SSM_HEAD_DIM).sum(axis=-1)
            dxc = jnp.concatenate([dxs, dbm, dcm], axis=-1)
            dxbc, dcw, dcb = ssm_conv_bwd(sv["xbc"], dxc, sv["cw"], sv["cb"], f"ssm_conv_b{i}")
            dxbc = fl(dxbc)
            G["ssm_conv_w"][i] = dcw[:SSM_CONV]
            G["ssm_conv_b"][i] = dcb[0]
            ddtr = ddtr4.reshape(t, SSM_GROUPS, LANE)[:, :, :HPG].reshape(t, SSM_HEADS)
            ddtr = jnp.pad(ddtr, ((0, 0), (0, LANE - SSM_HEADS))).astype(BF16)
            hn = sv["hn"]
            G["ssm_in_proj"][i] = jnp.concatenate([
                matmul(hn, dz, "tn", F32, f"ssm_z_dw{i}"), matmul(hn, dxbc, "tn", F32, f"ssm_xbc_dw{i}"),
                matmul(hn, ddtr, "tn", F32, f"ssm_dt_dw{i}")[:, :SSM_HEADS]], axis=1)
            dhn = matmul(dz, sv["wz"], "nt", F32, f"ssm_z_dx{i}")
            dhn = matmul(dxbc, sv["wxbc"], "nt", F32, f"ssm_xbc_dx{i}", add=dhn)
            dhn = matmul(ddtr, sv["wdt"], "nt", F32, f"ssm_dt_dx{i}", add=dhn)
        else:
            j = i - N_A
            do = bs(matmul(dh, full["w_o"][j], "nt", BF16, f"sb_o_dx{j}"))
            G["w_o"][j] = matmul(fl(sv["o"]), dh, "tn", F32, f"sb_o_dw{j}")
            dq, dk, dv = sb_attention_bwd(sv["q"], kv, sv["ltot"], do, dk, dv, f"sb_attn_b{j}")
            dq = fl(dq)
            G["w_q"][j] = matmul(sv["hn"], dq, "tn", F32, f"sb_q_dw{j}")
            dhn = matmul(dq, full["w_q"][j], "nt", F32, f"sb_q_dx{j}")
        dh, gn_ = rmsnorm_bwd(dhn, sv["h0"], row(attn_norm[i]) + (0.0 if i < N_A else tok), dh, f"attn_norm_b{i}")
        G["attn_norm"][i] = gn_[0]
        src = [grad_slots(n, idx) for n, idx in layer_items(2 * i)]
        if i == 0:
            small_g = {n: jnp.stack(G[n]) for n in SMALL}
            src.append(_pack_rows([_to_slots(small_g[n], SMALL[n]) for n in SMALL], (N_DEV,)))
        scatter_src[2 * i] = src
        scatter_h[2 * i] = exchange_start("scatter", src, f"a2a_start{2 * i}")
    grad_x = dh.reshape(bsz, seq, d)
    grads = {n: jnp.stack(G[n]) if isinstance(G[n], list) else G[n] for n in REPL}

    layer_parts = {n: {} for n in BIG}
    small_parts = None
    for i in reversed(range(N_GROUPS)):
        lands = exchange_wait(scatter_h[i], dh, f"a2a_wait{i}")
        got = [put_own(l, lax.dynamic_index_in_dim(s, me, 0, keepdims=False)) for l, s in zip(lands, scatter_src[i])]
        for (n, idx), g in zip(layer_items(i), got):
            layer_parts[n][idx] = g
        if i == 0:
            small_parts = _unpack_rows(got[-1], [given[n].size for n in SMALL], (N_DEV,))
    big_parts = [jnp.concatenate([layer_parts[n][k] for k in sorted(layer_parts[n], key=lambda v: -1 if v is None else v)], axis=1)
                 for n in BIG]
    parts_rp = all_gather([_pack_rows([grads[n].reshape(-1) for n in REPL])], "ag_repl_grads")[0]
    sh_names = list(BIG) + list(SMALL)
    sh_parts = big_parts + small_parts
    rp_parts = _unpack_rows(parts_rp, [given[n].size for n in REPL], (N_DEV,))
    res = {}
    for n, pr in list(zip(sh_names, sh_parts)) + list(zip(REPL, rp_parts)):
        w2 = _as2d(given[n])
        outs = adamw(w2, pr.reshape((N_DEV,) + w2.shape), _as2d(given["m_" + n]), _as2d(given["v_" + n]), f"adamw_{n}")
        res[n] = [o.reshape(given[n].shape) for o in outs]
    order = ["attn_norm", "ffn_norm", "ple_norm", "ssm_in_proj", "ssm_conv_w", "ssm_conv_b", "ssm_dt_bias", "ssm_a_log",
             "ssm_d", "ssm_norm", "ssm_out_proj", "kv_norm", "w_kv", "w_q", "w_o", "ffn_up", "ffn_conv_w", "ffn_conv_b",
             "ffn_down", "ple_gate", "ple_proj", "final_norm"]
    return (loss, grad_x, *[res[n][0] for n in order], *[res[n][1] for n in order],
            *[res[n][2] for n in order], *[res[n][3] for n in order])
```

```python
import functools
import math

import jax
import jax.numpy as jnp
from jax import lax
from jax.experimental import pallas as pl
from jax.experimental.pallas import tpu as pltpu

F32 = jnp.float32
BF16 = jnp.bfloat16

N_DEV = 8
D_MODEL = 1024
SEQ = 2048
DEPTH = 4
N_A = 2
N_B = 2
SSM_D_INNER = 2048
SSM_HEAD_DIM = 64
SSM_HEADS = 32
SSM_GROUPS = 4
SSM_STATE = 128
SSM_CONV = 4
SSM_CHUNK = 128
SSM_CONV_DIM = 3072
SSM_ZX = 5120
SSM_IN_DIM = 5152
SB_HEADS = 16
SB_HEAD_DIM = 64
SB_WIDTH = 1024
D_FF = 2816
FFN_CONV = 3
PLE_DIM = 256
NORM_EPS = 1e-6
SSM_NORM_EPS = 1e-5

ADAM_LR = 0.001
ADAM_B1 = 0.9
ADAM_B2 = 0.999
ADAM_EPS = 1e-08
ADAM_WD = 0.01
ADAM_STEP = 10

LANE = 128
SUBLANE = 8
VMEM_LIMIT = 48 * 1024 * 1024

NN = (((1,), (0,)), ((), ()))
NT = (((1,), (1,)), ((), ()))
TN = (((0,), (0,)), ((), ()))


def _params(sem):
    return pltpu.CompilerParams(dimension_semantics=sem, vmem_limit_bytes=VMEM_LIMIT)


def _pick(n, prefs):
    for p in prefs:
        if n % p == 0:
            return p
    return n


def matmul(a, b, mode, out_dtype, name, add=None):
    if mode == "nn":
        (m, k), (k2, n) = a.shape, b.shape
    elif mode == "nt":
        (m, k), (n, k2) = a.shape, b.shape
    else:
        (k, m), (k2, n) = a.shape, b.shape
    assert k == k2, (a.shape, b.shape, mode)
    tm = _pick(m, (512, 256, 128))
    tn = _pick(n, (1408, 1024, 512, 256, 128))
    tk = _pick(k, (2816, 2048, 1024, 512, 256, 128))
    nk = k // tk
    dims = {"nn": NN, "nt": NT, "tn": TN}[mode]

    def body(*refs):
        a_ref, b_ref = refs[:2]
        add_ref = refs[2] if add is not None else None
        o_ref = refs[-2] if nk > 1 else refs[-1]
        part = lax.dot_general(a_ref[...].astype(BF16), b_ref[...].astype(BF16), dims, preferred_element_type=F32)

        def finish(r):
            if add_ref is not None:
                r = r + add_ref[...].astype(F32)
            o_ref[...] = r.astype(o_ref.dtype)

        if nk == 1:
            finish(part)
            return
        acc_ref = refs[-1]
        kk = pl.program_id(2)

        @pl.when(kk == 0)
        def _():
            acc_ref[...] = part

        @pl.when((kk > 0) & (kk < nk - 1))
        def _():
            acc_ref[...] += part

        @pl.when(kk == nk - 1)
        def _():
            finish(acc_ref[...] + part)

    if mode == "nn":
        a_spec = pl.BlockSpec((tm, tk), lambda i, j, kk: (i, kk))
        b_spec = pl.BlockSpec((tk, tn), lambda i, j, kk: (kk, j))
    elif mode == "nt":
        a_spec = pl.BlockSpec((tm, tk), lambda i, j, kk: (i, kk))
        b_spec = pl.BlockSpec((tn, tk), lambda i, j, kk: (j, kk))
    else:
        a_spec = pl.BlockSpec((tk, tm), lambda i, j, kk: (kk, i))
        b_spec = pl.BlockSpec((tk, tn), lambda i, j, kk: (kk, j))
    o_spec = pl.BlockSpec((tm, tn), lambda i, j, kk: (i, j))
    in_specs = [a_spec, b_spec] + ([o_spec] if add is not None else [])
    args = (a, b) + ((add,) if add is not None else ())
    return pl.pallas_call(
        body,
        name=name,
        grid=(m // tm, n // tn, nk),
        in_specs=in_specs,
        out_specs=o_spec,
        out_shape=jax.ShapeDtypeStruct((m, n), out_dtype),
        scratch_shapes=[pltpu.VMEM((tm, tn), F32)] if nk > 1 else [],
        compiler_params=_params(("parallel", "parallel", "arbitrary")),
    )(*args)


ROW_TILE = 512


def rmsnorm_fwd(x, gain, name):
    t, d = x.shape

    def body(x_ref, g_ref, o_ref):
        xv = x_ref[...]
        r = lax.rsqrt(jnp.mean(xv * xv, axis=-1, keepdims=True) + NORM_EPS)
        o_ref[...] = (xv * r * g_ref[...]).astype(o_ref.dtype)

    return pl.pallas_call(
        body,
        name=name,
        grid=(t // ROW_TILE,),
        in_specs=[pl.BlockSpec((ROW_TILE, d), lambda i: (i, 0)), pl.BlockSpec((1, d), lambda i: (0, 0))],
        out_specs=pl.BlockSpec((ROW_TILE, d), lambda i: (i, 0)),
        out_shape=jax.ShapeDtypeStruct((t, d), BF16),
        compiler_params=_params(("parallel",)),
    )(x, gain)


def rmsnorm_bwd(dy, x, gain, dres, name):
    t, d = x.shape

    def body(dy_ref, x_ref, g_ref, dres_ref, dx_ref, dg_ref):
        i = pl.program_id(0)
        xv = x_ref[...]
        r = lax.rsqrt(jnp.mean(xv * xv, axis=-1, keepdims=True) + NORM_EPS)
        xh = xv * r
        dyv = dy_ref[...].astype(F32)
        dxh = dyv * g_ref[...]
        dx = r * (dxh - xh * jnp.mean(dxh * xh, axis=-1, keepdims=True))
        dx_ref[...] = dres_ref[...] + dx

        @pl.when(i == 0)
        def _():
            dg_ref[...] = jnp.zeros_like(dg_ref)

        dg_ref[...] += jnp.sum(dyv * xh, axis=0, keepdims=True)

    row = pl.BlockSpec((ROW_TILE, d), lambda i: (i, 0))
    vec = pl.BlockSpec((1, d), lambda i: (0, 0))
    return pl.pallas_call(
        body,
        name=name,
        grid=(t // ROW_TILE,),
        in_specs=[row, row, vec, row],
        out_specs=[row, vec],
        out_shape=[jax.ShapeDtypeStruct((t, d), F32), jax.ShapeDtypeStruct((1, d), F32)],
        compiler_params=_params(("arbitrary",)),
    )(dy, x, gain, dres)


CONV_ROWS = 512
HALO = SUBLANE


def _conv_apply(ext, w_ref, bias, k, lo, n):
    acc = bias + w_ref[k - 1:k, :] * ext[lo:lo + n]
    for j in range(1, k):
        acc = acc + w_ref[k - 1 - j:k - j, :] * pltpu.roll(ext, j, 0)[lo:lo + n]
    return acc


def _conv_apply_t(ext, w_ref, k, n):
    rows = ext.shape[0]
    acc = w_ref[k - 1:k, :] * ext[:n]
    for j in range(1, k):
        acc = acc + w_ref[k - 1 - j:k - j, :] * pltpu.roll(ext, rows - j, 0)[:n]
    return acc


def _conv_dw(dpre_main, ext_u, k, n, dw_ref):
    for kk in range(k):
        j = k - 1 - kk
        sh = ext_u[HALO:HALO + n] if j == 0 else pltpu.roll(ext_u, j, 0)[HALO:HALO + n]
        dw_ref[kk:kk + 1, :] += jnp.sum(dpre_main * sh, axis=0, keepdims=True)


def _silu_grad(pre):
    sg = jax.nn.sigmoid(pre)
    return sg * (1.0 + pre * (1.0 - sg))


def _conv_blockspecs(seq, r, tc):
    nh = r // HALO
    last = seq // HALO - 1
    main = pl.BlockSpec((None, r, tc), lambda c, b, s: (b, s, c))
    before = pl.BlockSpec((None, HALO, tc), lambda c, b, s: (b, jnp.maximum(s * nh - 1, 0), c))
    after = pl.BlockSpec((None, HALO, tc), lambda c, b, s: (b, jnp.minimum((s + 1) * nh, last), c))
    return main, before, after


def ssm_conv_fwd(u, w, bias, name):
    bsz, seq, ch = u.shape
    r, tc, k = min(CONV_ROWS, seq), 512, SSM_CONV
    main, before, _ = _conv_blockspecs(seq, r, tc)
    wspec = pl.BlockSpec((SUBLANE, tc), lambda c, b, s: (0, c))
    bspec = pl.BlockSpec((1, tc), lambda c, b, s: (0, c))

    def body(u_ref, hb_ref, w_ref, b_ref, o_ref):
        s = pl.program_id(2)
        hb = jnp.where(s == 0, 0.0, hb_ref[...])
        ext = jnp.concatenate([hb, u_ref[...]], axis=0)
        pre = _conv_apply(ext, w_ref, b_ref[...], k, HALO, r)
        o_ref[...] = pre * jax.nn.sigmoid(pre)

    return pl.pallas_call(
        body, name=name, grid=(ch // tc, bsz, seq // r),
        in_specs=[main, before, wspec, bspec], out_specs=main,
        out_shape=jax.ShapeDtypeStruct(u.shape, F32),
        compiler_params=_params(("parallel", "parallel", "parallel")),
    )(u, u, w, bias)


def ssm_conv_bwd(u, dxc, w, bias, name):
    bsz, seq, ch = u.shape
    r, tc, k = min(CONV_ROWS, seq), 512, SSM_CONV
    ns = seq // r
    main, before, after = _conv_blockspecs(seq, r, tc)
    wspec = pl.BlockSpec((SUBLANE, tc), lambda c, b, s: (0, c))
    bspec = pl.BlockSpec((1, tc), lambda c, b, s: (0, c))

    def body(u_ref, hb_ref, ha_ref, d_ref, da_ref, w_ref, b_ref, du_ref, dw_ref, db_ref):
        b, s = pl.program_id(1), pl.program_id(2)
        hb = jnp.where(s == 0, 0.0, hb_ref[...])
        ext_u = jnp.concatenate([hb, u_ref[...], ha_ref[...]], axis=0)
        pre = _conv_apply(ext_u, w_ref, b_ref[...], k, HALO, r + HALO)
        dxe = jnp.concatenate([d_ref[...], jnp.where(s == ns - 1, 0.0, da_ref[...])], axis=0)
        dpre = dxe * _silu_grad(pre)
        du_ref[...] = _conv_apply_t(dpre, w_ref, k, r).astype(du_ref.dtype)

        @pl.when((b == 0) & (s == 0))
        def _():
            dw_ref[...] = jnp.zeros_like(dw_ref)
            db_ref[...] = jnp.zeros_like(db_ref)

        dpm = dpre[:r]
        _conv_dw(dpm, ext_u, k, r, dw_ref)
        db_ref[...] += jnp.sum(dpm, axis=0, keepdims=True)

    return pl.pallas_call(
        body, name=name, grid=(ch // tc, bsz, ns),
        in_specs=[main, before, after, main, after, wspec, bspec],
        out_specs=[main, wspec, bspec],
        out_shape=[jax.ShapeDtypeStruct(u.shape, BF16), jax.ShapeDtypeStruct((SUBLANE, ch), F32),
                   jax.ShapeDtypeStruct((1, ch), F32)],
        compiler_params=_params(("arbitrary", "arbitrary", "arbitrary")),
    )(u, u, u, dxc, dxc, w, bias)


def ffn_conv_fwd(ug, uv, wg, wv, bg, bv, name):
    bsz, seq, ch = ug.shape
    r, tc, k = min(CONV_ROWS, seq), 256, FFN_CONV
    main, before, _ = _conv_blockspecs(seq, r, tc)
    wspec = pl.BlockSpec((SUBLANE, tc), lambda c, b, s: (0, c))
    bspec = pl.BlockSpec((1, tc), lambda c, b, s: (0, c))

    def body(ug_ref, hg_ref, uv_ref, hv_ref, wg_ref, wv_ref, bg_ref, bv_ref, o_ref):
        s = pl.program_id(2)
        eg = jnp.concatenate([jnp.where(s == 0, 0.0, hg_ref[...]), ug_ref[...]], axis=0)
        ev = jnp.concatenate([jnp.where(s == 0, 0.0, hv_ref[...]), uv_ref[...]], axis=0)
        pg = _conv_apply(eg, wg_ref, bg_ref[...], k, HALO, r)
        pv = _conv_apply(ev, wv_ref, bv_ref[...], k, HALO, r)
        o_ref[...] = (pg * jax.nn.sigmoid(pg) * pv).astype(o_ref.dtype)

    return pl.pallas_call(
        body, name=name, grid=(ch // tc, bsz, seq // r),
        in_specs=[main, before, main, before, wspec, wspec, bspec, bspec], out_specs=main,
        out_shape=jax.ShapeDtypeStruct(ug.shape, BF16),
        compiler_params=_params(("parallel", "parallel", "parallel")),
    )(ug, ug, uv, uv, wg, wv, bg, bv)


def ffn_conv_bwd(ug, uv, dact, wg, wv, bg, bv, name):
    bsz, seq, ch = ug.shape
    r, tc, k = min(CONV_ROWS, seq), 256, FFN_CONV
    ns = seq // r
    main, before, after = _conv_blockspecs(seq, r, tc)
    wspec = pl.BlockSpec((SUBLANE, tc), lambda c, b, s: (0, c))
    bspec = pl.BlockSpec((1, tc), lambda c, b, s: (0, c))

    def body(ug_ref, gb_ref, ga_ref, uv_ref, vb_ref, va_ref, d_ref, da_ref, wg_ref, wv_ref, bg_ref, bv_ref,
             dug_ref, duv_ref, dwg_ref, dwv_ref, dbg_ref, dbv_ref):
        b, s = pl.program_id(1), pl.program_id(2)
        eg = jnp.concatenate([jnp.where(s == 0, 0.0, gb_ref[...]), ug_ref[...], ga_ref[...]], axis=0)
        ev = jnp.concatenate([jnp.where(s == 0, 0.0, vb_ref[...]), uv_ref[...], va_ref[...]], axis=0)
        pg = _conv_apply(eg, wg_ref, bg_ref[...], k, HALO, r + HALO)
        pv = _conv_apply(ev, wv_ref, bv_ref[...], k, HALO, r + HALO)
        de = jnp.concatenate([d_ref[...], jnp.where(s == ns - 1, 0.0, da_ref[...])], axis=0)
        sg = jax.nn.sigmoid(pg)
        dpg = de * pv * (sg * (1.0 + pg * (1.0 - sg)))
        dpv = de * (pg * sg)
        dug_ref[...] = _conv_apply_t(dpg, wg_ref, k, r).astype(dug_ref.dtype)
        duv_ref[...] = _conv_apply_t(dpv, wv_ref, k, r).astype(duv_ref.dtype)

        @pl.when((b == 0) & (s == 0))
        def _():
            dwg_ref[...] = jnp.zeros_like(dwg_ref)
            dwv_ref[...] = jnp.zeros_like(dwv_ref)
            dbg_ref[...] = jnp.zeros_like(dbg_ref)
            dbv_ref[...] = jnp.zeros_like(dbv_ref)

        _conv_dw(dpg[:r], eg, k, r, dwg_ref)
        _conv_dw(dpv[:r], ev, k, r, dwv_ref)
        dbg_ref[...] += jnp.sum(dpg[:r], axis=0, keepdims=True)
        dbv_ref[...] += jnp.sum(dpv[:r], axis=0, keepdims=True)

    wshape = jax.ShapeDtypeStruct((SUBLANE, ch), F32)
    bshape = jax.ShapeDtypeStruct((1, ch), F32)
    return pl.pallas_call(
        body, name=name, grid=(ch // tc, bsz, ns),
        in_specs=[main, before, after, main, before, after, main, after, wspec, wspec, bspec, bspec],
        out_specs=[main, main, wspec, wspec, bspec, bspec],
        out_shape=[jax.ShapeDtypeStruct(ug.shape, BF16), jax.ShapeDtypeStruct(ug.shape, BF16), wshape, wshape, bshape, bshape],
        compiler_params=_params(("arbitrary", "arbitrary", "arbitrary")),
    )(ug, ug, ug, uv, uv, uv, dact, dact, wg, wv, bg, bv)


GN_ROWS = 256
GN_GROUP = SSM_D_INNER // SSM_GROUPS


def gated_norm_fwd(y, z, w, name):
    t, d = y.shape

    def body(y_ref, z_ref, w_ref, o_ref):
        for g in range(d // GN_GROUP):
            sl = slice(g * GN_GROUP, (g + 1) * GN_GROUP)
            zv = z_ref[:, sl]
            gv = y_ref[:, sl] * (zv * jax.nn.sigmoid(zv))
            r = lax.rsqrt(jnp.mean(gv * gv, axis=-1, keepdims=True) + SSM_NORM_EPS)
            o_ref[:, sl] = (gv * r * w_ref[:, sl]).astype(o_ref.dtype)

    row = pl.BlockSpec((GN_ROWS, d), lambda i: (i, 0))
    vec = pl.BlockSpec((1, d), lambda i: (0, 0))
    return pl.pallas_call(
        body, name=name, grid=(t // GN_ROWS,), in_specs=[row, row, vec], out_specs=row,
        out_shape=jax.ShapeDtypeStruct((t, d), BF16), compiler_params=_params(("parallel",)),
    )(y, z, w)


def gated_norm_bwd(dgn, y, z, w, name):
    t, d = y.shape

    def body(d_ref, y_ref, z_ref, w_ref, dy_ref, dz_ref, dw_ref):
        i = pl.program_id(0)

        @pl.when(i == 0)
        def _():
            dw_ref[...] = jnp.zeros_like(dw_ref)

        for g in range(d // GN_GROUP):
            sl = slice(g * GN_GROUP, (g + 1) * GN_GROUP)
            zv, yv, dv = z_ref[:, sl], y_ref[:, sl], d_ref[:, sl]
            sg = jax.nn.sigmoid(zv)
            sz = zv * sg
            gv = yv * sz
            r = lax.rsqrt(jnp.mean(gv * gv, axis=-1, keepdims=True) + SSM_NORM_EPS)
            gh = gv * r
            dgh = dv * w_ref[:, sl]
            dw_ref[:, sl] += jnp.sum(dv * gh, axis=0, keepdims=True)
            dg = r * (dgh - gh * jnp.mean(dgh * gh, axis=-1, keepdims=True))
            dy_ref[:, sl] = dg * sz
            dz_ref[:, sl] = (dg * yv * (sg * (1.0 + zv * (1.0 - sg)))).astype(dz_ref.dtype)

    row = pl.BlockSpec((GN_ROWS, d), lambda i: (i, 0))
    vec = pl.BlockSpec((1, d), lambda i: (0, 0))
    return pl.pallas_call(
        body, name=name, grid=(t // GN_ROWS,), in_specs=[row, row, row, vec], out_specs=[row, row, vec],
        out_shape=[jax.ShapeDtypeStruct((t, d), F32), jax.ShapeDtypeStruct((t, d), BF16), jax.ShapeDtypeStruct((1, d), F32)],
        compiler_params=_params(("arbitrary",)),
    )(dgn, y, z, w)


def ple_fwd(h, gpre, pp, name):
    t, d = h.shape

    def body(h_ref, g_ref, p_ref, o_ref):
        o_ref[...] = h_ref[...] + jax.nn.sigmoid(g_ref[...]) * p_ref[...]

    row = pl.BlockSpec((ROW_TILE, d), lambda i: (i, 0))
    return pl.pallas_call(
        body, name=name, grid=(t // ROW_TILE,), in_specs=[row, row, row], out_specs=row,
        out_shape=jax.ShapeDtypeStruct((t, d), F32), compiler_params=_params(("parallel",)),
    )(h, gpre, pp)


def ple_bwd(dh, gpre, pp, name):
    t, d = dh.shape

    def body(d_ref, g_ref, p_ref, dg_ref, dp_ref):
        sg = jax.nn.sigmoid(g_ref[...])
        dv = d_ref[...]
        dg_ref[...] = (dv * p_ref[...] * sg * (1.0 - sg)).astype(dg_ref.dtype)
        dp_ref[...] = (dv * sg).astype(dp_ref.dtype)

    row = pl.BlockSpec((ROW_TILE, d), lambda i: (i, 0))
    return pl.pallas_call(
        body, name=name, grid=(t // ROW_TILE,), in_specs=[row, row, row], out_specs=[row, row],
        out_shape=[jax.ShapeDtypeStruct((t, d), BF16), jax.ShapeDtypeStruct((t, d), BF16)],
        compiler_params=_params(("parallel",)),
    )(dh, gpre, pp)


def loss_head(h, gain, target, name):
    t, d = h.shape

    def body(x_ref, g_ref, t_ref, l_ref, dx_ref, dg_ref):
        i = pl.program_id(0)
        xv = x_ref[...]
        r = lax.rsqrt(jnp.mean(xv * xv, axis=-1, keepdims=True) + NORM_EPS)
        xh = xv * r
        err = xh * g_ref[...] - t_ref[...]
        part = 0.5 * jnp.sum(jnp.mean(err * err, axis=-1, keepdims=True), axis=0, keepdims=True)
        dyv = err * (1.0 / d)
        dxh = dyv * g_ref[...]
        dx_ref[...] = r * (dxh - xh * jnp.mean(dxh * xh, axis=-1, keepdims=True))

        @pl.when(i == 0)
        def _():
            dg_ref[...] = jnp.zeros_like(dg_ref)
            l_ref[...] = jnp.zeros_like(l_ref)

        dg_ref[...] += jnp.sum(dyv * xh, axis=0, keepdims=True)
        l_ref[...] += jnp.broadcast_to(part, l_ref.shape)

    row = pl.BlockSpec((ROW_TILE, d), lambda i: (i, 0))
    vec = pl.BlockSpec((1, d), lambda i: (0, 0))
    lspec = pl.BlockSpec((1, LANE), lambda i: (0, 0))
    return pl.pallas_call(
        body, name=name, grid=(t // ROW_TILE,), in_specs=[row, vec, row], out_specs=[lspec, row, vec],
        out_shape=[jax.ShapeDtypeStruct((1, LANE), F32), jax.ShapeDtypeStruct((t, d), F32), jax.ShapeDtypeStruct((1, d), F32)],
        compiler_params=_params(("arbitrary",)),
    )(h, gain, target)


CHUNK = SSM_CHUNK
HPG = SSM_HEADS // SSM_GROUPS
GW = HPG * SSM_HEAD_DIM
NEG = -1e30


def _split3(x):
    hi = x.astype(BF16)
    r1 = x - hi.astype(F32)
    mid = r1.astype(BF16)
    return hi, mid, (r1 - mid.astype(F32)).astype(BF16)


def _ssd_dt_terms(dtr_ref, bias_ref, alog_ref, g):
    shift = (LANE - HPG * g) % LANE
    li = lax.broadcasted_iota(jnp.int32, (CHUNK, CHUNK), 0)
    si = lax.broadcasted_iota(jnp.int32, (CHUNK, CHUNK), 1)
    x0 = pltpu.roll(dtr_ref[...] + bias_ref[...], shift, 1)
    dt = jnp.maximum(x0, 0.0) + jnp.log1p(jnp.exp(-jnp.abs(x0)))
    a = pltpu.roll(jnp.broadcast_to(-jnp.exp(alog_ref[...]), (CHUNK, LANE)), shift, 1)
    adt = dt * a
    acs = lax.dot_general((li >= si).astype(F32), adt, NN, precision=lax.Precision.HIGHEST, preferred_element_type=F32)
    last = jnp.broadcast_to(acs[CHUNK - 1:CHUNK, :], (CHUNK, LANE))
    e_last = jnp.exp(last)
    hr = lax.broadcasted_iota(jnp.int32, (LANE, GW), 0)
    hc = lax.broadcasted_iota(jnp.int32, (LANE, GW), 1)
    spread = (hc // SSM_HEAD_DIM == hr).astype(BF16)
    stack = jnp.concatenate([dt, jnp.exp(last - acs), jnp.exp(acs), e_last], axis=0)
    wide = jnp.dot(jnp.concatenate(_split3(stack), axis=1), jnp.concatenate([spread] * 3, axis=0), preferred_element_type=F32)
    return dict(li=li, si=si, x0=x0, dt=dt, a=a, acs=acs, acs_t=acs.T, e_last=e_last,
                dt_x=wide[:CHUNK], w_x=wide[CHUNK:2 * CHUNK], e_x=wide[2 * CHUNK:3 * CHUNK], el_x=wide[3 * CHUNK:])


def _ssd_pair_halves(parts):
    low = lax.broadcasted_iota(jnp.int32, (CHUNK, LANE), 1) < SSM_HEAD_DIM
    return jnp.concatenate([jnp.where(low, parts[2 * p], parts[2 * p + 1]) for p in range(HPG // 2)], axis=1)


def _ssd_specs(nc, rev):
    def ci(c):
        return nc - 1 - c if rev else c
    nb = SSM_D_INNER // LANE
    xs = pl.BlockSpec((None, CHUNK, GW), lambda g, b, c: (b, ci(c), g))
    bm = pl.BlockSpec((None, CHUNK, LANE), lambda g, b, c: (b, ci(c), nb + g))
    cm = pl.BlockSpec((None, CHUNK, LANE), lambda g, b, c: (b, ci(c), nb + SSM_GROUPS + g))
    dtr = pl.BlockSpec((None, CHUNK, LANE), lambda g, b, c: (b, ci(c), 0))
    vec = pl.BlockSpec((1, LANE), lambda g, b, c: (0, 0))
    dexp = pl.BlockSpec((1, GW), lambda g, b, c: (0, g))
    st = pl.BlockSpec((None, None, CHUNK, GW), lambda g, b, c: (b, ci(c), 0, g))
    return xs, bm, cm, dtr, vec, dexp, st


def ssd_fwd(xc, dtr, dt_bias, a_log, d_exp, name):
    bsz, seq, _ = xc.shape
    nc = seq // CHUNK
    xs_s, bm_s, cm_s, dtr_s, vec_s, dexp_s, st_s = _ssd_specs(nc, False)

    def body(xs_ref, b_ref, c_ref, dtr_ref, bias_ref, alog_ref, dexp_ref, y_ref, sp_ref, state_ref):
        g, c = pl.program_id(0), pl.program_id(2)

        @pl.when(c == 0)
        def _():
            state_ref[...] = jnp.zeros_like(state_ref)

        t = _ssd_dt_terms(dtr_ref, bias_ref, alog_ref, g)
        tril = t["li"] >= t["si"]
        xs = xs_ref[...]
        b16 = b_ref[...].astype(BF16)
        c16 = c_ref[...].astype(BF16)
        bt16 = b_ref[...].T.astype(BF16)
        cb = lax.dot_general(c16, b16, NT, preferred_element_type=F32)
        sp = state_ref[...]
        sp_ref[...] = sp
        cs = jnp.dot(c16, sp.astype(BF16), preferred_element_type=F32)
        xd = xs * t["dt_x"]
        xd16 = xd.astype(BF16)
        y_parts = []
        for r in range(HPG):
            pair = slice((r // 2) * LANE, (r // 2 + 1) * LANE)
            diff = t["acs"][:, r:r + 1] - t["acs_t"][r:r + 1, :]
            m16 = (cb * jnp.exp(jnp.where(tril, diff, NEG))).astype(BF16)
            y_parts.append(jnp.dot(m16, xd16[:, pair], preferred_element_type=F32))
        y_ref[...] = _ssd_pair_halves(y_parts) + t["e_x"] * cs + xs * dexp_ref[...]
        xw16 = (xd * t["w_x"]).astype(BF16)
        state_ref[...] = sp * t["el_x"] + jnp.dot(bt16, xw16, preferred_element_type=F32)

    return pl.pallas_call(
        body, name=name, grid=(SSM_GROUPS, bsz, nc),
        in_specs=[xs_s, bm_s, cm_s, dtr_s, vec_s, vec_s, dexp_s],
        out_specs=[xs_s, st_s],
        out_shape=[jax.ShapeDtypeStruct((bsz, seq, SSM_D_INNER), F32),
                   jax.ShapeDtypeStruct((bsz, nc, SSM_STATE, SSM_D_INNER), F32)],
        scratch_shapes=[pltpu.VMEM((SSM_STATE, GW), F32)],
        compiler_params=_params(("arbitrary", "arbitrary", "arbitrary")),
    )(xc, xc, xc, dtr, dt_bias, a_log, d_exp)


def ssd_bwd(xc, dtr, dt_bias, a_log, d_exp, dy, sprev, name):
    bsz, seq, _ = xc.shape
    nc = seq // CHUNK
    xs_s, bm_s, cm_s, dtr_s, vec_s, dexp_s, st_s = _ssd_specs(nc, True)
    grp = pl.BlockSpec((None, CHUNK, LANE), lambda g, b, c: (b, nc - 1 - c, g))
    acc = pl.BlockSpec((None, 1, LANE), lambda g, b, c: (g, 0, 0))

    def body(xs_ref, b_ref, c_ref, dtr_ref, bias_ref, alog_ref, dexp_ref, dy_ref, sp_ref,
             dxs_ref, db_ref, dc_ref, ddtr_ref, da_ref, dbias_ref, ddexp_ref, dstate_ref):
        g, b, c = pl.program_id(0), pl.program_id(1), pl.program_id(2)

        @pl.when(c == 0)
        def _():
            dstate_ref[...] = jnp.zeros_like(dstate_ref)

        @pl.when((b == 0) & (c == 0))
        def _():
            da_ref[...] = jnp.zeros_like(da_ref)
            dbias_ref[...] = jnp.zeros_like(dbias_ref)
            ddexp_ref[...] = jnp.zeros_like(ddexp_ref)

        t = _ssd_dt_terms(dtr_ref, bias_ref, alog_ref, g)
        li, si = t["li"], t["si"]
        tril, triu = li >= si, li <= si
        row_is_last = li[:, :1] == CHUNK - 1
        xs, dy = xs_ref[...], dy_ref[...]
        b16 = b_ref[...].astype(BF16)
        c16 = c_ref[...].astype(BF16)
        ct16 = c_ref[...].T.astype(BF16)
        cb = lax.dot_general(c16, b16, NT, preferred_element_type=F32)
        cbt = lax.dot_general(b16, c16, NT, preferred_element_type=F32)
        sp, ds = sp_ref[...], dstate_ref[...]
        sp16, ds16 = sp.astype(BF16), ds.astype(BF16)
        cs = jnp.dot(c16, sp16, preferred_element_type=F32)
        bds = jnp.dot(b16, ds16, preferred_element_type=F32)
        spds = jnp.sum(sp * ds, axis=0, keepdims=True)
        xd = xs * t["dt_x"]
        xw = xd * t["w_x"]
        edy = dy * t["e_x"]
        xd16, dy16, xw16, edy16 = xd.astype(BF16), dy.astype(BF16), xw.astype(BF16), edy.astype(BF16)
        low = lax.broadcasted_iota(jnp.int32, (CHUNK, LANE), 1) < SSM_HEAD_DIM
        zero = jnp.zeros((CHUNK, CHUNK), F32)
        dsum_g, dsum_gt, dacs = zero, zero, zero
        dxd_parts = []
        for r in range(HPG):
            pair = slice((r // 2) * LANE, (r // 2 + 1) * LANE)
            diff = t["acs"][:, r:r + 1] - t["acs_t"][r:r + 1, :]
            lam = jnp.exp(jnp.where(tril, diff, NEG))
            lam_t = jnp.exp(jnp.where(triu, -diff, NEG))
            m, m_t = cb * lam, cbt * lam_t
            dyh16 = jnp.where(low if r % 2 == 0 else ~low, dy16[:, pair], jnp.zeros_like(dy16[:, pair]))
            dxd_parts.append(jnp.dot(m_t.astype(BF16), dy16[:, pair], preferred_element_type=F32))
            dm = lax.dot_general(dyh16, xd16[:, pair], NT, preferred_element_type=F32)
            dm_t = lax.dot_general(xd16[:, pair], dyh16, NT, preferred_element_type=F32)
            dsum_g = dsum_g + dm * lam
            dsum_gt = dsum_gt + dm_t * lam_t
            dacs = dacs + jnp.sum(dm * m - dm_t * m_t, axis=1, keepdims=True) * (si[:1, :] == r).astype(F32)
        xwb = xw * bds
        dxd = _ssd_pair_halves(dxd_parts) + t["w_x"] * bds
        dxs_ref[...] = dxd * t["dt_x"] + dy * dexp_ref[...]
        gr = lax.broadcasted_iota(jnp.int32, (GW, LANE), 0)
        gc = lax.broadcasted_iota(jnp.int32, (GW, LANE), 1)
        gather = (gr // SSM_HEAD_DIM == gc).astype(BF16)
        tail = jnp.concatenate([jnp.sum(xwb, axis=0, keepdims=True), spds, jnp.zeros((SUBLANE - 2, GW), F32)], axis=0)
        stack = jnp.concatenate([edy * cs - xwb, dxd * xs, tail], axis=0)
        hs = sum(jnp.dot(part, gather, preferred_element_type=F32) for part in _split3(stack))
        last = hs[2 * CHUNK:2 * CHUNK + 1] + t["e_last"][0:1, :] * hs[2 * CHUNK + 1:2 * CHUNK + 2]
        dacs = dacs + hs[:CHUNK] + jnp.where(row_is_last, last, 0.0)
        ddt_dir = hs[CHUNK:2 * CHUNK]
        dc_ref[...] = (jnp.dot(dsum_g.astype(BF16), b16, preferred_element_type=F32)
                       + lax.dot_general(edy16, sp16, NT, preferred_element_type=F32))
        db_ref[...] = (jnp.dot(dsum_gt.astype(BF16), c16, preferred_element_type=F32)
                       + lax.dot_general(xw16, ds16, NT, preferred_element_type=F32))
        dstate_ref[...] = ds * t["el_x"] + jnp.dot(ct16, edy16, preferred_element_type=F32)
        dadt = lax.dot_general(triu.astype(F32), dacs, NN, precision=lax.Precision.HIGHEST, preferred_element_type=F32)
        head_lane = si < HPG
        ddtr = jnp.where(head_lane, (dadt * t["a"] + ddt_dir) * jax.nn.sigmoid(t["x0"]), 0.0)
        ddtr_ref[...] = ddtr
        da_ref[...] += jnp.sum(jnp.where(head_lane, dadt * t["dt"] * t["a"], 0.0), axis=0, keepdims=True)
        dbias_ref[...] += jnp.sum(ddtr, axis=0, keepdims=True)
        ddexp_ref[...] += jnp.sum(dy * xs, axis=0, keepdims=True)

    return pl.pallas_call(
        body, name=name, grid=(SSM_GROUPS, bsz, nc),
        in_specs=[xs_s, bm_s, cm_s, dtr_s, vec_s, vec_s, dexp_s, xs_s, st_s],
        out_specs=[xs_s, grp, grp, grp, acc, acc, dexp_s],
        out_shape=[jax.ShapeDtypeStruct((bsz, seq, SSM_D_INNER), F32),
                   jax.ShapeDtypeStruct((bsz, seq, SSM_GROUPS * SSM_STATE), F32),
                   jax.ShapeDtypeStruct((bsz, seq, SSM_GROUPS * SSM_STATE), F32),
                   jax.ShapeDtypeStruct((bsz, seq, SSM_GROUPS * LANE), F32),
                   jax.ShapeDtypeStruct((SSM_GROUPS, 1, LANE), F32),
                   jax.ShapeDtypeStruct((SSM_GROUPS, 1, LANE), F32),
                   jax.ShapeDtypeStruct((1, SSM_D_INNER), F32)],
        scratch_shapes=[pltpu.VMEM((SSM_STATE, GW), F32)],
        compiler_params=_params(("arbitrary", "arbitrary", "arbitrary")),
    )(xc, xc, xc, dtr, dt_bias, a_log, d_exp, dy, sprev)


SB_TQ = 512
SB_TK = 128
SB_SCALE = SB_HEAD_DIM ** -0.5


def _split_dot(x, u16):
    hi = x.astype(BF16)
    lo = (x - hi.astype(F32)).astype(BF16)
    return jnp.dot(hi, u16, preferred_element_type=F32) + jnp.dot(lo, u16, preferred_element_type=F32)


def _sb_scores(qh, kj, mask):
    z = lax.dot_general(qh, kj, NT, preferred_element_type=F32)
    l1p = jnp.log(1.0 + jnp.exp(-jnp.abs(z)))
    lb = jnp.minimum(z, 0.0) - l1p
    lk = -jnp.maximum(z, 0.0) - l1p
    if mask is not None:
        lk = jnp.where(mask, lk, 0.0)
    return lb, lk


def _sb_diag_masks(tq):
    r = lax.broadcasted_iota(jnp.int32, (tq, SB_TK), 0)
    c = lax.broadcasted_iota(jnp.int32, (tq, SB_TK), 1)
    return [r > c + d * SB_TK for d in range(tq // SB_TK)]


def _sb_heads(x, lane):
    return [jnp.where((lane // SB_HEAD_DIM) == hh, x, jnp.zeros_like(x)) for hh in range(2)]


def sb_attention_fwd(q, kv, name):
    bsz, seq, width = q.shape
    tq = min(SB_TQ, seq)
    npair = width // LANE
    qspec = pl.BlockSpec((None, tq, LANE), lambda b, p, i: (b, i, p))
    kspec = pl.BlockSpec((None, seq, LANE), lambda b, p, i: (b, 0, p))
    vspec = pl.BlockSpec((None, seq, LANE), lambda b, p, i: (b, 0, npair + p))

    def body(q_ref, k_ref, v_ref, o_ref, l_ref):
        i = pl.program_id(2)
        ndiag = tq // SB_TK
        nfull = i * ndiag
        lane = lax.broadcasted_iota(jnp.int32, (tq, LANE), 1)
        ui = lax.broadcasted_iota(jnp.int32, (SB_TK, SB_TK), 0)
        uj = lax.broadcasted_iota(jnp.int32, (SB_TK, SB_TK), 1)
        u_excl = (ui > uj).astype(BF16)
        qs = _sb_heads(q_ref[...] * SB_SCALE, lane)

        def tile(j, carry, mask):
            rows = pl.ds(pl.multiple_of(j * SB_TK, SB_TK), SB_TK)
            kj, vj = k_ref[rows, :], v_ref[rows, :]
            new = []
            for hh in range(2):
                acc, run = carry[hh]
                lb, lk = _sb_scores(qs[hh], kj, mask)
                w = jnp.exp(lb + _split_dot(lk, u_excl) + run)
                if mask is not None:
                    w = jnp.where(mask, w, 0.0)
                acc = acc + jnp.dot(w.astype(BF16), vj, preferred_element_type=F32)
                new.append((acc, run + jnp.sum(lk, axis=1, keepdims=True)))
            return tuple(new)

        carry = tuple((jnp.zeros((tq, LANE), F32), jnp.zeros((tq, 1), F32)) for _ in range(2))
        masks = _sb_diag_masks(tq)
        for dd in reversed(range(ndiag)):
            carry = tile(nfull + dd, carry, masks[dd])
        def trip(jj, cr):
            for u in range(ndiag):
                cr = tile(nfull - 1 - u - ndiag * jj, cr, None)
            return cr

        carry = lax.fori_loop(0, i, trip, carry)
        in0 = (lane // SB_HEAD_DIM) == 0
        o_ref[...] = jnp.where(in0, carry[0][0], carry[1][0]).astype(o_ref.dtype)
        l_ref[...] = jnp.where(in0, carry[0][1], carry[1][1])

    return pl.pallas_call(
        body, name=name, grid=(bsz, npair, seq // tq),
        in_specs=[qspec, kspec, vspec], out_specs=[qspec, qspec],
        out_shape=[jax.ShapeDtypeStruct(q.shape, BF16), jax.ShapeDtypeStruct(q.shape, F32)],
        compiler_params=_params(("parallel", "parallel", "arbitrary")),
    )(q, kv, kv)


def sb_attention_bwd(q, kv, ltot, do, dk_in, dv_in, name):
    bsz, seq, width = q.shape
    tq = min(SB_TQ, seq)
    npair = width // LANE
    qspec = pl.BlockSpec((None, tq, LANE), lambda b, p, i: (b, i, p))
    kspec = pl.BlockSpec((None, seq, LANE), lambda b, p, i: (b, 0, p))
    vspec = pl.BlockSpec((None, seq, LANE), lambda b, p, i: (b, 0, npair + p))

    def body(q_ref, k_ref, v_ref, l_ref, do_ref, dki_ref, dvi_ref, dq_ref, dk_ref, dv_ref):
        i = pl.program_id(2)

        @pl.when(i == 0)
        def _():
            dk_ref[...] = dki_ref[...]
            dv_ref[...] = dvi_ref[...]

        ndiag = tq // SB_TK
        nfull = i * ndiag
        lane = lax.broadcasted_iota(jnp.int32, (tq, LANE), 1)
        ui = lax.broadcasted_iota(jnp.int32, (SB_TK, SB_TK), 0)
        uj = lax.broadcasted_iota(jnp.int32, (SB_TK, SB_TK), 1)
        u_le = (ui <= uj).astype(BF16)
        u_lt = (ui < uj).astype(BF16)
        qs = _sb_heads(q_ref[...] * SB_SCALE, lane)
        dos = _sb_heads(do_ref[...], lane)
        ltots = [l_ref[:, hh * SB_HEAD_DIM:hh * SB_HEAD_DIM + 1] for hh in range(2)]

        def tile(j, carry, mask, r0=0):
            rows = pl.ds(pl.multiple_of(j * SB_TK, SB_TK), SB_TK)
            kj, vj = k_ref[rows, :], v_ref[rows, :]
            new, dk_t, dv_t = [], None, None
            for hh in range(2):
                acc, run, run_a = carry[hh]
                qh, doh = qs[hh][r0:], dos[hh][r0:]
                lb, lk = _sb_scores(qh, kj, mask)
                w = jnp.exp(lb + (ltots[hh][r0:] - run[r0:] - _split_dot(lk, u_le)))
                if mask is not None:
                    w = jnp.where(mask, w, 0.0)
                a = lax.dot_general(doh, vj, NT, preferred_element_type=F32) * w
                ca = run_a[r0:] + _split_dot(a, u_lt)
                sg = jnp.exp(lb)
                dz = a * (1.0 - sg) - ca * sg
                if mask is not None:
                    dz = jnp.where(mask, dz, 0.0)
                dz16 = dz.astype(BF16)
                part = [acc[r0:] + jnp.dot(dz16, kj, preferred_element_type=F32),
                        run[r0:] + jnp.sum(lk, axis=1, keepdims=True), run_a[r0:] + jnp.sum(a, axis=1, keepdims=True)]
                if r0:
                    part = [jnp.concatenate([old[:r0], p], axis=0) for old, p in zip((acc, run, run_a), part)]
                dkh = lax.dot_general(dz16, qh, TN, preferred_element_type=F32)
                dvh = lax.dot_general(w.astype(BF16), doh, TN, preferred_element_type=F32)
                dk_t = dkh if dk_t is None else dk_t + dkh
                dv_t = dvh if dv_t is None else dv_t + dvh
                new.append(tuple(part))
            dk_ref[rows, :] += dk_t
            dv_ref[rows, :] += dv_t
            return tuple(new)

        zcol = jnp.zeros((tq, 1), F32)
        carry = tuple((jnp.zeros((tq, LANE), F32), zcol, zcol) for _ in range(2))
        def trip(j, cr):
            for u in range(ndiag):
                cr = tile(ndiag * j + u, cr, None)
            return cr

        carry = lax.fori_loop(0, i, trip, carry)
        masks = _sb_diag_masks(tq)
        for dd in range(ndiag):
            carry = tile(nfull + dd, carry, masks[dd][dd * SB_TK:], dd * SB_TK)
        in0 = (lane // SB_HEAD_DIM) == 0
        dq_ref[...] = (jnp.where(in0, carry[0][0], carry[1][0]) * SB_SCALE).astype(dq_ref.dtype)

    return pl.pallas_call(
        body, name=name, grid=(bsz, npair, seq // tq),
        in_specs=[qspec, kspec, vspec, qspec, qspec, kspec, kspec], out_specs=[qspec, kspec, kspec],
        out_shape=[jax.ShapeDtypeStruct(q.shape, BF16), jax.ShapeDtypeStruct(q.shape, F32), jax.ShapeDtypeStruct(q.shape, F32)],
        compiler_params=_params(("parallel", "parallel", "arbitrary")),
    )(q, kv, kv, ltot, do, dk_in, dv_in)


def adamw(w, parts, m, v, name):
    r, c = w.shape
    tr = r
    for cand in (256, 176, 128, 64, 32, 16, 8):
        if r % cand == 0:
            tr = cand
            break
    bc1 = 1.0 - ADAM_B1 ** ADAM_STEP
    bc2 = 1.0 - ADAM_B2 ** ADAM_STEP

    def body(w_ref, p_ref, m_ref, v_ref, g_ref, d_ref, mo_ref, vo_ref):
        g = p_ref[0].astype(F32)
        for j in range(1, N_DEV):
            g = g + p_ref[j].astype(F32)
        mn = ADAM_B1 * m_ref[...] + (1.0 - ADAM_B1) * g
        vn = ADAM_B2 * v_ref[...] + (1.0 - ADAM_B2) * (g * g)
        g_ref[...] = g
        mo_ref[...] = mn
        vo_ref[...] = vn
        d_ref[...] = -ADAM_LR * ((mn / bc1) / (jnp.sqrt(vn / bc2) + ADAM_EPS) + ADAM_WD * w_ref[...])

    blk = pl.BlockSpec((tr, c), lambda i: (i, 0))
    pblk = pl.BlockSpec((N_DEV, tr, c), lambda i: (0, i, 0))
    shp = jax.ShapeDtypeStruct((r, c), F32)
    return pl.pallas_call(
        body, name=name, grid=(r // tr,), in_specs=[blk, pblk, blk, blk], out_specs=[blk, blk, blk, blk],
        out_shape=[shp, shp, shp, shp], compiler_params=_params(("parallel",)),
    )(w, parts, m, v)


MESH_ID = pl.DeviceIdType.MESH
ANY = pl.BlockSpec(memory_space=pl.ANY)


def all_gather(xs, name):
    n = len(xs)

    def body(*refs):
        x_refs, out_refs = refs[:n], refs[n:2 * n]
        send_sems, recv_sems, local_sems = refs[2 * n:]
        mx, my, mc = lax.axis_index("x"), lax.axis_index("y"), lax.axis_index("c")
        me, sibling = (mx, my, mc), (mx, my, 1 - mc)
        chips = [(1 - mx, my), (mx, 1 - my), (1 - mx, 1 - my)]

        def slot(a, px, py, pc):
            return out_refs[a].at[4 * px + 2 * py + pc]

        def copy(a, k, block, to, src=None):
            return pltpu.make_async_remote_copy(
                src_ref=slot(a, *block) if src is None else src, dst_ref=slot(a, *block),
                send_sem=send_sems.at[7 * a + k], recv_sem=recv_sems.at[7 * a + k], device_id=to, device_id_type=MESH_ID)

        mine = [pltpu.make_async_copy(x_refs[a], slot(a, *me), local_sems.at[a]) for a in range(n)]
        for cp in mine:
            cp.start()
        first = []
        for a in range(n):
            first.append(copy(a, 0, me, sibling, src=x_refs[a]))
            first += [copy(a, 1 + j, me, (*chip, mc), src=x_refs[a]) for j, chip in enumerate(chips)]
        for cp in first:
            cp.start()
        passed = []
        for j, chip in enumerate(chips):
            for a in range(n):
                copy(a, 1 + j, (*chip, mc), me).wait_recv()
                cp = copy(a, 4 + j, (*chip, mc), sibling)
                cp.start()
                passed.append(cp)
        for a in range(n):
            copy(a, 0, sibling, me).wait_recv()
            for j, chip in enumerate(chips):
                copy(a, 4 + j, (*chip, 1 - mc), me).wait_recv()
        for cp in first + passed:
            cp.wait_send()
        for cp in mine:
            cp.wait()

    return pl.pallas_call(
        body, name=name, out_shape=[jax.ShapeDtypeStruct((N_DEV,) + x.shape, x.dtype) for x in xs],
        in_specs=[ANY] * n, out_specs=[ANY] * n,
        scratch_shapes=[pltpu.SemaphoreType.DMA((7 * n,)), pltpu.SemaphoreType.DMA((7 * n,)), pltpu.SemaphoreType.DMA((n,))],
    )(*xs)


HBM_SPEC = pl.BlockSpec(memory_space=pltpu.HBM)
SEM_SPEC = pl.BlockSpec(memory_space=pltpu.SEMAPHORE)
DATAFLOW = pltpu.SideEffectType.DATAFLOW_SIDE_EFFECTING


def _peer(j, mx, my, mc):
    px = 1 - mx if j & 4 else mx
    py = 1 - my if j & 2 else my
    pc = 1 - mc if j & 1 else mc
    return (px, py, pc), 4 * px + 2 * py + pc


def _split_copy(kind, src_ref, land_ref, send_sems, recv_sems, a, j, mx, my, mc):
    dev, peer = _peer(j, mx, my, mc)
    me = 4 * mx + 2 * my + mc
    src = src_ref if kind == "gather" else src_ref.at[peer]
    k = 7 * a + j - 1
    return pltpu.make_async_remote_copy(src_ref=src, dst_ref=land_ref.at[me], send_sem=send_sems.at[k],
                                        recv_sem=recv_sems.at[k], device_id=dev, device_id_type=MESH_ID), peer


def exchange_start(kind, srcs, name):
    n = len(srcs)
    blocks = [s.shape if kind == "gather" else s.shape[1:] for s in srcs]
    lands = [lax.empty((N_DEV,) + tuple(b), s.dtype) for b, s in zip(blocks, srcs)]

    def body(*refs):
        src_refs, land_refs = refs[:n], refs[n:2 * n]
        send_sems, recv_sems = refs[2 * n], refs[2 * n + 1]
        token = refs[-1]
        mx, my, mc = lax.axis_index("x"), lax.axis_index("y"), lax.axis_index("c")
        for j in range(1, N_DEV):
            for a in range(n):
                _split_copy(kind, src_refs[a], land_refs[a], send_sems, recv_sems, a, j, mx, my, mc)[0].start()
        token[...] = jnp.zeros_like(token)

    hbm = lambda v: pltpu.HBM(v.shape, v.dtype)
    outs = pl.pallas_call(
        body, name=name,
        out_shape=(pltpu.SemaphoreType.DMA((7 * n,)), pltpu.SemaphoreType.DMA((7 * n,)), *[hbm(s) for s in srcs],
                   *[hbm(l) for l in lands], jax.ShapeDtypeStruct((SUBLANE, LANE), F32)),
        in_specs=[HBM_SPEC] * (2 * n),
        out_specs=(SEM_SPEC, SEM_SPEC, *[HBM_SPEC] * (2 * n), pl.BlockSpec(memory_space=pltpu.VMEM)),
        input_output_aliases={a: 2 + a for a in range(2 * n)},
        compiler_params=pltpu.CompilerParams(has_side_effects=DATAFLOW),
    )(*[pltpu.with_memory_space_constraint(v, pltpu.HBM) for v in list(srcs) + lands])
    return dict(kind=kind, n=n, send=outs[0], recv=outs[1], srcs=outs[2:2 + n], lands=outs[2 + n:2 + 2 * n], token=outs[-1])


def exchange_wait(h, after, name):
    n, kind = h["n"], h["kind"]

    def body(*refs):
        src_refs, land_refs = refs[:n], refs[n:2 * n]
        send_sems, recv_sems = refs[2 * n], refs[2 * n + 1]
        mx, my, mc = lax.axis_index("x"), lax.axis_index("y"), lax.axis_index("c")
        for j in range(1, N_DEV):
            for a in range(n):
                cp, peer = _split_copy(kind, src_refs[a], land_refs[a], send_sems, recv_sems, a, j, mx, my, mc)
                cp.wait_send()
                pltpu.make_async_remote_copy(
                    src_ref=land_refs[a].at[peer], dst_ref=land_refs[a].at[peer], send_sem=send_sems.at[7 * a + j - 1],
                    recv_sem=recv_sems.at[7 * a + j - 1], device_id=_peer(j, mx, my, mc)[0], device_id_type=MESH_ID).wait_recv()

    hbm = lambda v: pltpu.HBM(v.shape, v.dtype)
    outs = pl.pallas_call(
        body, name=name,
        out_shape=tuple(hbm(v) for v in list(h["srcs"]) + list(h["lands"])),
        in_specs=[HBM_SPEC] * (2 * n) + [SEM_SPEC, SEM_SPEC, ANY],
        out_specs=tuple([HBM_SPEC] * (2 * n)),
        input_output_aliases={a: a for a in range(2 * n)},
        compiler_params=pltpu.CompilerParams(has_side_effects=DATAFLOW),
    )(*h["srcs"], *h["lands"], h["send"], h["recv"], after)
    return list(outs[n:])


PACK_COLS = 1024
PACK_SEG = 16 * PACK_COLS

BIG = {"ssm_in_proj": 2, "ssm_out_proj": 1, "w_kv": 1, "w_q": 1, "w_o": 1, "ffn_up": 2, "ffn_down": 1,
       "ple_gate": 1, "ple_proj": 2}
SMALL = {"ssm_conv_w": 2, "ssm_conv_b": 1, "ssm_norm": 1, "ffn_conv_w": 2}
REPL = ["attn_norm", "ffn_norm", "ple_norm", "ssm_dt_bias", "ssm_a_log", "ssm_d", "ffn_conv_b", "kv_norm", "final_norm"]


def _seg(n):
    return -(-n // PACK_SEG) * PACK_SEG


def _pack_rows(flats, lead=()):
    padded = [jnp.pad(f, [(0, 0)] * len(lead) + [(0, _seg(f.shape[-1]) - f.shape[-1])]) for f in flats]
    return jnp.concatenate(padded, axis=-1).reshape(*lead, -1, PACK_COLS)


def _unpack_rows(buf, sizes, lead):
    flat = buf.reshape(*lead, -1)
    out, off = [], 0
    for n in sizes:
        out.append(lax.slice_in_dim(flat, off, off + n, axis=len(lead)))
        off += _seg(n)
    return out


def _to_slots(full, axis):
    shp = full.shape
    blk = shp[axis] // N_DEV
    t = full.reshape(shp[:axis] + (N_DEV, blk) + shp[axis + 1:])
    return jnp.moveaxis(t, axis, 0).reshape(N_DEV, -1)


def _from_slots(slots, local_shape, axis):
    t = jnp.moveaxis(slots.reshape((N_DEV,) + tuple(local_shape)), 0, axis)
    shp = list(local_shape)
    shp[axis] *= N_DEV
    return t.reshape(shp)


def _as2d(a):
    return a.reshape(1, -1) if a.ndim == 1 else a.reshape(-1, a.shape[-1])


def _pad_rows(w, rows=SUBLANE):
    return jnp.pad(w, ((0, rows - w.shape[0]), (0, 0)))


def kernel(x, p, attn_norm, ffn_norm, ple_norm, ssm_in_proj, ssm_conv_w, ssm_conv_b, ssm_dt_bias, ssm_a_log, ssm_d, ssm_norm, ssm_out_proj, kv_norm, w_kv, w_q, w_o, ffn_up, ffn_conv_w, ffn_conv_b, ffn_down, ple_gate, ple_proj, final_norm, loss_target, m_attn_norm, m_ffn_norm, m_ple_norm, m_ssm_in_proj, m_ssm_conv_w, m_ssm_conv_b, m_ssm_dt_bias, m_ssm_a_log, m_ssm_d, m_ssm_norm, m_ssm_out_proj, m_kv_norm, m_w_kv, m_w_q, m_w_o, m_ffn_up, m_ffn_conv_w, m_ffn_conv_b, m_ffn_down, m_ple_gate, m_ple_proj, m_final_norm, v_attn_norm, v_ffn_norm, v_ple_norm, v_ssm_in_proj, v_ssm_conv_w, v_ssm_conv_b, v_ssm_dt_bias, v_ssm_a_log, v_ssm_d, v_ssm_norm, v_ssm_out_proj, v_kv_norm, v_w_kv, v_w_q, v_w_o, v_ffn_up, v_ffn_conv_w, v_ffn_conv_b, v_ffn_down, v_ple_gate, v_ple_proj, v_final_norm):
    given = dict(locals())
    wnames = list(BIG) + list(SMALL) + REPL
    bsz, seq, d = x.shape
    t = bsz * seq
    bs = lambda a: a.reshape(bsz, seq, a.shape[-1])
    fl = lambda a: a.reshape(t, a.shape[-1])

    big_local = [given[n] for n in BIG]
    small_local = [given[n] for n in SMALL]
    me = 4 * lax.axis_index("x") + 2 * lax.axis_index("y") + lax.axis_index("c")

    N_GROUPS = 2 * DEPTH

    def layer_items(k):
        i, j = k // 2, k // 2 - N_A
        if k % 2 == 0:
            return [("ssm_in_proj", i), ("ssm_out_proj", i)] if i < N_A else [("w_q", j), ("w_o", j)]
        shared = [("w_kv", None)] if i == N_A - 1 else []
        return [("ffn_up", i), ("ffn_down", i), ("ple_gate", i), ("ple_proj", i)] + shared

    def local_block(n, idx):
        return given[n] if idx is None else given[n][idx]

    def put_own(land, own):
        return lax.dynamic_update_slice(land, own[None], (me,) + (0,) * own.ndim)

    small_pack = _pack_rows([w.reshape(-1) for w in small_local])
    gather_src, gather_h = [], []
    for i in range(N_GROUPS):
        src = [local_block(n, idx).astype(BF16) for n, idx in layer_items(i)] + ([small_pack] if i == 0 else [])
        gather_src.append(src)
        gather_h.append(exchange_start("gather", src, f"ag_start{i}"))
    full = {n: {} for n in BIG}
    rows_of = lambda g: g.reshape(-1, g.shape[-1])
    cols_of = lambda g: jnp.transpose(g, (1, 0, 2)).reshape(g.shape[1], -1)

    def fetch_layer(i, after):
        lands = exchange_wait(gather_h[i], after, f"ag_wait{i}")
        got = [put_own(l, s) for l, s in zip(lands, gather_src[i])]
        for (n, idx), g in zip(layer_items(i), got):
            if n == "ffn_up":
                full[n][idx] = (cols_of(g[:N_DEV // 2]), cols_of(g[N_DEV // 2:]))
            else:
                full[n][idx] = cols_of(g) if BIG[n] == given[n].ndim - 1 else rows_of(g)
        if i == 0:
            for (n, ax), w, s in zip(SMALL.items(), small_local, _unpack_rows(got[-1], [w.size for w in small_local], (N_DEV,))):
                full[n] = _from_slots(s, w.shape, ax)

    row = lambda v: v.reshape(1, -1)
    pad_lane = lambda v: jnp.pad(v.reshape(1, -1), ((0, 0), (0, LANE - v.size)))
    h = x.reshape(t, d)
    saved = []
    kv = hnkv = h_kv = None
    for i in range(DEPTH):
        fetch_layer(2 * i, h)
        sv = {"h0": h}
        gain = row(attn_norm[i])
        if i == 0:
            gain = gain + sum(hd["token"][0, 0] for hd in gather_h)
        hn = rmsnorm_fwd(h, gain, f"attn_norm_f{i}")
        sv["hn"] = hn
        if i < N_A:
            w_in = full["ssm_in_proj"][i]
            wz, wxbc = w_in[:, :SSM_D_INNER], w_in[:, SSM_D_INNER:SSM_ZX]
            wdt = jnp.pad(w_in[:, SSM_ZX:], ((0, 0), (0, LANE - SSM_HEADS)))
            z = matmul(hn, wz, "nn", F32, f"ssm_z_f{i}")
            xbc = bs(matmul(hn, wxbc, "nn", F32, f"ssm_xbc_f{i}"))
            dtr = bs(matmul(hn, wdt, "nn", F32, f"ssm_dt_f{i}"))
            cw = _pad_rows(full["ssm_conv_w"][i])
            cb = row(full["ssm_conv_b"][i])
            xc = ssm_conv_fwd(xbc, cw, cb, f"ssm_conv_f{i}")
            dtb, alog = pad_lane(ssm_dt_bias[i]), pad_lane(ssm_a_log[i])
            dexp = jnp.repeat(ssm_d[i], SSM_HEAD_DIM).reshape(1, -1)
            y, sprev = ssd_fwd(xc, dtr, dtb, alog, dexp, f"ssd_f{i}")
            gn = gated_norm_fwd(fl(y), z, row(full["ssm_norm"][i]), f"ssm_gnorm_f{i}")
            h1 = matmul(gn, full["ssm_out_proj"][i], "nn", F32, f"ssm_out_f{i}", add=h)
            sv.update(wz=wz, wxbc=wxbc, wdt=wdt, z=z, xbc=xbc, dtr=dtr, cw=cw, cb=cb, xc=xc, dtb=dtb, alog=alog,
                      dexp=dexp, y=y, sprev=sprev, gn=gn)
        else:
            j = i - N_A
            q = bs(matmul(hn, full["w_q"][j], "nn", BF16, f"sb_q_f{j}"))
            o, ltot = sb_attention_fwd(q, kv, f"sb_attn_f{j}")
            h1 = matmul(fl(o), full["w_o"][j], "nn", F32, f"sb_o_f{j}", add=h)
            sv.update(q=q, o=o, ltot=ltot)
        sv["h1"] = h1
        fetch_layer(2 * i + 1, h1)
        hn2 = rmsnorm_fwd(h1, row(ffn_norm[i]), f"ffn_norm_f{i}")
        wug, wuv = full["ffn_up"][i]
        ug = bs(matmul(hn2, wug, "nn", F32, f"ffn_upg_f{i}"))
        uv = bs(matmul(hn2, wuv, "nn", F32, f"ffn_upv_f{i}"))
        fcw = full["ffn_conv_w"][i]
        fwg, fwv = _pad_rows(fcw[:, :D_FF]), _pad_rows(fcw[:, D_FF:])
        fbg, fbv = row(ffn_conv_b[i, :D_FF]), row(ffn_conv_b[i, D_FF:])
        act = ffn_conv_fwd(ug, uv, fwg, fwv, fbg, fbv, f"ffn_conv_f{i}")
        h2 = matmul(fl(act), full["ffn_down"][i], "nn", F32, f"ffn_down_f{i}", add=h1)
        hn3 = rmsnorm_fwd(h2, row(ple_norm[i]), f"ple_norm_f{i}")
        gpre = matmul(hn3, full["ple_gate"][i], "nn", F32, f"ple_gate_f{i}")
        p_i = p[i].reshape(t, PLE_DIM)
        pp = matmul(p_i, full["ple_proj"][i], "nn", F32, f"ple_proj_f{i}")
        h3 = ple_fwd(h2, gpre, pp, f"ple_f{i}")
        sv.update(hn2=hn2, wug=wug, wuv=wuv, ug=ug, uv=uv, fwg=fwg, fwv=fwv, fbg=fbg, fbv=fbv, act=act, h2=h2,
                  hn3=hn3, gpre=gpre, pp=pp, p_i=p_i)
        if i == N_A - 1:
            h_kv = h3
            hnkv = rmsnorm_fwd(h3, row(kv_norm), "kv_norm_f")
            kv = bs(matmul(hnkv, full["w_kv"][None], "nn", BF16, "kv_f"))
        h = h3
        saved.append(sv)

    loss_row, dh, g_final = loss_head(h, row(final_norm), loss_target.reshape(t, d), "loss_head")
    loss = lax.psum(loss_row[0, 0], ("x", "y", "c"))

    G = {n: [None] * given[n].shape[0] for n in wnames if given[n].ndim > 1 and n not in ("w_kv",)}
    G["final_norm"] = g_final[0]

    def col_slots(g, ndev=N_DEV):
        return jnp.transpose(g.reshape(g.shape[0], ndev, -1), (1, 0, 2))

    def grad_slots(n, idx):
        g = G[n] if idx is None else G[n][idx]
        if n == "ffn_up":
            s = jnp.concatenate([col_slots(g[0], N_DEV // 2), col_slots(g[1], N_DEV // 2)], axis=0)
        elif BIG[n] == given[n].ndim - 1:
            s = col_slots(g)
        else:
            s = g.reshape(N_DEV, -1, g.shape[-1])
        return s.astype(BF16)

    scatter_src, scatter_h = {}, {}
    dk = jnp.zeros((bsz, seq, SB_WIDTH), F32)
    dv = jnp.zeros((bsz, seq, SB_WIDTH), F32)
    for i in reversed(range(DEPTH)):
        sv = saved[i]
        if i == N_A - 1:
            wk, wv = full["w_kv"][None][:, :SB_WIDTH], full["w_kv"][None][:, SB_WIDTH:]
            dkf, dvf = fl(dk), fl(dv)
            dhn = matmul(dkf, wk, "nt", F32, "kv_dx_k")
            dhn = matmul(dvf, wv, "nt", F32, "kv_dx_v", add=dhn)
            G["w_kv"] = jnp.concatenate([matmul(hnkv, dkf, "tn", F32, "kv_dw_k"), matmul(hnkv, dvf, "tn", F32, "kv_dw_v")], axis=1)
            dh, gk = rmsnorm_bwd(dhn, h_kv, row(kv_norm), dh, "kv_norm_b")
            G["kv_norm"] = gk[0]
        dgpre, dpp = ple_bwd(dh, sv["gpre"], sv["pp"], f"ple_b{i}")
        G["ple_proj"][i] = matmul(sv["p_i"], dpp, "tn", F32, f"ple_proj_dw{i}")
        G["ple_gate"][i] = matmul(sv["hn3"], dgpre, "tn", F32, f"ple_gate_dw{i}")
        dhn = matmul(dgpre, full["ple_gate"][i], "nt", F32, f"ple_gate_dx{i}")
        gain = row(ple_norm[i])
        if 2 * i + 2 in scatter_h:
            gain = gain + scatter_h[2 * i + 2]["token"][0, 0]
        dh, gn_ = rmsnorm_bwd(dhn, sv["h2"], gain, dh, f"ple_norm_b{i}")
        G["ple_norm"][i] = gn_[0]
        dact = bs(matmul(dh, full["ffn_down"][i], "nt", F32, f"ffn_down_dx{i}"))
        G["ffn_down"][i] = matmul(fl(sv["act"]), dh, "tn", F32, f"ffn_down_dw{i}")
        dug, duv, dwg, dwv, dbg, dbv = ffn_conv_bwd(sv["ug"], sv["uv"], dact, sv["fwg"], sv["fwv"], sv["fbg"], sv["fbv"], f"ffn_conv_b{i}")
        dug, duv = fl(dug), fl(duv)
        G["ffn_conv_w"][i] = jnp.concatenate([dwg[:FFN_CONV], dwv[:FFN_CONV]], axis=1)
        G["ffn_conv_b"][i] = jnp.concatenate([dbg[0], dbv[0]])
        G["ffn_up"][i] = (matmul(sv["hn2"], dug, "tn", F32, f"ffn_upg_dw{i}"), matmul(sv["hn2"], duv, "tn", F32, f"ffn_upv_dw{i}"))
        dhn = matmul(dug, sv["wug"], "nt", F32, f"ffn_upg_dx{i}")
        dhn = matmul(duv, sv["wuv"], "nt", F32, f"ffn_upv_dx{i}", add=dhn)
        dh, gn_ = rmsnorm_bwd(dhn, sv["h1"], row(ffn_norm[i]), dh, f"ffn_norm_b{i}")
        G["ffn_norm"][i] = gn_[0]
        scatter_src[2 * i + 1] = [grad_slots(n, idx) for n, idx in layer_items(2 * i + 1)]
        scatter_h[2 * i + 1] = exchange_start("scatter", scatter_src[2 * i + 1], f"a2a_start{2 * i + 1}")
        tok = scatter_h[2 * i + 1]["token"][0, 0]
        if i < N_A:
            dgn = matmul(dh, full["ssm_out_proj"][i], "nt", F32, f"ssm_out_dx{i}")
            G["ssm_out_proj"][i] = matmul(sv["gn"], dh, "tn", F32, f"ssm_out_dw{i}")
            dy, dz, dnw = gated_norm_bwd(dgn, fl(sv["y"]), sv["z"], row(full["ssm_norm"][i]) + tok, f"ssm_gnorm_b{i}")
            G["ssm_norm"][i] = dnw[0]
            dxs, dbm, dcm, ddtr4, da4, dbias4, ddexp = ssd_bwd(sv["xc"], sv["dtr"], sv["dtb"], sv["alog"], sv["dexp"], bs(dy), sv["sprev"], f"ssd_b{i}")
            G["ssm_a_log"][i] = da4[:, 0, :HPG].reshape(-1)
            G["ssm_dt_bias"][i] = dbias4[:, 0, :HPG].reshape(-1)
            G["ssm_d"][i] = ddexp.reshape(SSM_HEADS, SSM_HEAD_DIM).sum(axis=-1)
            dxc = jnp.concatenate([dxs, dbm, dcm], axis=-1)
            dxbc, dcw, dcb = ssm_conv_bwd(sv["xbc"], dxc, sv["cw"], sv["cb"], f"ssm_conv_b{i}")
            dxbc = fl(dxbc)
            G["ssm_conv_w"][i] = dcw[:SSM_CONV]
            G["ssm_conv_b"][i] = dcb[0]
            ddtr = ddtr4.reshape(t, SSM_GROUPS, LANE)[:, :, :HPG].reshape(t, SSM_HEADS)
            ddtr = jnp.pad(ddtr, ((0, 0), (0, LANE - SSM_HEADS))).astype(BF16)
            hn = sv["hn"]
            G["ssm_in_proj"][i] = jnp.concatenate([
                matmul(hn, dz, "tn", F32, f"ssm_z_dw{i}"), matmul(hn, dxbc, "tn", F32, f"ssm_xbc_dw{i}"),
                matmul(hn, ddtr, "tn", F32, f"ssm_dt_dw{i}")[:, :SSM_HEADS]], axis=1)
            dhn = matmul(dz, sv["wz"], "nt", F32, f"ssm_z_dx{i}")
            dhn = matmul(dxbc, sv["wxbc"], "nt", F32, f"ssm_xbc_dx{i}", add=dhn)
            dhn = matmul(ddtr, sv["wdt"], "nt", F32, f"ssm_dt_dx{i}", add=dhn)
        else:
            j = i - N_A
            do = bs(matmul(dh, full["w_o"][j], "nt", BF16, f"sb_o_dx{j}"))
            G["w_o"][j] = matmul(fl(sv["o"]), dh, "tn", F32, f"sb_o_dw{j}")
            dq, dk, dv = sb_attention_bwd(sv["q"], kv, sv["ltot"], do, dk, dv, f"sb_attn_b{j}")
            dq = fl(dq)
            G["w_q"][j] = matmul(sv["hn"], dq, "tn", F32, f"sb_q_dw{j}")
            dhn = matmul(dq, full["w_q"][j], "nt", F32, f"sb_q_dx{j}")
        dh, gn_ = rmsnorm_bwd(dhn, sv["h0"], row(attn_norm[i]) + (0.0 if i < N_A else tok), dh, f"attn_norm_b{i}")
        G["attn_norm"][i] = gn_[0]
        src = [grad_slots(n, idx) for n, idx in layer_items(2 * i)]
        if i == 0:
            small_g = {n: jnp.stack(G[n]) for n in SMALL}
            src.append(_pack_rows([_to_slots(small_g[n], SMALL[n]) for n in SMALL], (N_DEV,)))
        scatter_src[2 * i] = src
        scatter_h[2 * i] = exchange_start("scatter", src, f"a2a_start{2 * i}")
    grad_x = dh.reshape(bsz, seq, d)
    grads = {n: jnp.stack(G[n]) if isinstance(G[n], list) else G[n] for n in REPL}

    layer_parts = {n: {} for n in BIG}
    small_parts = None
    for i in reversed(range(N_GROUPS)):
        lands = exchange_wait(scatter_h[i], dh, f"a2a_wait{i}")
        got = [put_own(l, lax.dynamic_index_in_dim(s, me, 0, keepdims=False)) for l, s in zip(lands, scatter_src[i])]
        for (n, idx), g in zip(layer_items(i), got):
            layer_parts[n][idx] = g
        if i == 0:
            small_parts = _unpack_rows(got[-1], [given[n].size for n in SMALL], (N_DEV,))
    big_parts = [jnp.concatenate([layer_parts[n][k] for k in sorted(layer_parts[n], key=lambda v: -1 if v is None else v)], axis=1)
                 for n in BIG]
    parts_rp = all_gather([_pack_rows([grads[n].reshape(-1) for n in REPL])], "ag_repl_grads")[0]
    sh_names = list(BIG) + list(SMALL)
    sh_parts = big_parts + small_parts
    rp_parts = _unpack_rows(parts_rp, [given[n].size for n in REPL], (N_DEV,))
    res = {}
    for n, pr in list(zip(sh_names, sh_parts)) + list(zip(REPL, rp_parts)):
        w2 = _as2d(given[n])
        outs = adamw(w2, pr.reshape((N_DEV,) + w2.shape), _as2d(given["m_" + n]), _as2d(given["v_" + n]), f"adamw_{n}")
        res[n] = [o.reshape(given[n].shape) for o in outs]
    order = ["attn_norm", "ffn_norm", "ple_norm", "ssm_in_proj", "ssm_conv_w", "ssm_conv_b", "ssm_dt_bias", "ssm_a_log",
             "ssm_d", "ssm_norm", "ssm_out_proj", "kv_norm", "w_kv", "w_q", "w_o", "ffn_up", "ffn_conv_w", "ffn_conv_b",
             "ffn_down", "ple_gate", "ple_proj", "final_norm"]
    return (loss, grad_x, *[res[n][0] for n in order], *[res[n][1] for n in order],
            *[res[n][2] for n in order], *[res[n][3] for n in order])
```

```python
import functools
import math

import jax
import jax.numpy as jnp
from jax import lax
from jax.experimental import pallas as pl
from jax.experimental.pallas import tpu as pltpu

F32 = jnp.float32
BF16 = jnp.bfloat16

N_DEV = 8
D_MODEL = 1024
SEQ = 2048
DEPTH = 4
N_A = 2
N_B = 2
SSM_D_INNER = 2048
SSM_HEAD_DIM = 64
SSM_HEADS = 32
SSM_GROUPS = 4
SSM_STATE = 128
SSM_CONV = 4
SSM_CHUNK = 128
SSM_CONV_DIM = 3072
SSM_ZX = 5120
SSM_IN_DIM = 5152
SB_HEADS = 16
SB_HEAD_DIM = 64
SB_WIDTH = 1024
D_FF = 2816
FFN_CONV = 3
PLE_DIM = 256
NORM_EPS = 1e-6
SSM_NORM_EPS = 1e-5

ADAM_LR = 0.001
ADAM_B1 = 0.9
ADAM_B2 = 0.999
ADAM_EPS = 1e-08
ADAM_WD = 0.01
ADAM_STEP = 10

LANE = 128
SUBLANE = 8
VMEM_LIMIT = 48 * 1024 * 1024

NN = (((1,), (0,)), ((), ()))
NT = (((1,), (1,)), ((), ()))
TN = (((0,), (0,)), ((), ()))


def _params(sem):
    return pltpu.CompilerParams(dimension_semantics=sem, vmem_limit_bytes=VMEM_LIMIT)


def _pick(n, prefs):
    for p in prefs:
        if n % p == 0:
            return p
    return n


def matmul(a, b, mode, out_dtype, name, add=None):
    if mode == "nn":
        (m, k), (k2, n) = a.shape, b.shape
    elif mode == "nt":
        (m, k), (n, k2) = a.shape, b.shape
    else:
        (k, m), (k2, n) = a.shape, b.shape
    assert k == k2, (a.shape, b.shape, mode)
    tm = _pick(m, (1408, 512, 256, 128))
    tn = _pick(n, (1408, 1024, 512, 256, 128))
    tk = _pick(k, (2816, 2048, 1024, 512, 256, 128))

    def vmem_bytes(tk_):
        blocks = tm * tk_ * a.dtype.itemsize + tk_ * tn * b.dtype.itemsize + tm * tn * jnp.dtype(out_dtype).itemsize
        blocks += tm * tn * add.dtype.itemsize if add is not None else 0
        return 2 * blocks + (tm * tn * 4 if k // tk_ > 1 else 0)

    while vmem_bytes(tk) > (VMEM_LIMIT * 3) // 4 and tk % 256 == 0:
        tk //= 2
    nk = k // tk
    dims = {"nn": NN, "nt": NT, "tn": TN}[mode]

    def body(*refs):
        a_ref, b_ref = refs[:2]
        add_ref = refs[2] if add is not None else None
        o_ref = refs[-2] if nk > 1 else refs[-1]
        part = lax.dot_general(a_ref[...].astype(BF16), b_ref[...].astype(BF16), dims, preferred_element_type=F32)

        def finish(r):
            if add_ref is not None:
                r = r + add_ref[...].astype(F32)
            o_ref[...] = r.astype(o_ref.dtype)

        if nk == 1:
            finish(part)
            return
        acc_ref = refs[-1]
        kk = pl.program_id(2)

        @pl.when(kk == 0)
        def _():
            acc_ref[...] = part

        @pl.when((kk > 0) & (kk < nk - 1))
        def _():
            acc_ref[...] += part

        @pl.when(kk == nk - 1)
        def _():
            finish(acc_ref[...] + part)

    if mode == "nn":
        a_spec = pl.BlockSpec((tm, tk), lambda i, j, kk: (i, kk))
        b_spec = pl.BlockSpec((tk, tn), lambda i, j, kk: (kk, j))
    elif mode == "nt":
        a_spec = pl.BlockSpec((tm, tk), lambda i, j, kk: (i, kk))
        b_spec = pl.BlockSpec((tn, tk), lambda i, j, kk: (j, kk))
    else:
        a_spec = pl.BlockSpec((tk, tm), lambda i, j, kk: (kk, i))
        b_spec = pl.BlockSpec((tk, tn), lambda i, j, kk: (kk, j))
    o_spec = pl.BlockSpec((tm, tn), lambda i, j, kk: (i, j))
    in_specs = [a_spec, b_spec] + ([o_spec] if add is not None else [])
    args = (a, b) + ((add,) if add is not None else ())
    return pl.pallas_call(
        body,
        name=name,
        grid=(m // tm, n // tn, nk),
        in_specs=in_specs,
        out_specs=o_spec,
        out_shape=jax.ShapeDtypeStruct((m, n), out_dtype),
        scratch_shapes=[pltpu.VMEM((tm, tn), F32)] if nk > 1 else [],
        compiler_params=_params(("parallel", "parallel", "arbitrary")),
    )(*args)


ROW_TILE = 512


def rmsnorm_fwd(x, gain, name):
    t, d = x.shape

    def body(x_ref, g_ref, o_ref):
        xv = x_ref[...]
        r = lax.rsqrt(jnp.mean(xv * xv, axis=-1, keepdims=True) + NORM_EPS)
        o_ref[...] = (xv * r * g_ref[...]).astype(o_ref.dtype)

    return pl.pallas_call(
        body,
        name=name,
        grid=(t // ROW_TILE,),
        in_specs=[pl.BlockSpec((ROW_TILE, d), lambda i: (i, 0)), pl.BlockSpec((1, d), lambda i: (0, 0))],
        out_specs=pl.BlockSpec((ROW_TILE, d), lambda i: (i, 0)),
        out_shape=jax.ShapeDtypeStruct((t, d), BF16),
        compiler_params=_params(("parallel",)),
    )(x, gain)


def rmsnorm_bwd(dy, x, gain, dres, name):
    t, d = x.shape

    def body(dy_ref, x_ref, g_ref, dres_ref, dx_ref, dg_ref):
        i = pl.program_id(0)
        xv = x_ref[...]
        r = lax.rsqrt(jnp.mean(xv * xv, axis=-1, keepdims=True) + NORM_EPS)
        xh = xv * r
        dyv = dy_ref[...].astype(F32)
        dxh = dyv * g_ref[...]
        dx = r * (dxh - xh * jnp.mean(dxh * xh, axis=-1, keepdims=True))
        dx_ref[...] = dres_ref[...] + dx

        @pl.when(i == 0)
        def _():
            dg_ref[...] = jnp.zeros_like(dg_ref)

        dg_ref[...] += jnp.sum(dyv * xh, axis=0, keepdims=True)

    row = pl.BlockSpec((ROW_TILE, d), lambda i: (i, 0))
    vec = pl.BlockSpec((1, d), lambda i: (0, 0))
    return pl.pallas_call(
        body,
        name=name,
        grid=(t // ROW_TILE,),
        in_specs=[row, row, vec, row],
        out_specs=[row, vec],
        out_shape=[jax.ShapeDtypeStruct((t, d), F32), jax.ShapeDtypeStruct((1, d), F32)],
        compiler_params=_params(("arbitrary",)),
    )(dy, x, gain, dres)


CONV_ROWS = 512
HALO = SUBLANE


def _conv_apply(ext, w_ref, bias, k, lo, n):
    acc = bias + w_ref[k - 1:k, :] * ext[lo:lo + n]
    for j in range(1, k):
        acc = acc + w_ref[k - 1 - j:k - j, :] * pltpu.roll(ext, j, 0)[lo:lo + n]
    return acc


def _conv_apply_t(ext, w_ref, k, n):
    rows = ext.shape[0]
    acc = w_ref[k - 1:k, :] * ext[:n]
    for j in range(1, k):
        acc = acc + w_ref[k - 1 - j:k - j, :] * pltpu.roll(ext, rows - j, 0)[:n]
    return acc


def _conv_dw(dpre_main, ext_u, k, n, dw_ref):
    for kk in range(k):
        j = k - 1 - kk
        sh = ext_u[HALO:HALO + n] if j == 0 else pltpu.roll(ext_u, j, 0)[HALO:HALO + n]
        dw_ref[kk:kk + 1, :] += jnp.sum(dpre_main * sh, axis=0, keepdims=True)


def _silu_grad(pre):
    sg = jax.nn.sigmoid(pre)
    return sg * (1.0 + pre * (1.0 - sg))


def _conv_blockspecs(seq, r, tc):
    nh = r // HALO
    last = seq // HALO - 1
    main = pl.BlockSpec((None, r, tc), lambda c, b, s: (b, s, c))
    before = pl.BlockSpec((None, HALO, tc), lambda c, b, s: (b, jnp.maximum(s * nh - 1, 0), c))
    after = pl.BlockSpec((None, HALO, tc), lambda c, b, s: (b, jnp.minimum((s + 1) * nh, last), c))
    return main, before, after


def ssm_conv_fwd(u, w, bias, name):
    bsz, seq, ch = u.shape
    r, tc, k = min(CONV_ROWS, seq), 512, SSM_CONV
    main, before, _ = _conv_blockspecs(seq, r, tc)
    wspec = pl.BlockSpec((SUBLANE, tc), lambda c, b, s: (0, c))
    bspec = pl.BlockSpec((1, tc), lambda c, b, s: (0, c))

    def body(u_ref, hb_ref, w_ref, b_ref, o_ref):
        s = pl.program_id(2)
        hb = jnp.where(s == 0, 0.0, hb_ref[...])
        ext = jnp.concatenate([hb, u_ref[...]], axis=0)
        pre = _conv_apply(ext, w_ref, b_ref[...], k, HALO, r)
        o_ref[...] = pre * jax.nn.sigmoid(pre)

    return pl.pallas_call(
        body, name=name, grid=(ch // tc, bsz, seq // r),
        in_specs=[main, before, wspec, bspec], out_specs=main,
        out_shape=jax.ShapeDtypeStruct(u.shape, F32),
        compiler_params=_params(("parallel", "parallel", "parallel")),
    )(u, u, w, bias)


def ssm_conv_bwd(u, dxc, w, bias, name):
    bsz, seq, ch = u.shape
    r, tc, k = min(CONV_ROWS, seq), 512, SSM_CONV
    ns = seq // r
    main, before, after = _conv_blockspecs(seq, r, tc)
    wspec = pl.BlockSpec((SUBLANE, tc), lambda c, b, s: (0, c))
    bspec = pl.BlockSpec((1, tc), lambda c, b, s: (0, c))

    def body(u_ref, hb_ref, ha_ref, d_ref, da_ref, w_ref, b_ref, du_ref, dw_ref, db_ref):
        b, s = pl.program_id(1), pl.program_id(2)
        hb = jnp.where(s == 0, 0.0, hb_ref[...])
        ext_u = jnp.concatenate([hb, u_ref[...], ha_ref[...]], axis=0)
        pre = _conv_apply(ext_u, w_ref, b_ref[...], k, HALO, r + HALO)
        dxe = jnp.concatenate([d_ref[...], jnp.where(s == ns - 1, 0.0, da_ref[...])], axis=0)
        dpre = dxe * _silu_grad(pre)
        du_ref[...] = _conv_apply_t(dpre, w_ref, k, r).astype(du_ref.dtype)

        @pl.when((b == 0) & (s == 0))
        def _():
            dw_ref[...] = jnp.zeros_like(dw_ref)
            db_ref[...] = jnp.zeros_like(db_ref)

        dpm = dpre[:r]
        _conv_dw(dpm, ext_u, k, r, dw_ref)
        db_ref[...] += jnp.sum(dpm, axis=0, keepdims=True)

    return pl.pallas_call(
        body, name=name, grid=(ch // tc, bsz, ns),
        in_specs=[main, before, after, main, after, wspec, bspec],
        out_specs=[main, wspec, bspec],
        out_shape=[jax.ShapeDtypeStruct(u.shape, BF16), jax.ShapeDtypeStruct((SUBLANE, ch), F32),
                   jax.ShapeDtypeStruct((1, ch), F32)],
        compiler_params=_params(("arbitrary", "arbitrary", "arbitrary")),
    )(u, u, u, dxc, dxc, w, bias)


def ffn_conv_fwd(ug, uv, wg, wv, bg, bv, name):
    bsz, seq, ch = ug.shape
    r, tc, k = min(CONV_ROWS, seq), 256, FFN_CONV
    main, before, _ = _conv_blockspecs(seq, r, tc)
    wspec = pl.BlockSpec((SUBLANE, tc), lambda c, b, s: (0, c))
    bspec = pl.BlockSpec((1, tc), lambda c, b, s: (0, c))

    def body(ug_ref, hg_ref, uv_ref, hv_ref, wg_ref, wv_ref, bg_ref, bv_ref, o_ref):
        s = pl.program_id(2)
        eg = jnp.concatenate([jnp.where(s == 0, 0.0, hg_ref[...]), ug_ref[...]], axis=0)
        ev = jnp.concatenate([jnp.where(s == 0, 0.0, hv_ref[...]), uv_ref[...]], axis=0)
        pg = _conv_apply(eg, wg_ref, bg_ref[...], k, HALO, r)
        pv = _conv_apply(ev, wv_ref, bv_ref[...], k, HALO, r)
        o_ref[...] = (pg * jax.nn.sigmoid(pg) * pv).astype(o_ref.dtype)

    return pl.pallas_call(
        body, name=name, grid=(ch // tc, bsz, seq // r),
        in_specs=[main, before, main, before, wspec, wspec, bspec, bspec], out_specs=main,
        out_shape=jax.ShapeDtypeStruct(ug.shape, BF16),
        compiler_params=_params(("parallel", "parallel", "parallel")),
    )(ug, ug, uv, uv, wg, wv, bg, bv)


def ffn_conv_bwd(ug, uv, dact, wg, wv, bg, bv, name):
    bsz, seq, ch = ug.shape
    r, tc, k = min(CONV_ROWS, seq), 256, FFN_CONV
    ns = seq // r
    main, before, after = _conv_blockspecs(seq, r, tc)
    wspec = pl.BlockSpec((SUBLANE, tc), lambda c, b, s: (0, c))
    bspec = pl.BlockSpec((1, tc), lambda c, b, s: (0, c))

    def body(ug_ref, gb_ref, ga_ref, uv_ref, vb_ref, va_ref, d_ref, da_ref, wg_ref, wv_ref, bg_ref, bv_ref,
             dug_ref, duv_ref, dwg_ref, dwv_ref, dbg_ref, dbv_ref):
        b, s = pl.program_id(1), pl.program_id(2)
        eg = jnp.concatenate([jnp.where(s == 0, 0.0, gb_ref[...]), ug_ref[...], ga_ref[...]], axis=0)
        ev = jnp.concatenate([jnp.where(s == 0, 0.0, vb_ref[...]), uv_ref[...], va_ref[...]], axis=0)
        pg = _conv_apply(eg, wg_ref, bg_ref[...], k, HALO, r + HALO)
        pv = _conv_apply(ev, wv_ref, bv_ref[...], k, HALO, r + HALO)
        de = jnp.concatenate([d_ref[...], jnp.where(s == ns - 1, 0.0, da_ref[...])], axis=0)
        sg = jax.nn.sigmoid(pg)
        dpg = de * pv * (sg * (1.0 + pg * (1.0 - sg)))
        dpv = de * (pg * sg)
        dug_ref[...] = _conv_apply_t(dpg, wg_ref, k, r).astype(dug_ref.dtype)
        duv_ref[...] = _conv_apply_t(dpv, wv_ref, k, r).astype(duv_ref.dtype)

        @pl.when((b == 0) & (s == 0))
        def _():
            dwg_ref[...] = jnp.zeros_like(dwg_ref)
            dwv_ref[...] = jnp.zeros_like(dwv_ref)
            dbg_ref[...] = jnp.zeros_like(dbg_ref)
            dbv_ref[...] = jnp.zeros_like(dbv_ref)

        _conv_dw(dpg[:r], eg, k, r, dwg_ref)
        _conv_dw(dpv[:r], ev, k, r, dwv_ref)
        dbg_ref[...] += jnp.sum(dpg[:r], axis=0, keepdims=True)
        dbv_ref[...] += jnp.sum(dpv[:r], axis=0, keepdims=True)

    wshape = jax.ShapeDtypeStruct((SUBLANE, ch), F32)
    bshape = jax.ShapeDtypeStruct((1, ch), F32)
    return pl.pallas_call(
        body, name=name, grid=(ch // tc, bsz, ns),
        in_specs=[main, before, after, main, before, after, main, after, wspec, wspec, bspec, bspec],
        out_specs=[main, main, wspec, wspec, bspec, bspec],
        out_shape=[jax.ShapeDtypeStruct(ug.shape, BF16), jax.ShapeDtypeStruct(ug.shape, BF16), wshape, wshape, bshape, bshape],
        compiler_params=_params(("arbitrary", "arbitrary", "arbitrary")),
    )(ug, ug, ug, uv, uv, uv, dact, dact, wg, wv, bg, bv)


GN_ROWS = 256
GN_GROUP = SSM_D_INNER // SSM_GROUPS


def gated_norm_fwd(y, z, w, name):
    t, d = y.shape

    def body(y_ref, z_ref, w_ref, o_ref):
        for g in range(d // GN_GROUP):
            sl = slice(g * GN_GROUP, (g + 1) * GN_GROUP)
            zv = z_ref[:, sl]
            gv = y_ref[:, sl] * (zv * jax.nn.sigmoid(zv))
            r = lax.rsqrt(jnp.mean(gv * gv, axis=-1, keepdims=True) + SSM_NORM_EPS)
            o_ref[:, sl] = (gv * r * w_ref[:, sl]).astype(o_ref.dtype)

    row = pl.BlockSpec((GN_ROWS, d), lambda i: (i, 0))
    vec = pl.BlockSpec((1, d), lambda i: (0, 0))
    return pl.pallas_call(
        body, name=name, grid=(t // GN_ROWS,), in_specs=[row, row, vec], out_specs=row,
        out_shape=jax.ShapeDtypeStruct((t, d), BF16), compiler_params=_params(("parallel",)),
    )(y, z, w)


def gated_norm_bwd(dgn, y, z, w, name):
    t, d = y.shape

    def body(d_ref, y_ref, z_ref, w_ref, dy_ref, dz_ref, dw_ref):
        i = pl.program_id(0)

        @pl.when(i == 0)
        def _():
            dw_ref[...] = jnp.zeros_like(dw_ref)

        for g in range(d // GN_GROUP):
            sl = slice(g * GN_GROUP, (g + 1) * GN_GROUP)
            zv, yv, dv = z_ref[:, sl], y_ref[:, sl], d_ref[:, sl]
            sg = jax.nn.sigmoid(zv)
            sz = zv * sg
            gv = yv * sz
            r = lax.rsqrt(jnp.mean(gv * gv, axis=-1, keepdims=True) + SSM_NORM_EPS)
            gh = gv * r
            dgh = dv * w_ref[:, sl]
            dw_ref[:, sl] += jnp.sum(dv * gh, axis=0, keepdims=True)
            dg = r * (dgh - gh * jnp.mean(dgh * gh, axis=-1, keepdims=True))
            dy_ref[:, sl] = dg * sz
            dz_ref[:, sl] = (dg * yv * (sg * (1.0 + zv * (1.0 - sg)))).astype(dz_ref.dtype)

    row = pl.BlockSpec((GN_ROWS, d), lambda i: (i, 0))
    vec = pl.BlockSpec((1, d), lambda i: (0, 0))
    return pl.pallas_call(
        body, name=name, grid=(t // GN_ROWS,), in_specs=[row, row, row, vec], out_specs=[row, row, vec],
        out_shape=[jax.ShapeDtypeStruct((t, d), F32), jax.ShapeDtypeStruct((t, d), BF16), jax.ShapeDtypeStruct((1, d), F32)],
        compiler_params=_params(("arbitrary",)),
    )(dgn, y, z, w)


def ple_fwd(h, gpre, pp, name):
    t, d = h.shape

    def body(h_ref, g_ref, p_ref, o_ref):
        o_ref[...] = h_ref[...] + jax.nn.sigmoid(g_ref[...]) * p_ref[...]

    row = pl.BlockSpec((ROW_TILE, d), lambda i: (i, 0))
    return pl.pallas_call(
        body, name=name, grid=(t // ROW_TILE,), in_specs=[row, row, row], out_specs=row,
        out_shape=jax.ShapeDtypeStruct((t, d), F32), compiler_params=_params(("parallel",)),
    )(h, gpre, pp)


def ple_bwd(dh, gpre, pp, name):
    t, d = dh.shape

    def body(d_ref, g_ref, p_ref, dg_ref, dp_ref):
        sg = jax.nn.sigmoid(g_ref[...])
        dv = d_ref[...]
        dg_ref[...] = (dv * p_ref[...] * sg * (1.0 - sg)).astype(dg_ref.dtype)
        dp_ref[...] = (dv * sg).astype(dp_ref.dtype)

    row = pl.BlockSpec((ROW_TILE, d), lambda i: (i, 0))
    return pl.pallas_call(
        body, name=name, grid=(t // ROW_TILE,), in_specs=[row, row, row], out_specs=[row, row],
        out_shape=[jax.ShapeDtypeStruct((t, d), BF16), jax.ShapeDtypeStruct((t, d), BF16)],
        compiler_params=_params(("parallel",)),
    )(dh, gpre, pp)


def loss_head(h, gain, target, name):
    t, d = h.shape

    def body(x_ref, g_ref, t_ref, l_ref, dx_ref, dg_ref):
        i = pl.program_id(0)
        xv = x_ref[...]
        r = lax.rsqrt(jnp.mean(xv * xv, axis=-1, keepdims=True) + NORM_EPS)
        xh = xv * r
        err = xh * g_ref[...] - t_ref[...]
        part = 0.5 * jnp.sum(jnp.mean(err * err, axis=-1, keepdims=True), axis=0, keepdims=True)
        dyv = err * (1.0 / d)
        dxh = dyv * g_ref[...]
        dx_ref[...] = r * (dxh - xh * jnp.mean(dxh * xh, axis=-1, keepdims=True))

        @pl.when(i == 0)
        def _():
            dg_ref[...] = jnp.zeros_like(dg_ref)
            l_ref[...] = jnp.zeros_like(l_ref)

        dg_ref[...] += jnp.sum(dyv * xh, axis=0, keepdims=True)
        l_ref[...] += jnp.broadcast_to(part, l_ref.shape)

    row = pl.BlockSpec((ROW_TILE, d), lambda i: (i, 0))
    vec = pl.BlockSpec((1, d), lambda i: (0, 0))
    lspec = pl.BlockSpec((1, LANE), lambda i: (0, 0))
    return pl.pallas_call(
        body, name=name, grid=(t // ROW_TILE,), in_specs=[row, vec, row], out_specs=[lspec, row, vec],
        out_shape=[jax.ShapeDtypeStruct((1, LANE), F32), jax.ShapeDtypeStruct((t, d), F32), jax.ShapeDtypeStruct((1, d), F32)],
        compiler_params=_params(("arbitrary",)),
    )(h, gain, target)


CHUNK = SSM_CHUNK
HPG = SSM_HEADS // SSM_GROUPS
GW = HPG * SSM_HEAD_DIM
NEG = -1e30


def _split3(x):
    hi = x.astype(BF16)
    r1 = x - hi.astype(F32)
    mid = r1.astype(BF16)
    return hi, mid, (r1 - mid.astype(F32)).astype(BF16)


def _ssd_dt_terms(dtr_ref, bias_ref, alog_ref, g):
    shift = (LANE - HPG * g) % LANE
    li = lax.broadcasted_iota(jnp.int32, (CHUNK, CHUNK), 0)
    si = lax.broadcasted_iota(jnp.int32, (CHUNK, CHUNK), 1)
    x0 = pltpu.roll(dtr_ref[...] + bias_ref[...], shift, 1)
    dt = jnp.maximum(x0, 0.0) + jnp.log1p(jnp.exp(-jnp.abs(x0)))
    a = pltpu.roll(jnp.broadcast_to(-jnp.exp(alog_ref[...]), (CHUNK, LANE)), shift, 1)
    adt = dt * a
    acs = lax.dot_general((li >= si).astype(F32), adt, NN, precision=lax.Precision.HIGHEST, preferred_element_type=F32)
    last = jnp.broadcast_to(acs[CHUNK - 1:CHUNK, :], (CHUNK, LANE))
    e_last = jnp.exp(last)
    hr = lax.broadcasted_iota(jnp.int32, (LANE, GW), 0)
    hc = lax.broadcasted_iota(jnp.int32, (LANE, GW), 1)
    spread = (hc // SSM_HEAD_DIM == hr).astype(BF16)
    stack = jnp.concatenate([dt, jnp.exp(last - acs), jnp.exp(acs), e_last], axis=0)
    wide = jnp.dot(jnp.concatenate(_split3(stack), axis=1), jnp.concatenate([spread] * 3, axis=0), preferred_element_type=F32)
    return dict(li=li, si=si, x0=x0, dt=dt, a=a, acs=acs, acs_t=acs.T, e_last=e_last,
                dt_x=wide[:CHUNK], w_x=wide[CHUNK:2 * CHUNK], e_x=wide[2 * CHUNK:3 * CHUNK], el_x=wide[3 * CHUNK:])


def _ssd_pair_halves(parts):
    low = lax.broadcasted_iota(jnp.int32, (CHUNK, LANE), 1) < SSM_HEAD_DIM
    return jnp.concatenate([jnp.where(low, parts[2 * p], parts[2 * p + 1]) for p in range(HPG // 2)], axis=1)


def _ssd_specs(nc, rev):
    def ci(c):
        return nc - 1 - c if rev else c
    nb = SSM_D_INNER // LANE
    xs = pl.BlockSpec((None, CHUNK, GW), lambda g, b, c: (b, ci(c), g))
    bm = pl.BlockSpec((None, CHUNK, LANE), lambda g, b, c: (b, ci(c), nb + g))
    cm = pl.BlockSpec((None, CHUNK, LANE), lambda g, b, c: (b, ci(c), nb + SSM_GROUPS + g))
    dtr = pl.BlockSpec((None, CHUNK, LANE), lambda g, b, c: (b, ci(c), 0))
    vec = pl.BlockSpec((1, LANE), lambda g, b, c: (0, 0))
    dexp = pl.BlockSpec((1, GW), lambda g, b, c: (0, g))
    st = pl.BlockSpec((None, None, CHUNK, GW), lambda g, b, c: (b, ci(c), 0, g))
    return xs, bm, cm, dtr, vec, dexp, st


def ssd_fwd(xc, dtr, dt_bias, a_log, d_exp, name):
    bsz, seq, _ = xc.shape
    nc = seq // CHUNK
    xs_s, bm_s, cm_s, dtr_s, vec_s, dexp_s, st_s = _ssd_specs(nc, False)

    def body(xs_ref, b_ref, c_ref, dtr_ref, bias_ref, alog_ref, dexp_ref, y_ref, sp_ref, state_ref):
        g, c = pl.program_id(0), pl.program_id(2)

        @pl.when(c == 0)
        def _():
            state_ref[...] = jnp.zeros_like(state_ref)

        t = _ssd_dt_terms(dtr_ref, bias_ref, alog_ref, g)
        tril = t["li"] >= t["si"]
        xs = xs_ref[...]
        b16 = b_ref[...].astype(BF16)
        c16 = c_ref[...].astype(BF16)
        bt16 = b_ref[...].T.astype(BF16)
        cb = lax.dot_general(c16, b16, NT, preferred_element_type=F32)
        sp = state_ref[...]
        sp_ref[...] = sp
        cs = jnp.dot(c16, sp.astype(BF16), preferred_element_type=F32)
        xd = xs * t["dt_x"]
        xd16 = xd.astype(BF16)
        y_parts = []
        for r in range(HPG):
            pair = slice((r // 2) * LANE, (r // 2 + 1) * LANE)
            diff = t["acs"][:, r:r + 1] - t["acs_t"][r:r + 1, :]
            m16 = (cb * jnp.exp(jnp.where(tril, diff, NEG))).astype(BF16)
            y_parts.append(jnp.dot(m16, xd16[:, pair], preferred_element_type=F32))
        y_ref[...] = _ssd_pair_halves(y_parts) + t["e_x"] * cs + xs * dexp_ref[...]
        xw16 = (xd * t["w_x"]).astype(BF16)
        state_ref[...] = sp * t["el_x"] + jnp.dot(bt16, xw16, preferred_element_type=F32)

    return pl.pallas_call(
        body, name=name, grid=(SSM_GROUPS, bsz, nc),
        in_specs=[xs_s, bm_s, cm_s, dtr_s, vec_s, vec_s, dexp_s],
        out_specs=[xs_s, st_s],
        out_shape=[jax.ShapeDtypeStruct((bsz, seq, SSM_D_INNER), F32),
                   jax.ShapeDtypeStruct((bsz, nc, SSM_STATE, SSM_D_INNER), F32)],
        scratch_shapes=[pltpu.VMEM((SSM_STATE, GW), F32)],
        compiler_params=_params(("arbitrary", "arbitrary", "arbitrary")),
    )(xc, xc, xc, dtr, dt_bias, a_log, d_exp)


def ssd_bwd(xc, dtr, dt_bias, a_log, d_exp, dy, sprev, name):
    bsz, seq, _ = xc.shape
    nc = seq // CHUNK
    xs_s, bm_s, cm_s, dtr_s, vec_s, dexp_s, st_s = _ssd_specs(nc, True)
    grp = pl.BlockSpec((None, CHUNK, LANE), lambda g, b, c: (b, nc - 1 - c, g))
    acc = pl.BlockSpec((None, 1, LANE), lambda g, b, c: (g, 0, 0))

    def body(xs_ref, b_ref, c_ref, dtr_ref, bias_ref, alog_ref, dexp_ref, dy_ref, sp_ref,
             dxs_ref, db_ref, dc_ref, ddtr_ref, da_ref, dbias_ref, ddexp_ref, dstate_ref):
        g, b, c = pl.program_id(0), pl.program_id(1), pl.program_id(2)

        @pl.when(c == 0)
        def _():
            dstate_ref[...] = jnp.zeros_like(dstate_ref)

        @pl.when((b == 0) & (c == 0))
        def _():
            da_ref[...] = jnp.zeros_like(da_ref)
            dbias_ref[...] = jnp.zeros_like(dbias_ref)
            ddexp_ref[...] = jnp.zeros_like(ddexp_ref)

        t = _ssd_dt_terms(dtr_ref, bias_ref, alog_ref, g)
        li, si = t["li"], t["si"]
        tril, triu = li >= si, li <= si
        row_is_last = li[:, :1] == CHUNK - 1
        xs, dy = xs_ref[...], dy_ref[...]
        b16 = b_ref[...].astype(BF16)
        c16 = c_ref[...].astype(BF16)
        ct16 = c_ref[...].T.astype(BF16)
        cb = lax.dot_general(c16, b16, NT, preferred_element_type=F32)
        cbt = lax.dot_general(b16, c16, NT, preferred_element_type=F32)
        sp, ds = sp_ref[...], dstate_ref[...]
        sp16, ds16 = sp.astype(BF16), ds.astype(BF16)
        cs = jnp.dot(c16, sp16, preferred_element_type=F32)
        bds = jnp.dot(b16, ds16, preferred_element_type=F32)
        spds = jnp.sum(sp * ds, axis=0, keepdims=True)
        xd = xs * t["dt_x"]
        xw = xd * t["w_x"]
        edy = dy * t["e_x"]
        xd16, dy16, xw16, edy16 = xd.astype(BF16), dy.astype(BF16), xw.astype(BF16), edy.astype(BF16)
        low = lax.broadcasted_iota(jnp.int32, (CHUNK, LANE), 1) < SSM_HEAD_DIM
        zero = jnp.zeros((CHUNK, CHUNK), F32)
        dsum_g, dsum_gt, dacs = zero, zero, zero
        dxd_parts = []
        for r in range(HPG):
            pair = slice((r // 2) * LANE, (r // 2 + 1) * LANE)
            diff = t["acs"][:, r:r + 1] - t["acs_t"][r:r + 1, :]
            lam = jnp.exp(jnp.where(tril, diff, NEG))
            lam_t = jnp.exp(jnp.where(triu, -diff, NEG))
            m, m_t = cb * lam, cbt * lam_t
            dyh16 = jnp.where(low if r % 2 == 0 else ~low, dy16[:, pair], jnp.zeros_like(dy16[:, pair]))
            dxd_parts.append(jnp.dot(m_t.astype(BF16), dy16[:, pair], preferred_element_type=F32))
            dm = lax.dot_general(dyh16, xd16[:, pair], NT, preferred_element_type=F32)
            dm_t = lax.dot_general(xd16[:, pair], dyh16, NT, preferred_element_type=F32)
            dsum_g = dsum_g + dm * lam
            dsum_gt = dsum_gt + dm_t * lam_t
            dacs = dacs + jnp.sum(dm * m - dm_t * m_t, axis=1, keepdims=True) * (si[:1, :] == r).astype(F32)
        xwb = xw * bds
        dxd = _ssd_pair_halves(dxd_parts) + t["w_x"] * bds
        dxs_ref[...] = dxd * t["dt_x"] + dy * dexp_ref[...]
        gr = lax.broadcasted_iota(jnp.int32, (GW, LANE), 0)
        gc = lax.broadcasted_iota(jnp.int32, (GW, LANE), 1)
        gather = (gr // SSM_HEAD_DIM == gc).astype(BF16)
        tail = jnp.concatenate([jnp.sum(xwb, axis=0, keepdims=True), spds, jnp.zeros((SUBLANE - 2, GW), F32)], axis=0)
        stack = jnp.concatenate([edy * cs - xwb, dxd * xs, tail], axis=0)
        hs = sum(jnp.dot(part, gather, preferred_element_type=F32) for part in _split3(stack))
        last = hs[2 * CHUNK:2 * CHUNK + 1] + t["e_last"][0:1, :] * hs[2 * CHUNK + 1:2 * CHUNK + 2]
        dacs = dacs + hs[:CHUNK] + jnp.where(row_is_last, last, 0.0)
        ddt_dir = hs[CHUNK:2 * CHUNK]
        dc_ref[...] = (jnp.dot(dsum_g.astype(BF16), b16, preferred_element_type=F32)
                       + lax.dot_general(edy16, sp16, NT, preferred_element_type=F32))
        db_ref[...] = (jnp.dot(dsum_gt.astype(BF16), c16, preferred_element_type=F32)
                       + lax.dot_general(xw16, ds16, NT, preferred_element_type=F32))
        dstate_ref[...] = ds * t["el_x"] + jnp.dot(ct16, edy16, preferred_element_type=F32)
        dadt = lax.dot_general(triu.astype(F32), dacs, NN, precision=lax.Precision.HIGHEST, preferred_element_type=F32)
        head_lane = si < HPG
        ddtr = jnp.where(head_lane, (dadt * t["a"] + ddt_dir) * jax.nn.sigmoid(t["x0"]), 0.0)
        ddtr_ref[...] = ddtr
        da_ref[...] += jnp.sum(jnp.where(head_lane, dadt * t["dt"] * t["a"], 0.0), axis=0, keepdims=True)
        dbias_ref[...] += jnp.sum(ddtr, axis=0, keepdims=True)
        ddexp_ref[...] += jnp.sum(dy * xs, axis=0, keepdims=True)

    return pl.pallas_call(
        body, name=name, grid=(SSM_GROUPS, bsz, nc),
        in_specs=[xs_s, bm_s, cm_s, dtr_s, vec_s, vec_s, dexp_s, xs_s, st_s],
        out_specs=[xs_s, grp, grp, grp, acc, acc, dexp_s],
        out_shape=[jax.ShapeDtypeStruct((bsz, seq, SSM_D_INNER), F32),
                   jax.ShapeDtypeStruct((bsz, seq, SSM_GROUPS * SSM_STATE), F32),
                   jax.ShapeDtypeStruct((bsz, seq, SSM_GROUPS * SSM_STATE), F32),
                   jax.ShapeDtypeStruct((bsz, seq, SSM_GROUPS * LANE), F32),
                   jax.ShapeDtypeStruct((SSM_GROUPS, 1, LANE), F32),
                   jax.ShapeDtypeStruct((SSM_GROUPS, 1, LANE), F32),
                   jax.ShapeDtypeStruct((1, SSM_D_INNER), F32)],
        scratch_shapes=[pltpu.VMEM((SSM_STATE, GW), F32)],
        compiler_params=_params(("arbitrary", "arbitrary", "arbitrary")),
    )(xc, xc, xc, dtr, dt_bias, a_log, d_exp, dy, sprev)


SB_TQ = 512
SB_TK = 128
SB_SCALE = SB_HEAD_DIM ** -0.5


def _split_dot(x, u16):
    hi = x.astype(BF16)
    lo = (x - hi.astype(F32)).astype(BF16)
    return jnp.dot(hi, u16, preferred_element_type=F32) + jnp.dot(lo, u16, preferred_element_type=F32)


def _sb_scores(qh, kj, mask):
    z = lax.dot_general(qh, kj, NT, preferred_element_type=F32)
    l1p = jnp.log(1.0 + jnp.exp(-jnp.abs(z)))
    lb = jnp.minimum(z, 0.0) - l1p
    lk = -jnp.maximum(z, 0.0) - l1p
    if mask is not None:
        lk = jnp.where(mask, lk, 0.0)
    return lb, lk


def _sb_diag_masks(tq):
    r = lax.broadcasted_iota(jnp.int32, (tq, SB_TK), 0)
    c = lax.broadcasted_iota(jnp.int32, (tq, SB_TK), 1)
    return [r > c + d * SB_TK for d in range(tq // SB_TK)]


def _sb_heads(x, lane):
    return [jnp.where((lane // SB_HEAD_DIM) == hh, x, jnp.zeros_like(x)) for hh in range(2)]


def sb_attention_fwd(q, kv, name):
    bsz, seq, width = q.shape
    tq = min(SB_TQ, seq)
    npair = width // LANE
    qspec = pl.BlockSpec((None, tq, LANE), lambda b, p, i: (b, i, p))
    kspec = pl.BlockSpec((None, seq, LANE), lambda b, p, i: (b, 0, p))
    vspec = pl.BlockSpec((None, seq, LANE), lambda b, p, i: (b, 0, npair + p))

    def body(q_ref, k_ref, v_ref, o_ref, l_ref):
        i = pl.program_id(2)
        ndiag = tq // SB_TK
        nfull = i * ndiag
        lane = lax.broadcasted_iota(jnp.int32, (tq, LANE), 1)
        ui = lax.broadcasted_iota(jnp.int32, (SB_TK, SB_TK), 0)
        uj = lax.broadcasted_iota(jnp.int32, (SB_TK, SB_TK), 1)
        u_excl = (ui > uj).astype(BF16)
        qs = _sb_heads(q_ref[...] * SB_SCALE, lane)

        def tile(j, carry, mask):
            rows = pl.ds(pl.multiple_of(j * SB_TK, SB_TK), SB_TK)
            kj, vj = k_ref[rows, :], v_ref[rows, :]
            new = []
            for hh in range(2):
                acc, run = carry[hh]
                lb, lk = _sb_scores(qs[hh], kj, mask)
                w = jnp.exp(lb + _split_dot(lk, u_excl) + run)
                if mask is not None:
                    w = jnp.where(mask, w, 0.0)
                acc = acc + jnp.dot(w.astype(BF16), vj, preferred_element_type=F32)
                new.append((acc, run + jnp.sum(lk, axis=1, keepdims=True)))
            return tuple(new)

        carry = tuple((jnp.zeros((tq, LANE), F32), jnp.zeros((tq, 1), F32)) for _ in range(2))
        masks = _sb_diag_masks(tq)
        for dd in reversed(range(ndiag)):
            carry = tile(nfull + dd, carry, masks[dd])
        def trip(jj, cr):
            for u in range(ndiag):
                cr = tile(nfull - 1 - u - ndiag * jj, cr, None)
            return cr

        carry = lax.fori_loop(0, i, trip, carry)
        in0 = (lane // SB_HEAD_DIM) == 0
        o_ref[...] = jnp.where(in0, carry[0][0], carry[1][0]).astype(o_ref.dtype)
        l_ref[...] = jnp.where(in0, carry[0][1], carry[1][1])

    return pl.pallas_call(
        body, name=name, grid=(bsz, npair, seq // tq),
        in_specs=[qspec, kspec, vspec], out_specs=[qspec, qspec],
        out_shape=[jax.ShapeDtypeStruct(q.shape, BF16), jax.ShapeDtypeStruct(q.shape, F32)],
        compiler_params=_params(("parallel", "parallel", "arbitrary")),
    )(q, kv, kv)


def sb_attention_bwd(q, kv, ltot, do, dk_in, dv_in, name):
    bsz, seq, width = q.shape
    tq = min(SB_TQ, seq)
    npair = width // LANE
    qspec = pl.BlockSpec((None, tq, LANE), lambda b, p, i: (b, i, p))
    kspec = pl.BlockSpec((None, seq, LANE), lambda b, p, i: (b, 0, p))
    vspec = pl.BlockSpec((None, seq, LANE), lambda b, p, i: (b, 0, npair + p))

    def body(q_ref, k_ref, v_ref, l_ref, do_ref, dki_ref, dvi_ref, dq_ref, dk_ref, dv_ref):
        i = pl.program_id(2)

        @pl.when(i == 0)
        def _():
            dk_ref[...] = dki_ref[...]
            dv_ref[...] = dvi_ref[...]

        ndiag = tq // SB_TK
        nfull = i * ndiag
        lane = lax.broadcasted_iota(jnp.int32, (tq, LANE), 1)
        ui = lax.broadcasted_iota(jnp.int32, (SB_TK, SB_TK), 0)
        uj = lax.broadcasted_iota(jnp.int32, (SB_TK, SB_TK), 1)
        u_le = (ui <= uj).astype(BF16)
        u_lt = (ui < uj).astype(BF16)
        qs = _sb_heads(q_ref[...] * SB_SCALE, lane)
        dos = _sb_heads(do_ref[...], lane)
        ltots = [l_ref[:, hh * SB_HEAD_DIM:hh * SB_HEAD_DIM + 1] for hh in range(2)]

        def tile(j, carry, mask, r0=0):
            rows = pl.ds(pl.multiple_of(j * SB_TK, SB_TK), SB_TK)
            kj, vj = k_ref[rows, :], v_ref[rows, :]
            new, dk_t, dv_t = [], None, None
            for hh in range(2):
                acc, run, run_a = carry[hh]
                qh, doh = qs[hh][r0:], dos[hh][r0:]
                lb, lk = _sb_scores(qh, kj, mask)
                w = jnp.exp(lb + (ltots[hh][r0:] - run[r0:] - _split_dot(lk, u_le)))
                if mask is not None:
                    w = jnp.where(mask, w, 0.0)
                a = lax.dot_general(doh, vj, NT, preferred_element_type=F32) * w
                ca = run_a[r0:] + _split_dot(a, u_lt)
                sg = jnp.exp(lb)
                dz = a * (1.0 - sg) - ca * sg
                if mask is not None:
                    dz = jnp.where(mask, dz, 0.0)
                dz16 = dz.astype(BF16)
                part = [acc[r0:] + jnp.dot(dz16, kj, preferred_element_type=F32),
                        run[r0:] + jnp.sum(lk, axis=1, keepdims=True), run_a[r0:] + jnp.sum(a, axis=1, keepdims=True)]
                if r0:
                    part = [jnp.concatenate([old[:r0], p], axis=0) for old, p in zip((acc, run, run_a), part)]
                dkh = lax.dot_general(dz16, qh, TN, preferred_element_type=F32)
                dvh = lax.dot_general(w.astype(BF16), doh, TN, preferred_element_type=F32)
                dk_t = dkh if dk_t is None else dk_t + dkh
                dv_t = dvh if dv_t is None else dv_t + dvh
                new.append(tuple(part))
            dk_ref[rows, :] += dk_t
            dv_ref[rows, :] += dv_t
            return tuple(new)

        zcol = jnp.zeros((tq, 1), F32)
        carry = tuple((jnp.zeros((tq, LANE), F32), zcol, zcol) for _ in range(2))
        def trip(j, cr):
            for u in range(ndiag):
                cr = tile(ndiag * j + u, cr, None)
            return cr

        carry = lax.fori_loop(0, i, trip, carry)
        masks = _sb_diag_masks(tq)
        for dd in range(ndiag):
            carry = tile(nfull + dd, carry, masks[dd][dd * SB_TK:], dd * SB_TK)
        in0 = (lane // SB_HEAD_DIM) == 0
        dq_ref[...] = (jnp.where(in0, carry[0][0], carry[1][0]) * SB_SCALE).astype(dq_ref.dtype)

    return pl.pallas_call(
        body, name=name, grid=(bsz, npair, seq // tq),
        in_specs=[qspec, kspec, vspec, qspec, qspec, kspec, kspec], out_specs=[qspec, kspec, kspec],
        out_shape=[jax.ShapeDtypeStruct(q.shape, BF16), jax.ShapeDtypeStruct(q.shape, F32), jax.ShapeDtypeStruct(q.shape, F32)],
        compiler_params=_params(("parallel", "parallel", "arbitrary")),
    )(q, kv, kv, ltot, do, dk_in, dv_in)


def adamw(w, parts, m, v, name):
    r, c = w.shape
    tr = r
    for cand in (256, 176, 128, 64, 32, 16, 8):
        if r % cand == 0:
            tr = cand
            break
    bc1 = 1.0 - ADAM_B1 ** ADAM_STEP
    bc2 = 1.0 - ADAM_B2 ** ADAM_STEP

    def body(w_ref, p_ref, m_ref, v_ref, g_ref, d_ref, mo_ref, vo_ref):
        g = p_ref[0].astype(F32)
        for j in range(1, N_DEV):
            g = g + p_ref[j].astype(F32)
        mn = ADAM_B1 * m_ref[...] + (1.0 - ADAM_B1) * g
        vn = ADAM_B2 * v_ref[...] + (1.0 - ADAM_B2) * (g * g)
        g_ref[...] = g
        mo_ref[...] = mn
        vo_ref[...] = vn
        d_ref[...] = -ADAM_LR * ((mn / bc1) / (jnp.sqrt(vn / bc2) + ADAM_EPS) + ADAM_WD * w_ref[...])

    blk = pl.BlockSpec((tr, c), lambda i: (i, 0))
    pblk = pl.BlockSpec((N_DEV, tr, c), lambda i: (0, i, 0))
    shp = jax.ShapeDtypeStruct((r, c), F32)
    return pl.pallas_call(
        body, name=name, grid=(r // tr,), in_specs=[blk, pblk, blk, blk], out_specs=[blk, blk, blk, blk],
        out_shape=[shp, shp, shp, shp], compiler_params=_params(("parallel",)),
    )(w, parts, m, v)


MESH_ID = pl.DeviceIdType.MESH
ANY = pl.BlockSpec(memory_space=pl.ANY)


def all_gather(xs, name):
    n = len(xs)

    def body(*refs):
        x_refs, out_refs = refs[:n], refs[n:2 * n]
        send_sems, recv_sems, local_sems = refs[2 * n:]
        mx, my, mc = lax.axis_index("x"), lax.axis_index("y"), lax.axis_index("c")
        me, sibling = (mx, my, mc), (mx, my, 1 - mc)
        chips = [(1 - mx, my), (mx, 1 - my), (1 - mx, 1 - my)]

        def slot(a, px, py, pc):
            return out_refs[a].at[4 * px + 2 * py + pc]

        def copy(a, k, block, to, src=None):
            return pltpu.make_async_remote_copy(
                src_ref=slot(a, *block) if src is None else src, dst_ref=slot(a, *block),
                send_sem=send_sems.at[7 * a + k], recv_sem=recv_sems.at[7 * a + k], device_id=to, device_id_type=MESH_ID)

        mine = [pltpu.make_async_copy(x_refs[a], slot(a, *me), local_sems.at[a]) for a in range(n)]
        for cp in mine:
            cp.start()
        first = []
        for a in range(n):
            first.append(copy(a, 0, me, sibling, src=x_refs[a]))
            first += [copy(a, 1 + j, me, (*chip, mc), src=x_refs[a]) for j, chip in enumerate(chips)]
        for cp in first:
            cp.start()
        passed = []
        for j, chip in enumerate(chips):
            for a in range(n):
                copy(a, 1 + j, (*chip, mc), me).wait_recv()
                cp = copy(a, 4 + j, (*chip, mc), sibling)
                cp.start()
                passed.append(cp)
        for a in range(n):
            copy(a, 0, sibling, me).wait_recv()
            for j, chip in enumerate(chips):
                copy(a, 4 + j, (*chip, 1 - mc), me).wait_recv()
        for cp in first + passed:
            cp.wait_send()
        for cp in mine:
            cp.wait()

    return pl.pallas_call(
        body, name=name, out_shape=[jax.ShapeDtypeStruct((N_DEV,) + x.shape, x.dtype) for x in xs],
        in_specs=[ANY] * n, out_specs=[ANY] * n,
        scratch_shapes=[pltpu.SemaphoreType.DMA((7 * n,)), pltpu.SemaphoreType.DMA((7 * n,)), pltpu.SemaphoreType.DMA((n,))],
    )(*xs)


HBM_SPEC = pl.BlockSpec(memory_space=pltpu.HBM)
SEM_SPEC = pl.BlockSpec(memory_space=pltpu.SEMAPHORE)
DATAFLOW = pltpu.SideEffectType.DATAFLOW_SIDE_EFFECTING


def _peer(j, mx, my, mc):
    px = 1 - mx if j & 4 else mx
    py = 1 - my if j & 2 else my
    pc = 1 - mc if j & 1 else mc
    return (px, py, pc), 4 * px + 2 * py + pc


def _split_copy(kind, src_ref, land_ref, send_sems, recv_sems, a, j, mx, my, mc):
    dev, peer = _peer(j, mx, my, mc)
    me = 4 * mx + 2 * my + mc
    src = src_ref if kind == "gather" else src_ref.at[peer]
    k = 7 * a + j - 1
    return pltpu.make_async_remote_copy(src_ref=src, dst_ref=land_ref.at[me], send_sem=send_sems.at[k],
                                        recv_sem=recv_sems.at[k], device_id=dev, device_id_type=MESH_ID), peer


def exchange_start(kind, srcs, name):
    n = len(srcs)
    blocks = [s.shape if kind == "gather" else s.shape[1:] for s in srcs]
    lands = [lax.empty((N_DEV,) + tuple(b), s.dtype) for b, s in zip(blocks, srcs)]

    def body(*refs):
        src_refs, land_refs = refs[:n], refs[n:2 * n]
        send_sems, recv_sems = refs[2 * n], refs[2 * n + 1]
        token = refs[-1]
        mx, my, mc = lax.axis_index("x"), lax.axis_index("y"), lax.axis_index("c")
        for j in range(1, N_DEV):
            for a in range(n):
                _split_copy(kind, src_refs[a], land_refs[a], send_sems, recv_sems, a, j, mx, my, mc)[0].start()
        token[...] = jnp.zeros_like(token)

    hbm = lambda v: pltpu.HBM(v.shape, v.dtype)
    outs = pl.pallas_call(
        body, name=name,
        out_shape=(pltpu.SemaphoreType.DMA((7 * n,)), pltpu.SemaphoreType.DMA((7 * n,)), *[hbm(s) for s in srcs],
                   *[hbm(l) for l in lands], jax.ShapeDtypeStruct((SUBLANE, LANE), F32)),
        in_specs=[HBM_SPEC] * (2 * n),
        out_specs=(SEM_SPEC, SEM_SPEC, *[HBM_SPEC] * (2 * n), pl.BlockSpec(memory_space=pltpu.VMEM)),
        input_output_aliases={a: 2 + a for a in range(2 * n)},
        compiler_params=pltpu.CompilerParams(has_side_effects=DATAFLOW),
    )(*[pltpu.with_memory_space_constraint(v, pltpu.HBM) for v in list(srcs) + lands])
    return dict(kind=kind, n=n, send=outs[0], recv=outs[1], srcs=outs[2:2 + n], lands=outs[2 + n:2 + 2 * n], token=outs[-1])


def exchange_wait(h, after, name):
    n, kind = h["n"], h["kind"]

    def body(*refs):
        src_refs, land_refs = refs[:n], refs[n:2 * n]
        send_sems, recv_sems = refs[2 * n], refs[2 * n + 1]
        mx, my, mc = lax.axis_index("x"), lax.axis_index("y"), lax.axis_index("c")
        for j in range(1, N_DEV):
            for a in range(n):
                cp, peer = _split_copy(kind, src_refs[a], land_refs[a], send_sems, recv_sems, a, j, mx, my, mc)
                cp.wait_send()
                pltpu.make_async_remote_copy(
                    src_ref=land_refs[a].at[peer], dst_ref=land_refs[a].at[peer], send_sem=send_sems.at[7 * a + j - 1],
                    recv_sem=recv_sems.at[7 * a + j - 1], device_id=_peer(j, mx, my, mc)[0], device_id_type=MESH_ID).wait_recv()

    hbm = lambda v: pltpu.HBM(v.shape, v.dtype)
    outs = pl.pallas_call(
        body, name=name,
        out_shape=tuple(hbm(v) for v in list(h["srcs"]) + list(h["lands"])),
        in_specs=[HBM_SPEC] * (2 * n) + [SEM_SPEC, SEM_SPEC, ANY],
        out_specs=tuple([HBM_SPEC] * (2 * n)),
        input_output_aliases={a: a for a in range(2 * n)},
        compiler_params=pltpu.CompilerParams(has_side_effects=DATAFLOW),
    )(*h["srcs"], *h["lands"], h["send"], h["recv"], after)
    return list(outs[n:])


PACK_COLS = 1024
PACK_SEG = 16 * PACK_COLS

BIG = {"ssm_in_proj": 2, "ssm_out_proj": 1, "w_kv": 1, "w_q": 1, "w_o": 1, "ffn_up": 2, "ffn_down": 1,
       "ple_gate": 1, "ple_proj": 2}
SMALL = {"ssm_conv_w": 2, "ssm_conv_b": 1, "ssm_norm": 1, "ffn_conv_w": 2}
REPL = ["attn_norm", "ffn_norm", "ple_norm", "ssm_dt_bias", "ssm_a_log", "ssm_d", "ffn_conv_b", "kv_norm", "final_norm"]


def _seg(n):
    return -(-n // PACK_SEG) * PACK_SEG


def _pack_rows(flats, lead=()):
    padded = [jnp.pad(f, [(0, 0)] * len(lead) + [(0, _seg(f.shape[-1]) - f.shape[-1])]) for f in flats]
    return jnp.concatenate(padded, axis=-1).reshape(*lead, -1, PACK_COLS)


def _unpack_rows(buf, sizes, lead):
    flat = buf.reshape(*lead, -1)
    out, off = [], 0
    for n in sizes:
        out.append(lax.slice_in_dim(flat, off, off + n, axis=len(lead)))
        off += _seg(n)
    return out


def _to_slots(full, axis):
    shp = full.shape
    blk = shp[axis] // N_DEV
    t = full.reshape(shp[:axis] + (N_DEV, blk) + shp[axis + 1:])
    return jnp.moveaxis(t, axis, 0).reshape(N_DEV, -1)


def _from_slots(slots, local_shape, axis):
    t = jnp.moveaxis(slots.reshape((N_DEV,) + tuple(local_shape)), 0, axis)
    shp = list(local_shape)
    shp[axis] *= N_DEV
    return t.reshape(shp)


def _as2d(a):
    return a.reshape(1, -1) if a.ndim == 1 else a.reshape(-1, a.shape[-1])


def _pad_rows(w, rows=SUBLANE):
    return jnp.pad(w, ((0, rows - w.shape[0]), (0, 0)))


def kernel(x, p, attn_norm, ffn_norm, ple_norm, ssm_in_proj, ssm_conv_w, ssm_conv_b, ssm_dt_bias, ssm_a_log, ssm_d, ssm_norm, ssm_out_proj, kv_norm, w_kv, w_q, w_o, ffn_up, ffn_conv_w, ffn_conv_b, ffn_down, ple_gate, ple_proj, final_norm, loss_target, m_attn_norm, m_ffn_norm, m_ple_norm, m_ssm_in_proj, m_ssm_conv_w, m_ssm_conv_b, m_ssm_dt_bias, m_ssm_a_log, m_ssm_d, m_ssm_norm, m_ssm_out_proj, m_kv_norm, m_w_kv, m_w_q, m_w_o, m_ffn_up, m_ffn_conv_w, m_ffn_conv_b, m_ffn_down, m_ple_gate, m_ple_proj, m_final_norm, v_attn_norm, v_ffn_norm, v_ple_norm, v_ssm_in_proj, v_ssm_conv_w, v_ssm_conv_b, v_ssm_dt_bias, v_ssm_a_log, v_ssm_d, v_ssm_norm, v_ssm_out_proj, v_kv_norm, v_w_kv, v_w_q, v_w_o, v_ffn_up, v_ffn_conv_w, v_ffn_conv_b, v_ffn_down, v_ple_gate, v_ple_proj, v_final_norm):
    given = dict(locals())
    wnames = list(BIG) + list(SMALL) + REPL
    bsz, seq, d = x.shape
    t = bsz * seq
    bs = lambda a: a.reshape(bsz, seq, a.shape[-1])
    fl = lambda a: a.reshape(t, a.shape[-1])

    big_local = [given[n] for n in BIG]
    small_local = [given[n] for n in SMALL]
    me = 4 * lax.axis_index("x") + 2 * lax.axis_index("y") + lax.axis_index("c")

    N_GROUPS = 2 * DEPTH

    def layer_items(k):
        i, j = k // 2, k // 2 - N_A
        if k % 2 == 0:
            return [("ssm_in_proj", i), ("ssm_out_proj", i)] if i < N_A else [("w_q", j), ("w_o", j)]
        shared = [("w_kv", None)] if i == N_A - 1 else []
        return [("ffn_up", i), ("ffn_down", i), ("ple_gate", i), ("ple_proj", i)] + shared

    def local_block(n, idx):
        return given[n] if idx is None else given[n][idx]

    def put_own(land, own):
        return lax.dynamic_update_slice(land, own[None], (me,) + (0,) * own.ndim)

    small_pack = _pack_rows([w.reshape(-1) for w in small_local])
    gather_src, gather_h = [], []
    for i in range(N_GROUPS):
        src = [local_block(n, idx).astype(BF16) for n, idx in layer_items(i)] + ([small_pack] if i == 0 else [])
        gather_src.append(src)
        gather_h.append(exchange_start("gather", src, f"ag_start{i}"))
    full = {n: {} for n in BIG}
    rows_of = lambda g: g.reshape(-1, g.shape[-1])
    cols_of = lambda g: jnp.transpose(g, (1, 0, 2)).reshape(g.shape[1], -1)

    def fetch_layer(i, after):
        lands = exchange_wait(gather_h[i], after, f"ag_wait{i}")
        got = [put_own(l, s) for l, s in zip(lands, gather_src[i])]
        for (n, idx), g in zip(layer_items(i), got):
            if n == "ffn_up":
                full[n][idx] = (cols_of(g[:N_DEV // 2]), cols_of(g[N_DEV // 2:]))
            else:
                full[n][idx] = cols_of(g) if BIG[n] == given[n].ndim - 1 else rows_of(g)
        if i == 0:
            for (n, ax), w, s in zip(SMALL.items(), small_local, _unpack_rows(got[-1], [w.size for w in small_local], (N_DEV,))):
                full[n] = _from_slots(s, w.shape, ax)

    row = lambda v: v.reshape(1, -1)
    pad_lane = lambda v: jnp.pad(v.reshape(1, -1), ((0, 0), (0, LANE - v.size)))
    h = x.reshape(t, d)
    saved = []
    kv = hnkv = h_kv = None
    for i in range(DEPTH):
        fetch_layer(2 * i, h)
        sv = {"h0": h}
        gain = row(attn_norm[i])
        if i == 0:
            gain = gain + sum(hd["token"][0, 0] for hd in gather_h)
        hn = rmsnorm_fwd(h, gain, f"attn_norm_f{i}")
        sv["hn"] = hn
        if i < N_A:
            w_in = full["ssm_in_proj"][i]
            wz, wxbc = w_in[:, :SSM_D_INNER], w_in[:, SSM_D_INNER:SSM_ZX]
            wdt = jnp.pad(w_in[:, SSM_ZX:], ((0, 0), (0, LANE - SSM_HEADS)))
            z = matmul(hn, wz, "nn", F32, f"ssm_z_f{i}")
            xbc = bs(matmul(hn, wxbc, "nn", F32, f"ssm_xbc_f{i}"))
            dtr = bs(matmul(hn, wdt, "nn", F32, f"ssm_dt_f{i}"))
            cw = _pad_rows(full["ssm_conv_w"][i])
            cb = row(full["ssm_conv_b"][i])
            xc = ssm_conv_fwd(xbc, cw, cb, f"ssm_conv_f{i}")
            dtb, alog = pad_lane(ssm_dt_bias[i]), pad_lane(ssm_a_log[i])
            dexp = jnp.repeat(ssm_d[i], SSM_HEAD_DIM).reshape(1, -1)
            y, sprev = ssd_fwd(xc, dtr, dtb, alog, dexp, f"ssd_f{i}")
            gn = gated_norm_fwd(fl(y), z, row(full["ssm_norm"][i]), f"ssm_gnorm_f{i}")
            h1 = matmul(gn, full["ssm_out_proj"][i], "nn", F32, f"ssm_out_f{i}", add=h)
            sv.update(wz=wz, wxbc=wxbc, wdt=wdt, z=z, xbc=xbc, dtr=dtr, cw=cw, cb=cb, xc=xc, dtb=dtb, alog=alog,
                      dexp=dexp, y=y, sprev=sprev, gn=gn)
        else:
            j = i - N_A
            q = bs(matmul(hn, full["w_q"][j], "nn", BF16, f"sb_q_f{j}"))
            o, ltot = sb_attention_fwd(q, kv, f"sb_attn_f{j}")
            h1 = matmul(fl(o), full["w_o"][j], "nn", F32, f"sb_o_f{j}", add=h)
            sv.update(q=q, o=o, ltot=ltot)
        sv["h1"] = h1
        fetch_layer(2 * i + 1, h1)
        hn2 = rmsnorm_fwd(h1, row(ffn_norm[i]), f"ffn_norm_f{i}")
        wug, wuv = full["ffn_up"][i]
        ug = bs(matmul(hn2, wug, "nn", F32, f"ffn_upg_f{i}"))
        uv = bs(matmul(hn2, wuv, "nn", F32, f"ffn_upv_f{i}"))
        fcw = full["ffn_conv_w"][i]
        fwg, fwv = _pad_rows(fcw[:, :D_FF]), _pad_rows(fcw[:, D_FF:])
        fbg, fbv = row(ffn_conv_b[i, :D_FF]), row(ffn_conv_b[i, D_FF:])
        act = ffn_conv_fwd(ug, uv, fwg, fwv, fbg, fbv, f"ffn_conv_f{i}")
        h2 = matmul(fl(act), full["ffn_down"][i], "nn", F32, f"ffn_down_f{i}", add=h1)
        hn3 = rmsnorm_fwd(h2, row(ple_norm[i]), f"ple_norm_f{i}")
        gpre = matmul(hn3, full["ple_gate"][i], "nn", F32, f"ple_gate_f{i}")
        p_i = p[i].reshape(t, PLE_DIM)
        pp = matmul(p_i, full["ple_proj"][i], "nn", F32, f"ple_proj_f{i}")
        h3 = ple_fwd(h2, gpre, pp, f"ple_f{i}")
        sv.update(hn2=hn2, wug=wug, wuv=wuv, ug=ug, uv=uv, fwg=fwg, fwv=fwv, fbg=fbg, fbv=fbv, act=act, h2=h2,
                  hn3=hn3, gpre=gpre, pp=pp, p_i=p_i)
        if i == N_A - 1:
            h_kv = h3
            hnkv = rmsnorm_fwd(h3, row(kv_norm), "kv_norm_f")
            kv = bs(matmul(hnkv, full["w_kv"][None], "nn", BF16, "kv_f"))
        h = h3
        saved.append(sv)

    loss_row, dh, g_final = loss_head(h, row(final_norm), loss_target.reshape(t, d), "loss_head")
    loss = lax.psum(loss_row[0, 0], ("x", "y", "c"))

    G = {n: [None] * given[n].shape[0] for n in wnames if given[n].ndim > 1 and n not in ("w_kv",)}
    G["final_norm"] = g_final[0]

    def col_slots(g, ndev=N_DEV):
        return jnp.transpose(g.reshape(g.shape[0], ndev, -1), (1, 0, 2))

    def grad_slots(n, idx):
        g = G[n] if idx is None else G[n][idx]
        if n == "ffn_up":
            s = jnp.concatenate([col_slots(g[0], N_DEV // 2), col_slots(g[1], N_DEV // 2)], axis=0)
        elif BIG[n] == given[n].ndim - 1:
            s = col_slots(g)
        else:
            s = g.reshape(N_DEV, -1, g.shape[-1])
        return s.astype(BF16)

    scatter_src, scatter_h = {}, {}
    dk = jnp.zeros((bsz, seq, SB_WIDTH), F32)
    dv = jnp.zeros((bsz, seq, SB_WIDTH), F32)
    for i in reversed(range(DEPTH)):
        sv = saved[i]
        if i == N_A - 1:
            wk, wv = full["w_kv"][None][:, :SB_WIDTH], full["w_kv"][None][:, SB_WIDTH:]
            dkf, dvf = fl(dk), fl(dv)
            dhn = matmul(dkf, wk, "nt", F32, "kv_dx_k")
            dhn = matmul(dvf, wv, "nt", F32, "kv_dx_v", add=dhn)
            G["w_kv"] = jnp.concatenate([matmul(hnkv, dkf, "tn", F32, "kv_dw_k"), matmul(hnkv, dvf, "tn", F32, "kv_dw_v")], axis=1)
            dh, gk = rmsnorm_bwd(dhn, h_kv, row(kv_norm), dh, "kv_norm_b")
            G["kv_norm"] = gk[0]
        dgpre, dpp = ple_bwd(dh, sv["gpre"], sv["pp"], f"ple_b{i}")
        G["ple_proj"][i] = matmul(sv["p_i"], dpp, "tn", F32, f"ple_proj_dw{i}")
        G["ple_gate"][i] = matmul(sv["hn3"], dgpre, "tn", F32, f"ple_gate_dw{i}")
        dhn = matmul(dgpre, full["ple_gate"][i], "nt", F32, f"ple_gate_dx{i}")
        gain = row(ple_norm[i])
        if 2 * i + 2 in scatter_h:
            gain = gain + scatter_h[2 * i + 2]["token"][0, 0]
        dh, gn_ = rmsnorm_bwd(dhn, sv["h2"], gain, dh, f"ple_norm_b{i}")
        G["ple_norm"][i] = gn_[0]
        dact = bs(matmul(dh, full["ffn_down"][i], "nt", F32, f"ffn_down_dx{i}"))
        G["ffn_down"][i] = matmul(fl(sv["act"]), dh, "tn", F32, f"ffn_down_dw{i}")
        dug, duv, dwg, dwv, dbg, dbv = ffn_conv_bwd(sv["ug"], sv["uv"], dact, sv["fwg"], sv["fwv"], sv["fbg"], sv["fbv"], f"ffn_conv_b{i}")
        dug, duv = fl(dug), fl(duv)
        G["ffn_conv_w"][i] = jnp.concatenate([dwg[:FFN_CONV], dwv[:FFN_CONV]], axis=1)
        G["ffn_conv_b"][i] = jnp.concatenate([dbg[0], dbv[0]])
        G["ffn_up"][i] = (matmul(sv["hn2"], dug, "tn", F32, f"ffn_upg_dw{i}"), matmul(sv["hn2"], duv, "tn", F32, f"ffn_upv_dw{i}"))
        dhn = matmul(dug, sv["wug"], "nt", F32, f"ffn_upg_dx{i}")
        dhn = matmul(duv, sv["wuv"], "nt", F32, f"ffn_upv_dx{i}", add=dhn)
        dh, gn_ = rmsnorm_bwd(dhn, sv["h1"], row(ffn_norm[i]), dh, f"ffn_norm_b{i}")
        G["ffn_norm"][i] = gn_[0]
        scatter_src[2 * i + 1] = [grad_slots(n, idx) for n, idx in layer_items(2 * i + 1)]
        scatter_h[2 * i + 1] = exchange_start("scatter", scatter_src[2 * i + 1], f"a2a_start{2 * i + 1}")
        tok = scatter_h[2 * i + 1]["token"][0, 0]
        if i < N_A:
            dgn = matmul(dh, full["ssm_out_proj"][i], "nt", F32, f"ssm_out_dx{i}")
            G["ssm_out_proj"][i] = matmul(sv["gn"], dh, "tn", F32, f"ssm_out_dw{i}")
            dy, dz, dnw = gated_norm_bwd(dgn, fl(sv["y"]), sv["z"], row(full["ssm_norm"][i]) + tok, f"ssm_gnorm_b{i}")
            G["ssm_norm"][i] = dnw[0]
            dxs, dbm, dcm, ddtr4, da4, dbias4, ddexp = ssd_bwd(sv["xc"], sv["dtr"], sv["dtb"], sv["alog"], sv["dexp"], bs(dy), sv["sprev"], f"ssd_b{i}")
            G["ssm_a_log"][i] = da4[:, 0, :HPG].reshape(-1)
            G["ssm_dt_bias"][i] = dbias4[:, 0, :HPG].reshape(-1)
            G["ssm_d"][i] = ddexp.reshape(SSM_HEADS, SSM_HEAD_DIM).sum(axis=-1)
            dxc = jnp.concatenate([dxs, dbm, dcm], axis=-1)
            dxbc, dcw, dcb = ssm_conv_bwd(sv["xbc"], dxc, sv["cw"], sv["cb"], f"ssm_conv_b{i}")
            dxbc = fl(dxbc)
            G["ssm_conv_w"][i] = dcw[:SSM_CONV]
            G["ssm_conv_b"][i] = dcb[0]
            ddtr = ddtr4.reshape(t, SSM_GROUPS, LANE)[:, :, :HPG].reshape(t, SSM_HEADS)
            ddtr = jnp.pad(ddtr, ((0, 0), (0, LANE - SSM_HEADS))).astype(BF16)
            hn = sv["hn"]
            G["ssm_in_proj"][i] = jnp.concatenate([
                matmul(hn, dz, "tn", F32, f"ssm_z_dw{i}"), matmul(hn, dxbc, "tn", F32, f"ssm_xbc_dw{i}"),
                matmul(hn, ddtr, "tn", F32, f"ssm_dt_dw{i}")[:, :SSM_HEADS]], axis=1)
            dhn = matmul(dz, sv["wz"], "nt", F32, f"ssm_z_dx{i}")
            dhn = matmul(dxbc, sv["wxbc"], "nt", F32, f"ssm_xbc_dx{i}", add=dhn)
            dhn = matmul(ddtr, sv["wdt"], "nt", F32, f"ssm_dt_dx{i}", add=dhn)
        else:
            j = i - N_A
            do = bs(matmul(dh, full["w_o"][j], "nt", BF16, f"sb_o_dx{j}"))
            G["w_o"][j] = matmul(fl(sv["o"]), dh, "tn", F32, f"sb_o_dw{j}")
            dq, dk, dv = sb_attention_bwd(sv["q"], kv, sv["ltot"], do, dk, dv, f"sb_attn_b{j}")
            dq = fl(dq)
            G["w_q"][j] = matmul(sv["hn"], dq, "tn", F32, f"sb_q_dw{j}")
            dhn = matmul(dq, full["w_q"][j], "nt", F32, f"sb_q_dx{j}")
        dh, gn_ = rmsnorm_bwd(dhn, sv["h0"], row(attn_norm[i]) + (0.0 if i < N_A else tok), dh, f"attn_norm_b{i}")
        G["attn_norm"][i] = gn_[0]
        src = [grad_slots(n, idx) for n, idx in layer_items(2 * i)]
        if i == 0:
            small_g = {n: jnp.stack(G[n]) for n in SMALL}
            src.append(_pack_rows([_to_slots(small_g[n], SMALL[n]) for n in SMALL], (N_DEV,)))
        scatter_src[2 * i] = src
        scatter_h[2 * i] = exchange_start("scatter", src, f"a2a_start{2 * i}")
    grad_x = dh.reshape(bsz, seq, d)
    grads = {n: jnp.stack(G[n]) if isinstance(G[n], list) else G[n] for n in REPL}

    layer_parts = {n: {} for n in BIG}
    small_parts = None
    for i in reversed(range(N_GROUPS)):
        lands = exchange_wait(scatter_h[i], dh, f"a2a_wait{i}")
        got = [put_own(l, lax.dynamic_index_in_dim(s, me, 0, keepdims=False)) for l, s in zip(lands, scatter_src[i])]
        for (n, idx), g in zip(layer_items(i), got):
            layer_parts[n][idx] = g
        if i == 0:
            small_parts = _unpack_rows(got[-1], [given[n].size for n in SMALL], (N_DEV,))
    big_parts = [jnp.concatenate([layer_parts[n][k] for k in sorted(layer_parts[n], key=lambda v: -1 if v is None else v)], axis=1)
                 for n in BIG]
    parts_rp = all_gather([_pack_rows([grads[n].reshape(-1) for n in REPL])], "ag_repl_grads")[0]
    sh_names = list(BIG) + list(SMALL)
    sh_parts = big_parts + small_parts
    rp_parts = _unpack_rows(parts_rp, [given[n].size for n in REPL], (N_DEV,))
    res = {}
    for n, pr in list(zip(sh_names, sh_parts)) + list(zip(REPL, rp_parts)):
        w2 = _as2d(given[n])
        outs = adamw(w2, pr.reshape((N_DEV,) + w2.shape), _as2d(given["m_" + n]), _as2d(given["v_" + n]), f"adamw_{n}")
        res[n] = [o.reshape(given[n].shape) for o in outs]
    order = ["attn_norm", "ffn_norm", "ple_norm", "ssm_in_proj", "ssm_conv_w", "ssm_conv_b", "ssm_dt_bias", "ssm_a_log",
             "ssm_d", "ssm_norm", "ssm_out_proj", "kv_norm", "w_kv", "w_q", "w_o", "ffn_up", "ffn_conv_w", "ffn_conv_b",
             "ffn_down", "ple_gate", "ple_proj", "final_norm"]
    return (loss, grad_x, *[res[n][0] for n in order], *[res[n][1] for n in order],
            *[res[n][2] for n in order], *[res[n][3] for n in order])
```

```python
import functools
import math

import jax
import jax.numpy as jnp
from jax import lax
from jax.experimental import pallas as pl
from jax.experimental.pallas import tpu as pltpu

F32 = jnp.float32
BF16 = jnp.bfloat16

N_DEV = 8
D_MODEL = 1024
SEQ = 2048
DEPTH = 4
N_A = 2
N_B = 2
SSM_D_INNER = 2048
SSM_HEAD_DIM = 64
SSM_HEADS = 32
SSM_GROUPS = 4
SSM_STATE = 128
SSM_CONV = 4
SSM_CHUNK = 128
SSM_CONV_DIM = 3072
SSM_ZX = 5120
SSM_IN_DIM = 5152
SB_HEADS = 16
SB_HEAD_DIM = 64
SB_WIDTH = 1024
D_FF = 2816
FFN_CONV = 3
PLE_DIM = 256
NORM_EPS = 1e-6
SSM_NORM_EPS = 1e-5

ADAM_LR = 0.001
ADAM_B1 = 0.9
ADAM_B2 = 0.999
ADAM_EPS = 1e-08
ADAM_WD = 0.01
ADAM_STEP = 10

LANE = 128
SUBLANE = 8
VMEM_LIMIT = 48 * 1024 * 1024

NN = (((1,), (0,)), ((), ()))
NT = (((1,), (1,)), ((), ()))
TN = (((0,), (0,)), ((), ()))


def _params(sem):
    return pltpu.CompilerParams(dimension_semantics=sem, vmem_limit_bytes=VMEM_LIMIT)


def _pick(n, prefs):
    for p in prefs:
        if n % p == 0:
            return p
    return n


def matmul(a, b, mode, out_dtype, name, add=None):
    if mode == "nn":
        (m, k), (k2, n) = a.shape, b.shape
    elif mode == "nt":
        (m, k), (n, k2) = a.shape, b.shape
    else:
        (k, m), (k2, n) = a.shape, b.shape
    assert k == k2, (a.shape, b.shape, mode)
    tm = _pick(m, (1408, 512, 256, 128))
    tn = _pick(n, (1408, 1024, 512, 256, 128))
    tk = _pick(k, (2816, 2048, 1024, 512, 256, 128))

    def vmem_bytes(tk_):
        blocks = tm * tk_ * a.dtype.itemsize + tk_ * tn * b.dtype.itemsize + tm * tn * jnp.dtype(out_dtype).itemsize
        blocks += tm * tn * add.dtype.itemsize if add is not None else 0
        return 2 * blocks + (tm * tn * 4 if k // tk_ > 1 else 0)

    while vmem_bytes(tk) > (VMEM_LIMIT * 3) // 4 and tk % 256 == 0:
        tk //= 2
    nk = k // tk
    dims = {"nn": NN, "nt": NT, "tn": TN}[mode]

    def body(*refs):
        a_ref, b_ref = refs[:2]
        add_ref = refs[2] if add is not None else None
        o_ref = refs[-2] if nk > 1 else refs[-1]
        part = lax.dot_general(a_ref[...].astype(BF16), b_ref[...].astype(BF16), dims, preferred_element_type=F32)

        def finish(r):
            if add_ref is not None:
                r = r + add_ref[...].astype(F32)
            o_ref[...] = r.astype(o_ref.dtype)

        if nk == 1:
            finish(part)
            return
        acc_ref = refs[-1]
        kk = pl.program_id(2)

        @pl.when(kk == 0)
        def _():
            acc_ref[...] = part

        @pl.when((kk > 0) & (kk < nk - 1))
        def _():
            acc_ref[...] += part

        @pl.when(kk == nk - 1)
        def _():
            finish(acc_ref[...] + part)

    if mode == "nn":
        a_spec = pl.BlockSpec((tm, tk), lambda i, j, kk: (i, kk))
        b_spec = pl.BlockSpec((tk, tn), lambda i, j, kk: (kk, j))
    elif mode == "nt":
        a_spec = pl.BlockSpec((tm, tk), lambda i, j, kk: (i, kk))
        b_spec = pl.BlockSpec((tn, tk), lambda i, j, kk: (j, kk))
    else:
        a_spec = pl.BlockSpec((tk, tm), lambda i, j, kk: (kk, i))
        b_spec = pl.BlockSpec((tk, tn), lambda i, j, kk: (kk, j))
    o_spec = pl.BlockSpec((tm, tn), lambda i, j, kk: (i, j))
    in_specs = [a_spec, b_spec] + ([o_spec] if add is not None else [])
    args = (a, b) + ((add,) if add is not None else ())
    return pl.pallas_call(
        body,
        name=name,
        grid=(m // tm, n // tn, nk),
        in_specs=in_specs,
        out_specs=o_spec,
        out_shape=jax.ShapeDtypeStruct((m, n), out_dtype),
        scratch_shapes=[pltpu.VMEM((tm, tn), F32)] if nk > 1 else [],
        compiler_params=_params(("parallel", "parallel", "arbitrary")),
    )(*args)


ROW_TILE = 512


def rmsnorm_fwd(x, gain, name):
    t, d = x.shape

    def body(x_ref, g_ref, o_ref):
        xv = x_ref[...]
        r = lax.rsqrt(jnp.mean(xv * xv, axis=-1, keepdims=True) + NORM_EPS)
        o_ref[...] = (xv * r * g_ref[...]).astype(o_ref.dtype)

    return pl.pallas_call(
        body,
        name=name,
        grid=(t // ROW_TILE,),
        in_specs=[pl.BlockSpec((ROW_TILE, d), lambda i: (i, 0)), pl.BlockSpec((1, d), lambda i: (0, 0))],
        out_specs=pl.BlockSpec((ROW_TILE, d), lambda i: (i, 0)),
        out_shape=jax.ShapeDtypeStruct((t, d), BF16),
        compiler_params=_params(("parallel",)),
    )(x, gain)


def rmsnorm_bwd(dy, x, gain, dres, name):
    t, d = x.shape

    def body(dy_ref, x_ref, g_ref, dres_ref, dx_ref, dg_ref, dx16_ref):
        i = pl.program_id(0)
        xv = x_ref[...]
        r = lax.rsqrt(jnp.mean(xv * xv, axis=-1, keepdims=True) + NORM_EPS)
        xh = xv * r
        dyv = dy_ref[...].astype(F32)
        dxh = dyv * g_ref[...]
        dx = r * (dxh - xh * jnp.mean(dxh * xh, axis=-1, keepdims=True))
        total = dres_ref[...] + dx
        dx_ref[...] = total
        dx16_ref[...] = total.astype(dx16_ref.dtype)

        @pl.when(i == 0)
        def _():
            dg_ref[...] = jnp.zeros_like(dg_ref)

        dg_ref[...] += jnp.sum(dyv * xh, axis=0, keepdims=True)

    row = pl.BlockSpec((ROW_TILE, d), lambda i: (i, 0))
    vec = pl.BlockSpec((1, d), lambda i: (0, 0))
    return pl.pallas_call(
        body,
        name=name,
        grid=(t // ROW_TILE,),
        in_specs=[row, row, vec, row],
        out_specs=[row, vec, row],
        out_shape=[jax.ShapeDtypeStruct((t, d), F32), jax.ShapeDtypeStruct((1, d), F32), jax.ShapeDtypeStruct((t, d), BF16)],
        compiler_params=_params(("arbitrary",)),
    )(dy, x, gain, dres)


CONV_ROWS = 512
HALO = SUBLANE


def _conv_apply(ext, w_ref, bias, k, lo, n):
    acc = bias + w_ref[k - 1:k, :] * ext[lo:lo + n]
    for j in range(1, k):
        acc = acc + w_ref[k - 1 - j:k - j, :] * pltpu.roll(ext, j, 0)[lo:lo + n]
    return acc


def _conv_apply_t(ext, w_ref, k, n):
    rows = ext.shape[0]
    acc = w_ref[k - 1:k, :] * ext[:n]
    for j in range(1, k):
        acc = acc + w_ref[k - 1 - j:k - j, :] * pltpu.roll(ext, rows - j, 0)[:n]
    return acc


def _conv_dw(dpre_main, ext_u, k, n, dw_ref):
    for kk in range(k):
        j = k - 1 - kk
        sh = ext_u[HALO:HALO + n] if j == 0 else pltpu.roll(ext_u, j, 0)[HALO:HALO + n]
        dw_ref[kk:kk + 1, :] += jnp.sum(dpre_main * sh, axis=0, keepdims=True)


def _silu_grad(pre):
    sg = jax.nn.sigmoid(pre)
    return sg * (1.0 + pre * (1.0 - sg))


def _conv_blockspecs(seq, r, tc):
    nh = r // HALO
    last = seq // HALO - 1
    main = pl.BlockSpec((None, r, tc), lambda c, b, s: (b, s, c))
    before = pl.BlockSpec((None, HALO, tc), lambda c, b, s: (b, jnp.maximum(s * nh - 1, 0), c))
    after = pl.BlockSpec((None, HALO, tc), lambda c, b, s: (b, jnp.minimum((s + 1) * nh, last), c))
    return main, before, after


def ssm_conv_fwd(u, w, bias, name):
    bsz, seq, ch = u.shape
    r, tc, k = min(CONV_ROWS, seq), 512, SSM_CONV
    main, before, _ = _conv_blockspecs(seq, r, tc)
    wspec = pl.BlockSpec((SUBLANE, tc), lambda c, b, s: (0, c))
    bspec = pl.BlockSpec((1, tc), lambda c, b, s: (0, c))

    def body(u_ref, hb_ref, w_ref, b_ref, o_ref):
        s = pl.program_id(2)
        hb = jnp.where(s == 0, 0.0, hb_ref[...])
        ext = jnp.concatenate([hb, u_ref[...]], axis=0)
        pre = _conv_apply(ext, w_ref, b_ref[...], k, HALO, r)
        o_ref[...] = pre * jax.nn.sigmoid(pre)

    return pl.pallas_call(
        body, name=name, grid=(ch // tc, bsz, seq // r),
        in_specs=[main, before, wspec, bspec], out_specs=main,
        out_shape=jax.ShapeDtypeStruct(u.shape, F32),
        compiler_params=_params(("parallel", "parallel", "parallel")),
    )(u, u, w, bias)


def ssm_conv_bwd(u, dxc, w, bias, name):
    bsz, seq, ch = u.shape
    r, tc, k = min(CONV_ROWS, seq), 512, SSM_CONV
    ns = seq // r
    main, before, after = _conv_blockspecs(seq, r, tc)
    wspec = pl.BlockSpec((SUBLANE, tc), lambda c, b, s: (0, c))
    bspec = pl.BlockSpec((1, tc), lambda c, b, s: (0, c))

    def body(u_ref, hb_ref, ha_ref, d_ref, da_ref, w_ref, b_ref, du_ref, dw_ref, db_ref):
        b, s = pl.program_id(1), pl.program_id(2)
        hb = jnp.where(s == 0, 0.0, hb_ref[...])
        ext_u = jnp.concatenate([hb, u_ref[...], ha_ref[...]], axis=0)
        pre = _conv_apply(ext_u, w_ref, b_ref[...], k, HALO, r + HALO)
        dxe = jnp.concatenate([d_ref[...], jnp.where(s == ns - 1, 0.0, da_ref[...])], axis=0)
        dpre = dxe * _silu_grad(pre)
        du_ref[...] = _conv_apply_t(dpre, w_ref, k, r).astype(du_ref.dtype)

        @pl.when((b == 0) & (s == 0))
        def _():
            dw_ref[...] = jnp.zeros_like(dw_ref)
            db_ref[...] = jnp.zeros_like(db_ref)

        dpm = dpre[:r]
        _conv_dw(dpm, ext_u, k, r, dw_ref)
        db_ref[...] += jnp.sum(dpm, axis=0, keepdims=True)

    return pl.pallas_call(
        body, name=name, grid=(ch // tc, bsz, ns),
        in_specs=[main, before, after, main, after, wspec, bspec],
        out_specs=[main, wspec, bspec],
        out_shape=[jax.ShapeDtypeStruct(u.shape, BF16), jax.ShapeDtypeStruct((SUBLANE, ch), F32),
                   jax.ShapeDtypeStruct((1, ch), F32)],
        compiler_params=_params(("arbitrary", "arbitrary", "arbitrary")),
    )(u, u, u, dxc, dxc, w, bias)


def ffn_conv_fwd(ug, uv, wg, wv, bg, bv, name):
    bsz, seq, ch = ug.shape
    r, tc, k = min(CONV_ROWS, seq), 256, FFN_CONV
    main, before, _ = _conv_blockspecs(seq, r, tc)
    wspec = pl.BlockSpec((SUBLANE, tc), lambda c, b, s: (0, c))
    bspec = pl.BlockSpec((1, tc), lambda c, b, s: (0, c))

    def body(ug_ref, hg_ref, uv_ref, hv_ref, wg_ref, wv_ref, bg_ref, bv_ref, o_ref):
        s = pl.program_id(2)
        eg = jnp.concatenate([jnp.where(s == 0, 0.0, hg_ref[...]), ug_ref[...]], axis=0)
        ev = jnp.concatenate([jnp.where(s == 0, 0.0, hv_ref[...]), uv_ref[...]], axis=0)
        pg = _conv_apply(eg, wg_ref, bg_ref[...], k, HALO, r)
        pv = _conv_apply(ev, wv_ref, bv_ref[...], k, HALO, r)
        o_ref[...] = (pg * jax.nn.sigmoid(pg) * pv).astype(o_ref.dtype)

    return pl.pallas_call(
        body, name=name, grid=(ch // tc, bsz, seq // r),
        in_specs=[main, before, main, before, wspec, wspec, bspec, bspec], out_specs=main,
        out_shape=jax.ShapeDtypeStruct(ug.shape, BF16),
        compiler_params=_params(("parallel", "parallel", "parallel")),
    )(ug, ug, uv, uv, wg, wv, bg, bv)


def ffn_conv_bwd(ug, uv, dact, wg, wv, bg, bv, name):
    bsz, seq, ch = ug.shape
    r, tc, k = min(CONV_ROWS, seq), 256, FFN_CONV
    ns = seq // r
    main, before, after = _conv_blockspecs(seq, r, tc)
    wspec = pl.BlockSpec((SUBLANE, tc), lambda c, b, s: (0, c))
    bspec = pl.BlockSpec((1, tc), lambda c, b, s: (0, c))

    def body(ug_ref, gb_ref, ga_ref, uv_ref, vb_ref, va_ref, d_ref, da_ref, wg_ref, wv_ref, bg_ref, bv_ref,
             dug_ref, duv_ref, dwg_ref, dwv_ref, dbg_ref, dbv_ref):
        b, s = pl.program_id(1), pl.program_id(2)
        eg = jnp.concatenate([jnp.where(s == 0, 0.0, gb_ref[...]), ug_ref[...], ga_ref[...]], axis=0)
        ev = jnp.concatenate([jnp.where(s == 0, 0.0, vb_ref[...]), uv_ref[...], va_ref[...]], axis=0)
        pg = _conv_apply(eg, wg_ref, bg_ref[...], k, HALO, r + HALO)
        pv = _conv_apply(ev, wv_ref, bv_ref[...], k, HALO, r + HALO)
        de = jnp.concatenate([d_ref[...], jnp.where(s == ns - 1, 0.0, da_ref[...])], axis=0)
        sg = jax.nn.sigmoid(pg)
        dpg = de * pv * (sg * (1.0 + pg * (1.0 - sg)))
        dpv = de * (pg * sg)
        dug_ref[...] = _conv_apply_t(dpg, wg_ref, k, r).astype(dug_ref.dtype)
        duv_ref[...] = _conv_apply_t(dpv, wv_ref, k, r).astype(duv_ref.dtype)

        @pl.when((b == 0) & (s == 0))
        def _():
            dwg_ref[...] = jnp.zeros_like(dwg_ref)
            dwv_ref[...] = jnp.zeros_like(dwv_ref)
            dbg_ref[...] = jnp.zeros_like(dbg_ref)
            dbv_ref[...] = jnp.zeros_like(dbv_ref)

        _conv_dw(dpg[:r], eg, k, r, dwg_ref)
        _conv_dw(dpv[:r], ev, k, r, dwv_ref)
        dbg_ref[...] += jnp.sum(dpg[:r], axis=0, keepdims=True)
        dbv_ref[...] += jnp.sum(dpv[:r], axis=0, keepdims=True)

    wshape = jax.ShapeDtypeStruct((SUBLANE, ch), F32)
    bshape = jax.ShapeDtypeStruct((1, ch), F32)
    return pl.pallas_call(
        body, name=name, grid=(ch // tc, bsz, ns),
        in_specs=[main, before, after, main, before, after, main, after, wspec, wspec, bspec, bspec],
        out_specs=[main, main, wspec, wspec, bspec, bspec],
        out_shape=[jax.ShapeDtypeStruct(ug.shape, BF16), jax.ShapeDtypeStruct(ug.shape, BF16), wshape, wshape, bshape, bshape],
        compiler_params=_params(("arbitrary", "arbitrary", "arbitrary")),
    )(ug, ug, ug, uv, uv, uv, dact, dact, wg, wv, bg, bv)


GN_ROWS = 256
GN_GROUP = SSM_D_INNER // SSM_GROUPS


def gated_norm_fwd(y, z, w, name):
    t, d = y.shape

    def body(y_ref, z_ref, w_ref, o_ref):
        for g in range(d // GN_GROUP):
            sl = slice(g * GN_GROUP, (g + 1) * GN_GROUP)
            zv = z_ref[:, sl]
            gv = y_ref[:, sl] * (zv * jax.nn.sigmoid(zv))
            r = lax.rsqrt(jnp.mean(gv * gv, axis=-1, keepdims=True) + SSM_NORM_EPS)
            o_ref[:, sl] = (gv * r * w_ref[:, sl]).astype(o_ref.dtype)

    row = pl.BlockSpec((GN_ROWS, d), lambda i: (i, 0))
    vec = pl.BlockSpec((1, d), lambda i: (0, 0))
    return pl.pallas_call(
        body, name=name, grid=(t // GN_ROWS,), in_specs=[row, row, vec], out_specs=row,
        out_shape=jax.ShapeDtypeStruct((t, d), BF16), compiler_params=_params(("parallel",)),
    )(y, z, w)


def gated_norm_bwd(dgn, y, z, w, name):
    t, d = y.shape

    def body(d_ref, y_ref, z_ref, w_ref, dy_ref, dz_ref, dw_ref):
        i = pl.program_id(0)

        @pl.when(i == 0)
        def _():
            dw_ref[...] = jnp.zeros_like(dw_ref)

        for g in range(d // GN_GROUP):
            sl = slice(g * GN_GROUP, (g + 1) * GN_GROUP)
            zv, yv, dv = z_ref[:, sl], y_ref[:, sl], d_ref[:, sl]
            sg = jax.nn.sigmoid(zv)
            sz = zv * sg
            gv = yv * sz
            r = lax.rsqrt(jnp.mean(gv * gv, axis=-1, keepdims=True) + SSM_NORM_EPS)
            gh = gv * r
            dgh = dv * w_ref[:, sl]
            dw_ref[:, sl] += jnp.sum(dv * gh, axis=0, keepdims=True)
            dg = r * (dgh - gh * jnp.mean(dgh * gh, axis=-1, keepdims=True))
            dy_ref[:, sl] = dg * sz
            dz_ref[:, sl] = (dg * yv * (sg * (1.0 + zv * (1.0 - sg)))).astype(dz_ref.dtype)

    row = pl.BlockSpec((GN_ROWS, d), lambda i: (i, 0))
    vec = pl.BlockSpec((1, d), lambda i: (0, 0))
    return pl.pallas_call(
        body, name=name, grid=(t // GN_ROWS,), in_specs=[row, row, row, vec], out_specs=[row, row, vec],
        out_shape=[jax.ShapeDtypeStruct((t, d), F32), jax.ShapeDtypeStruct((t, d), BF16), jax.ShapeDtypeStruct((1, d), F32)],
        compiler_params=_params(("arbitrary",)),
    )(dgn, y, z, w)


def ple_fwd(h, gpre, pp, name):
    t, d = h.shape

    def body(h_ref, g_ref, p_ref, o_ref):
        o_ref[...] = h_ref[...] + jax.nn.sigmoid(g_ref[...]) * p_ref[...]

    row = pl.BlockSpec((ROW_TILE, d), lambda i: (i, 0))
    return pl.pallas_call(
        body, name=name, grid=(t // ROW_TILE,), in_specs=[row, row, row], out_specs=row,
        out_shape=jax.ShapeDtypeStruct((t, d), F32), compiler_params=_params(("parallel",)),
    )(h, gpre, pp)


def ple_bwd(dh, gpre, pp, name):
    t, d = dh.shape

    def body(d_ref, g_ref, p_ref, dg_ref, dp_ref):
        sg = jax.nn.sigmoid(g_ref[...])
        dv = d_ref[...]
        dg_ref[...] = (dv * p_ref[...] * sg * (1.0 - sg)).astype(dg_ref.dtype)
        dp_ref[...] = (dv * sg).astype(dp_ref.dtype)

    row = pl.BlockSpec((ROW_TILE, d), lambda i: (i, 0))
    return pl.pallas_call(
        body, name=name, grid=(t // ROW_TILE,), in_specs=[row, row, row], out_specs=[row, row],
        out_shape=[jax.ShapeDtypeStruct((t, d), BF16), jax.ShapeDtypeStruct((t, d), BF16)],
        compiler_params=_params(("parallel",)),
    )(dh, gpre, pp)


def loss_head(h, gain, target, name):
    t, d = h.shape

    def body(x_ref, g_ref, t_ref, l_ref, dx_ref, dg_ref):
        i = pl.program_id(0)
        xv = x_ref[...]
        r = lax.rsqrt(jnp.mean(xv * xv, axis=-1, keepdims=True) + NORM_EPS)
        xh = xv * r
        err = xh * g_ref[...] - t_ref[...]
        part = 0.5 * jnp.sum(jnp.mean(err * err, axis=-1, keepdims=True), axis=0, keepdims=True)
        dyv = err * (1.0 / d)
        dxh = dyv * g_ref[...]
        dx_ref[...] = r * (dxh - xh * jnp.mean(dxh * xh, axis=-1, keepdims=True))

        @pl.when(i == 0)
        def _():
            dg_ref[...] = jnp.zeros_like(dg_ref)
            l_ref[...] = jnp.zeros_like(l_ref)

        dg_ref[...] += jnp.sum(dyv * xh, axis=0, keepdims=True)
        l_ref[...] += jnp.broadcast_to(part, l_ref.shape)

    row = pl.BlockSpec((ROW_TILE, d), lambda i: (i, 0))
    vec = pl.BlockSpec((1, d), lambda i: (0, 0))
    lspec = pl.BlockSpec((1, LANE), lambda i: (0, 0))
    return pl.pallas_call(
        body, name=name, grid=(t // ROW_TILE,), in_specs=[row, vec, row], out_specs=[lspec, row, vec],
        out_shape=[jax.ShapeDtypeStruct((1, LANE), F32), jax.ShapeDtypeStruct((t, d), F32), jax.ShapeDtypeStruct((1, d), F32)],
        compiler_params=_params(("arbitrary",)),
    )(h, gain, target)


CHUNK = SSM_CHUNK
HPG = SSM_HEADS // SSM_GROUPS
GW = HPG * SSM_HEAD_DIM
NEG = -1e30


def _split3(x):
    hi = x.astype(BF16)
    r1 = x - hi.astype(F32)
    mid = r1.astype(BF16)
    return hi, mid, (r1 - mid.astype(F32)).astype(BF16)


def _ssd_dt_terms(dtr_ref, bias_ref, alog_ref, g):
    shift = (LANE - HPG * g) % LANE
    li = lax.broadcasted_iota(jnp.int32, (CHUNK, CHUNK), 0)
    si = lax.broadcasted_iota(jnp.int32, (CHUNK, CHUNK), 1)
    x0 = pltpu.roll(dtr_ref[...] + bias_ref[...], shift, 1)
    dt = jnp.maximum(x0, 0.0) + jnp.log1p(jnp.exp(-jnp.abs(x0)))
    a = pltpu.roll(jnp.broadcast_to(-jnp.exp(alog_ref[...]), (CHUNK, LANE)), shift, 1)
    adt = dt * a
    acs = lax.dot_general((li >= si).astype(F32), adt, NN, precision=lax.Precision.HIGHEST, preferred_element_type=F32)
    last = jnp.broadcast_to(acs[CHUNK - 1:CHUNK, :], (CHUNK, LANE))
    e_last = jnp.exp(last)
    hr = lax.broadcasted_iota(jnp.int32, (LANE, GW), 0)
    hc = lax.broadcasted_iota(jnp.int32, (LANE, GW), 1)
    spread = (hc // SSM_HEAD_DIM == hr).astype(BF16)
    stack = jnp.concatenate([dt, jnp.exp(last - acs), jnp.exp(acs), e_last], axis=0)
    wide = jnp.dot(jnp.concatenate(_split3(stack), axis=1), jnp.concatenate([spread] * 3, axis=0), preferred_element_type=F32)
    return dict(li=li, si=si, x0=x0, dt=dt, a=a, acs=acs, acs_t=acs.T, e_last=e_last,
                dt_x=wide[:CHUNK], w_x=wide[CHUNK:2 * CHUNK], e_x=wide[2 * CHUNK:3 * CHUNK], el_x=wide[3 * CHUNK:])


def _ssd_pair_halves(parts):
    low = lax.broadcasted_iota(jnp.int32, (CHUNK, LANE), 1) < SSM_HEAD_DIM
    return jnp.concatenate([jnp.where(low, parts[2 * p], parts[2 * p + 1]) for p in range(HPG // 2)], axis=1)


def _ssd_specs(nc, rev):
    def ci(c):
        return nc - 1 - c if rev else c
    nb = SSM_D_INNER // LANE
    xs = pl.BlockSpec((None, CHUNK, GW), lambda g, b, c: (b, ci(c), g))
    bm = pl.BlockSpec((None, CHUNK, LANE), lambda g, b, c: (b, ci(c), nb + g))
    cm = pl.BlockSpec((None, CHUNK, LANE), lambda g, b, c: (b, ci(c), nb + SSM_GROUPS + g))
    dtr = pl.BlockSpec((None, CHUNK, LANE), lambda g, b, c: (b, ci(c), 0))
    vec = pl.BlockSpec((1, LANE), lambda g, b, c: (0, 0))
    dexp = pl.BlockSpec((1, GW), lambda g, b, c: (0, g))
    st = pl.BlockSpec((None, None, CHUNK, GW), lambda g, b, c: (b, ci(c), 0, g))
    return xs, bm, cm, dtr, vec, dexp, st


def ssd_fwd(xc, dtr, dt_bias, a_log, d_exp, name):
    bsz, seq, _ = xc.shape
    nc = seq // CHUNK
    xs_s, bm_s, cm_s, dtr_s, vec_s, dexp_s, st_s = _ssd_specs(nc, False)

    def body(xs_ref, b_ref, c_ref, dtr_ref, bias_ref, alog_ref, dexp_ref, y_ref, sp_ref, state_ref):
        g, c = pl.program_id(0), pl.program_id(2)

        @pl.when(c == 0)
        def _():
            state_ref[...] = jnp.zeros_like(state_ref)

        t = _ssd_dt_terms(dtr_ref, bias_ref, alog_ref, g)
        tril = t["li"] >= t["si"]
        xs = xs_ref[...]
        b16 = b_ref[...].astype(BF16)
        c16 = c_ref[...].astype(BF16)
        bt16 = b_ref[...].T.astype(BF16)
        cb = lax.dot_general(c16, b16, NT, preferred_element_type=F32)
        sp = state_ref[...]
        sp_ref[...] = sp
        cs = jnp.dot(c16, sp.astype(BF16), preferred_element_type=F32)
        xd = xs * t["dt_x"]
        xd16 = xd.astype(BF16)
        y_parts = []
        for r in range(HPG):
            pair = slice((r // 2) * LANE, (r // 2 + 1) * LANE)
            diff = t["acs"][:, r:r + 1] - t["acs_t"][r:r + 1, :]
            m16 = (cb * jnp.exp(jnp.where(tril, diff, NEG))).astype(BF16)
            y_parts.append(jnp.dot(m16, xd16[:, pair], preferred_element_type=F32))
        y_ref[...] = _ssd_pair_halves(y_parts) + t["e_x"] * cs + xs * dexp_ref[...]
        xw16 = (xd * t["w_x"]).astype(BF16)
        state_ref[...] = sp * t["el_x"] + jnp.dot(bt16, xw16, preferred_element_type=F32)

    return pl.pallas_call(
        body, name=name, grid=(SSM_GROUPS, bsz, nc),
        in_specs=[xs_s, bm_s, cm_s, dtr_s, vec_s, vec_s, dexp_s],
        out_specs=[xs_s, st_s],
        out_shape=[jax.ShapeDtypeStruct((bsz, seq, SSM_D_INNER), F32),
                   jax.ShapeDtypeStruct((bsz, nc, SSM_STATE, SSM_D_INNER), F32)],
        scratch_shapes=[pltpu.VMEM((SSM_STATE, GW), F32)],
        compiler_params=_params(("arbitrary", "arbitrary", "arbitrary")),
    )(xc, xc, xc, dtr, dt_bias, a_log, d_exp)


def ssd_bwd(xc, dtr, dt_bias, a_log, d_exp, dy, sprev, name):
    bsz, seq, _ = xc.shape
    nc = seq // CHUNK
    xs_s, bm_s, cm_s, dtr_s, vec_s, dexp_s, st_s = _ssd_specs(nc, True)
    grp = pl.BlockSpec((None, CHUNK, LANE), lambda g, b, c: (b, nc - 1 - c, g))
    acc = pl.BlockSpec((None, 1, LANE), lambda g, b, c: (g, 0, 0))

    def body(xs_ref, b_ref, c_ref, dtr_ref, bias_ref, alog_ref, dexp_ref, dy_ref, sp_ref,
             dxs_ref, db_ref, dc_ref, ddtr_ref, da_ref, dbias_ref, ddexp_ref, dstate_ref):
        g, b, c = pl.program_id(0), pl.program_id(1), pl.program_id(2)

        @pl.when(c == 0)
        def _():
            dstate_ref[...] = jnp.zeros_like(dstate_ref)

        @pl.when((b == 0) & (c == 0))
        def _():
            da_ref[...] = jnp.zeros_like(da_ref)
            dbias_ref[...] = jnp.zeros_like(dbias_ref)
            ddexp_ref[...] = jnp.zeros_like(ddexp_ref)

        t = _ssd_dt_terms(dtr_ref, bias_ref, alog_ref, g)
        li, si = t["li"], t["si"]
        tril, triu = li >= si, li <= si
        row_is_last = li[:, :1] == CHUNK - 1
        xs, dy = xs_ref[...], dy_ref[...]
        b16 = b_ref[...].astype(BF16)
        c16 = c_ref[...].astype(BF16)
        ct16 = c_ref[...].T.astype(BF16)
        cb = lax.dot_general(c16, b16, NT, preferred_element_type=F32)
        cbt = lax.dot_general(b16, c16, NT, preferred_element_type=F32)
        sp, ds = sp_ref[...], dstate_ref[...]
        sp16, ds16 = sp.astype(BF16), ds.astype(BF16)
        cs = jnp.dot(c16, sp16, preferred_element_type=F32)
        bds = jnp.dot(b16, ds16, preferred_element_type=F32)
        spds = jnp.sum(sp * ds, axis=0, keepdims=True)
        xd = xs * t["dt_x"]
        xw = xd * t["w_x"]
        edy = dy * t["e_x"]
        xd16, dy16, xw16, edy16 = xd.astype(BF16), dy.astype(BF16), xw.astype(BF16), edy.astype(BF16)
        low = lax.broadcasted_iota(jnp.int32, (CHUNK, LANE), 1) < SSM_HEAD_DIM
        zero = jnp.zeros((CHUNK, CHUNK), F32)
        dsum_g, dsum_gt, dacs = zero, zero, zero
        dxd_parts = []
        for r in range(HPG):
            pair = slice((r // 2) * LANE, (r // 2 + 1) * LANE)
            diff = t["acs"][:, r:r + 1] - t["acs_t"][r:r + 1, :]
            lam = jnp.exp(jnp.where(tril, diff, NEG))
            lam_t = jnp.exp(jnp.where(triu, -diff, NEG))
            m, m_t = cb * lam, cbt * lam_t
            dyh16 = jnp.where(low if r % 2 == 0 else ~low, dy16[:, pair], jnp.zeros_like(dy16[:, pair]))
            dxd_parts.append(jnp.dot(m_t.astype(BF16), dy16[:, pair], preferred_element_type=F32))
            dm = lax.dot_general(dyh16, xd16[:, pair], NT, preferred_element_type=F32)
            dm_t = lax.dot_general(xd16[:, pair], dyh16, NT, preferred_element_type=F32)
            dsum_g = dsum_g + dm * lam
            dsum_gt = dsum_gt + dm_t * lam_t
            dacs = dacs + jnp.sum(dm * m - dm_t * m_t, axis=1, keepdims=True) * (si[:1, :] == r).astype(F32)
        xwb = xw * bds
        dxd = _ssd_pair_halves(dxd_parts) + t["w_x"] * bds
        dxs_ref[...] = dxd * t["dt_x"] + dy * dexp_ref[...]
        gr = lax.broadcasted_iota(jnp.int32, (GW, LANE), 0)
        gc = lax.broadcasted_iota(jnp.int32, (GW, LANE), 1)
        gather = (gr // SSM_HEAD_DIM == gc).astype(BF16)
        tail = jnp.concatenate([jnp.sum(xwb, axis=0, keepdims=True), spds, jnp.zeros((SUBLANE - 2, GW), F32)], axis=0)
        stack = jnp.concatenate([edy * cs - xwb, dxd * xs, tail], axis=0)
        hs = sum(jnp.dot(part, gather, preferred_element_type=F32) for part in _split3(stack))
        last = hs[2 * CHUNK:2 * CHUNK + 1] + t["e_last"][0:1, :] * hs[2 * CHUNK + 1:2 * CHUNK + 2]
        dacs = dacs + hs[:CHUNK] + jnp.where(row_is_last, last, 0.0)
        ddt_dir = hs[CHUNK:2 * CHUNK]
        dc_ref[...] = (jnp.dot(dsum_g.astype(BF16), b16, preferred_element_type=F32)
                       + lax.dot_general(edy16, sp16, NT, preferred_element_type=F32))
        db_ref[...] = (jnp.dot(dsum_gt.astype(BF16), c16, preferred_element_type=F32)
                       + lax.dot_general(xw16, ds16, NT, preferred_element_type=F32))
        dstate_ref[...] = ds * t["el_x"] + jnp.dot(ct16, edy16, preferred_element_type=F32)
        dadt = lax.dot_general(triu.astype(F32), dacs, NN, precision=lax.Precision.HIGHEST, preferred_element_type=F32)
        head_lane = si < HPG
        ddtr = jnp.where(head_lane, (dadt * t["a"] + ddt_dir) * jax.nn.sigmoid(t["x0"]), 0.0)
        ddtr_ref[...] = ddtr
        da_ref[...] += jnp.sum(jnp.where(head_lane, dadt * t["dt"] * t["a"], 0.0), axis=0, keepdims=True)
        dbias_ref[...] += jnp.sum(ddtr, axis=0, keepdims=True)
        ddexp_ref[...] += jnp.sum(dy * xs, axis=0, keepdims=True)

    return pl.pallas_call(
        body, name=name, grid=(SSM_GROUPS, bsz, nc),
        in_specs=[xs_s, bm_s, cm_s, dtr_s, vec_s, vec_s, dexp_s, xs_s, st_s],
        out_specs=[xs_s, grp, grp, grp, acc, acc, dexp_s],
        out_shape=[jax.ShapeDtypeStruct((bsz, seq, SSM_D_INNER), F32),
                   jax.ShapeDtypeStruct((bsz, seq, SSM_GROUPS * SSM_STATE), F32),
                   jax.ShapeDtypeStruct((bsz, seq, SSM_GROUPS * SSM_STATE), F32),
                   jax.ShapeDtypeStruct((bsz, seq, SSM_GROUPS * LANE), F32),
                   jax.ShapeDtypeStruct((SSM_GROUPS, 1, LANE), F32),
                   jax.ShapeDtypeStruct((SSM_GROUPS, 1, LANE), F32),
                   jax.ShapeDtypeStruct((1, SSM_D_INNER), F32)],
        scratch_shapes=[pltpu.VMEM((SSM_STATE, GW), F32)],
        compiler_params=_params(("arbitrary", "arbitrary", "arbitrary")),
    )(xc, xc, xc, dtr, dt_bias, a_log, d_exp, dy, sprev)


SB_TQ = 512
SB_TK = 128
SB_SCALE = SB_HEAD_DIM ** -0.5


def _split_dot(x, u16):
    hi = x.astype(BF16)
    lo = (x - hi.astype(F32)).astype(BF16)
    return jnp.dot(hi, u16, preferred_element_type=F32) + jnp.dot(lo, u16, preferred_element_type=F32)


def _sb_scores(qh, kj, mask):
    z = lax.dot_general(qh, kj, NT, preferred_element_type=F32)
    l1p = jnp.log(1.0 + jnp.exp(-jnp.abs(z)))
    lb = jnp.minimum(z, 0.0) - l1p
    lk = -jnp.maximum(z, 0.0) - l1p
    if mask is not None:
        lk = jnp.where(mask, lk, 0.0)
    return lb, lk


def _sb_diag_masks(tq):
    r = lax.broadcasted_iota(jnp.int32, (tq, SB_TK), 0)
    c = lax.broadcasted_iota(jnp.int32, (tq, SB_TK), 1)
    return [r > c + d * SB_TK for d in range(tq // SB_TK)]


def _sb_heads(x, lane):
    return [jnp.where((lane // SB_HEAD_DIM) == hh, x, jnp.zeros_like(x)) for hh in range(2)]


def sb_attention_fwd(q, kv, name):
    bsz, seq, width = q.shape
    tq = min(SB_TQ, seq)
    npair = width // LANE
    qspec = pl.BlockSpec((None, tq, LANE), lambda b, p, i: (b, i, p))
    kspec = pl.BlockSpec((None, seq, LANE), lambda b, p, i: (b, 0, p))
    vspec = pl.BlockSpec((None, seq, LANE), lambda b, p, i: (b, 0, npair + p))

    def body(q_ref, k_ref, v_ref, o_ref, l_ref):
        i = pl.program_id(2)
        ndiag = tq // SB_TK
        nfull = i * ndiag
        lane = lax.broadcasted_iota(jnp.int32, (tq, LANE), 1)
        ui = lax.broadcasted_iota(jnp.int32, (SB_TK, SB_TK), 0)
        uj = lax.broadcasted_iota(jnp.int32, (SB_TK, SB_TK), 1)
        u_excl = (ui > uj).astype(BF16)
        qs = _sb_heads(q_ref[...] * SB_SCALE, lane)

        def tile(j, carry, mask):
            rows = pl.ds(pl.multiple_of(j * SB_TK, SB_TK), SB_TK)
            kj, vj = k_ref[rows, :], v_ref[rows, :]
            new = []
            for hh in range(2):
                acc, run = carry[hh]
                lb, lk = _sb_scores(qs[hh], kj, mask)
                w = jnp.exp(lb + _split_dot(lk, u_excl) + run)
                if mask is not None:
                    w = jnp.where(mask, w, 0.0)
                acc = acc + jnp.dot(w.astype(BF16), vj, preferred_element_type=F32)
                new.append((acc, run + jnp.sum(lk, axis=1, keepdims=True)))
            return tuple(new)

        carry = tuple((jnp.zeros((tq, LANE), F32), jnp.zeros((tq, 1), F32)) for _ in range(2))
        masks = _sb_diag_masks(tq)
        for dd in reversed(range(ndiag)):
            carry = tile(nfull + dd, carry, masks[dd])
        def trip(jj, cr):
            for u in range(ndiag):
                cr = tile(nfull - 1 - u - ndiag * jj, cr, None)
            return cr

        carry = lax.fori_loop(0, i, trip, carry)
        in0 = (lane // SB_HEAD_DIM) == 0
        o_ref[...] = jnp.where(in0, carry[0][0], carry[1][0]).astype(o_ref.dtype)
        l_ref[...] = jnp.where(in0, carry[0][1], carry[1][1])

    return pl.pallas_call(
        body, name=name, grid=(bsz, npair, seq // tq),
        in_specs=[qspec, kspec, vspec], out_specs=[qspec, qspec],
        out_shape=[jax.ShapeDtypeStruct(q.shape, BF16), jax.ShapeDtypeStruct(q.shape, F32)],
        compiler_params=_params(("parallel", "parallel", "arbitrary")),
    )(q, kv, kv)


def sb_attention_bwd(q, kv, ltot, do, dk_in, dv_in, name):
    bsz, seq, width = q.shape
    tq = min(SB_TQ, seq)
    npair = width // LANE
    qspec = pl.BlockSpec((None, tq, LANE), lambda b, p, i: (b, i, p))
    kspec = pl.BlockSpec((None, seq, LANE), lambda b, p, i: (b, 0, p))
    vspec = pl.BlockSpec((None, seq, LANE), lambda b, p, i: (b, 0, npair + p))

    def body(q_ref, k_ref, v_ref, l_ref, do_ref, dki_ref, dvi_ref, dq_ref, dk_ref, dv_ref):
        i = pl.program_id(2)

        @pl.when(i == 0)
        def _():
            dk_ref[...] = dki_ref[...]
            dv_ref[...] = dvi_ref[...]

        ndiag = tq // SB_TK
        nfull = i * ndiag
        lane = lax.broadcasted_iota(jnp.int32, (tq, LANE), 1)
        ui = lax.broadcasted_iota(jnp.int32, (SB_TK, SB_TK), 0)
        uj = lax.broadcasted_iota(jnp.int32, (SB_TK, SB_TK), 1)
        u_le = (ui <= uj).astype(BF16)
        u_lt = (ui < uj).astype(BF16)
        qs = _sb_heads(q_ref[...] * SB_SCALE, lane)
        dos = _sb_heads(do_ref[...], lane)
        ltots = [l_ref[:, hh * SB_HEAD_DIM:hh * SB_HEAD_DIM + 1] for hh in range(2)]

        def tile(j, carry, mask, r0=0):
            rows = pl.ds(pl.multiple_of(j * SB_TK, SB_TK), SB_TK)
            kj, vj = k_ref[rows, :], v_ref[rows, :]
            new, dk_t, dv_t = [], None, None
            for hh in range(2):
                acc, run, run_a = carry[hh]
                qh, doh = qs[hh][r0:], dos[hh][r0:]
                lb, lk = _sb_scores(qh, kj, mask)
                w = jnp.exp(lb + (ltots[hh][r0:] - run[r0:] - _split_dot(lk, u_le)))
                if mask is not None:
                    w = jnp.where(mask, w, 0.0)
                a = lax.dot_general(doh, vj, NT, preferred_element_type=F32) * w
                ca = run_a[r0:] + _split_dot(a, u_lt)
                sg = jnp.exp(lb)
                dz = a * (1.0 - sg) - ca * sg
                if mask is not None:
                    dz = jnp.where(mask, dz, 0.0)
                dz16 = dz.astype(BF16)
                part = [acc[r0:] + jnp.dot(dz16, kj, preferred_element_type=F32),
                        run[r0:] + jnp.sum(lk, axis=1, keepdims=True), run_a[r0:] + jnp.sum(a, axis=1, keepdims=True)]
                if r0:
                    part = [jnp.concatenate([old[:r0], p], axis=0) for old, p in zip((acc, run, run_a), part)]
                dkh = lax.dot_general(dz16, qh, TN, preferred_element_type=F32)
                dvh = lax.dot_general(w.astype(BF16), doh, TN, preferred_element_type=F32)
                dk_t = dkh if dk_t is None else dk_t + dkh
                dv_t = dvh if dv_t is None else dv_t + dvh
                new.append(tuple(part))
            dk_ref[rows, :] += dk_t
            dv_ref[rows, :] += dv_t
            return tuple(new)

        zcol = jnp.zeros((tq, 1), F32)
        carry = tuple((jnp.zeros((tq, LANE), F32), zcol, zcol) for _ in range(2))
        def trip(j, cr):
            for u in range(ndiag):
                cr = tile(ndiag * j + u, cr, None)
            return cr

        carry = lax.fori_loop(0, i, trip, carry)
        masks = _sb_diag_masks(tq)
        for dd in range(ndiag):
            carry = tile(nfull + dd, carry, masks[dd][dd * SB_TK:], dd * SB_TK)
        in0 = (lane // SB_HEAD_DIM) == 0
        dq_ref[...] = (jnp.where(in0, carry[0][0], carry[1][0]) * SB_SCALE).astype(dq_ref.dtype)

    return pl.pallas_call(
        body, name=name, grid=(bsz, npair, seq // tq),
        in_specs=[qspec, kspec, vspec, qspec, qspec, kspec, kspec], out_specs=[qspec, kspec, kspec],
        out_shape=[jax.ShapeDtypeStruct(q.shape, BF16), jax.ShapeDtypeStruct(q.shape, F32), jax.ShapeDtypeStruct(q.shape, F32)],
        compiler_params=_params(("parallel", "parallel", "arbitrary")),
    )(q, kv, kv, ltot, do, dk_in, dv_in)


def adamw(w, parts, m, v, name):
    r, c = w.shape
    tr = r
    for cand in (256, 176, 128, 64, 32, 16, 8):
        if r % cand == 0:
            tr = cand
            break
    bc1 = 1.0 - ADAM_B1 ** ADAM_STEP
    bc2 = 1.0 - ADAM_B2 ** ADAM_STEP

    def body(w_ref, p_ref, m_ref, v_ref, g_ref, d_ref, mo_ref, vo_ref):
        g = p_ref[0].astype(F32)
        for j in range(1, N_DEV):
            g = g + p_ref[j].astype(F32)
        mn = ADAM_B1 * m_ref[...] + (1.0 - ADAM_B1) * g
        vn = ADAM_B2 * v_ref[...] + (1.0 - ADAM_B2) * (g * g)
        g_ref[...] = g
        mo_ref[...] = mn
        vo_ref[...] = vn
        d_ref[...] = -ADAM_LR * ((mn / bc1) / (jnp.sqrt(vn / bc2) + ADAM_EPS) + ADAM_WD * w_ref[...])

    blk = pl.BlockSpec((tr, c), lambda i: (i, 0))
    pblk = pl.BlockSpec((N_DEV, tr, c), lambda i: (0, i, 0))
    shp = jax.ShapeDtypeStruct((r, c), F32)
    return pl.pallas_call(
        body, name=name, grid=(r // tr,), in_specs=[blk, pblk, blk, blk], out_specs=[blk, blk, blk, blk],
        out_shape=[shp, shp, shp, shp], compiler_params=_params(("parallel",)),
    )(w, parts, m, v)


MESH_ID = pl.DeviceIdType.MESH
ANY = pl.BlockSpec(memory_space=pl.ANY)


def all_gather(xs, name):
    n = len(xs)

    def body(*refs):
        x_refs, out_refs = refs[:n], refs[n:2 * n]
        send_sems, recv_sems, local_sems = refs[2 * n:]
        mx, my, mc = lax.axis_index("x"), lax.axis_index("y"), lax.axis_index("c")
        me, sibling = (mx, my, mc), (mx, my, 1 - mc)
        chips = [(1 - mx, my), (mx, 1 - my), (1 - mx, 1 - my)]

        def slot(a, px, py, pc):
            return out_refs[a].at[4 * px + 2 * py + pc]

        def copy(a, k, block, to, src=None):
            return pltpu.make_async_remote_copy(
                src_ref=slot(a, *block) if src is None else src, dst_ref=slot(a, *block),
                send_sem=send_sems.at[7 * a + k], recv_sem=recv_sems.at[7 * a + k], device_id=to, device_id_type=MESH_ID)

        mine = [pltpu.make_async_copy(x_refs[a], slot(a, *me), local_sems.at[a]) for a in range(n)]
        for cp in mine:
            cp.start()
        first = []
        for a in range(n):
            first.append(copy(a, 0, me, sibling, src=x_refs[a]))
            first += [copy(a, 1 + j, me, (*chip, mc), src=x_refs[a]) for j, chip in enumerate(chips)]
        for cp in first:
            cp.start()
        passed = []
        for j, chip in enumerate(chips):
            for a in range(n):
                copy(a, 1 + j, (*chip, mc), me).wait_recv()
                cp = copy(a, 4 + j, (*chip, mc), sibling)
                cp.start()
                passed.append(cp)
        for a in range(n):
            copy(a, 0, sibling, me).wait_recv()
            for j, chip in enumerate(chips):
                copy(a, 4 + j, (*chip, 1 - mc), me).wait_recv()
        for cp in first + passed:
            cp.wait_send()
        for cp in mine:
            cp.wait()

    return pl.pallas_call(
        body, name=name, out_shape=[jax.ShapeDtypeStruct((N_DEV,) + x.shape, x.dtype) for x in xs],
        in_specs=[ANY] * n, out_specs=[ANY] * n,
        scratch_shapes=[pltpu.SemaphoreType.DMA((7 * n,)), pltpu.SemaphoreType.DMA((7 * n,)), pltpu.SemaphoreType.DMA((n,))],
    )(*xs)


HBM_SPEC = pl.BlockSpec(memory_space=pltpu.HBM)
SEM_SPEC = pl.BlockSpec(memory_space=pltpu.SEMAPHORE)
DATAFLOW = pltpu.SideEffectType.DATAFLOW_SIDE_EFFECTING


def _peer(j, mx, my, mc):
    px = 1 - mx if j & 4 else mx
    py = 1 - my if j & 2 else my
    pc = 1 - mc if j & 1 else mc
    return (px, py, pc), 4 * px + 2 * py + pc


def _split_copy(kind, src_ref, land_ref, send_sems, recv_sems, a, j, mx, my, mc):
    dev, peer = _peer(j, mx, my, mc)
    me = 4 * mx + 2 * my + mc
    src = src_ref if kind == "gather" else src_ref.at[peer]
    k = 7 * a + j - 1
    return pltpu.make_async_remote_copy(src_ref=src, dst_ref=land_ref.at[me], send_sem=send_sems.at[k],
                                        recv_sem=recv_sems.at[k], device_id=dev, device_id_type=MESH_ID), peer


def exchange_start(kind, srcs, name):
    n = len(srcs)
    blocks = [s.shape if kind == "gather" else s.shape[1:] for s in srcs]
    lands = [lax.empty((N_DEV,) + tuple(b), s.dtype) for b, s in zip(blocks, srcs)]

    def body(*refs):
        src_refs, land_refs = refs[:n], refs[n:2 * n]
        send_sems, recv_sems = refs[2 * n], refs[2 * n + 1]
        token = refs[-1]
        mx, my, mc = lax.axis_index("x"), lax.axis_index("y"), lax.axis_index("c")
        for j in range(1, N_DEV):
            for a in range(n):
                _split_copy(kind, src_refs[a], land_refs[a], send_sems, recv_sems, a, j, mx, my, mc)[0].start()
        token[...] = jnp.zeros_like(token)

    hbm = lambda v: pltpu.HBM(v.shape, v.dtype)
    outs = pl.pallas_call(
        body, name=name,
        out_shape=(pltpu.SemaphoreType.DMA((7 * n,)), pltpu.SemaphoreType.DMA((7 * n,)), *[hbm(s) for s in srcs],
                   *[hbm(l) for l in lands], jax.ShapeDtypeStruct((SUBLANE, LANE), F32)),
        in_specs=[HBM_SPEC] * (2 * n),
        out_specs=(SEM_SPEC, SEM_SPEC, *[HBM_SPEC] * (2 * n), pl.BlockSpec(memory_space=pltpu.VMEM)),
        input_output_aliases={a: 2 + a for a in range(2 * n)},
        compiler_params=pltpu.CompilerParams(has_side_effects=DATAFLOW),
    )(*[pltpu.with_memory_space_constraint(v, pltpu.HBM) for v in list(srcs) + lands])
    return dict(kind=kind, n=n, send=outs[0], recv=outs[1], srcs=outs[2:2 + n], lands=outs[2 + n:2 + 2 * n], token=outs[-1])


def exchange_wait(h, after, name):
    n, kind = h["n"], h["kind"]

    def body(*refs):
        src_refs, land_refs = refs[:n], refs[n:2 * n]
        send_sems, recv_sems = refs[2 * n], refs[2 * n + 1]
        mx, my, mc = lax.axis_index("x"), lax.axis_index("y"), lax.axis_index("c")
        for j in range(1, N_DEV):
            for a in range(n):
                cp, peer = _split_copy(kind, src_refs[a], land_refs[a], send_sems, recv_sems, a, j, mx, my, mc)
                cp.wait_send()
                pltpu.make_async_remote_copy(
                    src_ref=land_refs[a].at[peer], dst_ref=land_refs[a].at[peer], send_sem=send_sems.at[7 * a + j - 1],
                    recv_sem=recv_sems.at[7 * a + j - 1], device_id=_peer(j, mx, my, mc)[0], device_id_type=MESH_ID).wait_recv()

    hbm = lambda v: pltpu.HBM(v.shape, v.dtype)
    outs = pl.pallas_call(
        body, name=name,
        out_shape=tuple(hbm(v) for v in list(h["srcs"]) + list(h["lands"])),
        in_specs=[HBM_SPEC] * (2 * n) + [SEM_SPEC, SEM_SPEC, ANY],
        out_specs=tuple([HBM_SPEC] * (2 * n)),
        input_output_aliases={a: a for a in range(2 * n)},
        compiler_params=pltpu.CompilerParams(has_side_effects=DATAFLOW),
    )(*h["srcs"], *h["lands"], h["send"], h["recv"], after)
    return list(outs[n:])


PACK_COLS = 1024
PACK_SEG = 16 * PACK_COLS

BIG = {"ssm_in_proj": 2, "ssm_out_proj": 1, "w_kv": 1, "w_q": 1, "w_o": 1, "ffn_up": 2, "ffn_down": 1,
       "ple_gate": 1, "ple_proj": 2}
SMALL = {"ssm_conv_w": 2, "ssm_conv_b": 1, "ssm_norm": 1, "ffn_conv_w": 2}
REPL = ["attn_norm", "ffn_norm", "ple_norm", "ssm_dt_bias", "ssm_a_log", "ssm_d", "ffn_conv_b", "kv_norm", "final_norm"]


def _seg(n):
    return -(-n // PACK_SEG) * PACK_SEG


def _pack_rows(flats, lead=()):
    padded = [jnp.pad(f, [(0, 0)] * len(lead) + [(0, _seg(f.shape[-1]) - f.shape[-1])]) for f in flats]
    return jnp.concatenate(padded, axis=-1).reshape(*lead, -1, PACK_COLS)


def _unpack_rows(buf, sizes, lead):
    flat = buf.reshape(*lead, -1)
    out, off = [], 0
    for n in sizes:
        out.append(lax.slice_in_dim(flat, off, off + n, axis=len(lead)))
        off += _seg(n)
    return out


def _to_slots(full, axis):
    shp = full.shape
    blk = shp[axis] // N_DEV
    t = full.reshape(shp[:axis] + (N_DEV, blk) + shp[axis + 1:])
    return jnp.moveaxis(t, axis, 0).reshape(N_DEV, -1)


def _from_slots(slots, local_shape, axis):
    t = jnp.moveaxis(slots.reshape((N_DEV,) + tuple(local_shape)), 0, axis)
    shp = list(local_shape)
    shp[axis] *= N_DEV
    return t.reshape(shp)


def _as2d(a):
    return a.reshape(1, -1) if a.ndim == 1 else a.reshape(-1, a.shape[-1])


def _pad_rows(w, rows=SUBLANE):
    return jnp.pad(w, ((0, rows - w.shape[0]), (0, 0)))


def kernel(x, p, attn_norm, ffn_norm, ple_norm, ssm_in_proj, ssm_conv_w, ssm_conv_b, ssm_dt_bias, ssm_a_log, ssm_d, ssm_norm, ssm_out_proj, kv_norm, w_kv, w_q, w_o, ffn_up, ffn_conv_w, ffn_conv_b, ffn_down, ple_gate, ple_proj, final_norm, loss_target, m_attn_norm, m_ffn_norm, m_ple_norm, m_ssm_in_proj, m_ssm_conv_w, m_ssm_conv_b, m_ssm_dt_bias, m_ssm_a_log, m_ssm_d, m_ssm_norm, m_ssm_out_proj, m_kv_norm, m_w_kv, m_w_q, m_w_o, m_ffn_up, m_ffn_conv_w, m_ffn_conv_b, m_ffn_down, m_ple_gate, m_ple_proj, m_final_norm, v_attn_norm, v_ffn_norm, v_ple_norm, v_ssm_in_proj, v_ssm_conv_w, v_ssm_conv_b, v_ssm_dt_bias, v_ssm_a_log, v_ssm_d, v_ssm_norm, v_ssm_out_proj, v_kv_norm, v_w_kv, v_w_q, v_w_o, v_ffn_up, v_ffn_conv_w, v_ffn_conv_b, v_ffn_down, v_ple_gate, v_ple_proj, v_final_norm):
    given = dict(locals())
    wnames = list(BIG) + list(SMALL) + REPL
    bsz, seq, d = x.shape
    t = bsz * seq
    bs = lambda a: a.reshape(bsz, seq, a.shape[-1])
    fl = lambda a: a.reshape(t, a.shape[-1])

    big_local = [given[n] for n in BIG]
    small_local = [given[n] for n in SMALL]
    me = 4 * lax.axis_index("x") + 2 * lax.axis_index("y") + lax.axis_index("c")

    N_GROUPS = 2 * DEPTH

    def layer_items(k):
        i, j = k // 2, k // 2 - N_A
        if k % 2 == 0:
            return [("ssm_in_proj", i), ("ssm_out_proj", i)] if i < N_A else [("w_q", j), ("w_o", j)]
        shared = [("w_kv", None)] if i == N_A - 1 else []
        return [("ffn_up", i), ("ffn_down", i), ("ple_gate", i), ("ple_proj", i)] + shared

    def local_block(n, idx):
        return given[n] if idx is None else given[n][idx]

    def put_own(land, own):
        return lax.dynamic_update_slice(land, own[None], (me,) + (0,) * own.ndim)

    small_pack = _pack_rows([w.reshape(-1) for w in small_local])
    gather_src, gather_h = [], []
    for i in range(N_GROUPS):
        src = [local_block(n, idx).astype(BF16) for n, idx in layer_items(i)] + ([small_pack] if i == 0 else [])
        gather_src.append(src)
        gather_h.append(exchange_start("gather", src, f"ag_start{i}"))
    full = {n: {} for n in BIG}
    rows_of = lambda g: g.reshape(-1, g.shape[-1])
    cols_of = lambda g: jnp.transpose(g, (1, 0, 2)).reshape(g.shape[1], -1)

    def fetch_layer(i, after):
        lands = exchange_wait(gather_h[i], after, f"ag_wait{i}")
        got = [put_own(l, s) for l, s in zip(lands, gather_src[i])]
        for (n, idx), g in zip(layer_items(i), got):
            if n == "ffn_up":
                full[n][idx] = (cols_of(g[:N_DEV // 2]), cols_of(g[N_DEV // 2:]))
            else:
                full[n][idx] = cols_of(g) if BIG[n] == given[n].ndim - 1 else rows_of(g)
        if i == 0:
            for (n, ax), w, s in zip(SMALL.items(), small_local, _unpack_rows(got[-1], [w.size for w in small_local], (N_DEV,))):
                full[n] = _from_slots(s, w.shape, ax)

    row = lambda v: v.reshape(1, -1)
    pad_lane = lambda v: jnp.pad(v.reshape(1, -1), ((0, 0), (0, LANE - v.size)))
    h = x.reshape(t, d)
    saved = []
    kv = hnkv = h_kv = None
    for i in range(DEPTH):
        fetch_layer(2 * i, h)
        sv = {"h0": h}
        gain = row(attn_norm[i])
        if i == 0:
            gain = gain + sum(hd["token"][0, 0] for hd in gather_h)
        hn = rmsnorm_fwd(h, gain, f"attn_norm_f{i}")
        sv["hn"] = hn
        if i < N_A:
            w_in = full["ssm_in_proj"][i]
            wz, wxbc = w_in[:, :SSM_D_INNER], w_in[:, SSM_D_INNER:SSM_ZX]
            wdt = jnp.pad(w_in[:, SSM_ZX:], ((0, 0), (0, LANE - SSM_HEADS)))
            z = matmul(hn, wz, "nn", F32, f"ssm_z_f{i}")
            xbc = bs(matmul(hn, wxbc, "nn", F32, f"ssm_xbc_f{i}"))
            dtr = bs(matmul(hn, wdt, "nn", F32, f"ssm_dt_f{i}"))
            cw = _pad_rows(full["ssm_conv_w"][i])
            cb = row(full["ssm_conv_b"][i])
            xc = ssm_conv_fwd(xbc, cw, cb, f"ssm_conv_f{i}")
            dtb, alog = pad_lane(ssm_dt_bias[i]), pad_lane(ssm_a_log[i])
            dexp = jnp.repeat(ssm_d[i], SSM_HEAD_DIM).reshape(1, -1)
            y, sprev = ssd_fwd(xc, dtr, dtb, alog, dexp, f"ssd_f{i}")
            gn = gated_norm_fwd(fl(y), z, row(full["ssm_norm"][i]), f"ssm_gnorm_f{i}")
            h1 = matmul(gn, full["ssm_out_proj"][i], "nn", F32, f"ssm_out_f{i}", add=h)
            sv.update(wz=wz, wxbc=wxbc, wdt=wdt, z=z, xbc=xbc, dtr=dtr, cw=cw, cb=cb, xc=xc, dtb=dtb, alog=alog,
                      dexp=dexp, y=y, sprev=sprev, gn=gn)
        else:
            j = i - N_A
            q = bs(matmul(hn, full["w_q"][j], "nn", BF16, f"sb_q_f{j}"))
            o, ltot = sb_attention_fwd(q, kv, f"sb_attn_f{j}")
            h1 = matmul(fl(o), full["w_o"][j], "nn", F32, f"sb_o_f{j}", add=h)
            sv.update(q=q, o=o, ltot=ltot)
        sv["h1"] = h1
        fetch_layer(2 * i + 1, h1)
        hn2 = rmsnorm_fwd(h1, row(ffn_norm[i]), f"ffn_norm_f{i}")
        wug, wuv = full["ffn_up"][i]
        ug = bs(matmul(hn2, wug, "nn", F32, f"ffn_upg_f{i}"))
        uv = bs(matmul(hn2, wuv, "nn", F32, f"ffn_upv_f{i}"))
        fcw = full["ffn_conv_w"][i]
        fwg, fwv = _pad_rows(fcw[:, :D_FF]), _pad_rows(fcw[:, D_FF:])
        fbg, fbv = row(ffn_conv_b[i, :D_FF]), row(ffn_conv_b[i, D_FF:])
        act = ffn_conv_fwd(ug, uv, fwg, fwv, fbg, fbv, f"ffn_conv_f{i}")
        h2 = matmul(fl(act), full["ffn_down"][i], "nn", F32, f"ffn_down_f{i}", add=h1)
        hn3 = rmsnorm_fwd(h2, row(ple_norm[i]), f"ple_norm_f{i}")
        gpre = matmul(hn3, full["ple_gate"][i], "nn", F32, f"ple_gate_f{i}")
        p_i = p[i].reshape(t, PLE_DIM)
        pp = matmul(p_i, full["ple_proj"][i], "nn", F32, f"ple_proj_f{i}")
        h3 = ple_fwd(h2, gpre, pp, f"ple_f{i}")
        sv.update(hn2=hn2, wug=wug, wuv=wuv, ug=ug, uv=uv, fwg=fwg, fwv=fwv, fbg=fbg, fbv=fbv, act=act, h2=h2,
                  hn3=hn3, gpre=gpre, pp=pp, p_i=p_i)
        if i == N_A - 1:
            h_kv = h3
            hnkv = rmsnorm_fwd(h3, row(kv_norm), "kv_norm_f")
            kv = bs(matmul(hnkv, full["w_kv"][None], "nn", BF16, "kv_f"))
        h = h3
        saved.append(sv)

    loss_row, dh, g_final = loss_head(h, row(final_norm), loss_target.reshape(t, d), "loss_head")
    loss = lax.psum(loss_row[0, 0], ("x", "y", "c"))

    G = {n: [None] * given[n].shape[0] for n in wnames if given[n].ndim > 1 and n not in ("w_kv",)}
    G["final_norm"] = g_final[0]

    def col_slots(g, ndev=N_DEV):
        return jnp.transpose(g.reshape(g.shape[0], ndev, -1), (1, 0, 2))

    def grad_slots(n, idx):
        g = G[n] if idx is None else G[n][idx]
        if n == "ffn_up":
            s = jnp.concatenate([col_slots(g[0], N_DEV // 2), col_slots(g[1], N_DEV // 2)], axis=0)
        elif BIG[n] == given[n].ndim - 1:
            s = col_slots(g)
        else:
            s = g.reshape(N_DEV, -1, g.shape[-1])
        return s.astype(BF16)

    scatter_src, scatter_h = {}, {}
    dk = jnp.zeros((bsz, seq, SB_WIDTH), F32)
    dv = jnp.zeros((bsz, seq, SB_WIDTH), F32)
    for i in reversed(range(DEPTH)):
        sv = saved[i]
        if i == N_A - 1:
            wk, wv = full["w_kv"][None][:, :SB_WIDTH], full["w_kv"][None][:, SB_WIDTH:]
            dkf, dvf = fl(dk), fl(dv)
            dhn = matmul(dkf, wk, "nt", F32, "kv_dx_k")
            dhn = matmul(dvf, wv, "nt", F32, "kv_dx_v", add=dhn)
            G["w_kv"] = jnp.concatenate([matmul(hnkv, dkf, "tn", F32, "kv_dw_k"), matmul(hnkv, dvf, "tn", F32, "kv_dw_v")], axis=1)
            dh, gk, _ = rmsnorm_bwd(dhn, h_kv, row(kv_norm), dh, "kv_norm_b")
            G["kv_norm"] = gk[0]
        dgpre, dpp = ple_bwd(dh, sv["gpre"], sv["pp"], f"ple_b{i}")
        G["ple_proj"][i] = matmul(sv["p_i"], dpp, "tn", F32, f"ple_proj_dw{i}")
        G["ple_gate"][i] = matmul(sv["hn3"], dgpre, "tn", F32, f"ple_gate_dw{i}")
        dhn = matmul(dgpre, full["ple_gate"][i], "nt", F32, f"ple_gate_dx{i}")
        gain = row(ple_norm[i])
        if 2 * i + 2 in scatter_h:
            gain = gain + scatter_h[2 * i + 2]["token"][0, 0]
        dh, gn_, dh16 = rmsnorm_bwd(dhn, sv["h2"], gain, dh, f"ple_norm_b{i}")
        G["ple_norm"][i] = gn_[0]
        dact = bs(matmul(dh16,full["ffn_down"][i], "nt", F32, f"ffn_down_dx{i}"))
        G["ffn_down"][i] = matmul(fl(sv["act"]), dh16, "tn", F32, f"ffn_down_dw{i}")
        dug, duv, dwg, dwv, dbg, dbv = ffn_conv_bwd(sv["ug"], sv["uv"], dact, sv["fwg"], sv["fwv"], sv["fbg"], sv["fbv"], f"ffn_conv_b{i}")
        dug, duv = fl(dug), fl(duv)
        G["ffn_conv_w"][i] = jnp.concatenate([dwg[:FFN_CONV], dwv[:FFN_CONV]], axis=1)
        G["ffn_conv_b"][i] = jnp.concatenate([dbg[0], dbv[0]])
        G["ffn_up"][i] = (matmul(sv["hn2"], dug, "tn", F32, f"ffn_upg_dw{i}"), matmul(sv["hn2"], duv, "tn", F32, f"ffn_upv_dw{i}"))
        dhn = matmul(dug, sv["wug"], "nt", F32, f"ffn_upg_dx{i}")
        dhn = matmul(duv, sv["wuv"], "nt", F32, f"ffn_upv_dx{i}", add=dhn)
        dh, gn_, dh16 = rmsnorm_bwd(dhn, sv["h1"], row(ffn_norm[i]), dh, f"ffn_norm_b{i}")
        G["ffn_norm"][i] = gn_[0]
        scatter_src[2 * i + 1] = [grad_slots(n, idx) for n, idx in layer_items(2 * i + 1)]
        scatter_h[2 * i + 1] = exchange_start("scatter", scatter_src[2 * i + 1], f"a2a_start{2 * i + 1}")
        tok = scatter_h[2 * i + 1]["token"][0, 0]
        if i < N_A:
            dgn = matmul(dh16,full["ssm_out_proj"][i], "nt", F32, f"ssm_out_dx{i}")
            G["ssm_out_proj"][i] = matmul(sv["gn"], dh16, "tn", F32, f"ssm_out_dw{i}")
            dy, dz, dnw = gated_norm_bwd(dgn, fl(sv["y"]), sv["z"], row(full["ssm_norm"][i]) + tok, f"ssm_gnorm_b{i}")
            G["ssm_norm"][i] = dnw[0]
            dxs, dbm, dcm, ddtr4, da4, dbias4, ddexp = ssd_bwd(sv["xc"], sv["dtr"], sv["dtb"], sv["alog"], sv["dexp"], bs(dy), sv["sprev"], f"ssd_b{i}")
            G["ssm_a_log"][i] = da4[:, 0, :HPG].reshape(-1)
            G["ssm_dt_bias"][i] = dbias4[:, 0, :HPG].reshape(-1)
            G["ssm_d"][i] = ddexp.reshape(SSM_HEADS, SSM_HEAD_DIM).sum(axis=-1)
            dxc = jnp.concatenate([dxs, dbm, dcm], axis=-1)
            dxbc, dcw, dcb = ssm_conv_bwd(sv["xbc"], dxc, sv["cw"], sv["cb"], f"ssm_conv_b{i}")
            dxbc = fl(dxbc)
            G["ssm_conv_w"][i] = dcw[:SSM_CONV]
            G["ssm_conv_b"][i] = dcb[0]
            ddtr = ddtr4.reshape(t, SSM_GROUPS, LANE)[:, :, :HPG].reshape(t, SSM_HEADS)
            ddtr = jnp.pad(ddtr, ((0, 0), (0, LANE - SSM_HEADS))).astype(BF16)
            hn = sv["hn"]
            G["ssm_in_proj"][i] = jnp.concatenate([
                matmul(hn, dz, "tn", F32, f"ssm_z_dw{i}"), matmul(hn, dxbc, "tn", F32, f"ssm_xbc_dw{i}"),
                matmul(hn, ddtr, "tn", F32, f"ssm_dt_dw{i}")[:, :SSM_HEADS]], axis=1)
            dhn = matmul(dz, sv["wz"], "nt", F32, f"ssm_z_dx{i}")
            dhn = matmul(dxbc, sv["wxbc"], "nt", F32, f"ssm_xbc_dx{i}", add=dhn)
            dhn = matmul(ddtr, sv["wdt"], "nt", F32, f"ssm_dt_dx{i}", add=dhn)
        else:
            j = i - N_A
            do = bs(matmul(dh16,full["w_o"][j], "nt", BF16, f"sb_o_dx{j}"))
            G["w_o"][j] = matmul(fl(sv["o"]), dh16, "tn", F32, f"sb_o_dw{j}")
            dq, dk, dv = sb_attention_bwd(sv["q"], kv, sv["ltot"], do, dk, dv, f"sb_attn_b{j}")
            dq = fl(dq)
            G["w_q"][j] = matmul(sv["hn"], dq, "tn", F32, f"sb_q_dw{j}")
            dhn = matmul(dq, full["w_q"][j], "nt", F32, f"sb_q_dx{j}")
        dh, gn_, _ = rmsnorm_bwd(dhn, sv["h0"], row(attn_norm[i]) + (0.0 if i < N_A else tok), dh, f"attn_norm_b{i}")
        G["attn_norm"][i] = gn_[0]
        src = [grad_slots(n, idx) for n, idx in layer_items(2 * i)]
        if i == 0:
            small_g = {n: jnp.stack(G[n]) for n in SMALL}
            src.append(_pack_rows([_to_slots(small_g[n], SMALL[n]) for n in SMALL], (N_DEV,)))
        scatter_src[2 * i] = src
        scatter_h[2 * i] = exchange_start("scatter", src, f"a2a_start{2 * i}")
    grad_x = dh.reshape(bsz, seq, d)
    grads = {n: jnp.stack(G[n]) if isinstance(G[n], list) else G[n] for n in REPL}

    layer_parts = {n: {} for n in BIG}
    small_parts = None
    for i in reversed(range(N_GROUPS)):
        lands = exchange_wait(scatter_h[i], dh, f"a2a_wait{i}")
        got = [put_own(l, lax.dynamic_index_in_dim(s, me, 0, keepdims=False)) for l, s in zip(lands, scatter_src[i])]
        for (n, idx), g in zip(layer_items(i), got):
            layer_parts[n][idx] = g
        if i == 0:
            small_parts = _unpack_rows(got[-1], [given[n].size for n in SMALL], (N_DEV,))
    big_parts = [jnp.concatenate([layer_parts[n][k] for k in sorted(layer_parts[n], key=lambda v: -1 if v is None else v)], axis=1)
                 for n in BIG]
    parts_rp = all_gather([_pack_rows([grads[n].reshape(-1) for n in REPL])], "ag_repl_grads")[0]
    sh_names = list(BIG) + list(SMALL)
    sh_parts = big_parts + small_parts
    rp_parts = _unpack_rows(parts_rp, [given[n].size for n in REPL], (N_DEV,))
    res = {}
    for n, pr in list(zip(sh_names, sh_parts)) + list(zip(REPL, rp_parts)):
        w2 = _as2d(given[n])
        outs = adamw(w2, pr.reshape((N_DEV,) + w2.shape), _as2d(given["m_" + n]), _as2d(given["v_" + n]), f"adamw_{n}")
        res[n] = [o.reshape(given[n].shape) for o in outs]
    order = ["attn_norm", "ffn_norm", "ple_norm", "ssm_in_proj", "ssm_conv_w", "ssm_conv_b", "ssm_dt_bias", "ssm_a_log",
             "ssm_d", "ssm_norm", "ssm_out_proj", "kv_norm", "w_kv", "w_q", "w_o", "ffn_up", "ffn_conv_w", "ffn_conv_b",
             "ffn_down", "ple_gate", "ple_proj", "final_norm"]
    return (loss, grad_x, *[res[n][0] for n in order], *[res[n][1] for n in order],
            *[res[n][2] for n in order], *[res[n][3] for n in order])
```
